```python
import math
import jax, jax.numpy as jnp
from jax import lax
import numpy as np

D_MODEL = 1024
BATCH = 8
SEQ = 4096
DEPTH = 1

CHUNK = 64
Q_BLOCK = 128
MEM_LEN = 256
EPS = 1e-6

DA_HEADS = 4
DA_QK_DIM = D_MODEL // 16
DA_V_DIM = 2 * DA_QK_DIM
DA_QK_COLS = DA_HEADS * 2 * DA_QK_DIM
DA_WIDTH = DA_HEADS * DA_V_DIM

GLA_HEADS = 4
GLA_K_DIM = D_MODEL // 16
GLA_V_DIM = D_MODEL // 8
GLA_QK_COLS = GLA_HEADS * GLA_K_DIM
GLA_WIDTH = GLA_HEADS * GLA_V_DIM
GLA_GATE_RANK = 16
GLA_TAU = 16.0

MIX_WIDTH = DA_WIDTH + GLA_WIDTH
IN_WIDTH = 2 * DA_QK_COLS + DA_WIDTH + 2 * GLA_QK_COLS + 2 * GLA_WIDTH + GLA_GATE_RANK

CROSS_HEADS = 4
CROSS_DIM = D_MODEL // CROSS_HEADS

N_GROUPS = 4
EXPERTS_PER_GROUP = 8
N_EXPERTS = N_GROUPS * EXPERTS_PER_GROUP
TOP_K = 2
D_EXPERT = D_MODEL // 2
EXPERT_BLOCK = 128

kernel_name = 'hybrid_diffattn_gla_hiermoe_layer'


def _rmsnorm(t, g):
    tf = t.astype(jnp.float32)
    tf = tf * lax.rsqrt(jnp.mean(tf * tf, axis=-1, keepdims=True) + EPS)
    return (tf * g.astype(jnp.float32)).astype(t.dtype)


def _split_heads(t, n):
    b, s, _ = t.shape
    return t.reshape(b, s, n, -1).transpose(0, 2, 1, 3)


def _merge_heads(t):
    b, h, s, d = t.shape
    return t.transpose(0, 2, 1, 3).reshape(b, s, h * d)


def _diff_attention(q1, q2, k1, k2, v, lam):
    b, h, s, d = q1.shape
    n_blocks = s // Q_BLOCK
    scale = d ** -0.5
    key_chunk = jnp.arange(s) // CHUNK

    def to_blocks(t):
        return t.reshape(b, h, n_blocks, Q_BLOCK, d).transpose(2, 0, 1, 3, 4)

    def one_block(args):
        qb1, qb2, bi = args
        q_chunk = (bi * Q_BLOCK + jnp.arange(Q_BLOCK)) // CHUNK
        mask = key_chunk[None, :] <= q_chunk[:, None]

        def probs(qb, kk):
            sc = jnp.einsum('bhqd,bhkd->bhqk', qb, kk).astype(jnp.float32) * scale
            return jax.nn.softmax(jnp.where(mask, sc, -jnp.inf), axis=-1)

        p = probs(qb1, k1) - lam * probs(qb2, k2)
        return jnp.einsum('bhqk,bhkv->bhqv', p.astype(v.dtype), v)

    o = lax.map(one_block, (to_blocks(q1), to_blocks(q2), jnp.arange(n_blocks)))
    return o.transpose(1, 2, 0, 3, 4).reshape(b, h, s, v.shape[-1])


def _gla_chunked(q, k, v, log_a):
    b, h, s, dk = q.shape
    dv = v.shape[-1]
    nc = s // CHUNK
    q = q.reshape(b, h, nc, CHUNK, dk)
    k = k.reshape(b, h, nc, CHUNK, dk)
    v = v.reshape(b, h, nc, CHUNK, dv)
    L = jnp.cumsum(log_a.reshape(b, h, nc, CHUNK, dk), axis=3)
    L_end = L[:, :, :, -1, :]
    Lc = L - L[:, :, :, CHUNK // 2 - 1:CHUNK // 2, :]
    e_pos, e_neg = jnp.exp(Lc), jnp.exp(-Lc)
    a_past = jnp.einsum('bhnid,bhnjd->bhnij', q * e_pos, k * e_neg)
    a_fut = jnp.einsum('bhnid,bhnjd->bhnij', q * e_neg, k * e_pos)
    lower = jnp.tril(jnp.ones((CHUNK, CHUNK), dtype=bool))
    a = jnp.where(lower, a_past, a_fut)
    o_intra = jnp.einsum('bhnij,bhnjv->bhniv', a, v)
    u = jnp.einsum('bhnjd,bhnjv->bhndv', k * jnp.exp(L_end[:, :, :, None, :] - L), v)
    decay = jnp.exp(L_end)

    def step(state, inp):
        d_c, u_c = inp
        return state * d_c[..., None] + u_c, state

    _, s_prev = lax.scan(step, jnp.zeros((b, h, dk, dv), jnp.float32),
                         (jnp.moveaxis(decay, 2, 0), jnp.moveaxis(u, 2, 0)))
    s_prev = jnp.moveaxis(s_prev, 0, 2)
    o_inter = jnp.einsum('bhnid,bhndv->bhniv', q * jnp.exp(L), s_prev)
    return (o_intra + o_inter).reshape(b, h, s, dv)


def _mixer(h, norm_g, w_in, da_qn, da_kn, lq1, lk1, lq2, lk2, da_on, gate_w, gate_b, gla_on, w_o, lam_init):
    b, s, _ = h.shape
    f32 = jnp.float32
    u = _rmsnorm(h, norm_g)
    p = u @ w_in
    widths = (DA_QK_COLS, DA_QK_COLS, DA_WIDTH, GLA_QK_COLS, GLA_QK_COLS, GLA_WIDTH, GLA_WIDTH, GLA_GATE_RANK)
    parts = []
    off = 0
    for w in widths:
        parts.append(p[..., off:off + w])
        off += w
    dq, dk, dv, gq, gk, gv, gg, gr = parts

    dq = dq.reshape(b, s, DA_HEADS, 2, DA_QK_DIM).transpose(0, 2, 3, 1, 4)
    dk = dk.reshape(b, s, DA_HEADS, 2, DA_QK_DIM).transpose(0, 2, 3, 1, 4)
    dq = _rmsnorm(dq, da_qn)
    dk = _rmsnorm(dk, da_kn)
    dvh = _split_heads(dv, DA_HEADS)
    lam = (jnp.exp(jnp.sum(lq1.astype(f32) * lk1.astype(f32)))
           - jnp.exp(jnp.sum(lq2.astype(f32) * lk2.astype(f32))) + lam_init)
    da = _diff_attention(dq[:, :, 0], dq[:, :, 1], dk[:, :, 0], dk[:, :, 1], dvh, lam)
    da = _rmsnorm(da, da_on) * (1.0 - lam_init)
    da_out = _merge_heads(da)

    qg = _split_heads(gq, GLA_HEADS).astype(f32) * (GLA_K_DIM ** -0.5)
    kg = _split_heads(gk, GLA_HEADS).astype(f32)
    vg = _split_heads(gv, GLA_HEADS).astype(f32)
    z = gr.astype(f32) @ gate_w.astype(f32) + gate_b.astype(f32)
    log_a = _split_heads(jax.nn.log_sigmoid(z) / GLA_TAU, GLA_HEADS)
    og = _gla_chunked(qg, kg, vg, log_a)
    og = _rmsnorm(og, gla_on)
    gla_out = (_merge_heads(og) * jax.nn.silu(gg.astype(f32))).astype(h.dtype)

    return jnp.concatenate([da_out.astype(h.dtype), gla_out], axis=-1) @ w_o


def _cross_attention(h, mem, norm_g, norm_m, w_q, w_kv, qn, kn, w_out):
    b, s, d = h.shape
    m = mem.shape[1]
    u = _rmsnorm(h, norm_g)
    mn = _rmsnorm(mem, norm_m)
    q = _rmsnorm((u @ w_q).reshape(b, s, CROSS_HEADS, CROSS_DIM), qn)
    kv = mn @ w_kv
    k = _rmsnorm(kv[..., :d].reshape(b, m, CROSS_HEADS, CROSS_DIM), kn)
    v = kv[..., d:].reshape(b, m, CROSS_HEADS, CROSS_DIM)
    sc = jnp.einsum('bshd,bmhd->bhsm', q, k).astype(jnp.float32) * (CROSS_DIM ** -0.5)
    pr = jax.nn.softmax(sc, axis=-1).astype(v.dtype)
    o = jnp.einsum('bhsm,bmhd->bshd', pr, v).reshape(b, s, d)
    return o @ w_out


def _hier_moe(h, norm_g, w_group, b_group, w_expert, b_expert, w_gate, w_up, w_down):
    b, s, d = h.shape
    t = b * s
    f32 = jnp.float32
    xf = _rmsnorm(h, norm_g).reshape(t, d)
    xr = xf.astype(f32)
    p_group = jax.nn.softmax(xr @ w_group.astype(f32) + b_group.astype(f32), axis=-1)
    g_sel = jnp.argmax(p_group, axis=-1)
    p_g = jnp.take_along_axis(p_group, g_sel[:, None], axis=-1)
    e_logits = (xr @ w_expert.astype(f32) + b_expert.astype(f32)).reshape(t, N_GROUPS, EXPERTS_PER_GROUP)
    e_logits = jnp.take_along_axis(e_logits, g_sel[:, None, None], axis=1)[:, 0]
    top_p, top_i = lax.top_k(jax.nn.softmax(e_logits, axis=-1), TOP_K)
    gate = p_g * top_p / jnp.sum(top_p, axis=-1, keepdims=True)
    flat_id = (g_sel[:, None] * EXPERTS_PER_GROUP + top_i).reshape(-1).astype(jnp.int32)
    flat_gate = gate.reshape(-1)
    n_assign = t * TOP_K
    order = jnp.argsort(flat_id)
    sorted_id = flat_id[order]
    token_of = order // TOP_K
    sizes = jnp.bincount(flat_id, length=N_EXPERTS)
    padded = (sizes + EXPERT_BLOCK - 1) // EXPERT_BLOCK * EXPERT_BLOCK
    start = jnp.cumsum(sizes) - sizes
    padded_end = jnp.cumsum(padded)
    padded_start = padded_end - padded
    dest = padded_start[sorted_id] + jnp.arange(n_assign) - start[sorted_id]
    n_rows = n_assign + N_EXPERTS * EXPERT_BLOCK
    n_blocks = n_rows // EXPERT_BLOCK
    x_pad = jnp.zeros((n_rows, d), xf.dtype).at[dest].set(xf[token_of])
    block_expert = jnp.minimum(
        jnp.searchsorted(padded_end, jnp.arange(n_blocks) * EXPERT_BLOCK, side='right'), N_EXPERTS - 1)

    def expert_block(args):
        xb, e = args
        hid = jax.nn.silu(xb @ w_gate[e]) * (xb @ w_up[e])
        return hid @ w_down[e]

    out_pad = lax.map(expert_block, (x_pad.reshape(n_blocks, EXPERT_BLOCK, d), block_expert)).reshape(n_rows, d)
    out = out_pad[dest] * flat_gate[order][:, None].astype(out_pad.dtype)
    y = jnp.zeros((t, d), out.dtype).at[token_of].add(out)
    return y.reshape(b, s, d).astype(h.dtype)


def setup_inputs(seed: int = 0) -> dict:
    key = jax.random.key(seed)
    ks = iter(jax.random.split(key, 40))
    f32 = jnp.float32
    D = D_MODEL

    def nrm(shape, scale):
        return scale * jax.random.normal(next(ks), shape, f32)

    def gain(n):
        return 1.0 + 0.02 * jax.random.normal(next(ks), (DEPTH, n), f32)

    return {
        'x': nrm((BATCH, SEQ, D), 1.0),
        'mem': nrm((BATCH, MEM_LEN, D), 1.0),
        'norm_mix': gain(D),
        'w_in': nrm((DEPTH, D, IN_WIDTH), D ** -0.5),
        'da_q_norm': gain(DA_QK_DIM),
        'da_k_norm': gain(DA_QK_DIM),
        'lambda_q1': nrm((DEPTH, DA_QK_DIM), 0.1),
        'lambda_k1': nrm((DEPTH, DA_QK_DIM), 0.1),
        'lambda_q2': nrm((DEPTH, DA_QK_DIM), 0.1),
        'lambda_k2': nrm((DEPTH, DA_QK_DIM), 0.1),
        'da_out_norm': gain(DA_V_DIM),
        'gla_gate_w': nrm((DEPTH, GLA_GATE_RANK, GLA_QK_COLS), GLA_GATE_RANK ** -0.5),
        'gla_gate_b': nrm((DEPTH, GLA_QK_COLS), 0.1),
        'gla_out_norm': gain(GLA_V_DIM),
        'w_o': nrm((DEPTH, MIX_WIDTH, D), MIX_WIDTH ** -0.5),
        'norm_cross': gain(D),
        'norm_mem': gain(D),
        'w_cq': nrm((DEPTH, D, D), D ** -0.5),
        'w_ckv': nrm((DEPTH, D, 2 * D), D ** -0.5),
        'cross_q_norm': gain(CROSS_DIM),
        'cross_k_norm': gain(CROSS_DIM),
        'w_co': nrm((DEPTH, D, D), D ** -0.5),
        'norm_ffn': gain(D),
        'w_group': nrm((DEPTH, D, N_GROUPS), D ** -0.5),
        'b_group': nrm((DEPTH, N_GROUPS), 0.01),
        'w_expert': nrm((DEPTH, D, N_EXPERTS), D ** -0.5),
        'b_expert': nrm((DEPTH, N_EXPERTS), 0.01),
        'w_e_gate': nrm((DEPTH, N_EXPERTS, D, D_EXPERT), D ** -0.5),
        'w_e_up': nrm((DEPTH, N_EXPERTS, D, D_EXPERT), D ** -0.5),
        'w_e_down': nrm((DEPTH, N_EXPERTS, D_EXPERT, D), D_EXPERT ** -0.5),
    }


def reference(x, mem, norm_mix, w_in, da_q_norm, da_k_norm, lambda_q1, lambda_k1, lambda_q2, lambda_k2,
              da_out_norm, gla_gate_w, gla_gate_b, gla_out_norm, w_o, norm_cross, norm_mem, w_cq, w_ckv,
              cross_q_norm, cross_k_norm, w_co, norm_ffn, w_group, b_group, w_expert, b_expert,
              w_e_gate, w_e_up, w_e_down):
    h = x
    for l in range(DEPTH):
        lam_init = 0.8 - 0.6 * math.exp(-0.3 * l)
        h = h + _mixer(h, norm_mix[l], w_in[l], da_q_norm[l], da_k_norm[l], lambda_q1[l], lambda_k1[l],
                       lambda_q2[l], lambda_k2[l], da_out_norm[l], gla_gate_w[l], gla_gate_b[l],
                       gla_out_norm[l], w_o[l], lam_init)
        h = h + _cross_attention(h, mem, norm_cross[l], norm_mem[l], w_cq[l], w_ckv[l],
                                 cross_q_norm[l], cross_k_norm[l], w_co[l])
        h = h + _hier_moe(h, norm_ffn[l], w_group[l], b_group[l], w_expert[l], b_expert[l],
                          w_e_gate[l], w_e_up[l], w_e_down[l])
    return h
```

```python
import functools
import math

import jax
import jax.numpy as jnp
from jax import lax
from jax.experimental import pallas as pl
from jax.experimental.pallas import tpu as pltpu

F32 = jnp.float32
BF16 = jnp.bfloat16
I32 = jnp.int32

EPS = 1e-6
CHUNK = 64

DA_HEADS = 4
DA_QK_DIM = 64
DA_V_DIM = 128
GLA_HEADS = 4
GLA_K_DIM = 64
GLA_V_DIM = 128
GLA_GATE_RANK = 16
GLA_TAU = 16.0
CROSS_HEADS = 4
N_GROUPS = 4
EXPERTS_PER_GROUP = 8
N_EXPERTS = N_GROUPS * EXPERTS_PER_GROUP
LAM_INIT = 0.8 - 0.6 * math.exp(-0.3 * 0)

LANES = 128
VMEM_LIMIT = 56 * 1024 * 1024

TM_PROJ = 512
T_ATT = 512
TS_GLA = 1024
TM_POST = 512
TM_DISP = 512
EXPERT_ROWS = 256
TM_COMB = 256

NEG_INF = float("-inf")


def _params(*sem):
    return pltpu.CompilerParams(dimension_semantics=sem, vmem_limit_bytes=VMEM_LIMIT)


def _rms(t, g):
    ms = jnp.mean(t * t, axis=-1, keepdims=True)
    return t * lax.rsqrt(ms + EPS) * g


def _dot(a, b):
    return jnp.dot(a, b, preferred_element_type=F32)


def _dot_nt(a, b):
    return lax.dot_general(a, b, (((1,), (1,)), ((), ())), preferred_element_type=F32)


def _dot_tn(a, b):
    return lax.dot_general(a, b, (((0,), (0,)), ((), ())), preferred_element_type=F32)


def _split_bf16(t):
    hi = t.astype(BF16)
    lo = (t - hi.astype(F32)).astype(BF16)
    return hi, lo


def _mem_kv_kernel(mem_ref, g_ref, w_ref, kn_ref, k_ref, v_ref, *, d, heads):
    mn = _rms(mem_ref[0], g_ref[...]).astype(BF16)
    kv = _dot(mn, w_ref[...])
    hd = d // heads
    scale = hd ** -0.5
    for h in range(heads):
        kh = _rms(kv[:, h * hd:(h + 1) * hd], kn_ref[...]) * scale
        k_ref[0, :, h * hd:(h + 1) * hd] = kh.astype(BF16)
    v_ref[0] = kv[:, d:].astype(BF16)


def _mem_kv(mem, norm_m, w_ckv, kn):
    b, m, d = mem.shape
    return pl.pallas_call(
        functools.partial(_mem_kv_kernel, d=d, heads=CROSS_HEADS),
        grid=(b,),
        in_specs=[
            pl.BlockSpec((1, m, d), lambda i: (i, 0, 0)),
            pl.BlockSpec((1, d), lambda i: (0, 0)),
            pl.BlockSpec((d, 2 * d), lambda i: (0, 0)),
            pl.BlockSpec((1, d // CROSS_HEADS), lambda i: (0, 0)),
        ],
        out_specs=[
            pl.BlockSpec((1, m, d), lambda i: (i, 0, 0)),
            pl.BlockSpec((1, m, d), lambda i: (i, 0, 0)),
        ],
        out_shape=[jax.ShapeDtypeStruct((b, m, d), BF16)] * 2,
        compiler_params=_params("parallel"),
        name="mem_kv",
    )(mem, norm_m.reshape(1, d), w_ckv.astype(BF16), kn.reshape(1, -1))


_QK = DA_HEADS * 2 * DA_QK_DIM
_DAW = DA_HEADS * DA_V_DIM
_GQK = GLA_HEADS * GLA_K_DIM
_GW = GLA_HEADS * GLA_V_DIM
_OFF_DQ = 0
_OFF_DK = _OFF_DQ + _QK
_OFF_DV = _OFF_DK + _QK
_OFF_GQ = _OFF_DV + _DAW
_OFF_GK = _OFF_GQ + _GQK
_OFF_GV = _OFF_GK + _GQK
_OFF_GG = _OFF_GV + _GW
_OFF_GR = _OFF_GG + _GW
_IN_PAD = _OFF_GR + LANES


def _in_proj_kernel(x_ref, g_ref, w_ref, qg_ref, kg_ref, grp_ref, gw_ref, gb_ref,
                    qt_ref, k_ref, vt_ref, gq_ref, gk_ref, gv_ref, gg_ref, la_ref):
    u = _rms(x_ref[0], g_ref[...]).astype(BF16)

    def proj(off, width):
        return _dot(u, w_ref[:, off:off + width])

    def group_norm(p, gain):
        ms = _dot((p * p).astype(BF16), grp_ref[...])
        return p * lax.rsqrt(ms + EPS) * gain

    qn = group_norm(proj(_OFF_DQ, _QK), qg_ref[...]) * (DA_QK_DIM ** -0.5)
    qt_ref[0, 0] = qn.T.astype(BF16)
    k_ref[0] = group_norm(proj(_OFF_DK, _QK), kg_ref[...]).astype(BF16)
    vt_ref[0, 0] = proj(_OFF_DV, _DAW).T.astype(BF16)
    gq_ref[0] = proj(_OFF_GQ, _GQK) * (GLA_K_DIM ** -0.5)
    gk_ref[0] = proj(_OFF_GK, _GQK)
    gv_ref[0] = proj(_OFF_GV, _GW).astype(BF16)
    gg_ref[0] = proj(_OFF_GG, _GW)
    z = _dot(proj(_OFF_GR, LANES).astype(BF16), gw_ref[...]) + gb_ref[...]
    log_sig = jnp.minimum(z, 0.0) - jnp.log(1.0 + jnp.exp(-jnp.abs(z)))
    la_ref[0] = log_sig * (1.0 / GLA_TAU)


def _in_proj(x, norm_g, w_in, da_qn, da_kn, gate_w, gate_b):
    b, s, d = x.shape
    tm = TM_PROJ
    ns = s // tm
    w = jnp.pad(w_in, ((0, 0), (0, _IN_PAD - w_in.shape[1]))).astype(BF16)
    gw = jnp.pad(gate_w, ((0, LANES - GLA_GATE_RANK), (0, 0))).astype(BF16)
    lane = jnp.arange(_QK)
    grp = jnp.where((lane[:, None] // DA_QK_DIM) == (lane[None, :] // DA_QK_DIM),
                    1.0 / DA_QK_DIM, 0.0).astype(BF16)
    const = lambda shape: pl.BlockSpec(shape, lambda i, j: (0,) * len(shape))
    tile = lambda width: pl.BlockSpec((1, tm, width), lambda i, j: (i, j, 0))
    tile_t = lambda width: pl.BlockSpec((1, 1, width, tm), lambda i, j: (i, j, 0, 0))
    return pl.pallas_call(
        _in_proj_kernel,
        grid=(b, ns),
        in_specs=[tile(d), const((1, d)), const((d, _IN_PAD)), const((1, _QK)), const((1, _QK)),
                  const((_QK, _QK)), const((LANES, _GQK)), const((1, _GQK))],
        out_specs=[tile_t(_QK), tile(_QK), tile_t(_DAW), tile(_GQK), tile(_GQK), tile(_GW), tile(_GW),
                   tile(_GQK)],
        out_shape=[
            jax.ShapeDtypeStruct((b, ns, _QK, tm), BF16),
            jax.ShapeDtypeStruct((b, s, _QK), BF16),
            jax.ShapeDtypeStruct((b, ns, _DAW, tm), BF16),
            jax.ShapeDtypeStruct((b, s, _GQK), F32),
            jax.ShapeDtypeStruct((b, s, _GQK), F32),
            jax.ShapeDtypeStruct((b, s, _GW), BF16),
            jax.ShapeDtypeStruct((b, s, _GW), F32),
            jax.ShapeDtypeStruct((b, s, _GQK), F32),
        ],
        compiler_params=_params("parallel", "parallel"),
        name="in_proj",
    )(x, norm_g.reshape(1, d), w, jnp.tile(da_qn, 2 * DA_HEADS).reshape(1, _QK),
      jnp.tile(da_kn, 2 * DA_HEADS).reshape(1, _QK), grp, gw, gate_b.reshape(1, _GQK))


def _diff_attn_kernel(lq1_ref, lk1_ref, lq2_ref, lk2_ref, gain_ref, qt_ref, k_ref, vt_ref, out_ref,
                      m1_ref, l1_ref, a1_ref, m2_ref, l2_ref, a2_ref, *, blk):
    qi = pl.program_id(2)
    qt = qt_ref[0, 0]
    row = lax.broadcasted_iota(I32, qt.shape, 0)
    zero = jnp.zeros_like(qt)
    q1 = jnp.where(row < DA_QK_DIM, qt, zero)
    q2 = jnp.where(row >= DA_QK_DIM, qt, zero)

    for m_ref, l_ref, a_ref in ((m1_ref, l1_ref, a1_ref), (m2_ref, l2_ref, a2_ref)):
        m_ref[...] = jnp.full(m_ref.shape, NEG_INF, F32)
        l_ref[...] = jnp.zeros(l_ref.shape, F32)
        a_ref[...] = jnp.zeros(a_ref.shape, F32)

    def update(s, vb, m_ref, l_ref, a_ref):
        m_old = m_ref[...]
        m_new = jnp.maximum(m_old, jnp.max(s, axis=0, keepdims=True))
        alpha = jnp.exp(m_old - m_new)
        p = jnp.exp(s - m_new)
        l_ref[...] = alpha * l_ref[...] + jnp.sum(p, axis=0, keepdims=True)
        a_ref[...] = alpha * a_ref[...] + _dot(vb, p.astype(BF16))
        m_ref[...] = m_new

    def block(j, mask):
        kb = k_ref[0, pl.ds(pl.multiple_of(j * blk, blk), blk), :]
        vb = vt_ref[0, j]
        s1 = _dot(kb, q1)
        s2 = _dot(kb, q2)
        if mask is not None:
            s1 = jnp.where(mask, s1, NEG_INF)
            s2 = jnp.where(mask, s2, NEG_INF)
        update(s1, vb, m1_ref, l1_ref, a1_ref)
        update(s2, vb, m2_ref, l2_ref, a2_ref)

    def body(j, carry):
        block(j, None)
        return carry

    lax.fori_loop(0, qi, body, 0)
    key_chunk = lax.broadcasted_iota(I32, (blk, blk), 0) // CHUNK
    qry_chunk = lax.broadcasted_iota(I32, (blk, blk), 1) // CHUNK
    block(qi, key_chunk <= qry_chunk)

    lam = (jnp.exp(jnp.sum(lq1_ref[...] * lk1_ref[...], axis=-1, keepdims=True))
           - jnp.exp(jnp.sum(lq2_ref[...] * lk2_ref[...], axis=-1, keepdims=True)) + LAM_INIT)
    o = a1_ref[...] / l1_ref[...] - lam * (a2_ref[...] / l2_ref[...])
    ms = jnp.mean(o * o, axis=0, keepdims=True)
    o = o * lax.rsqrt(ms + EPS) * gain_ref[...] * (1.0 - LAM_INIT)
    out_ref[0] = o.T.astype(BF16)


def _diff_attn(qt, k, vt, lq1, lk1, lq2, lk2, da_on):
    b, nb, _, blk = qt.shape
    s = nb * blk
    vec = lambda: pl.BlockSpec((1, DA_QK_DIM), lambda i, h, q: (0, 0))
    return pl.pallas_call(
        functools.partial(_diff_attn_kernel, blk=blk),
        grid=(b, DA_HEADS, nb),
        in_specs=[
            vec(), vec(), vec(), vec(),
            pl.BlockSpec((DA_V_DIM, 1), lambda i, h, q: (0, 0)),
            pl.BlockSpec((1, 1, 2 * DA_QK_DIM, blk), lambda i, h, q: (i, q, h, 0)),
            pl.BlockSpec((1, s, 2 * DA_QK_DIM), lambda i, h, q: (i, 0, h)),
            pl.BlockSpec((1, nb, DA_V_DIM, blk), lambda i, h, q: (i, 0, h, 0)),
        ],
        out_specs=pl.BlockSpec((1, blk, DA_V_DIM), lambda i, h, q: (i, q, h)),
        out_shape=jax.ShapeDtypeStruct((b, s, _DAW), BF16),
        scratch_shapes=[
            pltpu.VMEM((1, blk), F32), pltpu.VMEM((1, blk), F32), pltpu.VMEM((DA_V_DIM, blk), F32),
            pltpu.VMEM((1, blk), F32), pltpu.VMEM((1, blk), F32), pltpu.VMEM((DA_V_DIM, blk), F32),
        ],
        compiler_params=_params("parallel", "parallel", "parallel"),
        name="diff_attn",
    )(lq1.reshape(1, -1), lk1.reshape(1, -1), lq2.reshape(1, -1), lk2.reshape(1, -1),
      da_on.reshape(-1, 1), qt, k, vt)


def _gla_kernel(q_ref, k_ref, la_ref, v_ref, g_ref, gain_ref, out_ref, st_ref, *, ts):
    @pl.when(pl.program_id(1) == 0)
    def _():
        st_ref[...] = jnp.zeros(st_ref.shape, F32)

    c = CHUNK
    hk, hv = _GQK, _GW
    r = lax.broadcasted_iota(I32, (c, c), 0)
    cc = lax.broadcasted_iota(I32, (c, c), 1)
    tri = jnp.where(r >= cc, 1.0, 0.0).astype(BF16)
    bd_k = (lax.broadcasted_iota(I32, (hk, hk), 0) // GLA_K_DIM
            == lax.broadcasted_iota(I32, (hk, hk), 1) // GLA_K_DIM)
    bd_v = (lax.broadcasted_iota(I32, (hk, hv), 0) // GLA_K_DIM
            == lax.broadcasted_iota(I32, (hk, hv), 1) // GLA_V_DIM)
    bd_vt = (lax.broadcasted_iota(I32, (hv, hk), 0) // GLA_V_DIM
             == lax.broadcasted_iota(I32, (hv, hk), 1) // GLA_K_DIM)
    lower = (lax.broadcasted_iota(I32, (c, hk), 0)
             >= lax.broadcasted_iota(I32, (c, hk), 1) % c)

    def chunk(ci, carry):
        sl = pl.ds(pl.multiple_of(ci * c, c), c)
        la_hi, la_lo = _split_bf16(la_ref[0, sl, :])
        big_l = _dot(tri, la_hi) + _dot(tri, la_lo)
        l_end = big_l[c - 1:c, :]
        lc = big_l - big_l[c // 2 - 1:c // 2, :]
        e_pos = jnp.exp(lc)
        e_neg = jnp.exp(-lc)
        q = q_ref[0, sl, :]
        k = k_ref[0, sl, :]
        v = v_ref[0, sl, :]

        def tiled(t, mask):
            t4 = jnp.concatenate([t] * GLA_HEADS, axis=0)
            return jnp.where(mask, t4, jnp.zeros_like(t4))

        a_past = _dot_nt((q * e_pos).astype(BF16), tiled((k * e_neg).astype(BF16), bd_k))
        a_fut = _dot_nt((q * e_neg).astype(BF16), tiled((k * e_pos).astype(BF16), bd_k))
        a = jnp.where(lower, a_past, a_fut).astype(BF16)
        o = _dot(a, tiled(v, bd_v))
        st = st_ref[...]
        o = o + _dot_nt((q * jnp.exp(big_l)).astype(BF16), st.astype(BF16))
        u_t = _dot_tn(v, (k * jnp.exp(l_end - big_l)).astype(BF16))
        st_ref[...] = st * jnp.exp(l_end) + jnp.where(bd_vt, u_t, 0.0)

        g = g_ref[0, sl, :]
        silu = g / (1.0 + jnp.exp(-g))
        for h in range(GLA_HEADS):
            hs = slice(h * GLA_V_DIM, (h + 1) * GLA_V_DIM)
            out_ref[0, sl, hs] = (_rms(o[:, hs], gain_ref[...]) * silu[:, hs]).astype(BF16)
        return carry

    lax.fori_loop(0, ts // c, chunk, 0)


def _gla(gq, gk, la, gv, gg, gla_on):
    b, s, _ = gq.shape
    ts = TS_GLA
    tile = lambda width: pl.BlockSpec((1, ts, width), lambda i, j: (i, j, 0))
    return pl.pallas_call(
        functools.partial(_gla_kernel, ts=ts),
        grid=(b, s // ts),
        in_specs=[tile(_GQK), tile(_GQK), tile(_GQK), tile(_GW), tile(_GW),
                  pl.BlockSpec((1, GLA_V_DIM), lambda i, j: (0, 0))],
        out_specs=tile(_GW),
        out_shape=jax.ShapeDtypeStruct((b, s, _GW), BF16),
        scratch_shapes=[pltpu.VMEM((_GW, _GQK), F32)],
        compiler_params=_params("parallel", "arbitrary"),
        name="gla",
    )(gq, gk, la, gv, gg, gla_on.reshape(1, -1))


_META_E0, _META_E1, _META_G0, _META_G1, _META_P0, _META_P1 = range(6)
_EXP_LANE0 = N_GROUPS


def _post_kernel(x_ref, da_ref, gla_ref, wo_ref, gc_ref, wq_ref, qn_ref, km_ref, vm_ref, wco_ref,
                 gf_ref, wr_hi_ref, wr_lo_ref, br_ref,
                 h_ref, xn_ref, meta_ref, cnt_ref, *, d, tm):
    first = jnp.logical_and(pl.program_id(0) == 0, pl.program_id(1) == 0)

    @pl.when(first)
    def _():
        cnt_ref[...] = jnp.zeros(cnt_ref.shape, F32)

    half = d // 2
    h1 = x_ref[0] + _dot(da_ref[0], wo_ref[:half, :]) + _dot(gla_ref[0], wo_ref[half:, :])

    u = _rms(h1, gc_ref[...]).astype(BF16)
    q = _dot(u, wq_ref[...])
    hd = d // CROSS_HEADS
    heads = []
    for h in range(CROSS_HEADS):
        hs = slice(h * hd, (h + 1) * hd)
        qh = _rms(q[:, hs], qn_ref[...]).astype(BF16)
        sc = _dot_nt(qh, km_ref[0, :, hs])
        sc = sc - jnp.max(sc, axis=-1, keepdims=True)
        p = jnp.exp(sc)
        p = p / jnp.sum(p, axis=-1, keepdims=True)
        heads.append(_dot(p.astype(BF16), vm_ref[0, :, hs]))
    o = jnp.concatenate(heads, axis=-1).astype(BF16)
    h2 = h1 + _dot(o, wco_ref[...])
    h_ref[0] = h2

    xn = _rms(h2, gf_ref[...])
    xn_ref[0] = xn
    x_hi, x_lo = _split_bf16(xn)
    logits = (_dot(x_hi, wr_hi_ref[...]) + _dot(x_lo, wr_hi_ref[...]) + _dot(x_hi, wr_lo_ref[...])
              + br_ref[...])
    lane = lax.broadcasted_iota(I32, logits.shape, 1)
    big = jnp.int32(LANES)

    def lane_argmax(vals):
        m = jnp.max(vals, axis=-1, keepdims=True)
        idx = jnp.min(jnp.where(vals == m, lane, big), axis=-1, keepdims=True)
        return m, idx

    lg = jnp.where(lane < N_GROUPS, logits, NEG_INF)
    g_max, g_sel = lane_argmax(lg)
    p_g = 1.0 / jnp.sum(jnp.exp(lg - g_max), axis=-1, keepdims=True)
    e_lo = _EXP_LANE0 + g_sel * EXPERTS_PER_GROUP
    in_group = jnp.logical_and(lane >= e_lo, lane < e_lo + EXPERTS_PER_GROUP)
    le = jnp.where(in_group, logits, NEG_INF)
    m1, i1 = lane_argmax(le)
    m2, i2 = lane_argmax(jnp.where(lane == i1, NEG_INF, le))
    e2 = jnp.exp(m2 - m1)
    gate0 = p_g / (1.0 + e2)
    gate1 = p_g * e2 / (1.0 + e2)
    e0 = i1 - _EXP_LANE0
    e1 = i2 - _EXP_LANE0

    hot0 = lane == e0
    hot1 = lane == e1
    onehot = jnp.where(jnp.logical_or(hot0, hot1), 1.0, 0.0)
    r = lax.broadcasted_iota(I32, (tm, tm), 0)
    c = lax.broadcasted_iota(I32, (tm, tm), 1)
    strict_lower = jnp.where(r > c, 1.0, 0.0).astype(BF16)
    base = cnt_ref[0:1, :]
    before = _dot(strict_lower, onehot.astype(BF16)) + base
    pos0 = jnp.sum(jnp.where(hot0, before, 0.0), axis=-1, keepdims=True)
    pos1 = jnp.sum(jnp.where(hot1, before, 0.0), axis=-1, keepdims=True)
    cnt_ref[...] = jnp.broadcast_to(base + jnp.sum(onehot, axis=0, keepdims=True), cnt_ref.shape)

    meta = jnp.zeros(logits.shape, F32)
    for idx, val in ((_META_E0, e0.astype(F32)), (_META_E1, e1.astype(F32)), (_META_G0, gate0),
                     (_META_G1, gate1), (_META_P0, pos0), (_META_P1, pos1)):
        meta = jnp.where(lane == idx, val, meta)
    meta_ref[0] = meta


def _post(x, da, gla, w_o, norm_cross, w_cq, cross_qn, k_mem, v_mem, w_co, norm_ffn, w_group, b_group,
          w_expert, b_expert):
    b, s, d = x.shape
    tm = TM_POST
    m = k_mem.shape[1]
    w_r = jnp.pad(jnp.concatenate([w_group, w_expert], axis=1), ((0, 0), (0, LANES - N_GROUPS - N_EXPERTS)))
    w_r_hi = w_r.astype(BF16)
    w_r_lo = (w_r - w_r_hi.astype(F32)).astype(BF16)
    b_r = jnp.pad(jnp.concatenate([b_group, b_expert]), (0, LANES - N_GROUPS - N_EXPERTS)).reshape(1, LANES)
    const = lambda shape: pl.BlockSpec(shape, lambda i, j: (0,) * len(shape))
    tile = lambda width: pl.BlockSpec((1, tm, width), lambda i, j: (i, j, 0))
    per_b = lambda: pl.BlockSpec((1, m, d), lambda i, j: (i, 0, 0))
    return pl.pallas_call(
        functools.partial(_post_kernel, d=d, tm=tm),
        grid=(b, s // tm),
        in_specs=[tile(d), tile(d // 2), tile(d // 2), const((d, d)), const((1, d)), const((d, d)),
                  const((1, d // CROSS_HEADS)), per_b(), per_b(), const((d, d)), const((1, d)),
                  const((d, LANES)), const((d, LANES)), const((1, LANES))],
        out_specs=[tile(d), tile(d), tile(LANES), const((8, LANES))],
        out_shape=[
            jax.ShapeDtypeStruct((b, s, d), F32),
            jax.ShapeDtypeStruct((b, s, d), F32),
            jax.ShapeDtypeStruct((b, s, LANES), F32),
            jax.ShapeDtypeStruct((8, LANES), F32),
        ],
        compiler_params=_params("arbitrary", "arbitrary"),
        name="post",
    )(x, da, gla, w_o.astype(BF16), norm_cross.reshape(1, d), w_cq.astype(BF16), cross_qn.reshape(1, -1),
      k_mem, v_mem, w_co.astype(BF16), norm_ffn.reshape(1, d), w_r_hi, w_r_lo, b_r)


def _row_copy(src_ref, src_row, dst_ref, dst_row, sem):
    return pltpu.make_async_copy(src_ref.at[pl.ds(src_row, 1)], dst_ref.at[pl.ds(dst_row, 1)], sem)


def _dispatch_kernel(pstart_ref, size_ref, pend_ref, ids_ref, xn_ref, xpad_ref, zero_ref, sem, zsem, *, tm):
    i = pl.program_id(0)

    @pl.when(i == 0)
    def _():
        zero_ref[...] = jnp.zeros(zero_ref.shape, F32)
        rows = zero_ref.shape[0]
        n_blocks = xpad_ref.shape[0] // rows
        n_used = pend_ref[N_EXPERTS - 1] // rows

        def zero_block(blk):
            return pltpu.make_async_copy(zero_ref, xpad_ref.at[pl.ds(pl.multiple_of(blk * rows, rows), rows)], zsem)

        def last_block(e, fn):
            @pl.when(size_ref[e] > 0)
            def _():
                fn(zero_block(pend_ref[e] // rows - 1))

        for fn in (lambda cp: cp.start(), lambda cp: cp.wait()):
            lax.fori_loop(0, N_EXPERTS, lambda e, c: (last_block(e, fn), c)[1], 0)
            lax.fori_loop(n_used, n_blocks, lambda blk, c: (fn(zero_block(blk)), c)[1], 0)

    base = i * tm

    def copies(t):
        return [_row_copy(xn_ref, base + t, xpad_ref, pstart_ref[ids_ref[0, k, t]] + ids_ref[0, 2 + k, t], sem)
                for k in range(2)]

    def start(t, carry):
        for cp in copies(t):
            cp.start()
        return carry

    def wait(t, carry):
        for cp in copies(t):
            cp.wait()
        return carry

    lax.fori_loop(0, tm, start, 0)
    lax.fori_loop(0, tm, wait, 0)


def _dispatch(xn, ids, pstart, sizes, pend, n_rows):
    t, d = xn.shape
    tm = TM_DISP
    return pl.pallas_call(
        functools.partial(_dispatch_kernel, tm=tm),
        grid_spec=pltpu.PrefetchScalarGridSpec(
            num_scalar_prefetch=3,
            grid=(t // tm,),
            in_specs=[
                pl.BlockSpec((1, 4, tm), lambda i, *_: (i, 0, 0), memory_space=pltpu.SMEM),
                pl.BlockSpec(memory_space=pl.ANY),
            ],
            out_specs=pl.BlockSpec(memory_space=pl.ANY),
            scratch_shapes=[pltpu.VMEM((EXPERT_ROWS, d), F32), pltpu.SemaphoreType.DMA, pltpu.SemaphoreType.DMA],
        ),
        out_shape=jax.ShapeDtypeStruct((n_rows, d), F32),
        compiler_params=_params("arbitrary"),
        name="dispatch",
    )(pstart, sizes, pend, ids, xn)


def _experts_kernel(be_ref, nused_ref, x_ref, wg_ref, wu_ref, wd_ref, out_ref):
    used = pl.program_id(0) < nused_ref[0]

    @pl.when(used)
    def _():
        xb = x_ref[...].astype(BF16)
        gate = _dot(xb, wg_ref[0])
        up = _dot(xb, wu_ref[0])
        hid = gate / (1.0 + jnp.exp(-gate)) * up
        out_ref[...] = _dot(hid.astype(BF16), wd_ref[0])

    @pl.when(jnp.logical_not(used))
    def _():
        out_ref[...] = jnp.zeros(out_ref.shape, F32)


def _experts(x_pad, block_expert, n_used, w_gate, w_up, w_down):
    n_rows, d = x_pad.shape
    f = w_gate.shape[-1]
    rows = EXPERT_ROWS
    row_blk = lambda i, be, nu: (jnp.minimum(i, nu[0] - 1), 0)
    return pl.pallas_call(
        _experts_kernel,
        grid_spec=pltpu.PrefetchScalarGridSpec(
            num_scalar_prefetch=2,
            grid=(n_rows // rows,),
            in_specs=[
                pl.BlockSpec((rows, d), row_blk),
                pl.BlockSpec((1, d, f), lambda i, be, nu: (be[i], 0, 0)),
                pl.BlockSpec((1, d, f), lambda i, be, nu: (be[i], 0, 0)),
                pl.BlockSpec((1, f, d), lambda i, be, nu: (be[i], 0, 0)),
            ],
            out_specs=pl.BlockSpec((rows, d), lambda i, be, nu: (i, 0)),
        ),
        out_shape=jax.ShapeDtypeStruct((n_rows, d), F32),
        compiler_params=_params("arbitrary"),
        name="experts",
    )(block_expert, n_used, x_pad, w_gate.astype(BF16), w_up.astype(BF16), w_down.astype(BF16))


def _combine_kernel(pstart_ref, ids_ref, h_ref, meta_ref, opad_ref, y_ref, buf_ref, sem, *, tm):
    def copies(t):
        return [_row_copy(opad_ref, pstart_ref[ids_ref[0, k, t]] + ids_ref[0, 2 + k, t], buf_ref.at[k], t, sem)
                for k in range(2)]

    def start(t, carry):
        for cp in copies(t):
            cp.start()
        return carry

    def wait(t, carry):
        for cp in copies(t):
            cp.wait()
        return carry

    lax.fori_loop(0, tm, start, 0)
    lax.fori_loop(0, tm, wait, 0)
    meta = meta_ref[...]
    g0 = meta[:, _META_G0:_META_G0 + 1]
    g1 = meta[:, _META_G1:_META_G1 + 1]
    y_ref[...] = h_ref[...] + g0 * buf_ref[0] + g1 * buf_ref[1]


def _combine(h2, meta, ids, pstart, out_pad):
    t, d = h2.shape
    tm = TM_COMB
    return pl.pallas_call(
        functools.partial(_combine_kernel, tm=tm),
        grid_spec=pltpu.PrefetchScalarGridSpec(
            num_scalar_prefetch=1,
            grid=(t // tm,),
            in_specs=[
                pl.BlockSpec((1, 4, tm), lambda i, *_: (i, 0, 0), memory_space=pltpu.SMEM),
                pl.BlockSpec((tm, d), lambda i, *_: (i, 0)),
                pl.BlockSpec((tm, LANES), lambda i, *_: (i, 0)),
                pl.BlockSpec(memory_space=pl.ANY),
            ],
            out_specs=pl.BlockSpec((tm, d), lambda i, *_: (i, 0)),
            scratch_shapes=[pltpu.VMEM((2, tm, d), F32), pltpu.SemaphoreType.DMA],
        ),
        out_shape=jax.ShapeDtypeStruct((t, d), F32),
        compiler_params=_params("arbitrary"),
        name="combine",
    )(pstart, ids, h2, meta, out_pad)


def _tile_ids(meta, tm):
    t = meta.shape[0]
    cols = jnp.stack([meta[:, _META_E0], meta[:, _META_E1], meta[:, _META_P0], meta[:, _META_P1]], axis=0)
    return cols.astype(I32).reshape(4, t // tm, tm).transpose(1, 0, 2)


def kernel(x, mem, norm_mix, w_in, da_q_norm, da_k_norm, lambda_q1, lambda_k1, lambda_q2, lambda_k2,
           da_out_norm, gla_gate_w, gla_gate_b, gla_out_norm, w_o, norm_cross, norm_mem, w_cq, w_ckv,
           cross_q_norm, cross_k_norm, w_co, norm_ffn, w_group, b_group, w_expert, b_expert,
           w_e_gate, w_e_up, w_e_down):
    b, s, d = x.shape
    t = b * s
    h = x
    for l in range(norm_mix.shape[0]):
        assert l == 0, "lam_init is fixed for a single layer"
        qt, kda, vt, gq, gk, gv, gg, la = _in_proj(h, norm_mix[l], w_in[l], da_q_norm[l], da_k_norm[l],
                                                   gla_gate_w[l], gla_gate_b[l])
        da = _diff_attn(qt, kda, vt, lambda_q1[l], lambda_k1[l], lambda_q2[l], lambda_k2[l], da_out_norm[l])
        gla = _gla(gq, gk, la, gv, gg, gla_out_norm[l])
        k_mem, v_mem = _mem_kv(mem, norm_mem[l], w_ckv[l], cross_k_norm[l])
        h2, xn, meta, counts = _post(h, da, gla, w_o[l], norm_cross[l], w_cq[l], cross_q_norm[l], k_mem, v_mem,
                                     w_co[l], norm_ffn[l], w_group[l], b_group[l], w_expert[l], b_expert[l])

        rows = EXPERT_ROWS
        sizes = counts[0, :N_EXPERTS].astype(I32)
        padded = (sizes + rows - 1) // rows * rows
        pend = jnp.cumsum(padded)
        pstart = pend - padded
        n_rows = 2 * t + N_EXPERTS * rows
        n_blocks = n_rows // rows
        block_expert = jnp.minimum(
            jnp.searchsorted(pend, jnp.arange(n_blocks, dtype=I32) * rows, side="right"), N_EXPERTS - 1
        ).astype(I32)
        n_used = (pend[-1:] // rows).astype(I32)

        h2 = h2.reshape(t, d)
        meta = meta.reshape(t, LANES)
        x_pad = _dispatch(xn.reshape(t, d), _tile_ids(meta, TM_DISP), pstart, sizes, pend, n_rows)
        out_pad = _experts(x_pad, block_expert, n_used, w_e_gate[l], w_e_up[l], w_e_down[l])
        h = _combine(h2, meta, _tile_ids(meta, TM_COMB), pstart, out_pad).reshape(b, s, d)
    return h
```

```python
import functools
import math

import jax
import jax.numpy as jnp
import numpy as np
from jax import lax
from jax.experimental import pallas as pl
from jax.experimental.pallas import tpu as pltpu

F32 = jnp.float32
BF16 = jnp.bfloat16
I32 = jnp.int32
U32 = jnp.uint32
HI16 = np.uint32(0xFFFF0000)

EPS = 1e-6
CHUNK = 64

DA_HEADS = 4
DA_QK_DIM = 64
DA_V_DIM = 128
GLA_HEADS = 4
GLA_K_DIM = 64
GLA_V_DIM = 128
GLA_GATE_RANK = 16
GLA_TAU = 16.0
CROSS_HEADS = 4
N_GROUPS = 4
EXPERTS_PER_GROUP = 8
N_EXPERTS = N_GROUPS * EXPERTS_PER_GROUP
LAM_INIT = 0.8 - 0.6 * math.exp(-0.3 * 0)

LANES = 128
VMEM_LIMIT = 56 * 1024 * 1024

TM_PROJ = 512
TS_GLA = 1024
TM_POST = 512
TM_ROWS = 256
EXPERT_ROWS = 256

NEG_INF = float("-inf")


def _params(*sem):
    return pltpu.CompilerParams(dimension_semantics=sem, vmem_limit_bytes=VMEM_LIMIT)


def _rms(t, g):
    ms = jnp.mean(t * t, axis=-1, keepdims=True)
    return t * lax.rsqrt(ms + EPS) * g


def _dot(a, b):
    return jnp.dot(a, b, preferred_element_type=F32)


def _dot_nt(a, b):
    return lax.dot_general(a, b, (((1,), (1,)), ((), ())), preferred_element_type=F32)


def _dot_tn(a, b):
    return lax.dot_general(a, b, (((0,), (0,)), ((), ())), preferred_element_type=F32)


def _split_bf16(t):
    hi = t.astype(BF16)
    lo = (t - hi.astype(F32)).astype(BF16)
    return hi, lo


def _mem_kv_kernel(mem_ref, g_ref, w_ref, kn_ref, k_ref, v_ref, *, d, heads):
    mn = _rms(mem_ref[0], g_ref[...]).astype(BF16)
    kv = _dot(mn, w_ref[...])
    hd = d // heads
    scale = hd ** -0.5
    for h in range(heads):
        kh = _rms(kv[:, h * hd:(h + 1) * hd], kn_ref[...]) * scale
        k_ref[0, :, h * hd:(h + 1) * hd] = kh.astype(BF16)
    v_ref[0] = kv[:, d:].astype(BF16)


def _mem_kv(mem, norm_m, w_ckv, kn):
    b, m, d = mem.shape
    return pl.pallas_call(
        functools.partial(_mem_kv_kernel, d=d, heads=CROSS_HEADS),
        grid=(b,),
        in_specs=[
            pl.BlockSpec((1, m, d), lambda i: (i, 0, 0)),
            pl.BlockSpec((1, d), lambda i: (0, 0)),
            pl.BlockSpec((d, 2 * d), lambda i: (0, 0)),
            pl.BlockSpec((1, d // CROSS_HEADS), lambda i: (0, 0)),
        ],
        out_specs=[
            pl.BlockSpec((1, m, d), lambda i: (i, 0, 0)),
            pl.BlockSpec((1, m, d), lambda i: (i, 0, 0)),
        ],
        out_shape=[jax.ShapeDtypeStruct((b, m, d), BF16)] * 2,
        compiler_params=_params("parallel"),
        name="mem_kv",
    )(mem, norm_m.reshape(1, d), w_ckv.astype(BF16), kn.reshape(1, -1))


_QK = DA_HEADS * 2 * DA_QK_DIM
_DAW = DA_HEADS * DA_V_DIM
_GQK = GLA_HEADS * GLA_K_DIM
_GW = GLA_HEADS * GLA_V_DIM
_OFF_DQ = 0
_OFF_DK = _OFF_DQ + _QK
_OFF_DV = _OFF_DK + _QK
_OFF_GQ = _OFF_DV + _DAW
_OFF_GK = _OFF_GQ + _GQK
_OFF_GV = _OFF_GK + _GQK
_OFF_GG = _OFF_GV + _GW
_OFF_GR = _OFF_GG + _GW
_IN_PAD = _OFF_GR + LANES


def _in_proj_kernel(x_ref, g_ref, w_ref, qg_ref, kg_ref, grp_ref, gw_ref, gb_ref,
                    qt_ref, k_ref, vt_ref, gq_ref, gk_ref, gv_ref, gg_ref, la_ref):
    u = _rms(x_ref[0], g_ref[...]).astype(BF16)

    def proj(off, width):
        return _dot(u, w_ref[:, off:off + width])

    def group_norm(p, gain):
        ms = _dot((p * p).astype(BF16), grp_ref[...])
        return p * lax.rsqrt(ms + EPS) * gain

    qn = group_norm(proj(_OFF_DQ, _QK), qg_ref[...]) * (DA_QK_DIM ** -0.5 * math.log2(math.e))
    qt_ref[0, 0] = qn.T.astype(BF16)
    k_ref[0] = group_norm(proj(_OFF_DK, _QK), kg_ref[...]).astype(BF16)
    vt_ref[0, 0] = proj(_OFF_DV, _DAW).T.astype(BF16)
    gq_ref[0] = proj(_OFF_GQ, _GQK) * (GLA_K_DIM ** -0.5)
    gk_ref[0] = proj(_OFF_GK, _GQK)
    gv_ref[0] = proj(_OFF_GV, _GW).astype(BF16)
    gg_ref[0] = proj(_OFF_GG, _GW)
    z = _dot(proj(_OFF_GR, LANES).astype(BF16), gw_ref[...]) + gb_ref[...]
    log_sig = jnp.minimum(z, 0.0) - jnp.log(1.0 + jnp.exp(-jnp.abs(z)))
    la_ref[0] = log_sig * (1.0 / GLA_TAU)


def _in_proj(x, norm_g, w_in, da_qn, da_kn, gate_w, gate_b):
    b, s, d = x.shape
    tm = TM_PROJ
    ns = s // tm
    w = jnp.pad(w_in, ((0, 0), (0, _IN_PAD - w_in.shape[1]))).astype(BF16)
    gw = jnp.pad(gate_w, ((0, LANES - GLA_GATE_RANK), (0, 0))).astype(BF16)
    lane = jnp.arange(_QK)
    grp = jnp.where((lane[:, None] // DA_QK_DIM) == (lane[None, :] // DA_QK_DIM),
                    1.0 / DA_QK_DIM, 0.0).astype(BF16)
    const = lambda shape: pl.BlockSpec(shape, lambda i, j: (0,) * len(shape))
    tile = lambda width: pl.BlockSpec((1, tm, width), lambda i, j: (i, j, 0))
    tile_t = lambda width: pl.BlockSpec((1, 1, width, tm), lambda i, j: (i, j, 0, 0))
    return pl.pallas_call(
        _in_proj_kernel,
        grid=(b, ns),
        in_specs=[tile(d), const((1, d)), const((d, _IN_PAD)), const((1, _QK)), const((1, _QK)),
                  const((_QK, _QK)), const((LANES, _GQK)), const((1, _GQK))],
        out_specs=[tile_t(_QK), tile(_QK), tile_t(_DAW), tile(_GQK), tile(_GQK), tile(_GW), tile(_GW),
                   tile(_GQK)],
        out_shape=[
            jax.ShapeDtypeStruct((b, ns, _QK, tm), BF16),
            jax.ShapeDtypeStruct((b, s, _QK), BF16),
            jax.ShapeDtypeStruct((b, ns, _DAW, tm), BF16),
            jax.ShapeDtypeStruct((b, s, _GQK), F32),
            jax.ShapeDtypeStruct((b, s, _GQK), F32),
            jax.ShapeDtypeStruct((b, s, _GW), BF16),
            jax.ShapeDtypeStruct((b, s, _GW), F32),
            jax.ShapeDtypeStruct((b, s, _GQK), F32),
        ],
        compiler_params=_params("parallel", "parallel"),
        name="in_proj",
    )(x, norm_g.reshape(1, d), w, jnp.tile(da_qn, 2 * DA_HEADS).reshape(1, _QK),
      jnp.tile(da_kn, 2 * DA_HEADS).reshape(1, _QK), grp, gw, gate_b.reshape(1, _GQK))


def _diff_attn_kernel(lq1_ref, lk1_ref, lq2_ref, lk2_ref, gain_ref, qt_ref, k_ref, vt_ref, out_ref,
                      m1_ref, l1_ref, a1_ref, m2_ref, l2_ref, a2_ref, *, blk):
    qi = pl.program_id(2)
    qt = qt_ref[0, 0]
    row = lax.broadcasted_iota(I32, qt.shape, 0)
    zero = jnp.zeros_like(qt)
    q1 = jnp.where(row < DA_QK_DIM, qt, zero)
    q2 = jnp.where(row >= DA_QK_DIM, qt, zero)

    for m_ref, l_ref, a_ref in ((m1_ref, l1_ref, a1_ref), (m2_ref, l2_ref, a2_ref)):
        m_ref[...] = jnp.full(m_ref.shape, NEG_INF, F32)
        l_ref[...] = jnp.zeros(l_ref.shape, F32)
        a_ref[...] = jnp.zeros(a_ref.shape, F32)

    def update(s, vb, m_ref, l_ref, a_ref):
        m_old = m_ref[...]
        m_new = jnp.maximum(m_old, jnp.max(s, axis=0, keepdims=True))
        alpha = jnp.exp2(m_old - m_new)
        p = jnp.exp2(s - m_new)
        l_ref[...] = alpha * l_ref[...] + jnp.sum(p, axis=0, keepdims=True)
        a_ref[...] = alpha * a_ref[...] + _dot(vb, p.astype(BF16))
        m_ref[...] = m_new

    def block(j, mask):
        kb = k_ref[0, pl.ds(pl.multiple_of(j * blk, blk), blk), :]
        vb = vt_ref[0, j]
        s1 = _dot(kb, q1)
        s2 = _dot(kb, q2)
        if mask is not None:
            s1 = jnp.where(mask, s1, NEG_INF)
            s2 = jnp.where(mask, s2, NEG_INF)
        update(s1, vb, m1_ref, l1_ref, a1_ref)
        update(s2, vb, m2_ref, l2_ref, a2_ref)

    def body(j, carry):
        block(j, None)
        return carry

    lax.fori_loop(0, qi, body, 0)
    key_chunk = lax.broadcasted_iota(I32, (blk, blk), 0) // CHUNK
    qry_chunk = lax.broadcasted_iota(I32, (blk, blk), 1) // CHUNK
    block(qi, key_chunk <= qry_chunk)

    lam = (jnp.exp(jnp.sum(lq1_ref[...] * lk1_ref[...], axis=-1, keepdims=True))
           - jnp.exp(jnp.sum(lq2_ref[...] * lk2_ref[...], axis=-1, keepdims=True)) + LAM_INIT)
    o = a1_ref[...] / l1_ref[...] - lam * (a2_ref[...] / l2_ref[...])
    ms = jnp.mean(o * o, axis=0, keepdims=True)
    o = o * lax.rsqrt(ms + EPS) * gain_ref[...] * (1.0 - LAM_INIT)
    out_ref[0] = o.T.astype(BF16)


def _diff_attn(qt, k, vt, lq1, lk1, lq2, lk2, da_on):
    b, nb, _, blk = qt.shape
    s = nb * blk
    vec = lambda: pl.BlockSpec((1, DA_QK_DIM), lambda i, h, q: (0, 0))
    return pl.pallas_call(
        functools.partial(_diff_attn_kernel, blk=blk),
        grid=(b, DA_HEADS, nb),
        in_specs=[
            vec(), vec(), vec(), vec(),
            pl.BlockSpec((DA_V_DIM, 1), lambda i, h, q: (0, 0)),
            pl.BlockSpec((1, 1, 2 * DA_QK_DIM, blk), lambda i, h, q: (i, q, h, 0)),
            pl.BlockSpec((1, s, 2 * DA_QK_DIM), lambda i, h, q: (i, 0, h)),
            pl.BlockSpec((1, nb, DA_V_DIM, blk), lambda i, h, q: (i, 0, h, 0)),
        ],
        out_specs=pl.BlockSpec((1, blk, DA_V_DIM), lambda i, h, q: (i, q, h)),
        out_shape=jax.ShapeDtypeStruct((b, s, _DAW), BF16),
        scratch_shapes=[
            pltpu.VMEM((1, blk), F32), pltpu.VMEM((1, blk), F32), pltpu.VMEM((DA_V_DIM, blk), F32),
            pltpu.VMEM((1, blk), F32), pltpu.VMEM((1, blk), F32), pltpu.VMEM((DA_V_DIM, blk), F32),
        ],
        compiler_params=_params("parallel", "parallel", "parallel"),
        name="diff_attn",
    )(lq1.reshape(1, -1), lk1.reshape(1, -1), lq2.reshape(1, -1), lk2.reshape(1, -1),
      da_on.reshape(-1, 1), qt, k, vt)


def _gla_kernel(q_ref, k_ref, la_ref, v_ref, g_ref, gain_ref, out_ref, st_ref, *, ts):
    @pl.when(pl.program_id(1) == 0)
    def _():
        st_ref[...] = jnp.zeros(st_ref.shape, F32)

    c = CHUNK
    hk, hv = _GQK, _GW
    r = lax.broadcasted_iota(I32, (c, c), 0)
    cc = lax.broadcasted_iota(I32, (c, c), 1)
    tri = jnp.where(r >= cc, 1.0, 0.0).astype(BF16)
    bd_k = (lax.broadcasted_iota(I32, (hk, hk), 0) // GLA_K_DIM
            == lax.broadcasted_iota(I32, (hk, hk), 1) // GLA_K_DIM)
    bd_v = (lax.broadcasted_iota(I32, (hk, hv), 0) // GLA_K_DIM
            == lax.broadcasted_iota(I32, (hk, hv), 1) // GLA_V_DIM)
    bd_vt = (lax.broadcasted_iota(I32, (hv, hk), 0) // GLA_V_DIM
             == lax.broadcasted_iota(I32, (hv, hk), 1) // GLA_K_DIM)
    lower = (lax.broadcasted_iota(I32, (c, hk), 0)
             >= lax.broadcasted_iota(I32, (c, hk), 1) % c)

    def chunk(ci, carry):
        sl = pl.ds(pl.multiple_of(ci * c, c), c)
        la_hi, la_lo = _split_bf16(la_ref[0, sl, :])
        big_l = _dot(tri, la_hi) + _dot(tri, la_lo)
        l_end = big_l[c - 1:c, :]
        lc = big_l - big_l[c // 2 - 1:c // 2, :]
        e_pos = jnp.exp(lc)
        e_neg = jnp.exp(-lc)
        q = q_ref[0, sl, :]
        k = k_ref[0, sl, :]
        v = v_ref[0, sl, :]

        def tiled(t, mask):
            t4 = jnp.concatenate([t] * GLA_HEADS, axis=0)
            return jnp.where(mask, t4, jnp.zeros_like(t4))

        a_past = _dot_nt((q * e_pos).astype(BF16), tiled((k * e_neg).astype(BF16), bd_k))
        a_fut = _dot_nt((q * e_neg).astype(BF16), tiled((k * e_pos).astype(BF16), bd_k))
        a = jnp.where(lower, a_past, a_fut).astype(BF16)
        o = _dot(a, tiled(v, bd_v))
        st = st_ref[...]
        o = o + _dot_nt((q * jnp.exp(big_l)).astype(BF16), st.astype(BF16))
        u_t = _dot_tn(v, (k * jnp.exp(l_end - big_l)).astype(BF16))
        st_ref[...] = st * jnp.exp(l_end) + jnp.where(bd_vt, u_t, 0.0)

        g = g_ref[0, sl, :]
        silu = g / (1.0 + jnp.exp(-g))
        for h in range(GLA_HEADS):
            hs = slice(h * GLA_V_DIM, (h + 1) * GLA_V_DIM)
            out_ref[0, sl, hs] = (_rms(o[:, hs], gain_ref[...]) * silu[:, hs]).astype(BF16)
        return carry

    lax.fori_loop(0, ts // c, chunk, 0, unroll=4)


def _gla(gq, gk, la, gv, gg, gla_on):
    b, s, _ = gq.shape
    ts = TS_GLA
    tile = lambda width: pl.BlockSpec((1, ts, width), lambda i, j: (i, j, 0))
    return pl.pallas_call(
        functools.partial(_gla_kernel, ts=ts),
        grid=(b, s // ts),
        in_specs=[tile(_GQK), tile(_GQK), tile(_GQK), tile(_GW), tile(_GW),
                  pl.BlockSpec((1, GLA_V_DIM), lambda i, j: (0, 0))],
        out_specs=tile(_GW),
        out_shape=jax.ShapeDtypeStruct((b, s, _GW), BF16),
        scratch_shapes=[pltpu.VMEM((_GW, _GQK), F32)],
        compiler_params=_params("parallel", "arbitrary"),
        name="gla",
    )(gq, gk, la, gv, gg, gla_on.reshape(1, -1))


_META_E0, _META_E1, _META_G0, _META_G1, _META_P0, _META_P1 = range(6)
_EXP_LANE0 = N_GROUPS


def _post_kernel(x_ref, da_ref, gla_ref, wo_ref, gc_ref, wq_ref, qn_ref, km_ref, vm_ref, wco_ref,
                 gf_ref, wr_ref, br_ref,
                 h_ref, xn_ref, meta_ref, cnt_ref, *, d, tm):
    first = jnp.logical_and(pl.program_id(0) == 0, pl.program_id(1) == 0)

    @pl.when(first)
    def _():
        cnt_ref[...] = jnp.zeros(cnt_ref.shape, F32)

    half = d // 2
    h1 = x_ref[0] + _dot(da_ref[0], wo_ref[:half, :]) + _dot(gla_ref[0], wo_ref[half:, :])

    u = _rms(h1, gc_ref[...]).astype(BF16)
    q = _dot(u, wq_ref[...])
    hd = d // CROSS_HEADS
    heads = []
    for h in range(CROSS_HEADS):
        hs = slice(h * hd, (h + 1) * hd)
        qh = _rms(q[:, hs], qn_ref[...]).astype(BF16)
        sc = _dot_nt(qh, km_ref[0, :, hs])
        sc = sc - jnp.max(sc, axis=-1, keepdims=True)
        p = jnp.exp(sc)
        p = p / jnp.sum(p, axis=-1, keepdims=True)
        heads.append(_dot(p.astype(BF16), vm_ref[0, :, hs]))
    o = jnp.concatenate(heads, axis=-1).astype(BF16)
    h2 = h1 + _dot(o, wco_ref[...])
    h_ref[0] = h2

    xn = _rms(h2, gf_ref[...]).astype(BF16)
    bits = lax.bitcast_convert_type(xn.astype(F32), U32)
    xn_ref[0] = (bits[:, :half] >> 16) | (bits[:, half:] & HI16)
    logits = _dot(xn, wr_ref[...]) + br_ref[...]
    lane = lax.broadcasted_iota(I32, logits.shape, 1)
    big = jnp.int32(LANES)

    def lane_argmax(vals):
        m = jnp.max(vals, axis=-1, keepdims=True)
        idx = jnp.min(jnp.where(vals == m, lane, big), axis=-1, keepdims=True)
        return m, idx

    lg = jnp.where(lane < N_GROUPS, logits, NEG_INF)
    g_max, g_sel = lane_argmax(lg)
    p_g = 1.0 / jnp.sum(jnp.exp(lg - g_max), axis=-1, keepdims=True)
    e_lo = _EXP_LANE0 + g_sel * EXPERTS_PER_GROUP
    in_group = jnp.logical_and(lane >= e_lo, lane < e_lo + EXPERTS_PER_GROUP)
    le = jnp.where(in_group, logits, NEG_INF)
    m1, i1 = lane_argmax(le)
    m2, i2 = lane_argmax(jnp.where(lane == i1, NEG_INF, le))
    e2 = jnp.exp(m2 - m1)
    gate0 = p_g / (1.0 + e2)
    gate1 = p_g * e2 / (1.0 + e2)
    e0 = i1 - _EXP_LANE0
    e1 = i2 - _EXP_LANE0

    hot0 = lane == e0
    hot1 = lane == e1
    onehot = jnp.where(jnp.logical_or(hot0, hot1), 1.0, 0.0)
    r = lax.broadcasted_iota(I32, (tm, tm), 0)
    c = lax.broadcasted_iota(I32, (tm, tm), 1)
    strict_lower = jnp.where(r > c, 1.0, 0.0).astype(BF16)
    base = cnt_ref[0:1, :]
    before = _dot(strict_lower, onehot.astype(BF16)) + base
    pos0 = jnp.sum(jnp.where(hot0, before, 0.0), axis=-1, keepdims=True)
    pos1 = jnp.sum(jnp.where(hot1, before, 0.0), axis=-1, keepdims=True)
    cnt_ref[...] = jnp.broadcast_to(base + jnp.sum(onehot, axis=0, keepdims=True), cnt_ref.shape)

    meta = jnp.zeros(logits.shape, F32)
    for idx, val in ((_META_E0, e0.astype(F32)), (_META_E1, e1.astype(F32)), (_META_G0, gate0),
                     (_META_G1, gate1), (_META_P0, pos0), (_META_P1, pos1)):
        meta = jnp.where(lane == idx, val, meta)
    meta_ref[0] = meta


def _post(x, da, gla, w_o, norm_cross, w_cq, cross_qn, k_mem, v_mem, w_co, norm_ffn, w_group, b_group,
          w_expert, b_expert):
    b, s, d = x.shape
    tm = TM_POST
    m = k_mem.shape[1]
    w_r = jnp.pad(jnp.concatenate([w_group, w_expert], axis=1), ((0, 0), (0, LANES - N_GROUPS - N_EXPERTS)))
    b_r = jnp.pad(jnp.concatenate([b_group, b_expert]), (0, LANES - N_GROUPS - N_EXPERTS)).reshape(1, LANES)
    const = lambda shape: pl.BlockSpec(shape, lambda i, j: (0,) * len(shape))
    tile = lambda width: pl.BlockSpec((1, tm, width), lambda i, j: (i, j, 0))
    per_b = lambda: pl.BlockSpec((1, m, d), lambda i, j: (i, 0, 0))
    return pl.pallas_call(
        functools.partial(_post_kernel, d=d, tm=tm),
        grid=(b, s // tm),
        in_specs=[tile(d), tile(d // 2), tile(d // 2), const((d, d)), const((1, d)), const((d, d)),
                  const((1, d // CROSS_HEADS)), per_b(), per_b(), const((d, d)), const((1, d)),
                  const((d, LANES)), const((1, LANES))],
        out_specs=[tile(d), tile(d // 2), tile(LANES), const((8, LANES))],
        out_shape=[
            jax.ShapeDtypeStruct((b, s, d), F32),
            jax.ShapeDtypeStruct((b, s, d // 2), U32),
            jax.ShapeDtypeStruct((b, s, LANES), F32),
            jax.ShapeDtypeStruct((8, LANES), F32),
        ],
        compiler_params=_params("arbitrary", "arbitrary"),
        name="post",
    )(x, da, gla, w_o.astype(BF16), norm_cross.reshape(1, d), w_cq.astype(BF16), cross_qn.reshape(1, -1),
      k_mem, v_mem, w_co.astype(BF16), norm_ffn.reshape(1, d), w_r.astype(BF16), b_r)


def _row_copy(src_ref, src_row, dst_ref, dst_row, sem):
    return pltpu.make_async_copy(src_ref.at[pl.ds(src_row, 1)], dst_ref.at[pl.ds(dst_row, 1)], sem)


def _dispatch_kernel(size_ref, pend_ref, dest_ref, xn_ref, xpad_ref, zero_ref, sem, zsem, *, tm):
    i = pl.program_id(0)

    @pl.when(i == 0)
    def _():
        zero_ref[...] = jnp.zeros(zero_ref.shape, zero_ref.dtype)
        rows = zero_ref.shape[0]
        n_blocks = xpad_ref.shape[0] // rows
        n_used = pend_ref[N_EXPERTS - 1] // rows

        def zero_block(blk):
            return pltpu.make_async_copy(zero_ref, xpad_ref.at[pl.ds(pl.multiple_of(blk * rows, rows), rows)], zsem)

        def last_block(e, fn):
            @pl.when(size_ref[e] > 0)
            def _():
                fn(zero_block(pend_ref[e] // rows - 1))

        for fn in (lambda cp: cp.start(), lambda cp: cp.wait()):
            lax.fori_loop(0, N_EXPERTS, lambda e, c: (last_block(e, fn), c)[1], 0)
            lax.fori_loop(n_used, n_blocks, lambda blk, c: (fn(zero_block(blk)), c)[1], 0)

    def start(t, carry):
        for k in range(2):
            _row_copy(xn_ref, t, xpad_ref, dest_ref[0, k, t], sem).start()
        return carry

    lax.fori_loop(0, tm, start, 0, unroll=8)
    for _ in range(2):
        pltpu.make_async_copy(xn_ref, xpad_ref.at[pl.ds(0, tm)], sem).wait()


def _dispatch(xn, dest, sizes, pend, n_rows):
    t, w = xn.shape
    tm = TM_ROWS
    return pl.pallas_call(
        functools.partial(_dispatch_kernel, tm=tm),
        grid_spec=pltpu.PrefetchScalarGridSpec(
            num_scalar_prefetch=2,
            grid=(t // tm,),
            in_specs=[
                pl.BlockSpec((1, 2, tm), lambda i, *_: (i, 0, 0), memory_space=pltpu.SMEM),
                pl.BlockSpec((tm, w), lambda i, *_: (i, 0)),
            ],
            out_specs=pl.BlockSpec(memory_space=pl.ANY),
            scratch_shapes=[pltpu.VMEM((EXPERT_ROWS, w), xn.dtype), pltpu.SemaphoreType.DMA,
                            pltpu.SemaphoreType.DMA],
        ),
        out_shape=jax.ShapeDtypeStruct((n_rows, w), xn.dtype),
        compiler_params=_params("arbitrary"),
        name="dispatch",
    )(sizes, pend, dest, xn)


def _experts_kernel(be_ref, nused_ref, x_ref, wg_ref, wu_ref, wd_ref, out_ref):
    used = pl.program_id(0) < nused_ref[0]

    @pl.when(used)
    def _():
        words = x_ref[...]
        half = words.shape[1]
        lo = lax.bitcast_convert_type(words << 16, F32).astype(BF16)
        hi = lax.bitcast_convert_type(words & HI16, F32).astype(BF16)
        gate = _dot(lo, wg_ref[0, :half, :]) + _dot(hi, wg_ref[0, half:, :])
        up = _dot(lo, wu_ref[0, :half, :]) + _dot(hi, wu_ref[0, half:, :])
        hid = gate / (1.0 + jnp.exp(-gate)) * up
        out_ref[...] = _dot(hid.astype(BF16), wd_ref[0])

    @pl.when(jnp.logical_not(used))
    def _():
        out_ref[...] = jnp.zeros(out_ref.shape, F32)


def _experts(x_pad, block_expert, n_used, w_gate, w_up, w_down):
    n_rows = x_pad.shape[0]
    _, d, f = w_gate.shape
    rows = EXPERT_ROWS
    row_blk = lambda i, be, nu: (jnp.minimum(i, nu[0] - 1), 0)
    return pl.pallas_call(
        _experts_kernel,
        grid_spec=pltpu.PrefetchScalarGridSpec(
            num_scalar_prefetch=2,
            grid=(n_rows // rows,),
            in_specs=[
                pl.BlockSpec((rows, d // 2), row_blk),
                pl.BlockSpec((1, d, f), lambda i, be, nu: (be[i], 0, 0)),
                pl.BlockSpec((1, d, f), lambda i, be, nu: (be[i], 0, 0)),
                pl.BlockSpec((1, f, d), lambda i, be, nu: (be[i], 0, 0)),
            ],
            out_specs=pl.BlockSpec((rows, d), lambda i, be, nu: (i, 0)),
        ),
        out_shape=jax.ShapeDtypeStruct((n_rows, d), F32),
        compiler_params=_params("arbitrary"),
        name="experts",
    )(block_expert, n_used, x_pad, w_gate.astype(BF16), w_up.astype(BF16), w_down.astype(BF16))


def _combine_kernel(dest_ref, h_ref, meta_ref, opad_ref, y_ref, buf_ref, sem, *, tm):
    def start(t, carry):
        for k in range(2):
            _row_copy(opad_ref, dest_ref[0, k, t], buf_ref.at[k], t, sem).start()
        return carry

    lax.fori_loop(0, tm, start, 0, unroll=8)
    for k in range(2):
        pltpu.make_async_copy(opad_ref.at[pl.ds(0, tm)], buf_ref.at[k], sem).wait()
    meta = meta_ref[...]
    g0 = meta[:, _META_G0:_META_G0 + 1]
    g1 = meta[:, _META_G1:_META_G1 + 1]
    y_ref[...] = h_ref[...] + g0 * buf_ref[0] + g1 * buf_ref[1]


def _combine(h2, meta, dest, out_pad):
    t, d = h2.shape
    tm = TM_ROWS
    return pl.pallas_call(
        functools.partial(_combine_kernel, tm=tm),
        grid=(t // tm,),
        in_specs=[
            pl.BlockSpec((1, 2, tm), lambda i: (i, 0, 0), memory_space=pltpu.SMEM),
            pl.BlockSpec((tm, d), lambda i: (i, 0)),
            pl.BlockSpec((tm, LANES), lambda i: (i, 0)),
            pl.BlockSpec(memory_space=pl.ANY),
        ],
        out_specs=pl.BlockSpec((tm, d), lambda i: (i, 0)),
        scratch_shapes=[pltpu.VMEM((2, tm, d), F32), pltpu.SemaphoreType.DMA],
        out_shape=jax.ShapeDtypeStruct((t, d), F32),
        compiler_params=_params("arbitrary"),
        name="combine",
    )(dest, h2, meta, out_pad)


def _tile_dest(meta, pstart, tm):
    t = meta.shape[0]
    expert = meta[:, _META_E0:_META_E1 + 1].astype(I32)
    rank = meta[:, _META_P0:_META_P1 + 1].astype(I32)
    start = jnp.sum(jnp.where(expert[..., None] == jnp.arange(N_EXPERTS), pstart, 0), axis=-1)
    return (start + rank).reshape(t // tm, tm, 2).transpose(0, 2, 1)


def kernel(x, mem, norm_mix, w_in, da_q_norm, da_k_norm, lambda_q1, lambda_k1, lambda_q2, lambda_k2,
           da_out_norm, gla_gate_w, gla_gate_b, gla_out_norm, w_o, norm_cross, norm_mem, w_cq, w_ckv,
           cross_q_norm, cross_k_norm, w_co, norm_ffn, w_group, b_group, w_expert, b_expert,
           w_e_gate, w_e_up, w_e_down):
    b, s, d = x.shape
    t = b * s
    h = x
    for l in range(norm_mix.shape[0]):
        assert l == 0, "lam_init is fixed for a single layer"
        qt, kda, vt, gq, gk, gv, gg, la = _in_proj(h, norm_mix[l], w_in[l], da_q_norm[l], da_k_norm[l],
                                                   gla_gate_w[l], gla_gate_b[l])
        da = _diff_attn(qt, kda, vt, lambda_q1[l], lambda_k1[l], lambda_q2[l], lambda_k2[l], da_out_norm[l])
        gla = _gla(gq, gk, la, gv, gg, gla_out_norm[l])
        k_mem, v_mem = _mem_kv(mem, norm_mem[l], w_ckv[l], cross_k_norm[l])
        h2, xn, meta, counts = _post(h, da, gla, w_o[l], norm_cross[l], w_cq[l], cross_q_norm[l], k_mem, v_mem,
                                     w_co[l], norm_ffn[l], w_group[l], b_group[l], w_expert[l], b_expert[l])

        rows = EXPERT_ROWS
        sizes = counts[0, :N_EXPERTS].astype(I32)
        padded = (sizes + rows - 1) // rows * rows
        pend = jnp.cumsum(padded)
        pstart = pend - padded
        n_rows = 2 * t + N_EXPERTS * rows
        n_blocks = n_rows // rows
        block_start = jnp.arange(n_blocks, dtype=I32) * rows
        block_expert = jnp.minimum(jnp.sum(pend[None, :] <= block_start[:, None], axis=1), N_EXPERTS - 1).astype(I32)
        n_used = (pend[-1:] // rows).astype(I32)

        h2 = h2.reshape(t, d)
        meta = meta.reshape(t, LANES)
        dest = _tile_dest(meta, pstart, TM_ROWS)
        x_pad = _dispatch(xn.reshape(t, d // 2), dest, sizes, pend, n_rows)
        out_pad = _experts(x_pad, block_expert, n_used, w_e_gate[l], w_e_up[l], w_e_down[l])
        h = _combine(h2, meta, dest, out_pad).reshape(b, s, d)
    return h
```

```python
import functools
import math

import jax
import jax.numpy as jnp
import numpy as np
from jax import lax
from jax.experimental import pallas as pl
from jax.experimental.pallas import tpu as pltpu

F32 = jnp.float32
BF16 = jnp.bfloat16
I32 = jnp.int32
U32 = jnp.uint32
HI16 = np.uint32(0xFFFF0000)

EPS = 1e-6
CHUNK = 64

DA_HEADS = 4
DA_QK_DIM = 64
DA_V_DIM = 128
GLA_HEADS = 4
GLA_K_DIM = 64
GLA_V_DIM = 128
GLA_GATE_RANK = 16
GLA_TAU = 16.0
CROSS_HEADS = 4
N_GROUPS = 4
EXPERTS_PER_GROUP = 8
N_EXPERTS = N_GROUPS * EXPERTS_PER_GROUP
LAM_INIT = 0.8 - 0.6 * math.exp(-0.3 * 0)

LANES = 128
VMEM_LIMIT = 56 * 1024 * 1024

TM_PROJ = 512
TS_GLA = 1024
TM_POST = 512
SUB_POST = 512
TM_ROWS = 512
EXPERT_ROWS = 512

NEG_INF = float("-inf")


def _params(*sem):
    return pltpu.CompilerParams(dimension_semantics=sem, vmem_limit_bytes=VMEM_LIMIT)


def _rms(t, g):
    ms = jnp.mean(t * t, axis=-1, keepdims=True)
    return t * lax.rsqrt(ms + EPS) * g


def _dot(a, b):
    return jnp.dot(a, b, preferred_element_type=F32)


def _dot_nt(a, b):
    return lax.dot_general(a, b, (((1,), (1,)), ((), ())), preferred_element_type=F32)


def _dot_tn(a, b):
    return lax.dot_general(a, b, (((0,), (0,)), ((), ())), preferred_element_type=F32)


def _split_bf16(t):
    hi = t.astype(BF16)
    lo = (t - hi.astype(F32)).astype(BF16)
    return hi, lo


def _mem_kv_kernel(mem_ref, g_ref, w_ref, kn_ref, k_ref, v_ref, *, d, heads):
    mn = _rms(mem_ref[0], g_ref[...]).astype(BF16)
    kv = _dot(mn, w_ref[...])
    hd = d // heads
    scale = hd ** -0.5
    for h in range(heads):
        kh = _rms(kv[:, h * hd:(h + 1) * hd], kn_ref[...]) * scale
        k_ref[0, :, h * hd:(h + 1) * hd] = kh.astype(BF16)
    v_ref[0] = kv[:, d:].astype(BF16)


def _mem_kv(mem, norm_m, w_ckv, kn):
    b, m, d = mem.shape
    return pl.pallas_call(
        functools.partial(_mem_kv_kernel, d=d, heads=CROSS_HEADS),
        grid=(b,),
        in_specs=[
            pl.BlockSpec((1, m, d), lambda i: (i, 0, 0)),
            pl.BlockSpec((1, d), lambda i: (0, 0)),
            pl.BlockSpec((d, 2 * d), lambda i: (0, 0)),
            pl.BlockSpec((1, d // CROSS_HEADS), lambda i: (0, 0)),
        ],
        out_specs=[
            pl.BlockSpec((1, m, d), lambda i: (i, 0, 0)),
            pl.BlockSpec((1, m, d), lambda i: (i, 0, 0)),
        ],
        out_shape=[jax.ShapeDtypeStruct((b, m, d), BF16)] * 2,
        compiler_params=_params("parallel"),
        name="mem_kv",
    )(mem, norm_m.reshape(1, d), w_ckv.astype(BF16), kn.reshape(1, -1))


_QK = DA_HEADS * 2 * DA_QK_DIM
_DAW = DA_HEADS * DA_V_DIM
_GQK = GLA_HEADS * GLA_K_DIM
_GW = GLA_HEADS * GLA_V_DIM
_OFF_DQ = 0
_OFF_DK = _OFF_DQ + _QK
_OFF_DV = _OFF_DK + _QK
_OFF_GQ = _OFF_DV + _DAW
_OFF_GK = _OFF_GQ + _GQK
_OFF_GV = _OFF_GK + _GQK
_OFF_GG = _OFF_GV + _GW
_OFF_GR = _OFF_GG + _GW
_IN_PAD = _OFF_GR + LANES


def _in_proj_kernel(x_ref, g_ref, w_ref, qg_ref, kg_ref, grp_ref, gw_ref, gb_ref,
                    qt_ref, k_ref, vt_ref, gq_ref, gk_ref, gv_ref, gg_ref, la_ref):
    u = _rms(x_ref[0], g_ref[...]).astype(BF16)

    def proj(off, width):
        return _dot(u, w_ref[:, off:off + width])

    def group_norm(p, gain):
        ms = _dot((p * p).astype(BF16), grp_ref[...])
        return p * lax.rsqrt(ms + EPS) * gain

    qn = group_norm(proj(_OFF_DQ, _QK), qg_ref[...]) * (DA_QK_DIM ** -0.5 * math.log2(math.e))
    qt_ref[0, 0] = qn.T.astype(BF16)
    k_ref[0] = group_norm(proj(_OFF_DK, _QK), kg_ref[...]).astype(BF16)
    vt_ref[0, 0] = proj(_OFF_DV, _DAW).T.astype(BF16)
    gq_ref[0] = proj(_OFF_GQ, _GQK) * (GLA_K_DIM ** -0.5)
    gk_ref[0] = proj(_OFF_GK, _GQK)
    gv_ref[0] = proj(_OFF_GV, _GW).astype(BF16)
    gg_ref[0] = proj(_OFF_GG, _GW)
    z = _dot(proj(_OFF_GR, LANES).astype(BF16), gw_ref[...]) + gb_ref[...]
    log_sig = jnp.minimum(z, 0.0) - jnp.log(1.0 + jnp.exp(-jnp.abs(z)))
    la_ref[0] = log_sig * (1.0 / GLA_TAU)


def _in_proj(x, norm_g, w_in, da_qn, da_kn, gate_w, gate_b):
    b, s, d = x.shape
    tm = TM_PROJ
    ns = s // tm
    w = jnp.pad(w_in, ((0, 0), (0, _IN_PAD - w_in.shape[1]))).astype(BF16)
    gw = jnp.pad(gate_w, ((0, LANES - GLA_GATE_RANK), (0, 0))).astype(BF16)
    lane = jnp.arange(_QK)
    grp = jnp.where((lane[:, None] // DA_QK_DIM) == (lane[None, :] // DA_QK_DIM),
                    1.0 / DA_QK_DIM, 0.0).astype(BF16)
    const = lambda shape: pl.BlockSpec(shape, lambda i, j: (0,) * len(shape))
    tile = lambda width: pl.BlockSpec((1, tm, width), lambda i, j: (i, j, 0))
    tile_t = lambda width: pl.BlockSpec((1, 1, width, tm), lambda i, j: (i, j, 0, 0))
    return pl.pallas_call(
        _in_proj_kernel,
        grid=(b, ns),
        in_specs=[tile(d), const((1, d)), const((d, _IN_PAD)), const((1, _QK)), const((1, _QK)),
                  const((_QK, _QK)), const((LANES, _GQK)), const((1, _GQK))],
        out_specs=[tile_t(_QK), tile(_QK), tile_t(_DAW), tile(_GQK), tile(_GQK), tile(_GW), tile(_GW),
                   tile(_GQK)],
        out_shape=[
            jax.ShapeDtypeStruct((b, ns, _QK, tm), BF16),
            jax.ShapeDtypeStruct((b, s, _QK), BF16),
            jax.ShapeDtypeStruct((b, ns, _DAW, tm), BF16),
            jax.ShapeDtypeStruct((b, s, _GQK), F32),
            jax.ShapeDtypeStruct((b, s, _GQK), F32),
            jax.ShapeDtypeStruct((b, s, _GW), BF16),
            jax.ShapeDtypeStruct((b, s, _GW), F32),
            jax.ShapeDtypeStruct((b, s, _GQK), F32),
        ],
        compiler_params=_params("parallel", "parallel"),
        name="in_proj",
    )(x, norm_g.reshape(1, d), w, jnp.tile(da_qn, 2 * DA_HEADS).reshape(1, _QK),
      jnp.tile(da_kn, 2 * DA_HEADS).reshape(1, _QK), grp, gw, gate_b.reshape(1, _GQK))


def _split_q(qt_ref):
    qt = qt_ref[0, 0]
    row = lax.broadcasted_iota(I32, qt.shape, 0)
    zero = jnp.zeros_like(qt)
    return jnp.where(row < DA_QK_DIM, qt, zero), jnp.where(row >= DA_QK_DIM, qt, zero)


def _chunk_causal_mask(blk):
    key_chunk = lax.broadcasted_iota(I32, (blk, blk), 0) // CHUNK
    qry_chunk = lax.broadcasted_iota(I32, (blk, blk), 1) // CHUNK
    return key_chunk <= qry_chunk


def _diff_attn_finish(lq1_ref, lk1_ref, lq2_ref, lk2_ref, gain_ref, out_ref, a1, l1, a2, l2):
    lam = (jnp.exp(jnp.sum(lq1_ref[...] * lk1_ref[...], axis=-1, keepdims=True))
           - jnp.exp(jnp.sum(lq2_ref[...] * lk2_ref[...], axis=-1, keepdims=True)) + LAM_INIT)
    o = a1 / l1 - lam * (a2 / l2)
    ms = jnp.mean(o * o, axis=0, keepdims=True)
    o = o * lax.rsqrt(ms + EPS) * gain_ref[...] * (1.0 - LAM_INIT)
    out_ref[0] = o.T.astype(BF16)


def _diff_attn_bounded_kernel(lq1_ref, lk1_ref, lq2_ref, lk2_ref, gain_ref, qt_ref, k_ref, vt_ref, out_ref,
                              s_ref, l1_ref, a1_ref, l2_ref, a2_ref, *, blk):
    qi = pl.program_id(2)
    q1, q2 = _split_q(qt_ref)
    stats = ((l1_ref, a1_ref), (l2_ref, a2_ref))
    for l_ref, a_ref in stats:
        l_ref[...] = jnp.zeros(l_ref.shape, F32)
        a_ref[...] = jnp.zeros(a_ref.shape, F32)

    def scores(j, slot):
        kb = k_ref[0, pl.ds(pl.multiple_of(j * blk, blk), blk), :]
        s_ref[slot, 0] = _dot(kb, q1)
        s_ref[slot, 1] = _dot(kb, q2)

    def consume(j, slot, mask):
        vb = vt_ref[0, j]
        for m, (l_ref, a_ref) in enumerate(stats):
            s = s_ref[slot, m]
            if mask is not None:
                s = jnp.where(mask, s, NEG_INF)
            p = jnp.exp2(s)
            l_ref[...] += jnp.sum(p, axis=0, keepdims=True)
            a_ref[...] += _dot(vb, p.astype(BF16))

    def step(j, cur):
        scores(j + 1, 1 - cur)
        consume(j, cur, None)

    def pair(i, carry):
        step(2 * i, 0)
        step(2 * i + 1, 1)
        return carry

    scores(0, 0)
    lax.fori_loop(0, qi // 2, pair, 0)
    mask = _chunk_causal_mask(blk)
    odd = qi % 2 == 1

    @pl.when(odd)
    def _():
        step(qi - 1, 0)
        consume(qi, 1, mask)

    @pl.when(jnp.logical_not(odd))
    def _():
        consume(qi, 0, mask)

    _diff_attn_finish(lq1_ref, lk1_ref, lq2_ref, lk2_ref, gain_ref, out_ref,
                      a1_ref[...], l1_ref[...], a2_ref[...], l2_ref[...])


def _diff_attn_online_kernel(lq1_ref, lk1_ref, lq2_ref, lk2_ref, gain_ref, qt_ref, k_ref, vt_ref, out_ref,
                             m1_ref, l1_ref, a1_ref, m2_ref, l2_ref, a2_ref, *, blk):
    qi = pl.program_id(2)
    q1, q2 = _split_q(qt_ref)

    for m_ref, l_ref, a_ref in ((m1_ref, l1_ref, a1_ref), (m2_ref, l2_ref, a2_ref)):
        m_ref[...] = jnp.full(m_ref.shape, NEG_INF, F32)
        l_ref[...] = jnp.zeros(l_ref.shape, F32)
        a_ref[...] = jnp.zeros(a_ref.shape, F32)

    def update(s, vb, m_ref, l_ref, a_ref):
        m_old = m_ref[...]
        m_new = jnp.maximum(m_old, jnp.max(s, axis=0, keepdims=True))
        alpha = jnp.exp2(m_old - m_new)
        p = jnp.exp2(s - m_new)
        l_ref[...] = alpha * l_ref[...] + jnp.sum(p, axis=0, keepdims=True)
        a_ref[...] = alpha * a_ref[...] + _dot(vb, p.astype(BF16))
        m_ref[...] = m_new

    def block(j, mask):
        kb = k_ref[0, pl.ds(pl.multiple_of(j * blk, blk), blk), :]
        vb = vt_ref[0, j]
        s1 = _dot(kb, q1)
        s2 = _dot(kb, q2)
        if mask is not None:
            s1 = jnp.where(mask, s1, NEG_INF)
            s2 = jnp.where(mask, s2, NEG_INF)
        update(s1, vb, m1_ref, l1_ref, a1_ref)
        update(s2, vb, m2_ref, l2_ref, a2_ref)

    def body(j, carry):
        block(j, None)
        return carry

    lax.fori_loop(0, qi, body, 0)
    block(qi, _chunk_causal_mask(blk))
    _diff_attn_finish(lq1_ref, lk1_ref, lq2_ref, lk2_ref, gain_ref, out_ref,
                      a1_ref[...], l1_ref[...], a2_ref[...], l2_ref[...])


SCORE_BOUND = 60.0


def _diff_attn(qt, k, vt, lq1, lk1, lq2, lk2, da_on, da_qn, da_kn):
    b, nb, _, blk = qt.shape
    s = nb * blk
    vec = lambda: pl.BlockSpec((1, DA_QK_DIM), lambda i, h, q: (0, 0))
    stat = lambda: pltpu.VMEM((1, blk), F32)
    acc = lambda: pltpu.VMEM((DA_V_DIM, blk), F32)

    def call(body, name, scratch):
        return pl.pallas_call(
            functools.partial(body, blk=blk),
            grid=(b, DA_HEADS, nb),
            in_specs=[
                vec(), vec(), vec(), vec(),
                pl.BlockSpec((DA_V_DIM, 1), lambda i, h, q: (0, 0)),
                pl.BlockSpec((1, 1, 2 * DA_QK_DIM, blk), lambda i, h, q: (i, q, h, 0)),
                pl.BlockSpec((1, s, 2 * DA_QK_DIM), lambda i, h, q: (i, 0, h)),
                pl.BlockSpec((1, nb, DA_V_DIM, blk), lambda i, h, q: (i, 0, h, 0)),
            ],
            out_specs=pl.BlockSpec((1, blk, DA_V_DIM), lambda i, h, q: (i, q, h)),
            out_shape=jax.ShapeDtypeStruct((b, s, _DAW), BF16),
            scratch_shapes=scratch,
            compiler_params=_params("parallel", "parallel", "parallel"),
            name=name,
        )

    args = (lq1.reshape(1, -1), lk1.reshape(1, -1), lq2.reshape(1, -1), lk2.reshape(1, -1),
            da_on.reshape(-1, 1), qt, k, vt)
    bound = (1.01 * DA_QK_DIM ** 0.5 * math.log2(math.e)) * jnp.max(jnp.abs(da_qn)) * jnp.max(jnp.abs(da_kn))
    bounded = call(_diff_attn_bounded_kernel, "diff_attn",
                   [pltpu.VMEM((2, 2, blk, blk), F32), stat(), acc(), stat(), acc()])
    online = call(_diff_attn_online_kernel, "diff_attn_online", [stat(), stat(), acc(), stat(), stat(), acc()])
    return lax.cond(bound <= SCORE_BOUND, bounded, online, *args)


def _gla_kernel(q_ref, k_ref, la_ref, v_ref, g_ref, gain_ref, out_ref, st_ref, *, ts):
    @pl.when(pl.program_id(1) == 0)
    def _():
        st_ref[...] = jnp.zeros(st_ref.shape, F32)

    c = CHUNK
    hk, hv = _GQK, _GW
    r = lax.broadcasted_iota(I32, (c, c), 0)
    cc = lax.broadcasted_iota(I32, (c, c), 1)
    tri = jnp.where(r >= cc, 1.0, 0.0).astype(BF16)
    bd_k = (lax.broadcasted_iota(I32, (hk, hk), 0) // GLA_K_DIM
            == lax.broadcasted_iota(I32, (hk, hk), 1) // GLA_K_DIM)
    bd_v = (lax.broadcasted_iota(I32, (hk, hv), 0) // GLA_K_DIM
            == lax.broadcasted_iota(I32, (hk, hv), 1) // GLA_V_DIM)
    bd_vt = (lax.broadcasted_iota(I32, (hv, hk), 0) // GLA_V_DIM
             == lax.broadcasted_iota(I32, (hv, hk), 1) // GLA_K_DIM)
    lower = (lax.broadcasted_iota(I32, (c, hk), 0)
             >= lax.broadcasted_iota(I32, (c, hk), 1) % c)

    def chunk(ci, carry):
        sl = pl.ds(pl.multiple_of(ci * c, c), c)
        la_hi, la_lo = _split_bf16(la_ref[0, sl, :])
        big_l = _dot(tri, la_hi) + _dot(tri, la_lo)
        l_end = big_l[c - 1:c, :]
        lc = big_l - big_l[c // 2 - 1:c // 2, :]
        e_pos = jnp.exp(lc)
        e_neg = jnp.exp(-lc)
        q = q_ref[0, sl, :]
        k = k_ref[0, sl, :]
        v = v_ref[0, sl, :]

        def tiled(t, mask):
            t4 = jnp.concatenate([t] * GLA_HEADS, axis=0)
            return jnp.where(mask, t4, jnp.zeros_like(t4))

        a_past = _dot_nt((q * e_pos).astype(BF16), tiled((k * e_neg).astype(BF16), bd_k))
        a_fut = _dot_nt((q * e_neg).astype(BF16), tiled((k * e_pos).astype(BF16), bd_k))
        a = jnp.where(lower, a_past, a_fut).astype(BF16)
        o = _dot(a, tiled(v, bd_v))
        st = st_ref[...]
        o = o + _dot_nt((q * jnp.exp(big_l)).astype(BF16), st.astype(BF16))
        u_t = _dot_tn(v, (k * jnp.exp(l_end - big_l)).astype(BF16))
        st_ref[...] = st * jnp.exp(l_end) + jnp.where(bd_vt, u_t, 0.0)

        g = g_ref[0, sl, :]
        silu = g / (1.0 + jnp.exp(-g))
        for h in range(GLA_HEADS):
            hs = slice(h * GLA_V_DIM, (h + 1) * GLA_V_DIM)
            out_ref[0, sl, hs] = (_rms(o[:, hs], gain_ref[...]) * silu[:, hs]).astype(BF16)
        return carry

    lax.fori_loop(0, ts // c, chunk, 0, unroll=4)


def _gla(gq, gk, la, gv, gg, gla_on):
    b, s, _ = gq.shape
    ts = TS_GLA
    tile = lambda width: pl.BlockSpec((1, ts, width), lambda i, j: (i, j, 0))
    return pl.pallas_call(
        functools.partial(_gla_kernel, ts=ts),
        grid=(b, s // ts),
        in_specs=[tile(_GQK), tile(_GQK), tile(_GQK), tile(_GW), tile(_GW),
                  pl.BlockSpec((1, GLA_V_DIM), lambda i, j: (0, 0))],
        out_specs=tile(_GW),
        out_shape=jax.ShapeDtypeStruct((b, s, _GW), BF16),
        scratch_shapes=[pltpu.VMEM((_GW, _GQK), F32)],
        compiler_params=_params("parallel", "arbitrary"),
        name="gla",
    )(gq, gk, la, gv, gg, gla_on.reshape(1, -1))


_META_E0, _META_E1, _META_G0, _META_G1, _META_P0, _META_P1 = range(6)
_EXP_LANE0 = N_GROUPS


def _post_kernel(x_ref, da_ref, gla_ref, wo_ref, gc_ref, wq_ref, qn_ref, km_ref, vm_ref, wco_ref,
                 gf_ref, wr_ref, br_ref,
                 h_ref, xn_ref, meta_ref, cnt_ref, *, d, tm, sub):
    first = jnp.logical_and(pl.program_id(0) == 0, pl.program_id(1) == 0)

    @pl.when(first)
    def _():
        cnt_ref[...] = jnp.zeros(cnt_ref.shape, F32)

    half = d // 2
    hd = d // CROSS_HEADS
    lane = lax.broadcasted_iota(I32, (sub, LANES), 1)
    big = jnp.int32(LANES)
    strict_lower = jnp.where(lax.broadcasted_iota(I32, (sub, sub), 0) > lax.broadcasted_iota(I32, (sub, sub), 1),
                             1.0, 0.0).astype(BF16)

    def lane_argmax(vals):
        m = jnp.max(vals, axis=-1, keepdims=True)
        idx = jnp.min(jnp.where(vals == m, lane, big), axis=-1, keepdims=True)
        return m, idx

    def rows(rs, base):
        h1 = x_ref[0, rs, :] + _dot(da_ref[0, rs, :], wo_ref[:half, :]) + _dot(gla_ref[0, rs, :], wo_ref[half:, :])

        u = _rms(h1, gc_ref[...]).astype(BF16)
        q = _dot(u, wq_ref[...])
        heads = []
        for h in range(CROSS_HEADS):
            hs = slice(h * hd, (h + 1) * hd)
            qh = _rms(q[:, hs], qn_ref[...]).astype(BF16)
            sc = _dot_nt(qh, km_ref[0, :, hs])
            sc = sc - jnp.max(sc, axis=-1, keepdims=True)
            p = jnp.exp(sc)
            p = p / jnp.sum(p, axis=-1, keepdims=True)
            heads.append(_dot(p.astype(BF16), vm_ref[0, :, hs]))
        o = jnp.concatenate(heads, axis=-1).astype(BF16)
        h2 = h1 + _dot(o, wco_ref[...])
        h_ref[0, rs, :] = h2

        xn = _rms(h2, gf_ref[...]).astype(BF16)
        bits = lax.bitcast_convert_type(xn.astype(F32), U32)
        xn_ref[0, rs, :] = (bits[:, :half] >> 16) | (bits[:, half:] & HI16)
        logits = _dot(xn, wr_ref[...]) + br_ref[...]

        lg = jnp.where(lane < N_GROUPS, logits, NEG_INF)
        g_max, g_sel = lane_argmax(lg)
        p_g = 1.0 / jnp.sum(jnp.exp(lg - g_max), axis=-1, keepdims=True)
        e_lo = _EXP_LANE0 + g_sel * EXPERTS_PER_GROUP
        in_group = jnp.logical_and(lane >= e_lo, lane < e_lo + EXPERTS_PER_GROUP)
        le = jnp.where(in_group, logits, NEG_INF)
        m1, i1 = lane_argmax(le)
        m2, i2 = lane_argmax(jnp.where(lane == i1, NEG_INF, le))
        e2 = jnp.exp(m2 - m1)
        gate0 = p_g / (1.0 + e2)
        gate1 = p_g * e2 / (1.0 + e2)
        e0 = i1 - _EXP_LANE0
        e1 = i2 - _EXP_LANE0

        hot0 = lane == e0
        hot1 = lane == e1
        onehot = jnp.where(jnp.logical_or(hot0, hot1), 1.0, 0.0)
        before = _dot(strict_lower, onehot.astype(BF16)) + base
        pos0 = jnp.sum(jnp.where(hot0, before, 0.0), axis=-1, keepdims=True)
        pos1 = jnp.sum(jnp.where(hot1, before, 0.0), axis=-1, keepdims=True)

        meta = jnp.zeros(logits.shape, F32)
        for idx, val in ((_META_E0, e0.astype(F32)), (_META_E1, e1.astype(F32)), (_META_G0, gate0),
                         (_META_G1, gate1), (_META_P0, pos0), (_META_P1, pos1)):
            meta = jnp.where(lane == idx, val, meta)
        meta_ref[0, rs, :] = meta
        return base + jnp.sum(onehot, axis=0, keepdims=True)

    base = cnt_ref[0:1, :]
    for r0 in range(0, tm, sub):
        base = rows(slice(r0, r0 + sub), base)
    cnt_ref[...] = jnp.broadcast_to(base, cnt_ref.shape)


def _post(x, da, gla, w_o, norm_cross, w_cq, cross_qn, k_mem, v_mem, w_co, norm_ffn, w_group, b_group,
          w_expert, b_expert):
    b, s, d = x.shape
    tm = TM_POST
    m = k_mem.shape[1]
    w_r = jnp.pad(jnp.concatenate([w_group, w_expert], axis=1), ((0, 0), (0, LANES - N_GROUPS - N_EXPERTS)))
    b_r = jnp.pad(jnp.concatenate([b_group, b_expert]), (0, LANES - N_GROUPS - N_EXPERTS)).reshape(1, LANES)
    const = lambda shape: pl.BlockSpec(shape, lambda i, j: (0,) * len(shape))
    tile = lambda width: pl.BlockSpec((1, tm, width), lambda i, j: (i, j, 0))
    per_b = lambda: pl.BlockSpec((1, m, d), lambda i, j: (i, 0, 0))
    return pl.pallas_call(
        functools.partial(_post_kernel, d=d, tm=tm, sub=SUB_POST),
        grid=(b, s // tm),
        in_specs=[tile(d), tile(d // 2), tile(d // 2), const((d, d)), const((1, d)), const((d, d)),
                  const((1, d // CROSS_HEADS)), per_b(), per_b(), const((d, d)), const((1, d)),
                  const((d, LANES)), const((1, LANES))],
        out_specs=[tile(d), tile(d // 2), tile(LANES), const((8, LANES))],
        out_shape=[
            jax.ShapeDtypeStruct((b, s, d), F32),
            jax.ShapeDtypeStruct((b, s, d // 2), U32),
            jax.ShapeDtypeStruct((b, s, LANES), F32),
            jax.ShapeDtypeStruct((8, LANES), F32),
        ],
        compiler_params=_params("arbitrary", "arbitrary"),
        name="post",
    )(x, da, gla, w_o.astype(BF16), norm_cross.reshape(1, d), w_cq.astype(BF16), cross_qn.reshape(1, -1),
      k_mem, v_mem, w_co.astype(BF16), norm_ffn.reshape(1, d), w_r.astype(BF16), b_r)


def _row_copy(src_ref, src_row, dst_ref, dst_row, sem):
    return pltpu.make_async_copy(src_ref.at[pl.ds(src_row, 1)], dst_ref.at[pl.ds(dst_row, 1)], sem)


def _dispatch_kernel(size_ref, pend_ref, dest_ref, xn_ref, xpad_ref, zero_ref, sem, zsem, *, tm):
    i = pl.program_id(0)

    @pl.when(i == 0)
    def _():
        zero_ref[...] = jnp.zeros(zero_ref.shape, zero_ref.dtype)
        rows = zero_ref.shape[0]
        n_blocks = xpad_ref.shape[0] // rows
        n_used = pend_ref[N_EXPERTS - 1] // rows

        def zero_block(blk):
            return pltpu.make_async_copy(zero_ref, xpad_ref.at[pl.ds(pl.multiple_of(blk * rows, rows), rows)], zsem)

        def last_block(e, fn):
            @pl.when(size_ref[e] > 0)
            def _():
                fn(zero_block(pend_ref[e] // rows - 1))

        for fn in (lambda cp: cp.start(), lambda cp: cp.wait()):
            lax.fori_loop(0, N_EXPERTS, lambda e, c: (last_block(e, fn), c)[1], 0)
            lax.fori_loop(n_used, n_blocks, lambda blk, c: (fn(zero_block(blk)), c)[1], 0)

    def start(t, carry):
        for k in range(2):
            _row_copy(xn_ref, t, xpad_ref, dest_ref[0, k, t], sem).start()
        return carry

    lax.fori_loop(0, tm, start, 0, unroll=True)
    for _ in range(2):
        pltpu.make_async_copy(xn_ref, xpad_ref.at[pl.ds(0, tm)], sem).wait()


def _dispatch(xn, dest, sizes, pend, n_rows):
    t, w = xn.shape
    tm = TM_ROWS
    return pl.pallas_call(
        functools.partial(_dispatch_kernel, tm=tm),
        grid_spec=pltpu.PrefetchScalarGridSpec(
            num_scalar_prefetch=2,
            grid=(t // tm,),
            in_specs=[
                pl.BlockSpec((1, 2, tm), lambda i, *_: (i, 0, 0), memory_space=pltpu.SMEM),
                pl.BlockSpec((tm, w), lambda i, *_: (i, 0)),
            ],
            out_specs=pl.BlockSpec(memory_space=pl.ANY),
            scratch_shapes=[pltpu.VMEM((EXPERT_ROWS, w), xn.dtype), pltpu.SemaphoreType.DMA,
                            pltpu.SemaphoreType.DMA],
        ),
        out_shape=jax.ShapeDtypeStruct((n_rows, w), xn.dtype),
        compiler_params=_params("arbitrary"),
        name="dispatch",
    )(sizes, pend, dest, xn)


def _experts_kernel(be_ref, nused_ref, x_ref, wg_ref, wu_ref, wd_ref, out_ref, wg_s, wu_s, wd_s):
    i = pl.program_id(0)
    used = i < nused_ref[0]
    new_expert = jnp.logical_or(i == 0, be_ref[i] != be_ref[jnp.maximum(i - 1, 0)])

    @pl.when(jnp.logical_and(used, new_expert))
    def _():
        wg_s[...] = wg_ref[0].astype(BF16)
        wu_s[...] = wu_ref[0].astype(BF16)
        wd_s[...] = wd_ref[0].astype(BF16)

    @pl.when(used)
    def _():
        words = x_ref[...]
        half = words.shape[1]
        lo = lax.bitcast_convert_type(words << 16, F32).astype(BF16)
        hi = lax.bitcast_convert_type(words & HI16, F32).astype(BF16)
        gate = _dot(lo, wg_s[:half, :]) + _dot(hi, wg_s[half:, :])
        up = _dot(lo, wu_s[:half, :]) + _dot(hi, wu_s[half:, :])
        hid = gate / (1.0 + jnp.exp(-gate)) * up
        out_ref[...] = _dot(hid.astype(BF16), wd_s[...])

    @pl.when(jnp.logical_not(used))
    def _():
        out_ref[...] = jnp.zeros(out_ref.shape, F32)


def _experts(x_pad, block_expert, n_used, w_gate, w_up, w_down):
    n_rows = x_pad.shape[0]
    _, d, f = w_gate.shape
    rows = EXPERT_ROWS
    row_blk = lambda i, be, nu: (jnp.minimum(i, nu[0] - 1), 0)
    return pl.pallas_call(
        _experts_kernel,
        grid_spec=pltpu.PrefetchScalarGridSpec(
            num_scalar_prefetch=2,
            grid=(n_rows // rows,),
            in_specs=[
                pl.BlockSpec((rows, d // 2), row_blk),
                pl.BlockSpec((1, d, f), lambda i, be, nu: (be[i], 0, 0)),
                pl.BlockSpec((1, d, f), lambda i, be, nu: (be[i], 0, 0)),
                pl.BlockSpec((1, f, d), lambda i, be, nu: (be[i], 0, 0)),
            ],
            out_specs=pl.BlockSpec((rows, d), lambda i, be, nu: (i, 0)),
            scratch_shapes=[pltpu.VMEM((d, f), BF16), pltpu.VMEM((d, f), BF16), pltpu.VMEM((f, d), BF16)],
        ),
        out_shape=jax.ShapeDtypeStruct((n_rows, d), F32),
        compiler_params=_params("arbitrary"),
        name="experts",
    )(block_expert, n_used, x_pad, w_gate, w_up, w_down)


def _combine_kernel(dest_ref, h_ref, meta_ref, opad_ref, y_ref, buf_ref, sem, *, tm):
    def start(t, carry):
        for k in range(2):
            _row_copy(opad_ref, dest_ref[0, k, t], buf_ref.at[k], t, sem).start()
        return carry

    lax.fori_loop(0, tm, start, 0, unroll=True)
    for k in range(2):
        pltpu.make_async_copy(opad_ref.at[pl.ds(0, tm)], buf_ref.at[k], sem).wait()
    meta = meta_ref[...]
    g0 = meta[:, _META_G0:_META_G0 + 1]
    g1 = meta[:, _META_G1:_META_G1 + 1]
    y_ref[...] = h_ref[...] + g0 * buf_ref[0] + g1 * buf_ref[1]


def _combine(h2, meta, dest, out_pad):
    t, d = h2.shape
    tm = TM_ROWS
    return pl.pallas_call(
        functools.partial(_combine_kernel, tm=tm),
        grid=(t // tm,),
        in_specs=[
            pl.BlockSpec((1, 2, tm), lambda i: (i, 0, 0), memory_space=pltpu.SMEM),
            pl.BlockSpec((tm, d), lambda i: (i, 0)),
            pl.BlockSpec((tm, LANES), lambda i: (i, 0)),
            pl.BlockSpec(memory_space=pl.ANY),
        ],
        out_specs=pl.BlockSpec((tm, d), lambda i: (i, 0)),
        scratch_shapes=[pltpu.VMEM((2, tm, d), F32), pltpu.SemaphoreType.DMA],
        out_shape=jax.ShapeDtypeStruct((t, d), F32),
        compiler_params=_params("arbitrary"),
        name="combine",
    )(dest, h2, meta, out_pad)


def _tile_dest(meta, pstart, tm):
    t = meta.shape[0]
    expert = meta[:, _META_E0:_META_E1 + 1].astype(I32)
    rank = meta[:, _META_P0:_META_P1 + 1].astype(I32)
    start = jnp.sum(jnp.where(expert[..., None] == jnp.arange(N_EXPERTS), pstart, 0), axis=-1)
    return (start + rank).reshape(t // tm, tm, 2).transpose(0, 2, 1)


def kernel(x, mem, norm_mix, w_in, da_q_norm, da_k_norm, lambda_q1, lambda_k1, lambda_q2, lambda_k2,
           da_out_norm, gla_gate_w, gla_gate_b, gla_out_norm, w_o, norm_cross, norm_mem, w_cq, w_ckv,
           cross_q_norm, cross_k_norm, w_co, norm_ffn, w_group, b_group, w_expert, b_expert,
           w_e_gate, w_e_up, w_e_down):
    b, s, d = x.shape
    t = b * s
    h = x
    for l in range(norm_mix.shape[0]):
        assert l == 0, "lam_init is fixed for a single layer"
        qt, kda, vt, gq, gk, gv, gg, la = _in_proj(h, norm_mix[l], w_in[l], da_q_norm[l], da_k_norm[l],
                                                   gla_gate_w[l], gla_gate_b[l])
        da = _diff_attn(qt, kda, vt, lambda_q1[l], lambda_k1[l], lambda_q2[l], lambda_k2[l], da_out_norm[l],
                        da_q_norm[l], da_k_norm[l])
        gla = _gla(gq, gk, la, gv, gg, gla_out_norm[l])
        k_mem, v_mem = _mem_kv(mem, norm_mem[l], w_ckv[l], cross_k_norm[l])
        h2, xn, meta, counts = _post(h, da, gla, w_o[l], norm_cross[l], w_cq[l], cross_q_norm[l], k_mem, v_mem,
                                     w_co[l], norm_ffn[l], w_group[l], b_group[l], w_expert[l], b_expert[l])

        rows = EXPERT_ROWS
        sizes = counts[0, :N_EXPERTS].astype(I32)
        padded = (sizes + rows - 1) // rows * rows
        pend = jnp.cumsum(padded)
        pstart = pend - padded
        n_rows = 2 * t + N_EXPERTS * rows
        n_blocks = n_rows // rows
        block_start = jnp.arange(n_blocks, dtype=I32) * rows
        block_expert = jnp.minimum(jnp.sum(pend[None, :] <= block_start[:, None], axis=1), N_EXPERTS - 1).astype(I32)
        n_used = (pend[-1:] // rows).astype(I32)

        h2 = h2.reshape(t, d)
        meta = meta.reshape(t, LANES)
        dest = _tile_dest(meta, pstart, TM_ROWS)
        x_pad = _dispatch(xn.reshape(t, d // 2), dest, sizes, pend, n_rows)
        out_pad = _experts(x_pad, block_expert, n_used, w_e_gate[l], w_e_up[l], w_e_down[l])
        h = _combine(h2, meta, dest, out_pad).reshape(b, s, d)
    return h
```

```python
import functools
import math

import jax
import jax.numpy as jnp
import numpy as np
from jax import lax
from jax.experimental import pallas as pl
from jax.experimental.pallas import tpu as pltpu

F32 = jnp.float32
BF16 = jnp.bfloat16
I32 = jnp.int32
U32 = jnp.uint32
HI16 = np.uint32(0xFFFF0000)

EPS = 1e-6
CHUNK = 64

DA_HEADS = 4
DA_QK_DIM = 64
DA_V_DIM = 128
GLA_HEADS = 4
GLA_K_DIM = 64
GLA_V_DIM = 128
GLA_GATE_RANK = 16
GLA_TAU = 16.0
CROSS_HEADS = 4
N_GROUPS = 4
EXPERTS_PER_GROUP = 8
N_EXPERTS = N_GROUPS * EXPERTS_PER_GROUP
LAM_INIT = 0.8 - 0.6 * math.exp(-0.3 * 0)

LANES = 128
VMEM_LIMIT = 56 * 1024 * 1024

TM_PROJ = 512
TS_GLA = 1024
TM_POST = 512
SUB_POST = 512
TM_ROWS = 512
EXPERT_ROWS = 512

NEG_INF = float("-inf")


def _params(*sem):
    return pltpu.CompilerParams(dimension_semantics=sem, vmem_limit_bytes=VMEM_LIMIT)


def _rms(t, g):
    ms = jnp.mean(t * t, axis=-1, keepdims=True)
    return t * lax.rsqrt(ms + EPS) * g


def _dot(a, b):
    return jnp.dot(a, b, preferred_element_type=F32)


def _dot_nt(a, b):
    return lax.dot_general(a, b, (((1,), (1,)), ((), ())), preferred_element_type=F32)


def _dot_tn(a, b):
    return lax.dot_general(a, b, (((0,), (0,)), ((), ())), preferred_element_type=F32)


def _split_bf16(t):
    hi = t.astype(BF16)
    lo = (t - hi.astype(F32)).astype(BF16)
    return hi, lo


def _mem_kv_kernel(mem_ref, g_ref, w_ref, kn_ref, k_ref, v_ref, *, d, heads):
    mn = _rms(mem_ref[0], g_ref[...]).astype(BF16)
    kv = _dot(mn, w_ref[...])
    hd = d // heads
    scale = hd ** -0.5
    for h in range(heads):
        kh = _rms(kv[:, h * hd:(h + 1) * hd], kn_ref[...]) * scale
        k_ref[0, :, h * hd:(h + 1) * hd] = kh.astype(BF16)
    v_ref[0] = kv[:, d:].astype(BF16)


def _mem_kv(mem, norm_m, w_ckv, kn):
    b, m, d = mem.shape
    return pl.pallas_call(
        functools.partial(_mem_kv_kernel, d=d, heads=CROSS_HEADS),
        grid=(b,),
        in_specs=[
            pl.BlockSpec((1, m, d), lambda i: (i, 0, 0)),
            pl.BlockSpec((1, d), lambda i: (0, 0)),
            pl.BlockSpec((d, 2 * d), lambda i: (0, 0)),
            pl.BlockSpec((1, d // CROSS_HEADS), lambda i: (0, 0)),
        ],
        out_specs=[
            pl.BlockSpec((1, m, d), lambda i: (i, 0, 0)),
            pl.BlockSpec((1, m, d), lambda i: (i, 0, 0)),
        ],
        out_shape=[jax.ShapeDtypeStruct((b, m, d), BF16)] * 2,
        compiler_params=_params("parallel"),
        name="mem_kv",
    )(mem, norm_m.reshape(1, d), w_ckv.astype(BF16), kn.reshape(1, -1))


_QK = DA_HEADS * 2 * DA_QK_DIM
_DAW = DA_HEADS * DA_V_DIM
_GQK = GLA_HEADS * GLA_K_DIM
_GW = GLA_HEADS * GLA_V_DIM
_OFF_DQ = 0
_OFF_DK = _OFF_DQ + _QK
_OFF_DV = _OFF_DK + _QK
_OFF_GQ = _OFF_DV + _DAW
_OFF_GK = _OFF_GQ + _GQK
_OFF_GV = _OFF_GK + _GQK
_OFF_GG = _OFF_GV + _GW
_OFF_GR = _OFF_GG + _GW
_IN_PAD = _OFF_GR + LANES


def _in_proj_kernel(x_ref, g_ref, w_ref, qg_ref, kg_ref, grp_ref, gw_ref, gb_ref,
                    qt_ref, k_ref, vt_ref, gq_ref, gk_ref, gv_ref, gg_ref, la_ref):
    u = _rms(x_ref[0], g_ref[...]).astype(BF16)

    def proj(off, width):
        return _dot(u, w_ref[:, off:off + width])

    def group_norm(p, gain):
        ms = _dot((p * p).astype(BF16), grp_ref[...])
        return p * lax.rsqrt(ms + EPS) * gain

    qn = group_norm(proj(_OFF_DQ, _QK), qg_ref[...]) * (DA_QK_DIM ** -0.5 * math.log2(math.e))
    qt_ref[0, 0] = qn.T.astype(BF16)
    k_ref[0] = group_norm(proj(_OFF_DK, _QK), kg_ref[...]).astype(BF16)
    vt_ref[0, 0] = proj(_OFF_DV, _DAW).T.astype(BF16)
    gq_ref[0] = proj(_OFF_GQ, _GQK) * (GLA_K_DIM ** -0.5)
    gk_ref[0] = proj(_OFF_GK, _GQK)
    gv_ref[0] = proj(_OFF_GV, _GW).astype(BF16)
    gg_ref[0] = proj(_OFF_GG, _GW)
    z = _dot(proj(_OFF_GR, LANES).astype(BF16), gw_ref[...]) + gb_ref[...]
    log_sig = jnp.minimum(z, 0.0) - jnp.log(1.0 + jnp.exp(-jnp.abs(z)))
    la_ref[0] = log_sig * (1.0 / GLA_TAU)


def _in_proj(x, norm_g, w_in, da_qn, da_kn, gate_w, gate_b):
    b, s, d = x.shape
    tm = TM_PROJ
    ns = s // tm
    w = jnp.pad(w_in, ((0, 0), (0, _IN_PAD - w_in.shape[1]))).astype(BF16)
    gw = jnp.pad(gate_w, ((0, LANES - GLA_GATE_RANK), (0, 0))).astype(BF16)
    lane = jnp.arange(_QK)
    grp = jnp.where((lane[:, None] // DA_QK_DIM) == (lane[None, :] // DA_QK_DIM),
                    1.0 / DA_QK_DIM, 0.0).astype(BF16)
    const = lambda shape: pl.BlockSpec(shape, lambda i, j: (0,) * len(shape))
    tile = lambda width: pl.BlockSpec((1, tm, width), lambda i, j: (i, j, 0))
    tile_t = lambda width: pl.BlockSpec((1, 1, width, tm), lambda i, j: (i, j, 0, 0))
    return pl.pallas_call(
        _in_proj_kernel,
        grid=(b, ns),
        in_specs=[tile(d), const((1, d)), const((d, _IN_PAD)), const((1, _QK)), const((1, _QK)),
                  const((_QK, _QK)), const((LANES, _GQK)), const((1, _GQK))],
        out_specs=[tile_t(_QK), tile(_QK), tile_t(_DAW), tile(_GQK), tile(_GQK), tile(_GW), tile(_GW),
                   tile(_GQK)],
        out_shape=[
            jax.ShapeDtypeStruct((b, ns, _QK, tm), BF16),
            jax.ShapeDtypeStruct((b, s, _QK), BF16),
            jax.ShapeDtypeStruct((b, ns, _DAW, tm), BF16),
            jax.ShapeDtypeStruct((b, s, _GQK), F32),
            jax.ShapeDtypeStruct((b, s, _GQK), F32),
            jax.ShapeDtypeStruct((b, s, _GW), BF16),
            jax.ShapeDtypeStruct((b, s, _GW), F32),
            jax.ShapeDtypeStruct((b, s, _GQK), F32),
        ],
        compiler_params=_params("parallel", "parallel"),
        name="in_proj",
    )(x, norm_g.reshape(1, d), w, jnp.tile(da_qn, 2 * DA_HEADS).reshape(1, _QK),
      jnp.tile(da_kn, 2 * DA_HEADS).reshape(1, _QK), grp, gw, gate_b.reshape(1, _GQK))


def _split_q(qt):
    row = lax.broadcasted_iota(I32, qt.shape, 0)
    zero = jnp.zeros_like(qt)
    return jnp.where(row < DA_QK_DIM, qt, zero), jnp.where(row >= DA_QK_DIM, qt, zero)


def _chunk_causal_mask(blk):
    key_chunk = lax.broadcasted_iota(I32, (blk, blk), 0) // CHUNK
    qry_chunk = lax.broadcasted_iota(I32, (blk, blk), 1) // CHUNK
    return key_chunk <= qry_chunk


def _diff_attn_finish(lq1_ref, lk1_ref, lq2_ref, lk2_ref, gain_ref, a1, l1, a2, l2):
    lam = (jnp.exp(jnp.sum(lq1_ref[...] * lk1_ref[...], axis=-1, keepdims=True))
           - jnp.exp(jnp.sum(lq2_ref[...] * lk2_ref[...], axis=-1, keepdims=True)) + LAM_INIT)
    o = a1 / l1 - lam * (a2 / l2)
    ms = jnp.mean(o * o, axis=0, keepdims=True)
    o = o * lax.rsqrt(ms + EPS) * gain_ref[...] * (1.0 - LAM_INIT)
    return o.T.astype(BF16)


def _diff_attn_bounded_kernel(lq1_ref, lk1_ref, lq2_ref, lk2_ref, gain_ref, qt_ref, k_ref, vt_ref, out_ref,
                              s_ref, l1_ref, a1_ref, l2_ref, a2_ref, *, blk, nb):
    stats = ((l1_ref, a1_ref), (l2_ref, a2_ref))
    mask = _chunk_causal_mask(blk)

    def reset():
        for l_ref, a_ref in stats:
            l_ref[...] = jnp.zeros(l_ref.shape, F32)
            a_ref[...] = jnp.zeros(a_ref.shape, F32)

    def scores(q, j, slot):
        kb = k_ref[0, pl.ds(pl.multiple_of(j * blk, blk), blk), :]
        s_ref[slot, 0] = _dot(kb, q[0])
        s_ref[slot, 1] = _dot(kb, q[1])

    def consume(j, slot, masked):
        vb = vt_ref[0, j]
        for m, (l_ref, a_ref) in enumerate(stats):
            s = s_ref[slot, m]
            if masked:
                s = jnp.where(mask, s, NEG_INF)
            p = jnp.exp2(s)
            l_ref[...] += jnp.sum(p, axis=0, keepdims=True)
            a_ref[...] += _dot(vb, p.astype(BF16))

    def step(q, j, slot):
        scores(q, j + 1, 1 - slot)
        consume(j, slot, False)

    reset()
    q = _split_q(qt_ref[0, 0])
    slot = 0
    scores(q, 0, slot)
    for qi in range(nb):
        def pair(i, carry, q=q, slot=slot):
            step(q, 2 * i, slot)
            step(q, 2 * i + 1, 1 - slot)
            return carry

        if qi // 2:
            lax.fori_loop(0, qi // 2, pair, 0)
        if qi % 2:
            step(q, qi - 1, slot)
            slot = 1 - slot
        if qi + 1 < nb:
            q = _split_q(qt_ref[0, qi + 1])
            scores(q, 0, 1 - slot)
        consume(qi, slot, True)
        out_ref[0, qi * blk:(qi + 1) * blk, :] = _diff_attn_finish(
            lq1_ref, lk1_ref, lq2_ref, lk2_ref, gain_ref, a1_ref[...], l1_ref[...], a2_ref[...], l2_ref[...])
        if qi + 1 < nb:
            reset()
        slot = 1 - slot


def _diff_attn_online_kernel(lq1_ref, lk1_ref, lq2_ref, lk2_ref, gain_ref, qt_ref, k_ref, vt_ref, out_ref,
                             m1_ref, l1_ref, a1_ref, m2_ref, l2_ref, a2_ref, *, blk):
    qi = pl.program_id(2)
    q1, q2 = _split_q(qt_ref[0, 0])

    for m_ref, l_ref, a_ref in ((m1_ref, l1_ref, a1_ref), (m2_ref, l2_ref, a2_ref)):
        m_ref[...] = jnp.full(m_ref.shape, NEG_INF, F32)
        l_ref[...] = jnp.zeros(l_ref.shape, F32)
        a_ref[...] = jnp.zeros(a_ref.shape, F32)

    def update(s, vb, m_ref, l_ref, a_ref):
        m_old = m_ref[...]
        m_new = jnp.maximum(m_old, jnp.max(s, axis=0, keepdims=True))
        alpha = jnp.exp2(m_old - m_new)
        p = jnp.exp2(s - m_new)
        l_ref[...] = alpha * l_ref[...] + jnp.sum(p, axis=0, keepdims=True)
        a_ref[...] = alpha * a_ref[...] + _dot(vb, p.astype(BF16))
        m_ref[...] = m_new

    def block(j, mask):
        kb = k_ref[0, pl.ds(pl.multiple_of(j * blk, blk), blk), :]
        vb = vt_ref[0, j]
        s1 = _dot(kb, q1)
        s2 = _dot(kb, q2)
        if mask is not None:
            s1 = jnp.where(mask, s1, NEG_INF)
            s2 = jnp.where(mask, s2, NEG_INF)
        update(s1, vb, m1_ref, l1_ref, a1_ref)
        update(s2, vb, m2_ref, l2_ref, a2_ref)

    def body(j, carry):
        block(j, None)
        return carry

    lax.fori_loop(0, qi, body, 0)
    block(qi, _chunk_causal_mask(blk))
    out_ref[0] = _diff_attn_finish(lq1_ref, lk1_ref, lq2_ref, lk2_ref, gain_ref,
                                   a1_ref[...], l1_ref[...], a2_ref[...], l2_ref[...])


SCORE_BOUND = 60.0


def _diff_attn(qt, k, vt, lq1, lk1, lq2, lk2, da_on, da_qn, da_kn):
    b, nb, _, blk = qt.shape
    s = nb * blk
    stat = lambda: pltpu.VMEM((1, blk), F32)
    acc = lambda: pltpu.VMEM((DA_V_DIM, blk), F32)

    args = (lq1.reshape(1, -1), lk1.reshape(1, -1), lq2.reshape(1, -1), lk2.reshape(1, -1),
            da_on.reshape(-1, 1), qt, k, vt)
    out_shape = jax.ShapeDtypeStruct((b, s, _DAW), BF16)
    head = lambda *trailing: (lambda i, h: (i, 0, h) + trailing)
    vec2 = lambda: pl.BlockSpec((1, DA_QK_DIM), lambda i, h: (0, 0))
    bounded = pl.pallas_call(
        functools.partial(_diff_attn_bounded_kernel, blk=blk, nb=nb),
        grid=(b, DA_HEADS),
        in_specs=[
            vec2(), vec2(), vec2(), vec2(),
            pl.BlockSpec((DA_V_DIM, 1), lambda i, h: (0, 0)),
            pl.BlockSpec((1, nb, 2 * DA_QK_DIM, blk), head(0)),
            pl.BlockSpec((1, s, 2 * DA_QK_DIM), head()),
            pl.BlockSpec((1, nb, DA_V_DIM, blk), head(0)),
        ],
        out_specs=pl.BlockSpec((1, s, DA_V_DIM), head()),
        out_shape=out_shape,
        scratch_shapes=[pltpu.VMEM((2, 2, blk, blk), F32), stat(), acc(), stat(), acc()],
        compiler_params=_params("parallel", "parallel"),
        name="diff_attn",
    )
    vec3 = lambda: pl.BlockSpec((1, DA_QK_DIM), lambda i, h, q: (0, 0))
    online = pl.pallas_call(
        functools.partial(_diff_attn_online_kernel, blk=blk),
        grid=(b, DA_HEADS, nb),
        in_specs=[
            vec3(), vec3(), vec3(), vec3(),
            pl.BlockSpec((DA_V_DIM, 1), lambda i, h, q: (0, 0)),
            pl.BlockSpec((1, 1, 2 * DA_QK_DIM, blk), lambda i, h, q: (i, q, h, 0)),
            pl.BlockSpec((1, s, 2 * DA_QK_DIM), lambda i, h, q: (i, 0, h)),
            pl.BlockSpec((1, nb, DA_V_DIM, blk), lambda i, h, q: (i, 0, h, 0)),
        ],
        out_specs=pl.BlockSpec((1, blk, DA_V_DIM), lambda i, h, q: (i, q, h)),
        out_shape=out_shape,
        scratch_shapes=[stat(), stat(), acc(), stat(), stat(), acc()],
        compiler_params=_params("parallel", "parallel", "parallel"),
        name="diff_attn_online",
    )
    bound = (1.01 * DA_QK_DIM ** 0.5 * math.log2(math.e)) * jnp.max(jnp.abs(da_qn)) * jnp.max(jnp.abs(da_kn))
    return lax.cond(bound <= SCORE_BOUND, bounded, online, *args)


def _gla_kernel(q_ref, k_ref, la_ref, v_ref, g_ref, gain_ref, out_ref, st_ref, *, ts):
    @pl.when(pl.program_id(1) == 0)
    def _():
        st_ref[...] = jnp.zeros(st_ref.shape, F32)

    c = CHUNK
    hk, hv = _GQK, _GW
    r = lax.broadcasted_iota(I32, (c, c), 0)
    cc = lax.broadcasted_iota(I32, (c, c), 1)
    tri = jnp.where(r >= cc, 1.0, 0.0).astype(BF16)
    bd_k = (lax.broadcasted_iota(I32, (hk, hk), 0) // GLA_K_DIM
            == lax.broadcasted_iota(I32, (hk, hk), 1) // GLA_K_DIM)
    bd_v = (lax.broadcasted_iota(I32, (hk, hv), 0) // GLA_K_DIM
            == lax.broadcasted_iota(I32, (hk, hv), 1) // GLA_V_DIM)
    bd_vt = (lax.broadcasted_iota(I32, (hv, hk), 0) // GLA_V_DIM
             == lax.broadcasted_iota(I32, (hv, hk), 1) // GLA_K_DIM)
    lower = (lax.broadcasted_iota(I32, (c, hk), 0)
             >= lax.broadcasted_iota(I32, (c, hk), 1) % c)

    def chunk(ci, carry):
        sl = pl.ds(pl.multiple_of(ci * c, c), c)
        la_hi, la_lo = _split_bf16(la_ref[0, sl, :])
        big_l = _dot(tri, la_hi) + _dot(tri, la_lo)
        l_end = big_l[c - 1:c, :]
        lc = big_l - big_l[c // 2 - 1:c // 2, :]
        e_pos = jnp.exp(lc)
        e_neg = jnp.exp(-lc)
        q = q_ref[0, sl, :]
        k = k_ref[0, sl, :]
        v = v_ref[0, sl, :]

        def tiled(t, mask):
            t4 = jnp.concatenate([t] * GLA_HEADS, axis=0)
            return jnp.where(mask, t4, jnp.zeros_like(t4))

        a_past = _dot_nt((q * e_pos).astype(BF16), tiled((k * e_neg).astype(BF16), bd_k))
        a_fut = _dot_nt((q * e_neg).astype(BF16), tiled((k * e_pos).astype(BF16), bd_k))
        a = jnp.where(lower, a_past, a_fut).astype(BF16)
        o = _dot(a, tiled(v, bd_v))
        st = st_ref[...]
        o = o + _dot_nt((q * jnp.exp(big_l)).astype(BF16), st.astype(BF16))
        u_t = _dot_tn(v, (k * jnp.exp(l_end - big_l)).astype(BF16))
        st_ref[...] = st * jnp.exp(l_end) + jnp.where(bd_vt, u_t, 0.0)

        g = g_ref[0, sl, :]
        silu = g / (1.0 + jnp.exp(-g))
        for h in range(GLA_HEADS):
            hs = slice(h * GLA_V_DIM, (h + 1) * GLA_V_DIM)
            out_ref[0, sl, hs] = (_rms(o[:, hs], gain_ref[...]) * silu[:, hs]).astype(BF16)
        return carry

    lax.fori_loop(0, ts // c, chunk, 0, unroll=4)


def _gla(gq, gk, la, gv, gg, gla_on):
    b, s, _ = gq.shape
    ts = TS_GLA
    tile = lambda width: pl.BlockSpec((1, ts, width), lambda i, j: (i, j, 0))
    return pl.pallas_call(
        functools.partial(_gla_kernel, ts=ts),
        grid=(b, s // ts),
        in_specs=[tile(_GQK), tile(_GQK), tile(_GQK), tile(_GW), tile(_GW),
                  pl.BlockSpec((1, GLA_V_DIM), lambda i, j: (0, 0))],
        out_specs=tile(_GW),
        out_shape=jax.ShapeDtypeStruct((b, s, _GW), BF16),
        scratch_shapes=[pltpu.VMEM((_GW, _GQK), F32)],
        compiler_params=_params("parallel", "arbitrary"),
        name="gla",
    )(gq, gk, la, gv, gg, gla_on.reshape(1, -1))


_META_E0, _META_E1, _META_G0, _META_G1, _META_P0, _META_P1 = range(6)
_EXP_LANE0 = N_GROUPS


def _post_kernel(x_ref, da_ref, gla_ref, wo_ref, gc_ref, wq_ref, qn_ref, km_ref, vm_ref, wco_ref,
                 gf_ref, wr_ref, br_ref,
                 h_ref, xn_ref, meta_ref, cnt_ref, *, d, tm, sub):
    first = jnp.logical_and(pl.program_id(0) == 0, pl.program_id(1) == 0)

    @pl.when(first)
    def _():
        cnt_ref[...] = jnp.zeros(cnt_ref.shape, F32)

    half = d // 2
    hd = d // CROSS_HEADS
    lane = lax.broadcasted_iota(I32, (sub, LANES), 1)
    big = jnp.int32(LANES)
    strict_lower = jnp.where(lax.broadcasted_iota(I32, (sub, sub), 0) > lax.broadcasted_iota(I32, (sub, sub), 1),
                             1.0, 0.0).astype(BF16)

    def lane_argmax(vals):
        m = jnp.max(vals, axis=-1, keepdims=True)
        idx = jnp.min(jnp.where(vals == m, lane, big), axis=-1, keepdims=True)
        return m, idx

    def rows(rs, base):
        h1 = x_ref[0, rs, :] + _dot(da_ref[0, rs, :], wo_ref[:half, :]) + _dot(gla_ref[0, rs, :], wo_ref[half:, :])

        u = _rms(h1, gc_ref[...]).astype(BF16)
        q = _dot(u, wq_ref[...])
        heads = []
        for h in range(CROSS_HEADS):
            hs = slice(h * hd, (h + 1) * hd)
            qh = _rms(q[:, hs], qn_ref[...]).astype(BF16)
            sc = _dot_nt(qh, km_ref[0, :, hs])
            sc = sc - jnp.max(sc, axis=-1, keepdims=True)
            p = jnp.exp(sc)
            p = p / jnp.sum(p, axis=-1, keepdims=True)
            heads.append(_dot(p.astype(BF16), vm_ref[0, :, hs]))
        o = jnp.concatenate(heads, axis=-1).astype(BF16)
        h2 = h1 + _dot(o, wco_ref[...])
        h_ref[0, rs, :] = h2

        xn = _rms(h2, gf_ref[...]).astype(BF16)
        bits = lax.bitcast_convert_type(xn.astype(F32), U32)
        xn_ref[0, rs, :] = (bits[:, :half] >> 16) | (bits[:, half:] & HI16)
        logits = _dot(xn, wr_ref[...]) + br_ref[...]

        lg = jnp.where(lane < N_GROUPS, logits, NEG_INF)
        g_max, g_sel = lane_argmax(lg)
        p_g = 1.0 / jnp.sum(jnp.exp(lg - g_max), axis=-1, keepdims=True)
        e_lo = _EXP_LANE0 + g_sel * EXPERTS_PER_GROUP
        in_group = jnp.logical_and(lane >= e_lo, lane < e_lo + EXPERTS_PER_GROUP)
        le = jnp.where(in_group, logits, NEG_INF)
        m1, i1 = lane_argmax(le)
        m2, i2 = lane_argmax(jnp.where(lane == i1, NEG_INF, le))
        e2 = jnp.exp(m2 - m1)
        gate0 = p_g / (1.0 + e2)
        gate1 = p_g * e2 / (1.0 + e2)
        e0 = i1 - _EXP_LANE0
        e1 = i2 - _EXP_LANE0

        hot0 = lane == e0
        hot1 = lane == e1
        onehot = jnp.where(jnp.logical_or(hot0, hot1), 1.0, 0.0)
        before = _dot(strict_lower, onehot.astype(BF16)) + base
        pos0 = jnp.sum(jnp.where(hot0, before, 0.0), axis=-1, keepdims=True)
        pos1 = jnp.sum(jnp.where(hot1, before, 0.0), axis=-1, keepdims=True)

        meta = jnp.zeros(logits.shape, F32)
        for idx, val in ((_META_E0, e0.astype(F32)), (_META_E1, e1.astype(F32)), (_META_G0, gate0),
                         (_META_G1, gate1), (_META_P0, pos0), (_META_P1, pos1)):
            meta = jnp.where(lane == idx, val, meta)
        meta_ref[0, rs, :] = meta
        return base + jnp.sum(onehot, axis=0, keepdims=True)

    base = cnt_ref[0:1, :]
    for r0 in range(0, tm, sub):
        base = rows(slice(r0, r0 + sub), base)
    cnt_ref[...] = jnp.broadcast_to(base, cnt_ref.shape)


def _post(x, da, gla, w_o, norm_cross, w_cq, cross_qn, k_mem, v_mem, w_co, norm_ffn, w_group, b_group,
          w_expert, b_expert):
    b, s, d = x.shape
    tm = TM_POST
    m = k_mem.shape[1]
    w_r = jnp.pad(jnp.concatenate([w_group, w_expert], axis=1), ((0, 0), (0, LANES - N_GROUPS - N_EXPERTS)))
    b_r = jnp.pad(jnp.concatenate([b_group, b_expert]), (0, LANES - N_GROUPS - N_EXPERTS)).reshape(1, LANES)
    const = lambda shape: pl.BlockSpec(shape, lambda i, j: (0,) * len(shape))
    tile = lambda width: pl.BlockSpec((1, tm, width), lambda i, j: (i, j, 0))
    per_b = lambda: pl.BlockSpec((1, m, d), lambda i, j: (i, 0, 0))
    return pl.pallas_call(
        functools.partial(_post_kernel, d=d, tm=tm, sub=SUB_POST),
        grid=(b, s // tm),
        in_specs=[tile(d), tile(d // 2), tile(d // 2), const((d, d)), const((1, d)), const((d, d)),
                  const((1, d // CROSS_HEADS)), per_b(), per_b(), const((d, d)), const((1, d)),
                  const((d, LANES)), const((1, LANES))],
        out_specs=[tile(d), tile(d // 2), tile(LANES), const((8, LANES))],
        out_shape=[
            jax.ShapeDtypeStruct((b, s, d), F32),
            jax.ShapeDtypeStruct((b, s, d // 2), U32),
            jax.ShapeDtypeStruct((b, s, LANES), F32),
            jax.ShapeDtypeStruct((8, LANES), F32),
        ],
        compiler_params=_params("arbitrary", "arbitrary"),
        name="post",
    )(x, da, gla, w_o.astype(BF16), norm_cross.reshape(1, d), w_cq.astype(BF16), cross_qn.reshape(1, -1),
      k_mem, v_mem, w_co.astype(BF16), norm_ffn.reshape(1, d), w_r.astype(BF16), b_r)


def _row_copy(src_ref, src_row, dst_ref, dst_row, sem):
    return pltpu.make_async_copy(src_ref.at[pl.ds(src_row, 1)], dst_ref.at[pl.ds(dst_row, 1)], sem)


def _dispatch_kernel(size_ref, pend_ref, dest_ref, xn_ref, xpad_ref, zero_ref, sem, zsem, *, tm):
    i = pl.program_id(0)

    @pl.when(i == 0)
    def _():
        zero_ref[...] = jnp.zeros(zero_ref.shape, zero_ref.dtype)
        rows = zero_ref.shape[0]
        n_blocks = xpad_ref.shape[0] // rows
        n_used = pend_ref[N_EXPERTS - 1] // rows

        def zero_block(blk):
            return pltpu.make_async_copy(zero_ref, xpad_ref.at[pl.ds(pl.multiple_of(blk * rows, rows), rows)], zsem)

        def last_block(e, fn):
            @pl.when(size_ref[e] > 0)
            def _():
                fn(zero_block(pend_ref[e] // rows - 1))

        for fn in (lambda cp: cp.start(), lambda cp: cp.wait()):
            lax.fori_loop(0, N_EXPERTS, lambda e, c: (last_block(e, fn), c)[1], 0)
            lax.fori_loop(n_used, n_blocks, lambda blk, c: (fn(zero_block(blk)), c)[1], 0)

    def start(t, carry):
        for k in range(2):
            _row_copy(xn_ref, t, xpad_ref, dest_ref[0, k, t], sem).start(priority=k)
        return carry

    lax.fori_loop(0, tm, start, 0, unroll=True)
    for _ in range(2):
        pltpu.make_async_copy(xn_ref, xpad_ref.at[pl.ds(0, tm)], sem).wait()


def _dispatch(xn, dest, sizes, pend, n_rows):
    t, w = xn.shape
    tm = TM_ROWS
    return pl.pallas_call(
        functools.partial(_dispatch_kernel, tm=tm),
        grid_spec=pltpu.PrefetchScalarGridSpec(
            num_scalar_prefetch=2,
            grid=(t // tm,),
            in_specs=[
                pl.BlockSpec((1, 2, tm), lambda i, *_: (i, 0, 0), memory_space=pltpu.SMEM),
                pl.BlockSpec((tm, w), lambda i, *_: (i, 0)),
            ],
            out_specs=pl.BlockSpec(memory_space=pl.ANY),
            scratch_shapes=[pltpu.VMEM((EXPERT_ROWS, w), xn.dtype), pltpu.SemaphoreType.DMA,
                            pltpu.SemaphoreType.DMA],
        ),
        out_shape=jax.ShapeDtypeStruct((n_rows, w), xn.dtype),
        compiler_params=_params("arbitrary"),
        name="dispatch",
    )(sizes, pend, dest, xn)


def _experts_kernel(be_ref, nused_ref, x_ref, wg_ref, wu_ref, wd_ref, out_ref, wg_s, wu_s, wd_s):
    i = pl.program_id(0)
    used = i < nused_ref[0]
    new_expert = jnp.logical_or(i == 0, be_ref[i] != be_ref[jnp.maximum(i - 1, 0)])

    @pl.when(jnp.logical_and(used, new_expert))
    def _():
        wg_s[...] = wg_ref[0].astype(BF16)
        wu_s[...] = wu_ref[0].astype(BF16)
        wd_s[...] = wd_ref[0].astype(BF16)

    @pl.when(used)
    def _():
        words = x_ref[...]
        half = words.shape[1]
        lo = lax.bitcast_convert_type(words << 16, F32).astype(BF16)
        hi = lax.bitcast_convert_type(words & HI16, F32).astype(BF16)
        gate = _dot(lo, wg_s[:half, :]) + _dot(hi, wg_s[half:, :])
        up = _dot(lo, wu_s[:half, :]) + _dot(hi, wu_s[half:, :])
        hid = gate / (1.0 + jnp.exp(-gate)) * up
        out_ref[...] = _dot(hid.astype(BF16), wd_s[...])

    @pl.when(jnp.logical_not(used))
    def _():
        out_ref[...] = jnp.zeros(out_ref.shape, F32)


def _experts(x_pad, block_expert, n_used, w_gate, w_up, w_down):
    n_rows = x_pad.shape[0]
    _, d, f = w_gate.shape
    rows = EXPERT_ROWS
    row_blk = lambda i, be, nu: (jnp.minimum(i, nu[0] - 1), 0)
    return pl.pallas_call(
        _experts_kernel,
        grid_spec=pltpu.PrefetchScalarGridSpec(
            num_scalar_prefetch=2,
            grid=(n_rows // rows,),
            in_specs=[
                pl.BlockSpec((rows, d // 2), row_blk),
                pl.BlockSpec((1, d, f), lambda i, be, nu: (be[i], 0, 0)),
                pl.BlockSpec((1, d, f), lambda i, be, nu: (be[i], 0, 0)),
                pl.BlockSpec((1, f, d), lambda i, be, nu: (be[i], 0, 0)),
            ],
            out_specs=pl.BlockSpec((rows, d), lambda i, be, nu: (i, 0)),
            scratch_shapes=[pltpu.VMEM((d, f), BF16), pltpu.VMEM((d, f), BF16), pltpu.VMEM((f, d), BF16)],
        ),
        out_shape=jax.ShapeDtypeStruct((n_rows, d), F32),
        compiler_params=_params("arbitrary"),
        name="experts",
    )(block_expert, n_used, x_pad, w_gate, w_up, w_down)


def _combine_kernel(dest_ref, h_ref, meta_ref, opad_ref, y_ref, buf_ref, sem, *, tm, n_tiles):
    s = pl.program_id(0)

    @pl.when(s < n_tiles)
    def _():
        slot = s % 2
        for t in range(tm):
            for k in range(2):
                _row_copy(opad_ref, dest_ref[0, k, t], buf_ref.at[slot, k], t, sem.at[slot]).start(priority=k)

    @pl.when(s > 0)
    def _():
        slot = (s - 1) % 2
        for k in range(2):
            pltpu.make_async_copy(opad_ref.at[pl.ds(0, tm)], buf_ref.at[slot, k], sem.at[slot]).wait()
        meta = meta_ref[...]
        g0 = meta[:, _META_G0:_META_G0 + 1]
        g1 = meta[:, _META_G1:_META_G1 + 1]
        y_ref[...] = h_ref[...] + g0 * buf_ref[slot, 0] + g1 * buf_ref[slot, 1]


def _combine(h2, meta, dest, out_pad):
    t, d = h2.shape
    tm = TM_ROWS
    n_tiles = t // tm
    prev = lambda s: (jnp.maximum(s - 1, 0), 0)
    return pl.pallas_call(
        functools.partial(_combine_kernel, tm=tm, n_tiles=n_tiles),
        grid=(n_tiles + 1,),
        in_specs=[
            pl.BlockSpec((1, 2, tm), lambda s: (jnp.minimum(s, n_tiles - 1), 0, 0), memory_space=pltpu.SMEM),
            pl.BlockSpec((tm, d), prev),
            pl.BlockSpec((tm, LANES), prev),
            pl.BlockSpec(memory_space=pl.ANY),
        ],
        out_specs=pl.BlockSpec((tm, d), prev),
        scratch_shapes=[pltpu.VMEM((2, 2, tm, d), F32), pltpu.SemaphoreType.DMA((2,))],
        out_shape=jax.ShapeDtypeStruct((t, d), F32),
        compiler_params=_params("arbitrary"),
        name="combine",
    )(dest, h2, meta, out_pad)


def _tile_dest(meta, pstart, tm):
    t = meta.shape[0]
    expert = meta[:, _META_E0:_META_E1 + 1].astype(I32)
    rank = meta[:, _META_P0:_META_P1 + 1].astype(I32)
    start = jnp.sum(jnp.where(expert[..., None] == jnp.arange(N_EXPERTS), pstart, 0), axis=-1)
    return (start + rank).reshape(t // tm, tm, 2).transpose(0, 2, 1)


def kernel(x, mem, norm_mix, w_in, da_q_norm, da_k_norm, lambda_q1, lambda_k1, lambda_q2, lambda_k2,
           da_out_norm, gla_gate_w, gla_gate_b, gla_out_norm, w_o, norm_cross, norm_mem, w_cq, w_ckv,
           cross_q_norm, cross_k_norm, w_co, norm_ffn, w_group, b_group, w_expert, b_expert,
           w_e_gate, w_e_up, w_e_down):
    b, s, d = x.shape
    t = b * s
    h = x
    for l in range(norm_mix.shape[0]):
        assert l == 0, "lam_init is fixed for a single layer"
        qt, kda, vt, gq, gk, gv, gg, la = _in_proj(h, norm_mix[l], w_in[l], da_q_norm[l], da_k_norm[l],
                                                   gla_gate_w[l], gla_gate_b[l])
        da = _diff_attn(qt, kda, vt, lambda_q1[l], lambda_k1[l], lambda_q2[l], lambda_k2[l], da_out_norm[l],
                        da_q_norm[l], da_k_norm[l])
        gla = _gla(gq, gk, la, gv, gg, gla_out_norm[l])
        k_mem, v_mem = _mem_kv(mem, norm_mem[l], w_ckv[l], cross_k_norm[l])
        h2, xn, meta, counts = _post(h, da, gla, w_o[l], norm_cross[l], w_cq[l], cross_q_norm[l], k_mem, v_mem,
                                     w_co[l], norm_ffn[l], w_group[l], b_group[l], w_expert[l], b_expert[l])

        rows = EXPERT_ROWS
        sizes = counts[0, :N_EXPERTS].astype(I32)
        padded = (sizes + rows - 1) // rows * rows
        pend = jnp.cumsum(padded)
        pstart = pend - padded
        n_rows = 2 * t + N_EXPERTS * rows
        n_blocks = n_rows // rows
        block_start = jnp.arange(n_blocks, dtype=I32) * rows
        block_expert = jnp.minimum(jnp.sum(pend[None, :] <= block_start[:, None], axis=1), N_EXPERTS - 1).astype(I32)
        n_used = (pend[-1:] // rows).astype(I32)

        h2 = h2.reshape(t, d)
        meta = meta.reshape(t, LANES)
        dest = _tile_dest(meta, pstart, TM_ROWS)
        x_pad = _dispatch(xn.reshape(t, d // 2), dest, sizes, pend, n_rows)
        out_pad = _experts(x_pad, block_expert, n_used, w_e_gate[l], w_e_up[l], w_e_down[l])
        h = _combine(h2, meta, dest, out_pad).reshape(b, s, d)
    return h
```

```python
import functools
import math

import jax
import jax.numpy as jnp
import numpy as np
from jax import lax
from jax.experimental import pallas as pl
from jax.experimental.pallas import tpu as pltpu

F32 = jnp.float32
BF16 = jnp.bfloat16
I32 = jnp.int32
U32 = jnp.uint32
HI16 = np.uint32(0xFFFF0000)

EPS = 1e-6
CHUNK = 64

DA_HEADS = 4
DA_QK_DIM = 64
DA_V_DIM = 128
GLA_HEADS = 4
GLA_K_DIM = 64
GLA_V_DIM = 128
GLA_GATE_RANK = 16
GLA_TAU = 16.0
CROSS_HEADS = 4
N_GROUPS = 4
EXPERTS_PER_GROUP = 8
N_EXPERTS = N_GROUPS * EXPERTS_PER_GROUP
LAM_INIT = 0.8 - 0.6 * math.exp(-0.3 * 0)

LANES = 128
VMEM_LIMIT = 56 * 1024 * 1024

TM_PROJ = 512
TS_GLA = 1024
GLA_GROUP = 4
TM_POST = 512
SUB_POST = 512
TM_ROWS = 512
EXPERT_ROWS = 512

NEG_INF = float("-inf")


def _params(*sem):
    return pltpu.CompilerParams(dimension_semantics=sem, vmem_limit_bytes=VMEM_LIMIT)


def _rms(t, g):
    ms = jnp.mean(t * t, axis=-1, keepdims=True)
    return t * lax.rsqrt(ms + EPS) * g


def _dot(a, b):
    return jnp.dot(a, b, preferred_element_type=F32)


def _dot_nt(a, b):
    return lax.dot_general(a, b, (((1,), (1,)), ((), ())), preferred_element_type=F32)


def _dot_tn(a, b):
    return lax.dot_general(a, b, (((0,), (0,)), ((), ())), preferred_element_type=F32)


def _split_bf16(t):
    hi = t.astype(BF16)
    lo = (t - hi.astype(F32)).astype(BF16)
    return hi, lo


def _mem_kv_kernel(mem_ref, g_ref, w_ref, kn_ref, k_ref, v_ref, *, d, heads):
    mn = _rms(mem_ref[0], g_ref[...]).astype(BF16)
    kv = _dot(mn, w_ref[...])
    hd = d // heads
    scale = hd ** -0.5
    for h in range(heads):
        kh = _rms(kv[:, h * hd:(h + 1) * hd], kn_ref[...]) * scale
        k_ref[0, :, h * hd:(h + 1) * hd] = kh.astype(BF16)
    v_ref[0] = kv[:, d:].astype(BF16)


def _mem_kv(mem, norm_m, w_ckv, kn):
    b, m, d = mem.shape
    return pl.pallas_call(
        functools.partial(_mem_kv_kernel, d=d, heads=CROSS_HEADS),
        grid=(b,),
        in_specs=[
            pl.BlockSpec((1, m, d), lambda i: (i, 0, 0)),
            pl.BlockSpec((1, d), lambda i: (0, 0)),
            pl.BlockSpec((d, 2 * d), lambda i: (0, 0)),
            pl.BlockSpec((1, d // CROSS_HEADS), lambda i: (0, 0)),
        ],
        out_specs=[
            pl.BlockSpec((1, m, d), lambda i: (i, 0, 0)),
            pl.BlockSpec((1, m, d), lambda i: (i, 0, 0)),
        ],
        out_shape=[jax.ShapeDtypeStruct((b, m, d), BF16)] * 2,
        compiler_params=_params("parallel"),
        name="mem_kv",
    )(mem, norm_m.reshape(1, d), w_ckv.astype(BF16), kn.reshape(1, -1))


_QK = DA_HEADS * 2 * DA_QK_DIM
_DAW = DA_HEADS * DA_V_DIM
_GQK = GLA_HEADS * GLA_K_DIM
_GW = GLA_HEADS * GLA_V_DIM
_OFF_DQ = 0
_OFF_DK = _OFF_DQ + _QK
_OFF_DV = _OFF_DK + _QK
_OFF_GQ = _OFF_DV + _DAW
_OFF_GK = _OFF_GQ + _GQK
_OFF_GV = _OFF_GK + _GQK
_OFF_GG = _OFF_GV + _GW
_OFF_GR = _OFF_GG + _GW
_IN_PAD = _OFF_GR + LANES


def _in_proj_kernel(x_ref, g_ref, w_ref, qg_ref, kg_ref, grp_ref, gw_ref, gb_ref,
                    qt_ref, k_ref, vt_ref, gq_ref, gk_ref, gv_ref, gg_ref, la_ref):
    u = _rms(x_ref[0], g_ref[...]).astype(BF16)

    def proj(off, width):
        return _dot(u, w_ref[:, off:off + width])

    def group_norm(p, gain):
        ms = _dot((p * p).astype(BF16), grp_ref[...])
        return p * lax.rsqrt(ms + EPS) * gain

    qn = group_norm(proj(_OFF_DQ, _QK), qg_ref[...]) * (DA_QK_DIM ** -0.5 * math.log2(math.e))
    qt_ref[0, 0] = qn.T.astype(BF16)
    k_ref[0] = group_norm(proj(_OFF_DK, _QK), kg_ref[...]).astype(BF16)
    vt_ref[0, 0] = proj(_OFF_DV, _DAW).T.astype(BF16)
    gq_ref[0] = proj(_OFF_GQ, _GQK) * (GLA_K_DIM ** -0.5)
    gk_ref[0] = proj(_OFF_GK, _GQK)
    gv_ref[0] = proj(_OFF_GV, _GW).astype(BF16)
    gg_ref[0] = proj(_OFF_GG, _GW)
    z = _dot(proj(_OFF_GR, LANES).astype(BF16), gw_ref[...]) + gb_ref[...]
    log_sig = jnp.minimum(z, 0.0) - jnp.log(1.0 + jnp.exp(-jnp.abs(z)))
    la_ref[0] = log_sig * (math.log2(math.e) / GLA_TAU)


def _in_proj(x, norm_g, w_in, da_qn, da_kn, gate_w, gate_b):
    b, s, d = x.shape
    tm = TM_PROJ
    ns = s // tm
    w = jnp.pad(w_in, ((0, 0), (0, _IN_PAD - w_in.shape[1]))).astype(BF16)
    gw = jnp.pad(gate_w, ((0, LANES - GLA_GATE_RANK), (0, 0))).astype(BF16)
    lane = jnp.arange(_QK)
    grp = jnp.where((lane[:, None] // DA_QK_DIM) == (lane[None, :] // DA_QK_DIM),
                    1.0 / DA_QK_DIM, 0.0).astype(BF16)
    const = lambda shape: pl.BlockSpec(shape, lambda i, j: (0,) * len(shape))
    tile = lambda width: pl.BlockSpec((1, tm, width), lambda i, j: (i, j, 0))
    tile_t = lambda width: pl.BlockSpec((1, 1, width, tm), lambda i, j: (i, j, 0, 0))
    return pl.pallas_call(
        _in_proj_kernel,
        grid=(b, ns),
        in_specs=[tile(d), const((1, d)), const((d, _IN_PAD)), const((1, _QK)), const((1, _QK)),
                  const((_QK, _QK)), const((LANES, _GQK)), const((1, _GQK))],
        out_specs=[tile_t(_QK), tile(_QK), tile_t(_DAW), tile(_GQK), tile(_GQK), tile(_GW), tile(_GW),
                   tile(_GQK)],
        out_shape=[
            jax.ShapeDtypeStruct((b, ns, _QK, tm), BF16),
            jax.ShapeDtypeStruct((b, s, _QK), BF16),
            jax.ShapeDtypeStruct((b, ns, _DAW, tm), BF16),
            jax.ShapeDtypeStruct((b, s, _GQK), F32),
            jax.ShapeDtypeStruct((b, s, _GQK), F32),
            jax.ShapeDtypeStruct((b, s, _GW), BF16),
            jax.ShapeDtypeStruct((b, s, _GW), F32),
            jax.ShapeDtypeStruct((b, s, _GQK), F32),
        ],
        compiler_params=_params("parallel", "parallel"),
        name="in_proj",
    )(x, norm_g.reshape(1, d), w, jnp.tile(da_qn, 2 * DA_HEADS).reshape(1, _QK),
      jnp.tile(da_kn, 2 * DA_HEADS).reshape(1, _QK), grp, gw, gate_b.reshape(1, _GQK))


def _split_q(qt):
    row = lax.broadcasted_iota(I32, qt.shape, 0)
    zero = jnp.zeros_like(qt)
    return jnp.where(row < DA_QK_DIM, qt, zero), jnp.where(row >= DA_QK_DIM, qt, zero)


def _chunk_causal_mask(blk):
    key_chunk = lax.broadcasted_iota(I32, (blk, blk), 0) // CHUNK
    qry_chunk = lax.broadcasted_iota(I32, (blk, blk), 1) // CHUNK
    return key_chunk <= qry_chunk


def _diff_attn_finish(lq1_ref, lk1_ref, lq2_ref, lk2_ref, gain_ref, a1, l1, a2, l2):
    lam = (jnp.exp(jnp.sum(lq1_ref[...] * lk1_ref[...], axis=-1, keepdims=True))
           - jnp.exp(jnp.sum(lq2_ref[...] * lk2_ref[...], axis=-1, keepdims=True)) + LAM_INIT)
    o = a1 / l1 - lam * (a2 / l2)
    ms = jnp.mean(o * o, axis=0, keepdims=True)
    o = o * lax.rsqrt(ms + EPS) * gain_ref[...] * (1.0 - LAM_INIT)
    return o.T.astype(BF16)


def _diff_attn_bounded_kernel(lq1_ref, lk1_ref, lq2_ref, lk2_ref, gain_ref, qt_ref, k_ref, vt_ref, out_ref,
                              s_ref, l1_ref, a1_ref, l2_ref, a2_ref, *, blk, nb):
    stats = ((l1_ref, a1_ref), (l2_ref, a2_ref))
    mask = _chunk_causal_mask(blk)

    def reset():
        for l_ref, a_ref in stats:
            l_ref[...] = jnp.zeros(l_ref.shape, F32)
            a_ref[...] = jnp.zeros(a_ref.shape, F32)

    def scores(q, j, slot):
        kb = k_ref[0, pl.ds(pl.multiple_of(j * blk, blk), blk), :]
        s_ref[slot, 0] = _dot(kb, q[0])
        s_ref[slot, 1] = _dot(kb, q[1])

    def consume(j, slot, masked):
        vb = vt_ref[0, j]
        for m, (l_ref, a_ref) in enumerate(stats):
            s = s_ref[slot, m]
            if masked:
                s = jnp.where(mask, s, NEG_INF)
            p = jnp.exp2(s)
            l_ref[...] += jnp.sum(p, axis=0, keepdims=True)
            a_ref[...] += _dot(vb, p.astype(BF16))

    def step(q, j, slot):
        scores(q, j + 1, 1 - slot)
        consume(j, slot, False)

    reset()
    q = _split_q(qt_ref[0, 0])
    slot = 0
    scores(q, 0, slot)
    for qi in range(nb):
        def pair(i, carry, q=q, slot=slot):
            step(q, 2 * i, slot)
            step(q, 2 * i + 1, 1 - slot)
            return carry

        if qi // 2:
            lax.fori_loop(0, qi // 2, pair, 0)
        if qi % 2:
            step(q, qi - 1, slot)
            slot = 1 - slot
        if qi + 1 < nb:
            q = _split_q(qt_ref[0, qi + 1])
            scores(q, 0, 1 - slot)
        consume(qi, slot, True)
        out_ref[0, qi * blk:(qi + 1) * blk, :] = _diff_attn_finish(
            lq1_ref, lk1_ref, lq2_ref, lk2_ref, gain_ref, a1_ref[...], l1_ref[...], a2_ref[...], l2_ref[...])
        if qi + 1 < nb:
            reset()
        slot = 1 - slot


def _diff_attn_online_kernel(lq1_ref, lk1_ref, lq2_ref, lk2_ref, gain_ref, qt_ref, k_ref, vt_ref, out_ref,
                             m1_ref, l1_ref, a1_ref, m2_ref, l2_ref, a2_ref, *, blk):
    qi = pl.program_id(2)
    q1, q2 = _split_q(qt_ref[0, 0])

    for m_ref, l_ref, a_ref in ((m1_ref, l1_ref, a1_ref), (m2_ref, l2_ref, a2_ref)):
        m_ref[...] = jnp.full(m_ref.shape, NEG_INF, F32)
        l_ref[...] = jnp.zeros(l_ref.shape, F32)
        a_ref[...] = jnp.zeros(a_ref.shape, F32)

    def update(s, vb, m_ref, l_ref, a_ref):
        m_old = m_ref[...]
        m_new = jnp.maximum(m_old, jnp.max(s, axis=0, keepdims=True))
        alpha = jnp.exp2(m_old - m_new)
        p = jnp.exp2(s - m_new)
        l_ref[...] = alpha * l_ref[...] + jnp.sum(p, axis=0, keepdims=True)
        a_ref[...] = alpha * a_ref[...] + _dot(vb, p.astype(BF16))
        m_ref[...] = m_new

    def block(j, mask):
        kb = k_ref[0, pl.ds(pl.multiple_of(j * blk, blk), blk), :]
        vb = vt_ref[0, j]
        s1 = _dot(kb, q1)
        s2 = _dot(kb, q2)
        if mask is not None:
            s1 = jnp.where(mask, s1, NEG_INF)
            s2 = jnp.where(mask, s2, NEG_INF)
        update(s1, vb, m1_ref, l1_ref, a1_ref)
        update(s2, vb, m2_ref, l2_ref, a2_ref)

    def body(j, carry):
        block(j, None)
        return carry

    lax.fori_loop(0, qi, body, 0)
    block(qi, _chunk_causal_mask(blk))
    out_ref[0] = _diff_attn_finish(lq1_ref, lk1_ref, lq2_ref, lk2_ref, gain_ref,
                                   a1_ref[...], l1_ref[...], a2_ref[...], l2_ref[...])


SCORE_BOUND = 60.0


def _diff_attn(qt, k, vt, lq1, lk1, lq2, lk2, da_on, da_qn, da_kn):
    b, nb, _, blk = qt.shape
    s = nb * blk
    stat = lambda: pltpu.VMEM((1, blk), F32)
    acc = lambda: pltpu.VMEM((DA_V_DIM, blk), F32)

    args = (lq1.reshape(1, -1), lk1.reshape(1, -1), lq2.reshape(1, -1), lk2.reshape(1, -1),
            da_on.reshape(-1, 1), qt, k, vt)
    out_shape = jax.ShapeDtypeStruct((b, s, _DAW), BF16)
    head = lambda *trailing: (lambda i, h: (i, 0, h) + trailing)
    vec2 = lambda: pl.BlockSpec((1, DA_QK_DIM), lambda i, h: (0, 0))
    bounded = pl.pallas_call(
        functools.partial(_diff_attn_bounded_kernel, blk=blk, nb=nb),
        grid=(b, DA_HEADS),
        in_specs=[
            vec2(), vec2(), vec2(), vec2(),
            pl.BlockSpec((DA_V_DIM, 1), lambda i, h: (0, 0)),
            pl.BlockSpec((1, nb, 2 * DA_QK_DIM, blk), head(0)),
            pl.BlockSpec((1, s, 2 * DA_QK_DIM), head()),
            pl.BlockSpec((1, nb, DA_V_DIM, blk), head(0)),
        ],
        out_specs=pl.BlockSpec((1, s, DA_V_DIM), head()),
        out_shape=out_shape,
        scratch_shapes=[pltpu.VMEM((2, 2, blk, blk), F32), stat(), acc(), stat(), acc()],
        compiler_params=_params("parallel", "parallel"),
        name="diff_attn",
    )
    vec3 = lambda: pl.BlockSpec((1, DA_QK_DIM), lambda i, h, q: (0, 0))
    online = pl.pallas_call(
        functools.partial(_diff_attn_online_kernel, blk=blk),
        grid=(b, DA_HEADS, nb),
        in_specs=[
            vec3(), vec3(), vec3(), vec3(),
            pl.BlockSpec((DA_V_DIM, 1), lambda i, h, q: (0, 0)),
            pl.BlockSpec((1, 1, 2 * DA_QK_DIM, blk), lambda i, h, q: (i, q, h, 0)),
            pl.BlockSpec((1, s, 2 * DA_QK_DIM), lambda i, h, q: (i, 0, h)),
            pl.BlockSpec((1, nb, DA_V_DIM, blk), lambda i, h, q: (i, 0, h, 0)),
        ],
        out_specs=pl.BlockSpec((1, blk, DA_V_DIM), lambda i, h, q: (i, q, h)),
        out_shape=out_shape,
        scratch_shapes=[stat(), stat(), acc(), stat(), stat(), acc()],
        compiler_params=_params("parallel", "parallel", "parallel"),
        name="diff_attn_online",
    )
    bound = (1.01 * DA_QK_DIM ** 0.5 * math.log2(math.e)) * jnp.max(jnp.abs(da_qn)) * jnp.max(jnp.abs(da_kn))
    return lax.cond(bound <= SCORE_BOUND, bounded, online, *args)


def _gla_kernel(q_ref, k_ref, la_ref, v_ref, g_ref, gain_ref, out_ref, st_ref, *, ts, group):
    @pl.when(pl.program_id(1) == 0)
    def _():
        st_ref[...] = jnp.zeros(st_ref.shape, F32)

    c = CHUNK
    rows = group * c
    hk, hv = _GQK, _GW
    r = lax.broadcasted_iota(I32, (rows, rows), 0)
    cc = lax.broadcasted_iota(I32, (rows, rows), 1)
    tri = jnp.where(jnp.logical_and(r // c == cc // c, r >= cc), 1.0, 0.0).astype(BF16)
    bd_k = (lax.broadcasted_iota(I32, (hk, hk), 0) // GLA_K_DIM
            == lax.broadcasted_iota(I32, (hk, hk), 1) // GLA_K_DIM)
    bd_v = (lax.broadcasted_iota(I32, (hk, hv), 0) // GLA_K_DIM
            == lax.broadcasted_iota(I32, (hk, hv), 1) // GLA_V_DIM)
    bd_vt = (lax.broadcasted_iota(I32, (hv, hk), 0) // GLA_V_DIM
             == lax.broadcasted_iota(I32, (hv, hk), 1) // GLA_K_DIM)
    lower = (lax.broadcasted_iota(I32, (c, hk), 0)
             >= lax.broadcasted_iota(I32, (c, hk), 1) % c)

    def tiled(t, mask):
        t4 = jnp.concatenate([t] * GLA_HEADS, axis=0)
        return jnp.where(mask, t4, jnp.zeros_like(t4))

    def chunk_row(t, row):
        return jnp.concatenate([jnp.broadcast_to(t[i * c + row:i * c + row + 1, :], (c, hk)) for i in range(group)],
                               axis=0)

    def body(gi, carry):
        sl = pl.ds(pl.multiple_of(gi * rows, rows), rows)
        la_hi, la_lo = _split_bf16(la_ref[0, sl, :])
        big_l = _dot(tri, la_hi) + _dot(tri, la_lo)
        l_end = chunk_row(big_l, c - 1)
        lc = big_l - chunk_row(big_l, c // 2 - 1)
        e_pos = jnp.exp2(lc)
        e_neg = jnp.exp2(-lc)
        q = q_ref[0, sl, :]
        k = k_ref[0, sl, :]
        v = v_ref[0, sl, :]
        q_pos = (q * e_pos).astype(BF16)
        q_neg = (q * e_neg).astype(BF16)
        k_pos = (k * e_pos).astype(BF16)
        k_neg = (k * e_neg).astype(BF16)
        q_in = (q * jnp.exp2(big_l)).astype(BF16)
        k_out = (k * jnp.exp2(l_end - big_l)).astype(BF16)
        decay = jnp.exp2(l_end)

        o_intra, u_t = [], []
        for i in range(group):
            cs = slice(i * c, (i + 1) * c)
            a_past = _dot_nt(q_pos[cs], tiled(k_neg[cs], bd_k))
            a_fut = _dot_nt(q_neg[cs], tiled(k_pos[cs], bd_k))
            a = jnp.where(lower, a_past, a_fut).astype(BF16)
            o_intra.append(_dot(a, tiled(v[cs], bd_v)))
            u_t.append(jnp.where(bd_vt, _dot_tn(v[cs], k_out[cs]), 0.0))

        st = st_ref[...]
        o_inter = []
        for i in range(group):
            cs = slice(i * c, (i + 1) * c)
            o_inter.append(_dot_nt(q_in[cs], st.astype(BF16)))
            st = st * decay[i * c:i * c + 1, :] + u_t[i]
        st_ref[...] = st

        o = jnp.concatenate(o_intra, axis=0) + jnp.concatenate(o_inter, axis=0)
        g = g_ref[0, sl, :]
        silu = g / (1.0 + jnp.exp(-g))
        for h in range(GLA_HEADS):
            hs = slice(h * GLA_V_DIM, (h + 1) * GLA_V_DIM)
            out_ref[0, sl, hs] = (_rms(o[:, hs], gain_ref[...]) * silu[:, hs]).astype(BF16)
        return carry

    lax.fori_loop(0, ts // rows, body, 0)


def _gla(gq, gk, la, gv, gg, gla_on):
    b, s, _ = gq.shape
    ts = TS_GLA
    tile = lambda width: pl.BlockSpec((1, ts, width), lambda i, j: (i, j, 0))
    return pl.pallas_call(
        functools.partial(_gla_kernel, ts=ts, group=GLA_GROUP),
        grid=(b, s // ts),
        in_specs=[tile(_GQK), tile(_GQK), tile(_GQK), tile(_GW), tile(_GW),
                  pl.BlockSpec((1, GLA_V_DIM), lambda i, j: (0, 0))],
        out_specs=tile(_GW),
        out_shape=jax.ShapeDtypeStruct((b, s, _GW), BF16),
        scratch_shapes=[pltpu.VMEM((_GW, _GQK), F32)],
        compiler_params=_params("parallel", "arbitrary"),
        name="gla",
    )(gq, gk, la, gv, gg, gla_on.reshape(1, -1))


_META_E0, _META_E1, _META_G0, _META_G1, _META_P0, _META_P1 = range(6)
_EXP_LANE0 = N_GROUPS


def _post_kernel(x_ref, da_ref, gla_ref, wo_ref, gc_ref, wq_ref, qn_ref, km_ref, vm_ref, wco_ref,
                 gf_ref, wr_ref, br_ref,
                 h_ref, xn_ref, meta_ref, ids_ref, cnt_ref, *, d, tm, sub):
    first = jnp.logical_and(pl.program_id(0) == 0, pl.program_id(1) == 0)

    @pl.when(first)
    def _():
        cnt_ref[...] = jnp.zeros(cnt_ref.shape, F32)

    half = d // 2
    hd = d // CROSS_HEADS
    lane = lax.broadcasted_iota(I32, (sub, LANES), 1)
    big = jnp.int32(LANES)
    strict_lower = jnp.where(lax.broadcasted_iota(I32, (sub, sub), 0) > lax.broadcasted_iota(I32, (sub, sub), 1),
                             1.0, 0.0).astype(BF16)

    def lane_argmax(vals):
        m = jnp.max(vals, axis=-1, keepdims=True)
        idx = jnp.min(jnp.where(vals == m, lane, big), axis=-1, keepdims=True)
        return m, idx

    def rows(rs, base):
        h1 = x_ref[0, rs, :] + _dot(da_ref[0, rs, :], wo_ref[:half, :]) + _dot(gla_ref[0, rs, :], wo_ref[half:, :])

        u = _rms(h1, gc_ref[...]).astype(BF16)
        q = _dot(u, wq_ref[...])
        heads = []
        for h in range(CROSS_HEADS):
            hs = slice(h * hd, (h + 1) * hd)
            qh = _rms(q[:, hs], qn_ref[...]).astype(BF16)
            sc = _dot_nt(qh, km_ref[0, :, hs])
            sc = sc - jnp.max(sc, axis=-1, keepdims=True)
            p = jnp.exp(sc)
            p = p / jnp.sum(p, axis=-1, keepdims=True)
            heads.append(_dot(p.astype(BF16), vm_ref[0, :, hs]))
        o = jnp.concatenate(heads, axis=-1).astype(BF16)
        h2 = h1 + _dot(o, wco_ref[...])
        h_ref[0, rs, :] = h2

        xn = _rms(h2, gf_ref[...]).astype(BF16)
        bits = lax.bitcast_convert_type(xn.astype(F32), U32)
        xn_ref[0, rs, :] = (bits[:, :half] >> 16) | (bits[:, half:] & HI16)
        logits = _dot(xn, wr_ref[...]) + br_ref[...]

        lg = jnp.where(lane < N_GROUPS, logits, NEG_INF)
        g_max, g_sel = lane_argmax(lg)
        p_g = 1.0 / jnp.sum(jnp.exp(lg - g_max), axis=-1, keepdims=True)
        e_lo = _EXP_LANE0 + g_sel * EXPERTS_PER_GROUP
        in_group = jnp.logical_and(lane >= e_lo, lane < e_lo + EXPERTS_PER_GROUP)
        le = jnp.where(in_group, logits, NEG_INF)
        m1, i1 = lane_argmax(le)
        m2, i2 = lane_argmax(jnp.where(lane == i1, NEG_INF, le))
        e2 = jnp.exp(m2 - m1)
        gate0 = p_g / (1.0 + e2)
        gate1 = p_g * e2 / (1.0 + e2)
        e0 = i1 - _EXP_LANE0
        e1 = i2 - _EXP_LANE0

        hot0 = lane == e0
        hot1 = lane == e1
        onehot = jnp.where(jnp.logical_or(hot0, hot1), 1.0, 0.0)
        before = _dot(strict_lower, onehot.astype(BF16)) + base
        pos0 = jnp.sum(jnp.where(hot0, before, 0.0), axis=-1, keepdims=True)
        pos1 = jnp.sum(jnp.where(hot1, before, 0.0), axis=-1, keepdims=True)

        meta = jnp.zeros(logits.shape, F32)
        for idx, val in ((_META_E0, e0.astype(F32)), (_META_E1, e1.astype(F32)), (_META_G0, gate0),
                         (_META_G1, gate1), (_META_P0, pos0), (_META_P1, pos1)):
            meta = jnp.where(lane == idx, val, meta)
        meta_ref[0, rs, :] = meta
        ids_ref[0, 0, :, rs] = meta.T[:ids_ref.shape[2], :]
        return base + jnp.sum(onehot, axis=0, keepdims=True)

    base = cnt_ref[0:1, :]
    for r0 in range(0, tm, sub):
        base = rows(slice(r0, r0 + sub), base)
    cnt_ref[...] = jnp.broadcast_to(base, cnt_ref.shape)


def _post(x, da, gla, w_o, norm_cross, w_cq, cross_qn, k_mem, v_mem, w_co, norm_ffn, w_group, b_group,
          w_expert, b_expert):
    b, s, d = x.shape
    tm = TM_POST
    m = k_mem.shape[1]
    w_r = jnp.pad(jnp.concatenate([w_group, w_expert], axis=1), ((0, 0), (0, LANES - N_GROUPS - N_EXPERTS)))
    b_r = jnp.pad(jnp.concatenate([b_group, b_expert]), (0, LANES - N_GROUPS - N_EXPERTS)).reshape(1, LANES)
    const = lambda shape: pl.BlockSpec(shape, lambda i, j: (0,) * len(shape))
    tile = lambda width: pl.BlockSpec((1, tm, width), lambda i, j: (i, j, 0))
    per_b = lambda: pl.BlockSpec((1, m, d), lambda i, j: (i, 0, 0))
    return pl.pallas_call(
        functools.partial(_post_kernel, d=d, tm=tm, sub=SUB_POST),
        grid=(b, s // tm),
        in_specs=[tile(d), tile(d // 2), tile(d // 2), const((d, d)), const((1, d)), const((d, d)),
                  const((1, d // CROSS_HEADS)), per_b(), per_b(), const((d, d)), const((1, d)),
                  const((d, LANES)), const((1, LANES))],
        out_specs=[tile(d), tile(d // 2), tile(LANES), pl.BlockSpec((1, 1, 8, tm), lambda i, j: (i, j, 0, 0)),
                   const((8, LANES))],
        out_shape=[
            jax.ShapeDtypeStruct((b, s, d), F32),
            jax.ShapeDtypeStruct((b, s, d // 2), U32),
            jax.ShapeDtypeStruct((b, s, LANES), F32),
            jax.ShapeDtypeStruct((b, s // tm, 8, tm), F32),
            jax.ShapeDtypeStruct((8, LANES), F32),
        ],
        compiler_params=_params("arbitrary", "arbitrary"),
        name="post",
    )(x, da, gla, w_o.astype(BF16), norm_cross.reshape(1, d), w_cq.astype(BF16), cross_qn.reshape(1, -1),
      k_mem, v_mem, w_co.astype(BF16), norm_ffn.reshape(1, d), w_r.astype(BF16), b_r)


def _row_copy(src_ref, src_row, dst_ref, dst_row, sem):
    return pltpu.make_async_copy(src_ref.at[pl.ds(src_row, 1)], dst_ref.at[pl.ds(dst_row, 1)], sem)


def _dispatch_kernel(size_ref, pend_ref, dest_ref, xn_ref, xpad_ref, zero_ref, sem, zsem, *, tm):
    i = pl.program_id(0)

    @pl.when(i == 0)
    def _():
        zero_ref[...] = jnp.zeros(zero_ref.shape, zero_ref.dtype)
        rows = zero_ref.shape[0]
        n_blocks = xpad_ref.shape[0] // rows
        n_used = pend_ref[N_EXPERTS - 1] // rows

        def zero_block(blk):
            return pltpu.make_async_copy(zero_ref, xpad_ref.at[pl.ds(pl.multiple_of(blk * rows, rows), rows)], zsem)

        def last_block(e, fn):
            @pl.when(size_ref[e] > 0)
            def _():
                fn(zero_block(pend_ref[e] // rows - 1))

        for fn in (lambda cp: cp.start(), lambda cp: cp.wait()):
            lax.fori_loop(0, N_EXPERTS, lambda e, c: (last_block(e, fn), c)[1], 0)
            lax.fori_loop(n_used, n_blocks, lambda blk, c: (fn(zero_block(blk)), c)[1], 0)

    def start(t, carry):
        for k in range(2):
            _row_copy(xn_ref, t, xpad_ref, dest_ref[0, k, t], sem).start(priority=k)
        return carry

    lax.fori_loop(0, tm, start, 0, unroll=True)
    for _ in range(2):
        pltpu.make_async_copy(xn_ref, xpad_ref.at[pl.ds(0, tm)], sem).wait()


def _dispatch(xn, dest, sizes, pend, n_rows):
    t, w = xn.shape
    tm = TM_ROWS
    return pl.pallas_call(
        functools.partial(_dispatch_kernel, tm=tm),
        grid_spec=pltpu.PrefetchScalarGridSpec(
            num_scalar_prefetch=2,
            grid=(t // tm,),
            in_specs=[
                pl.BlockSpec((1, 2, tm), lambda i, *_: (i, 0, 0), memory_space=pltpu.SMEM),
                pl.BlockSpec((tm, w), lambda i, *_: (i, 0)),
            ],
            out_specs=pl.BlockSpec(memory_space=pl.ANY),
            scratch_shapes=[pltpu.VMEM((EXPERT_ROWS, w), xn.dtype), pltpu.SemaphoreType.DMA,
                            pltpu.SemaphoreType.DMA],
        ),
        out_shape=jax.ShapeDtypeStruct((n_rows, w), xn.dtype),
        compiler_params=_params("arbitrary"),
        name="dispatch",
    )(sizes, pend, dest, xn)


def _experts_kernel(be_ref, nused_ref, x_ref, wg_ref, wu_ref, wd_ref, out_ref, wg_s, wu_s, wd_s):
    i = pl.program_id(0)
    used = i < nused_ref[0]
    new_expert = jnp.logical_or(i == 0, be_ref[i] != be_ref[jnp.maximum(i - 1, 0)])

    @pl.when(jnp.logical_and(used, new_expert))
    def _():
        wg_s[...] = wg_ref[0].astype(BF16)
        wu_s[...] = wu_ref[0].astype(BF16)
        wd_s[...] = wd_ref[0].astype(BF16)

    @pl.when(used)
    def _():
        words = x_ref[...]
        half = words.shape[1]
        lo = lax.bitcast_convert_type(words << 16, F32).astype(BF16)
        hi = lax.bitcast_convert_type(words & HI16, F32).astype(BF16)
        gate = _dot(lo, wg_s[:half, :]) + _dot(hi, wg_s[half:, :])
        up = _dot(lo, wu_s[:half, :]) + _dot(hi, wu_s[half:, :])
        hid = gate / (1.0 + jnp.exp(-gate)) * up
        out_ref[...] = _dot(hid.astype(BF16), wd_s[...])

    @pl.when(jnp.logical_not(used))
    def _():
        out_ref[...] = jnp.zeros(out_ref.shape, F32)


def _experts(x_pad, block_expert, n_used, w_gate, w_up, w_down):
    n_rows = x_pad.shape[0]
    _, d, f = w_gate.shape
    rows = EXPERT_ROWS
    row_blk = lambda i, be, nu: (jnp.minimum(i, nu[0] - 1), 0)
    return pl.pallas_call(
        _experts_kernel,
        grid_spec=pltpu.PrefetchScalarGridSpec(
            num_scalar_prefetch=2,
            grid=(n_rows // rows,),
            in_specs=[
                pl.BlockSpec((rows, d // 2), row_blk),
                pl.BlockSpec((1, d, f), lambda i, be, nu: (be[i], 0, 0)),
                pl.BlockSpec((1, d, f), lambda i, be, nu: (be[i], 0, 0)),
                pl.BlockSpec((1, f, d), lambda i, be, nu: (be[i], 0, 0)),
            ],
            out_specs=pl.BlockSpec((rows, d), lambda i, be, nu: (i, 0)),
            scratch_shapes=[pltpu.VMEM((d, f), BF16), pltpu.VMEM((d, f), BF16), pltpu.VMEM((f, d), BF16)],
        ),
        out_shape=jax.ShapeDtypeStruct((n_rows, d), F32),
        compiler_params=_params("arbitrary"),
        name="experts",
    )(block_expert, n_used, x_pad, w_gate, w_up, w_down)


def _combine_kernel(dest_ref, h_ref, meta_ref, opad_ref, y_ref, buf_ref, sem, *, tm, n_tiles):
    s = pl.program_id(0)

    @pl.when(s < n_tiles)
    def _():
        slot = s % 2
        for t in range(tm):
            for k in range(2):
                _row_copy(opad_ref, dest_ref[0, k, t], buf_ref.at[slot, k], t, sem.at[slot]).start(priority=k)

    @pl.when(s > 0)
    def _():
        slot = (s - 1) % 2
        for k in range(2):
            pltpu.make_async_copy(opad_ref.at[pl.ds(0, tm)], buf_ref.at[slot, k], sem.at[slot]).wait()
        meta = meta_ref[...]
        g0 = meta[:, _META_G0:_META_G0 + 1]
        g1 = meta[:, _META_G1:_META_G1 + 1]
        y_ref[...] = h_ref[...] + g0 * buf_ref[slot, 0] + g1 * buf_ref[slot, 1]


def _combine(h2, meta, dest, out_pad):
    t, d = h2.shape
    tm = TM_ROWS
    n_tiles = t // tm
    prev = lambda s: (jnp.maximum(s - 1, 0), 0)
    return pl.pallas_call(
        functools.partial(_combine_kernel, tm=tm, n_tiles=n_tiles),
        grid=(n_tiles + 1,),
        in_specs=[
            pl.BlockSpec((1, 2, tm), lambda s: (jnp.minimum(s, n_tiles - 1), 0, 0), memory_space=pltpu.SMEM),
            pl.BlockSpec((tm, d), prev),
            pl.BlockSpec((tm, LANES), prev),
            pl.BlockSpec(memory_space=pl.ANY),
        ],
        out_specs=pl.BlockSpec((tm, d), prev),
        scratch_shapes=[pltpu.VMEM((2, 2, tm, d), F32), pltpu.SemaphoreType.DMA((2,))],
        out_shape=jax.ShapeDtypeStruct((t, d), F32),
        compiler_params=_params("arbitrary"),
        name="combine",
    )(dest, h2, meta, out_pad)


def _tile_dest(ids, pstart):
    expert = ids[:, _META_E0:_META_E1 + 1].astype(I32)
    rank = ids[:, _META_P0:_META_P1 + 1].astype(I32)
    start = sum(jnp.where(expert == e, pstart[e], 0) for e in range(N_EXPERTS))
    return start + rank


def kernel(x, mem, norm_mix, w_in, da_q_norm, da_k_norm, lambda_q1, lambda_k1, lambda_q2, lambda_k2,
           da_out_norm, gla_gate_w, gla_gate_b, gla_out_norm, w_o, norm_cross, norm_mem, w_cq, w_ckv,
           cross_q_norm, cross_k_norm, w_co, norm_ffn, w_group, b_group, w_expert, b_expert,
           w_e_gate, w_e_up, w_e_down):
    b, s, d = x.shape
    t = b * s
    h = x
    for l in range(norm_mix.shape[0]):
        assert l == 0, "lam_init is fixed for a single layer"
        qt, kda, vt, gq, gk, gv, gg, la = _in_proj(h, norm_mix[l], w_in[l], da_q_norm[l], da_k_norm[l],
                                                   gla_gate_w[l], gla_gate_b[l])
        da = _diff_attn(qt, kda, vt, lambda_q1[l], lambda_k1[l], lambda_q2[l], lambda_k2[l], da_out_norm[l],
                        da_q_norm[l], da_k_norm[l])
        gla = _gla(gq, gk, la, gv, gg, gla_out_norm[l])
        k_mem, v_mem = _mem_kv(mem, norm_mem[l], w_ckv[l], cross_k_norm[l])
        h2, xn, meta, ids, counts = _post(h, da, gla, w_o[l], norm_cross[l], w_cq[l], cross_q_norm[l], k_mem,
                                          v_mem, w_co[l], norm_ffn[l], w_group[l], b_group[l], w_expert[l],
                                          b_expert[l])

        rows = EXPERT_ROWS
        sizes = counts[0, :N_EXPERTS].astype(I32)
        padded = (sizes + rows - 1) // rows * rows
        pend = jnp.cumsum(padded)
        pstart = pend - padded
        n_rows = 2 * t + N_EXPERTS * rows
        n_blocks = n_rows // rows
        block_start = jnp.arange(n_blocks, dtype=I32) * rows
        block_expert = jnp.minimum(jnp.sum(pend[None, :] <= block_start[:, None], axis=1), N_EXPERTS - 1).astype(I32)
        n_used = (pend[-1:] // rows).astype(I32)

        h2 = h2.reshape(t, d)
        meta = meta.reshape(t, LANES)
        assert TM_POST == TM_ROWS, "router tiles double as dispatch / combine tiles"
        dest = _tile_dest(ids.reshape(t // TM_ROWS, 8, TM_ROWS), pstart)
        x_pad = _dispatch(xn.reshape(t, d // 2), dest, sizes, pend, n_rows)
        out_pad = _experts(x_pad, block_expert, n_used, w_e_gate[l], w_e_up[l], w_e_down[l])
        h = _combine(h2, meta, dest, out_pad).reshape(b, s, d)
    return h
```

```python
import functools
import math

import jax
import jax.numpy as jnp
import numpy as np
from jax import lax
from jax.experimental import pallas as pl
from jax.experimental.pallas import tpu as pltpu

F32 = jnp.float32
BF16 = jnp.bfloat16
I32 = jnp.int32
U32 = jnp.uint32
HI16 = np.uint32(0xFFFF0000)

EPS = 1e-6
CHUNK = 64

DA_HEADS = 4
DA_QK_DIM = 64
DA_V_DIM = 128
GLA_HEADS = 4
GLA_K_DIM = 64
GLA_V_DIM = 128
GLA_GATE_RANK = 16
GLA_TAU = 16.0
CROSS_HEADS = 4
N_GROUPS = 4
EXPERTS_PER_GROUP = 8
N_EXPERTS = N_GROUPS * EXPERTS_PER_GROUP
LAM_INIT = 0.8 - 0.6 * math.exp(-0.3 * 0)

LANES = 128
VMEM_LIMIT = 56 * 1024 * 1024

TM_PROJ = 512
TS_GLA = 1024
GLA_GROUP = 4
TM_POST = 1024
SUB_POST = 1024
TM_ROWS = 1024
RIDER_ROWS = 256
EXPERT_ROWS = 512

NEG_INF = float("-inf")


def _params(*sem):
    return pltpu.CompilerParams(dimension_semantics=sem, vmem_limit_bytes=VMEM_LIMIT)


def _rms(t, g):
    ms = jnp.mean(t * t, axis=-1, keepdims=True)
    return t * lax.rsqrt(ms + EPS) * g


def _dot(a, b):
    return jnp.dot(a, b, preferred_element_type=F32)


def _dot_nt(a, b):
    return lax.dot_general(a, b, (((1,), (1,)), ((), ())), preferred_element_type=F32)


def _dot_tn(a, b):
    return lax.dot_general(a, b, (((0,), (0,)), ((), ())), preferred_element_type=F32)


def _split_bf16(t):
    hi = t.astype(BF16)
    lo = (t - hi.astype(F32)).astype(BF16)
    return hi, lo


def _mem_kv_kernel(mem_ref, g_ref, w_ref, kn_ref, k_ref, v_ref, *, d, heads):
    mn = _rms(mem_ref[0], g_ref[...]).astype(BF16)
    kv = _dot(mn, w_ref[...])
    hd = d // heads
    scale = hd ** -0.5
    for h in range(heads):
        kh = _rms(kv[:, h * hd:(h + 1) * hd], kn_ref[...]) * scale
        k_ref[0, :, h * hd:(h + 1) * hd] = kh.astype(BF16)
    v_ref[0] = kv[:, d:].astype(BF16)


def _mem_kv(mem, norm_m, w_ckv, kn):
    b, m, d = mem.shape
    return pl.pallas_call(
        functools.partial(_mem_kv_kernel, d=d, heads=CROSS_HEADS),
        grid=(b,),
        in_specs=[
            pl.BlockSpec((1, m, d), lambda i: (i, 0, 0)),
            pl.BlockSpec((1, d), lambda i: (0, 0)),
            pl.BlockSpec((d, 2 * d), lambda i: (0, 0)),
            pl.BlockSpec((1, d // CROSS_HEADS), lambda i: (0, 0)),
        ],
        out_specs=[
            pl.BlockSpec((1, m, d), lambda i: (i, 0, 0)),
            pl.BlockSpec((1, m, d), lambda i: (i, 0, 0)),
        ],
        out_shape=[jax.ShapeDtypeStruct((b, m, d), BF16)] * 2,
        compiler_params=_params("parallel"),
        name="mem_kv",
    )(mem, norm_m.reshape(1, d), w_ckv.astype(BF16), kn.reshape(1, -1))


_QK = DA_HEADS * 2 * DA_QK_DIM
_DAW = DA_HEADS * DA_V_DIM
_GQK = GLA_HEADS * GLA_K_DIM
_GW = GLA_HEADS * GLA_V_DIM
_OFF_DQ = 0
_OFF_DK = _OFF_DQ + _QK
_OFF_DV = _OFF_DK + _QK
_OFF_GQ = _OFF_DV + _DAW
_OFF_GK = _OFF_GQ + _GQK
_OFF_GV = _OFF_GK + _GQK
_OFF_GG = _OFF_GV + _GW
_OFF_GR = _OFF_GG + _GW
_IN_PAD = _OFF_GR + LANES


def _in_proj_kernel(x_ref, g_ref, w_ref, qg_ref, kg_ref, grp_ref, gw_ref, gb_ref,
                    qt_ref, k_ref, vt_ref, gq_ref, gk_ref, gv_ref, gg_ref, la_ref):
    u = _rms(x_ref[0], g_ref[...]).astype(BF16)

    def proj(off, width):
        return _dot(u, w_ref[:, off:off + width])

    def group_norm(p, gain):
        ms = _dot((p * p).astype(BF16), grp_ref[...])
        return p * lax.rsqrt(ms + EPS) * gain

    qn = group_norm(proj(_OFF_DQ, _QK), qg_ref[...]) * (DA_QK_DIM ** -0.5 * math.log2(math.e))
    qt_ref[0, 0] = qn.T.astype(BF16)
    k_ref[0] = group_norm(proj(_OFF_DK, _QK), kg_ref[...]).astype(BF16)
    vt_ref[0, 0] = proj(_OFF_DV, _DAW).T.astype(BF16)
    gq_ref[0] = proj(_OFF_GQ, _GQK) * (GLA_K_DIM ** -0.5)
    gk_ref[0] = proj(_OFF_GK, _GQK)
    gv_ref[0] = proj(_OFF_GV, _GW).astype(BF16)
    gg_ref[0] = proj(_OFF_GG, _GW)
    z = _dot(proj(_OFF_GR, LANES).astype(BF16), gw_ref[...]) + gb_ref[...]
    log_sig = jnp.minimum(z, 0.0) - jnp.log(1.0 + jnp.exp(-jnp.abs(z)))
    la_ref[0] = log_sig * (math.log2(math.e) / GLA_TAU)


def _in_proj(x, norm_g, w_in, da_qn, da_kn, gate_w, gate_b):
    b, s, d = x.shape
    tm = TM_PROJ
    ns = s // tm
    w = jnp.pad(w_in, ((0, 0), (0, _IN_PAD - w_in.shape[1]))).astype(BF16)
    gw = jnp.pad(gate_w, ((0, LANES - GLA_GATE_RANK), (0, 0))).astype(BF16)
    lane = jnp.arange(_QK)
    grp = jnp.where((lane[:, None] // DA_QK_DIM) == (lane[None, :] // DA_QK_DIM),
                    1.0 / DA_QK_DIM, 0.0).astype(BF16)
    const = lambda shape: pl.BlockSpec(shape, lambda i, j: (0,) * len(shape))
    tile = lambda width: pl.BlockSpec((1, tm, width), lambda i, j: (i, j, 0))
    tile_t = lambda width: pl.BlockSpec((1, 1, width, tm), lambda i, j: (i, j, 0, 0))
    return pl.pallas_call(
        _in_proj_kernel,
        grid=(b, ns),
        in_specs=[tile(d), const((1, d)), const((d, _IN_PAD)), const((1, _QK)), const((1, _QK)),
                  const((_QK, _QK)), const((LANES, _GQK)), const((1, _GQK))],
        out_specs=[tile_t(_QK), tile(_QK), tile_t(_DAW), tile(_GQK), tile(_GQK), tile(_GW), tile(_GW),
                   tile(_GQK)],
        out_shape=[
            jax.ShapeDtypeStruct((b, ns, _QK, tm), BF16),
            jax.ShapeDtypeStruct((b, s, _QK), BF16),
            jax.ShapeDtypeStruct((b, ns, _DAW, tm), BF16),
            jax.ShapeDtypeStruct((b, s, _GQK), F32),
            jax.ShapeDtypeStruct((b, s, _GQK), F32),
            jax.ShapeDtypeStruct((b, s, _GW), BF16),
            jax.ShapeDtypeStruct((b, s, _GW), F32),
            jax.ShapeDtypeStruct((b, s, _GQK), F32),
        ],
        compiler_params=_params("parallel", "parallel"),
        name="in_proj",
    )(x, norm_g.reshape(1, d), w, jnp.tile(da_qn, 2 * DA_HEADS).reshape(1, _QK),
      jnp.tile(da_kn, 2 * DA_HEADS).reshape(1, _QK), grp, gw, gate_b.reshape(1, _GQK))


def _split_q(qt):
    row = lax.broadcasted_iota(I32, qt.shape, 0)
    zero = jnp.zeros_like(qt)
    return jnp.where(row < DA_QK_DIM, qt, zero), jnp.where(row >= DA_QK_DIM, qt, zero)


def _chunk_causal_mask(blk):
    key_chunk = lax.broadcasted_iota(I32, (blk, blk), 0) // CHUNK
    qry_chunk = lax.broadcasted_iota(I32, (blk, blk), 1) // CHUNK
    return key_chunk <= qry_chunk


def _diff_attn_finish(lq1_ref, lk1_ref, lq2_ref, lk2_ref, gain_ref, a1, l1, a2, l2):
    lam = (jnp.exp(jnp.sum(lq1_ref[...] * lk1_ref[...], axis=-1, keepdims=True))
           - jnp.exp(jnp.sum(lq2_ref[...] * lk2_ref[...], axis=-1, keepdims=True)) + LAM_INIT)
    o = a1 / l1 - lam * (a2 / l2)
    ms = jnp.mean(o * o, axis=0, keepdims=True)
    o = o * lax.rsqrt(ms + EPS) * gain_ref[...] * (1.0 - LAM_INIT)
    return o.T.astype(BF16)


def _diff_attn_bounded_kernel(lq1_ref, lk1_ref, lq2_ref, lk2_ref, gain_ref, qt_ref, k_ref, vt_ref, out_ref,
                              s_ref, l1_ref, a1_ref, l2_ref, a2_ref, *, blk, nb):
    stats = ((l1_ref, a1_ref), (l2_ref, a2_ref))
    mask = _chunk_causal_mask(blk)

    def reset():
        for l_ref, a_ref in stats:
            l_ref[...] = jnp.zeros(l_ref.shape, F32)
            a_ref[...] = jnp.zeros(a_ref.shape, F32)

    def scores(q, j, slot):
        kb = k_ref[0, pl.ds(pl.multiple_of(j * blk, blk), blk), :]
        s_ref[slot, 0] = _dot(kb, q[0])
        s_ref[slot, 1] = _dot(kb, q[1])

    def consume(j, slot, masked):
        vb = vt_ref[0, j]
        for m, (l_ref, a_ref) in enumerate(stats):
            s = s_ref[slot, m]
            if masked:
                s = jnp.where(mask, s, NEG_INF)
            p = jnp.exp2(s)
            l_ref[...] += jnp.sum(p, axis=0, keepdims=True)
            a_ref[...] += _dot(vb, p.astype(BF16))

    def step(q, j, slot):
        scores(q, j + 1, 1 - slot)
        consume(j, slot, False)

    reset()
    q = _split_q(qt_ref[0, 0])
    slot = 0
    scores(q, 0, slot)
    for qi in range(nb):
        def pair(i, carry, q=q, slot=slot):
            step(q, 2 * i, slot)
            step(q, 2 * i + 1, 1 - slot)
            return carry

        if qi // 2:
            lax.fori_loop(0, qi // 2, pair, 0)
        if qi % 2:
            step(q, qi - 1, slot)
            slot = 1 - slot
        if qi + 1 < nb:
            q = _split_q(qt_ref[0, qi + 1])
            scores(q, 0, 1 - slot)
        consume(qi, slot, True)
        out_ref[0, qi * blk:(qi + 1) * blk, :] = _diff_attn_finish(
            lq1_ref, lk1_ref, lq2_ref, lk2_ref, gain_ref, a1_ref[...], l1_ref[...], a2_ref[...], l2_ref[...])
        if qi + 1 < nb:
            reset()
        slot = 1 - slot


def _diff_attn_online_kernel(lq1_ref, lk1_ref, lq2_ref, lk2_ref, gain_ref, qt_ref, k_ref, vt_ref, out_ref,
                             m1_ref, l1_ref, a1_ref, m2_ref, l2_ref, a2_ref, *, blk):
    qi = pl.program_id(2)
    q1, q2 = _split_q(qt_ref[0, 0])

    for m_ref, l_ref, a_ref in ((m1_ref, l1_ref, a1_ref), (m2_ref, l2_ref, a2_ref)):
        m_ref[...] = jnp.full(m_ref.shape, NEG_INF, F32)
        l_ref[...] = jnp.zeros(l_ref.shape, F32)
        a_ref[...] = jnp.zeros(a_ref.shape, F32)

    def update(s, vb, m_ref, l_ref, a_ref):
        m_old = m_ref[...]
        m_new = jnp.maximum(m_old, jnp.max(s, axis=0, keepdims=True))
        alpha = jnp.exp2(m_old - m_new)
        p = jnp.exp2(s - m_new)
        l_ref[...] = alpha * l_ref[...] + jnp.sum(p, axis=0, keepdims=True)
        a_ref[...] = alpha * a_ref[...] + _dot(vb, p.astype(BF16))
        m_ref[...] = m_new

    def block(j, mask):
        kb = k_ref[0, pl.ds(pl.multiple_of(j * blk, blk), blk), :]
        vb = vt_ref[0, j]
        s1 = _dot(kb, q1)
        s2 = _dot(kb, q2)
        if mask is not None:
            s1 = jnp.where(mask, s1, NEG_INF)
            s2 = jnp.where(mask, s2, NEG_INF)
        update(s1, vb, m1_ref, l1_ref, a1_ref)
        update(s2, vb, m2_ref, l2_ref, a2_ref)

    def body(j, carry):
        block(j, None)
        return carry

    lax.fori_loop(0, qi, body, 0)
    block(qi, _chunk_causal_mask(blk))
    out_ref[0] = _diff_attn_finish(lq1_ref, lk1_ref, lq2_ref, lk2_ref, gain_ref,
                                   a1_ref[...], l1_ref[...], a2_ref[...], l2_ref[...])


SCORE_BOUND = 60.0


def _diff_attn(qt, k, vt, lq1, lk1, lq2, lk2, da_on, da_qn, da_kn):
    b, nb, _, blk = qt.shape
    s = nb * blk
    stat = lambda: pltpu.VMEM((1, blk), F32)
    acc = lambda: pltpu.VMEM((DA_V_DIM, blk), F32)

    args = (lq1.reshape(1, -1), lk1.reshape(1, -1), lq2.reshape(1, -1), lk2.reshape(1, -1),
            da_on.reshape(-1, 1), qt, k, vt)
    out_shape = jax.ShapeDtypeStruct((b, s, _DAW), BF16)
    head = lambda *trailing: (lambda i, h: (i, 0, h) + trailing)
    vec2 = lambda: pl.BlockSpec((1, DA_QK_DIM), lambda i, h: (0, 0))
    bounded = pl.pallas_call(
        functools.partial(_diff_attn_bounded_kernel, blk=blk, nb=nb),
        grid=(b, DA_HEADS),
        in_specs=[
            vec2(), vec2(), vec2(), vec2(),
            pl.BlockSpec((DA_V_DIM, 1), lambda i, h: (0, 0)),
            pl.BlockSpec((1, nb, 2 * DA_QK_DIM, blk), head(0)),
            pl.BlockSpec((1, s, 2 * DA_QK_DIM), head()),
            pl.BlockSpec((1, nb, DA_V_DIM, blk), head(0)),
        ],
        out_specs=pl.BlockSpec((1, s, DA_V_DIM), head()),
        out_shape=out_shape,
        scratch_shapes=[pltpu.VMEM((2, 2, blk, blk), F32), stat(), acc(), stat(), acc()],
        compiler_params=_params("parallel", "parallel"),
        name="diff_attn",
    )
    vec3 = lambda: pl.BlockSpec((1, DA_QK_DIM), lambda i, h, q: (0, 0))
    online = pl.pallas_call(
        functools.partial(_diff_attn_online_kernel, blk=blk),
        grid=(b, DA_HEADS, nb),
        in_specs=[
            vec3(), vec3(), vec3(), vec3(),
            pl.BlockSpec((DA_V_DIM, 1), lambda i, h, q: (0, 0)),
            pl.BlockSpec((1, 1, 2 * DA_QK_DIM, blk), lambda i, h, q: (i, q, h, 0)),
            pl.BlockSpec((1, s, 2 * DA_QK_DIM), lambda i, h, q: (i, 0, h)),
            pl.BlockSpec((1, nb, DA_V_DIM, blk), lambda i, h, q: (i, 0, h, 0)),
        ],
        out_specs=pl.BlockSpec((1, blk, DA_V_DIM), lambda i, h, q: (i, q, h)),
        out_shape=out_shape,
        scratch_shapes=[stat(), stat(), acc(), stat(), stat(), acc()],
        compiler_params=_params("parallel", "parallel", "parallel"),
        name="diff_attn_online",
    )
    bound = (1.01 * DA_QK_DIM ** 0.5 * math.log2(math.e)) * jnp.max(jnp.abs(da_qn)) * jnp.max(jnp.abs(da_kn))
    return lax.cond(bound <= SCORE_BOUND, bounded, online, *args)


def _gla_kernel(q_ref, k_ref, la_ref, v_ref, g_ref, gain_ref, out_ref, st_ref, *, ts, group):
    @pl.when(pl.program_id(1) == 0)
    def _():
        st_ref[...] = jnp.zeros(st_ref.shape, F32)

    c = CHUNK
    rows = group * c
    hk, hv = _GQK, _GW
    r = lax.broadcasted_iota(I32, (rows, rows), 0)
    cc = lax.broadcasted_iota(I32, (rows, rows), 1)
    tri = jnp.where(jnp.logical_and(r // c == cc // c, r >= cc), 1.0, 0.0).astype(BF16)
    bd_k = (lax.broadcasted_iota(I32, (hk, hk), 0) // GLA_K_DIM
            == lax.broadcasted_iota(I32, (hk, hk), 1) // GLA_K_DIM)
    bd_v = (lax.broadcasted_iota(I32, (hk, hv), 0) // GLA_K_DIM
            == lax.broadcasted_iota(I32, (hk, hv), 1) // GLA_V_DIM)
    bd_vt = (lax.broadcasted_iota(I32, (hv, hk), 0) // GLA_V_DIM
             == lax.broadcasted_iota(I32, (hv, hk), 1) // GLA_K_DIM)
    lower = (lax.broadcasted_iota(I32, (c, hk), 0)
             >= lax.broadcasted_iota(I32, (c, hk), 1) % c)

    def tiled(t, mask):
        t4 = jnp.concatenate([t] * GLA_HEADS, axis=0)
        return jnp.where(mask, t4, jnp.zeros_like(t4))

    def chunk_row(t, row):
        return jnp.concatenate([jnp.broadcast_to(t[i * c + row:i * c + row + 1, :], (c, hk)) for i in range(group)],
                               axis=0)

    def body(gi, carry):
        sl = pl.ds(pl.multiple_of(gi * rows, rows), rows)
        la_hi, la_lo = _split_bf16(la_ref[0, sl, :])
        big_l = _dot(tri, la_hi) + _dot(tri, la_lo)
        l_end = chunk_row(big_l, c - 1)
        lc = big_l - chunk_row(big_l, c // 2 - 1)
        e_pos = jnp.exp2(lc)
        e_neg = jnp.exp2(-lc)
        q = q_ref[0, sl, :]
        k = k_ref[0, sl, :]
        v = v_ref[0, sl, :]
        q_pos = (q * e_pos).astype(BF16)
        q_neg = (q * e_neg).astype(BF16)
        k_pos = (k * e_pos).astype(BF16)
        k_neg = (k * e_neg).astype(BF16)
        q_in = (q * jnp.exp2(big_l)).astype(BF16)
        k_out = (k * jnp.exp2(l_end - big_l)).astype(BF16)
        decay = jnp.exp2(l_end)

        o_intra, u_t = [], []
        for i in range(group):
            cs = slice(i * c, (i + 1) * c)
            a_past = _dot_nt(q_pos[cs], tiled(k_neg[cs], bd_k))
            a_fut = _dot_nt(q_neg[cs], tiled(k_pos[cs], bd_k))
            a = jnp.where(lower, a_past, a_fut).astype(BF16)
            o_intra.append(_dot(a, tiled(v[cs], bd_v)))
            u_t.append(jnp.where(bd_vt, _dot_tn(v[cs], k_out[cs]), 0.0))

        st = st_ref[...]
        o_inter = []
        for i in range(group):
            cs = slice(i * c, (i + 1) * c)
            o_inter.append(_dot_nt(q_in[cs], st.astype(BF16)))
            st = st * decay[i * c:i * c + 1, :] + u_t[i]
        st_ref[...] = st

        o = jnp.concatenate(o_intra, axis=0) + jnp.concatenate(o_inter, axis=0)
        g = g_ref[0, sl, :]
        silu = g / (1.0 + jnp.exp(-g))
        for h in range(GLA_HEADS):
            hs = slice(h * GLA_V_DIM, (h + 1) * GLA_V_DIM)
            out_ref[0, sl, hs] = (_rms(o[:, hs], gain_ref[...]) * silu[:, hs]).astype(BF16)
        return carry

    lax.fori_loop(0, ts // rows, body, 0)


def _gla(gq, gk, la, gv, gg, gla_on):
    b, s, _ = gq.shape
    ts = TS_GLA
    tile = lambda width: pl.BlockSpec((1, ts, width), lambda i, j: (i, j, 0))
    return pl.pallas_call(
        functools.partial(_gla_kernel, ts=ts, group=GLA_GROUP),
        grid=(b, s // ts),
        in_specs=[tile(_GQK), tile(_GQK), tile(_GQK), tile(_GW), tile(_GW),
                  pl.BlockSpec((1, GLA_V_DIM), lambda i, j: (0, 0))],
        out_specs=tile(_GW),
        out_shape=jax.ShapeDtypeStruct((b, s, _GW), BF16),
        scratch_shapes=[pltpu.VMEM((_GW, _GQK), F32)],
        compiler_params=_params("parallel", "arbitrary"),
        name="gla",
    )(gq, gk, la, gv, gg, gla_on.reshape(1, -1))


_META_E0, _META_E1, _META_G0, _META_G1, _META_P0, _META_P1 = range(6)
_EXP_LANE0 = N_GROUPS


def _post_kernel(*refs, d, tm, sub, rider):
    if rider:
        (size_ref, pend_ref, x_ref, da_ref, gla_ref, wo_ref, gc_ref, wq_ref, qn_ref, km_ref, vm_ref, wco_ref,
         gf_ref, wr_ref, br_ref, rdest_ref, rsrc_ref,
         h_ref, xn_ref, meta_ref, ids_ref, cnt_ref, rpad_ref, zero_ref, rsem, zsem) = refs
    else:
        (x_ref, da_ref, gla_ref, wo_ref, gc_ref, wq_ref, qn_ref, km_ref, vm_ref, wco_ref,
         gf_ref, wr_ref, br_ref, h_ref, xn_ref, meta_ref, ids_ref, cnt_ref) = refs
    first = jnp.logical_and(pl.program_id(0) == 0, pl.program_id(1) == 0)

    @pl.when(first)
    def _():
        cnt_ref[...] = jnp.zeros(cnt_ref.shape, F32)
        if rider:
            _zero_fill(size_ref, pend_ref, rpad_ref, zero_ref, zsem)

    if rider:
        _scatter_rows(rdest_ref, rsrc_ref, rpad_ref, rsem)

    half = d // 2
    hd = d // CROSS_HEADS
    lane = lax.broadcasted_iota(I32, (sub, LANES), 1)
    big = jnp.int32(LANES)
    strict_lower = jnp.where(lax.broadcasted_iota(I32, (sub, sub), 0) > lax.broadcasted_iota(I32, (sub, sub), 1),
                             1.0, 0.0).astype(BF16)

    def lane_argmax(vals):
        m = jnp.max(vals, axis=-1, keepdims=True)
        idx = jnp.min(jnp.where(vals == m, lane, big), axis=-1, keepdims=True)
        return m, idx

    def rows(rs, base):
        h1 = x_ref[0, rs, :] + _dot(da_ref[0, rs, :], wo_ref[:half, :]) + _dot(gla_ref[0, rs, :], wo_ref[half:, :])

        u = _rms(h1, gc_ref[...]).astype(BF16)
        q = _dot(u, wq_ref[...])
        heads = []
        for h in range(CROSS_HEADS):
            hs = slice(h * hd, (h + 1) * hd)
            qh = _rms(q[:, hs], qn_ref[...]).astype(BF16)
            sc = _dot_nt(qh, km_ref[0, :, hs])
            sc = sc - jnp.max(sc, axis=-1, keepdims=True)
            p = jnp.exp(sc)
            p = p / jnp.sum(p, axis=-1, keepdims=True)
            heads.append(_dot(p.astype(BF16), vm_ref[0, :, hs]))
        o = jnp.concatenate(heads, axis=-1).astype(BF16)
        h2 = h1 + _dot(o, wco_ref[...])
        h_ref[0, rs, :] = h2

        xn = _rms(h2, gf_ref[...]).astype(BF16)
        bits = lax.bitcast_convert_type(xn.astype(F32), U32)
        xn_ref[0, rs, :] = (bits[:, :half] >> 16) | (bits[:, half:] & HI16)
        logits = _dot(xn, wr_ref[...]) + br_ref[...]

        lg = jnp.where(lane < N_GROUPS, logits, NEG_INF)
        g_max, g_sel = lane_argmax(lg)
        p_g = 1.0 / jnp.sum(jnp.exp(lg - g_max), axis=-1, keepdims=True)
        e_lo = _EXP_LANE0 + g_sel * EXPERTS_PER_GROUP
        in_group = jnp.logical_and(lane >= e_lo, lane < e_lo + EXPERTS_PER_GROUP)
        le = jnp.where(in_group, logits, NEG_INF)
        m1, i1 = lane_argmax(le)
        m2, i2 = lane_argmax(jnp.where(lane == i1, NEG_INF, le))
        e2 = jnp.exp(m2 - m1)
        gate0 = p_g / (1.0 + e2)
        gate1 = p_g * e2 / (1.0 + e2)
        e0 = i1 - _EXP_LANE0
        e1 = i2 - _EXP_LANE0

        hot0 = lane == e0
        hot1 = lane == e1
        onehot = jnp.where(jnp.logical_or(hot0, hot1), 1.0, 0.0)
        before = _dot(strict_lower, onehot.astype(BF16)) + base
        pos0 = jnp.sum(jnp.where(hot0, before, 0.0), axis=-1, keepdims=True)
        pos1 = jnp.sum(jnp.where(hot1, before, 0.0), axis=-1, keepdims=True)

        meta = jnp.zeros(logits.shape, F32)
        for idx, val in ((_META_E0, e0.astype(F32)), (_META_E1, e1.astype(F32)), (_META_G0, gate0),
                         (_META_G1, gate1), (_META_P0, pos0), (_META_P1, pos1)):
            meta = jnp.where(lane == idx, val, meta)
        meta_ref[0, rs, :] = meta
        ids_ref[0, 0, :, rs] = meta.T[:ids_ref.shape[2], :]
        return base + jnp.sum(onehot, axis=0, keepdims=True)

    base = cnt_ref[0:1, :]
    for r0 in range(0, tm, sub):
        base = rows(slice(r0, r0 + sub), base)
    cnt_ref[...] = jnp.broadcast_to(base, cnt_ref.shape)
    if rider:
        _scatter_wait(rsrc_ref, rpad_ref, rsem)


def _post(x, da, gla, w_o, norm_cross, w_cq, cross_qn, k_mem, v_mem, w_co, norm_ffn, w_group, b_group,
          w_expert, b_expert, *, half, rider=None):
    b, s, d = x.shape
    bh = b // 2
    b0 = half * bh
    tm = TM_POST
    ns = s // tm
    m = k_mem.shape[1]
    w_r = jnp.pad(jnp.concatenate([w_group, w_expert], axis=1), ((0, 0), (0, LANES - N_GROUPS - N_EXPERTS)))
    b_r = jnp.pad(jnp.concatenate([b_group, b_expert]), (0, LANES - N_GROUPS - N_EXPERTS)).reshape(1, LANES)
    const = lambda shape: pl.BlockSpec(shape, lambda i, j, *_: (0,) * len(shape))
    tile_in = lambda width: pl.BlockSpec((1, tm, width), lambda i, j, *_: (i + b0, j, 0))
    tile_out = lambda width: pl.BlockSpec((1, tm, width), lambda i, j, *_: (i, j, 0))
    per_b = lambda: pl.BlockSpec((1, m, d), lambda i, j, *_: (i + b0, 0, 0))
    in_specs = [tile_in(d), tile_in(d // 2), tile_in(d // 2), const((d, d)), const((1, d)), const((d, d)),
                const((1, d // CROSS_HEADS)), per_b(), per_b(), const((d, d)), const((1, d)),
                const((d, LANES)), const((1, LANES))]
    out_specs = [tile_out(d), tile_out(d // 2), tile_out(LANES),
                 pl.BlockSpec((1, 1, 8, tm), lambda i, j, *_: (i, j, 0, 0)), const((8, LANES))]
    out_shape = [
        jax.ShapeDtypeStruct((bh, s, d), F32),
        jax.ShapeDtypeStruct((bh, s, d // 2), U32),
        jax.ShapeDtypeStruct((bh, s, LANES), F32),
        jax.ShapeDtypeStruct((bh, ns, 8, tm), F32),
        jax.ShapeDtypeStruct((8, LANES), F32),
    ]
    args = (x, da, gla, w_o.astype(BF16), norm_cross.reshape(1, d), w_cq.astype(BF16), cross_qn.reshape(1, -1),
            k_mem, v_mem, w_co.astype(BF16), norm_ffn.reshape(1, d), w_r.astype(BF16), b_r)
    scalars, scratch = (), []
    if rider is not None:
        sizes, pend, dest, xn_src, n_rows = rider
        scalars = (sizes, pend)
        in_specs += [pl.BlockSpec((1, 2, tm), lambda i, j, *_: (i * ns + j, 0, 0), memory_space=pltpu.SMEM),
                     pl.BlockSpec((tm, d // 2), lambda i, j, *_: (i * ns + j, 0))]
        out_specs.append(pl.BlockSpec(memory_space=pl.ANY))
        out_shape.append(jax.ShapeDtypeStruct((n_rows, d // 2), U32))
        scratch = [pltpu.VMEM((EXPERT_ROWS, d // 2), U32), pltpu.SemaphoreType.DMA, pltpu.SemaphoreType.DMA]
        args += (dest, xn_src)
    return pl.pallas_call(
        functools.partial(_post_kernel, d=d, tm=tm, sub=SUB_POST, rider=rider is not None),
        grid_spec=pltpu.PrefetchScalarGridSpec(
            num_scalar_prefetch=len(scalars), grid=(bh, ns), in_specs=in_specs, out_specs=out_specs,
            scratch_shapes=scratch),
        out_shape=out_shape,
        compiler_params=_params("arbitrary", "arbitrary"),
        name="post_scatter" if rider is not None else "post",
    )(*scalars, *args)


def _row_copy(src_ref, src_row, dst_ref, dst_row, sem):
    return pltpu.make_async_copy(src_ref.at[pl.ds(src_row, 1)], dst_ref.at[pl.ds(dst_row, 1)], sem)


def _zero_fill(size_ref, pend_ref, xpad_ref, zero_ref, zsem):
    zero_ref[...] = jnp.zeros(zero_ref.shape, zero_ref.dtype)
    rows = zero_ref.shape[0]
    n_blocks = xpad_ref.shape[0] // rows
    n_used = pend_ref[N_EXPERTS - 1] // rows

    def zero_block(blk):
        return pltpu.make_async_copy(zero_ref, xpad_ref.at[pl.ds(pl.multiple_of(blk * rows, rows), rows)], zsem)

    def last_block(e, fn):
        @pl.when(size_ref[e] > 0)
        def _():
            fn(zero_block(pend_ref[e] // rows - 1))

    for fn in (lambda cp: cp.start(), lambda cp: cp.wait()):
        lax.fori_loop(0, N_EXPERTS, lambda e, c: (last_block(e, fn), c)[1], 0)
        lax.fori_loop(n_used, n_blocks, lambda blk, c: (fn(zero_block(blk)), c)[1], 0)


def _scatter_rows(dest_ref, src_ref, xpad_ref, sem):
    for t in range(src_ref.shape[0]):
        for k in range(2):
            _row_copy(src_ref, t, xpad_ref, dest_ref[0, k, t], sem).start(priority=k)


def _scatter_wait(src_ref, xpad_ref, sem):
    for _ in range(2):
        pltpu.make_async_copy(src_ref, xpad_ref.at[pl.ds(0, src_ref.shape[0])], sem).wait()


def _experts_kernel(*refs, rider_steps):
    if rider_steps:
        (be_ref, nused_ref, size_ref, pend_ref, x_ref, wg_ref, wu_ref, wd_ref, rdest_ref, rsrc_ref,
         out_ref, rpad_ref, wg_s, wu_s, wd_s, zero_ref, rsem, zsem) = refs
    else:
        be_ref, nused_ref, x_ref, wg_ref, wu_ref, wd_ref, out_ref, wg_s, wu_s, wd_s = refs
    i = pl.program_id(0)
    used = i < nused_ref[0]
    new_expert = jnp.logical_or(i == 0, be_ref[i] != be_ref[jnp.maximum(i - 1, 0)])

    if rider_steps:
        @pl.when(i == 0)
        def _():
            _zero_fill(size_ref, pend_ref, rpad_ref, zero_ref, zsem)

    @pl.when(jnp.logical_and(used, new_expert))
    def _():
        wg_s[...] = wg_ref[0].astype(BF16)
        wu_s[...] = wu_ref[0].astype(BF16)
        wd_s[...] = wd_ref[0].astype(BF16)

    def mlp():
        words = x_ref[...]
        half = words.shape[1]
        lo = lax.bitcast_convert_type(words << 16, F32).astype(BF16)
        hi = lax.bitcast_convert_type(words & HI16, F32).astype(BF16)
        gate = _dot(lo, wg_s[:half, :]) + _dot(hi, wg_s[half:, :])
        up = _dot(lo, wu_s[:half, :]) + _dot(hi, wu_s[half:, :])
        hid = gate / (1.0 + jnp.exp(-gate)) * up
        out_ref[...] = _dot(hid.astype(BF16), wd_s[...])

    if rider_steps:
        riding = i < rider_steps

        @pl.when(riding)
        def _():
            _scatter_rows(rdest_ref, rsrc_ref, rpad_ref, rsem)
            mlp()
            _scatter_wait(rsrc_ref, rpad_ref, rsem)

        pl.when(jnp.logical_and(used, jnp.logical_not(riding)))(mlp)
    else:
        pl.when(used)(mlp)

    @pl.when(jnp.logical_not(used))
    def _():
        out_ref[...] = jnp.zeros(out_ref.shape, F32)


def _experts(x_pad, block_expert, n_used, w_gate, w_up, w_down, rider=None):
    n_rows = x_pad.shape[0]
    _, d, f = w_gate.shape
    rows = EXPERT_ROWS
    row_blk = lambda i, be, nu, *_: (jnp.minimum(i, nu[0] - 1), 0)
    weights = lambda shape: pl.BlockSpec(shape, lambda i, be, *_: (be[i], 0, 0))
    in_specs = [pl.BlockSpec((rows, d // 2), row_blk), weights((1, d, f)), weights((1, d, f)), weights((1, f, d))]
    out_specs = [pl.BlockSpec((rows, d), lambda i, *_: (i, 0))]
    out_shape = [jax.ShapeDtypeStruct((n_rows, d), F32)]
    scratch = [pltpu.VMEM((d, f), BF16), pltpu.VMEM((d, f), BF16), pltpu.VMEM((f, d), BF16)]
    scalars, args, rider_steps = (block_expert, n_used), (x_pad, w_gate, w_up, w_down), 0
    if rider is not None:
        sizes, pend, dest, xn_src, rider_rows = rider
        tr = RIDER_ROWS
        rider_steps = xn_src.shape[0] // tr
        assert 2 * xn_src.shape[0] >= rider_steps * rows, "riding steps must all be used expert blocks"
        last = rider_steps - 1
        scalars += (sizes, pend)
        in_specs += [pl.BlockSpec((1, 2, tr), lambda i, *_: (jnp.minimum(i, last), 0, 0), memory_space=pltpu.SMEM),
                     pl.BlockSpec((tr, d // 2), lambda i, *_: (jnp.minimum(i, last), 0))]
        out_specs.append(pl.BlockSpec(memory_space=pl.ANY))
        out_shape.append(jax.ShapeDtypeStruct((rider_rows, d // 2), U32))
        scratch += [pltpu.VMEM((rows, d // 2), U32), pltpu.SemaphoreType.DMA, pltpu.SemaphoreType.DMA]
        tiles, _, tm = dest.shape
        dest = dest.reshape(tiles, 2, tm // tr, tr).transpose(0, 2, 1, 3).reshape(tiles * (tm // tr), 2, tr)
        args += (dest, xn_src)
    out = pl.pallas_call(
        functools.partial(_experts_kernel, rider_steps=rider_steps),
        grid_spec=pltpu.PrefetchScalarGridSpec(
            num_scalar_prefetch=len(scalars), grid=(n_rows // rows,), in_specs=in_specs, out_specs=out_specs,
            scratch_shapes=scratch),
        out_shape=out_shape,
        compiler_params=_params("arbitrary"),
        name="experts_scatter" if rider is not None else "experts",
    )(*scalars, *args)
    return out if rider is not None else out[0]


def _combine_kernel(dest_ref, h0_ref, h1_ref, m0_ref, m1_ref, op0_ref, op1_ref, y_ref, buf_ref, sem, *, tm, nh):
    s = pl.program_id(0)
    halves = ((h0_ref, m0_ref, op0_ref), (h1_ref, m1_ref, op1_ref))

    def gather(op_ref):
        slot = s % 2
        for t in range(tm):
            for k in range(2):
                _row_copy(op_ref, dest_ref[0, k, t], buf_ref.at[slot, k], t, sem.at[slot]).start(priority=k)

    def combine(h_ref, meta_ref, op_ref):
        slot = (s - 1) % 2
        for k in range(2):
            pltpu.make_async_copy(op_ref.at[pl.ds(0, tm)], buf_ref.at[slot, k], sem.at[slot]).wait()
        meta = meta_ref[...]
        g0 = meta[:, _META_G0:_META_G0 + 1]
        g1 = meta[:, _META_G1:_META_G1 + 1]
        y_ref[...] = h_ref[...] + g0 * buf_ref[slot, 0] + g1 * buf_ref[slot, 1]

    for half, (h_ref, meta_ref, op_ref) in enumerate(halves):
        lo = half * nh
        pl.when(jnp.logical_and(s >= lo, s < lo + nh))(functools.partial(gather, op_ref))
    for half, (h_ref, meta_ref, op_ref) in enumerate(halves):
        lo = half * nh
        pl.when(jnp.logical_and(s > lo, s <= lo + nh))(functools.partial(combine, h_ref, meta_ref, op_ref))


def _combine(h2, meta, dest, out_pad):
    th, d = h2[0].shape
    tm = TM_ROWS
    nh = th // tm
    clamp = lambda v: jnp.clip(v, 0, nh - 1)
    first = lambda s: (clamp(s - 1), 0)
    second = lambda s: (clamp(s - 1 - nh), 0)
    return pl.pallas_call(
        functools.partial(_combine_kernel, tm=tm, nh=nh),
        grid=(2 * nh + 1,),
        in_specs=[
            pl.BlockSpec((1, 2, tm), lambda s: (jnp.minimum(s, 2 * nh - 1), 0, 0), memory_space=pltpu.SMEM),
            pl.BlockSpec((tm, d), first), pl.BlockSpec((tm, d), second),
            pl.BlockSpec((tm, LANES), first), pl.BlockSpec((tm, LANES), second),
            pl.BlockSpec(memory_space=pl.ANY), pl.BlockSpec(memory_space=pl.ANY),
        ],
        out_specs=pl.BlockSpec((tm, d), lambda s: (jnp.maximum(s - 1, 0), 0)),
        scratch_shapes=[pltpu.VMEM((2, 2, tm, d), F32), pltpu.SemaphoreType.DMA((2,))],
        out_shape=jax.ShapeDtypeStruct((2 * th, d), F32),
        compiler_params=_params("arbitrary"),
        name="combine",
    )(dest, h2[0], h2[1], meta[0], meta[1], out_pad[0], out_pad[1])


def _route_tables(counts, ids, n_tokens):
    rows = EXPERT_ROWS
    sizes = counts[0, :N_EXPERTS].astype(I32)
    padded = (sizes + rows - 1) // rows * rows
    pend = jnp.cumsum(padded)
    pstart = pend - padded
    n_rows = 2 * n_tokens + N_EXPERTS * rows
    block_start = jnp.arange(n_rows // rows, dtype=I32) * rows
    block_expert = jnp.minimum(jnp.sum(pend[None, :] <= block_start[:, None], axis=1), N_EXPERTS - 1).astype(I32)
    n_used = (pend[-1:] // rows).astype(I32)
    expert = ids[:, _META_E0:_META_E1 + 1].astype(I32)
    rank = ids[:, _META_P0:_META_P1 + 1].astype(I32)
    dest = sum(jnp.where(expert == e, pstart[e], 0) for e in range(N_EXPERTS)) + rank
    return sizes, pend, block_expert, n_used, dest, n_rows


def kernel(x, mem, norm_mix, w_in, da_q_norm, da_k_norm, lambda_q1, lambda_k1, lambda_q2, lambda_k2,
           da_out_norm, gla_gate_w, gla_gate_b, gla_out_norm, w_o, norm_cross, norm_mem, w_cq, w_ckv,
           cross_q_norm, cross_k_norm, w_co, norm_ffn, w_group, b_group, w_expert, b_expert,
           w_e_gate, w_e_up, w_e_down):
    b, s, d = x.shape
    th = b // 2 * s
    assert TM_POST == TM_ROWS, "router tiles double as combine tiles"
    h = x
    for l in range(norm_mix.shape[0]):
        assert l == 0, "lam_init is fixed for a single layer"
        qt, kda, vt, gq, gk, gv, gg, la = _in_proj(h, norm_mix[l], w_in[l], da_q_norm[l], da_k_norm[l],
                                                   gla_gate_w[l], gla_gate_b[l])
        da = _diff_attn(qt, kda, vt, lambda_q1[l], lambda_k1[l], lambda_q2[l], lambda_k2[l], da_out_norm[l],
                        da_q_norm[l], da_k_norm[l])
        gla = _gla(gq, gk, la, gv, gg, gla_out_norm[l])
        k_mem, v_mem = _mem_kv(mem, norm_mem[l], w_ckv[l], cross_k_norm[l])
        post = functools.partial(_post, h, da, gla, w_o[l], norm_cross[l], w_cq[l], cross_q_norm[l], k_mem, v_mem,
                                 w_co[l], norm_ffn[l], w_group[l], b_group[l], w_expert[l], b_expert[l])
        experts = functools.partial(_experts, w_gate=w_e_gate[l], w_up=w_e_up[l], w_down=w_e_down[l])

        h2_0, xn_0, meta_0, ids_0, counts_0 = post(half=0)
        sizes_0, pend_0, be_0, used_0, dest_0, n_rows = _route_tables(counts_0, ids_0.reshape(-1, 8, TM_POST), th)
        h2_1, xn_1, meta_1, ids_1, counts_1, xpad_0 = post(
            half=1, rider=(sizes_0, pend_0, dest_0, xn_0.reshape(th, d // 2), n_rows))
        sizes_1, pend_1, be_1, used_1, dest_1, _ = _route_tables(counts_1, ids_1.reshape(-1, 8, TM_POST), th)
        opad_0, xpad_1 = experts(xpad_0, be_0, used_0, rider=(sizes_1, pend_1, dest_1, xn_1.reshape(th, d // 2),
                                                              n_rows))
        opad_1 = experts(xpad_1, be_1, used_1)
        h = _combine((h2_0.reshape(th, d), h2_1.reshape(th, d)),
                     (meta_0.reshape(th, LANES), meta_1.reshape(th, LANES)),
                     jnp.concatenate([dest_0, dest_1], axis=0), (opad_0, opad_1)).reshape(b, s, d)
    return h
```

```python
import functools
import math

import jax
import jax.numpy as jnp
import numpy as np
from jax import lax
from jax.experimental import pallas as pl
from jax.experimental.pallas import tpu as pltpu

F32 = jnp.float32
BF16 = jnp.bfloat16
I32 = jnp.int32
U32 = jnp.uint32
HI16 = np.uint32(0xFFFF0000)

EPS = 1e-6
CHUNK = 64

DA_HEADS = 4
DA_QK_DIM = 64
DA_V_DIM = 128
GLA_HEADS = 4
GLA_K_DIM = 64
GLA_V_DIM = 128
GLA_GATE_RANK = 16
GLA_TAU = 16.0
CROSS_HEADS = 4
N_GROUPS = 4
EXPERTS_PER_GROUP = 8
N_EXPERTS = N_GROUPS * EXPERTS_PER_GROUP
LAM_INIT = 0.8 - 0.6 * math.exp(-0.3 * 0)

LANES = 128
VMEM_LIMIT = 56 * 1024 * 1024

TM_PROJ = 512
TS_GLA = 1024
GLA_GROUP = 4
TM_POST = 1024
SUB_POST = 1024
TM_TAIL = 512
RIDER_ROWS = 256
EXPERT_ROWS = 512

NEG_INF = float("-inf")


def _params(*sem):
    return pltpu.CompilerParams(dimension_semantics=sem, vmem_limit_bytes=VMEM_LIMIT)


def _rms(t, g):
    ms = jnp.mean(t * t, axis=-1, keepdims=True)
    return t * lax.rsqrt(ms + EPS) * g


def _dot(a, b):
    return jnp.dot(a, b, preferred_element_type=F32)


def _dot_nt(a, b):
    return lax.dot_general(a, b, (((1,), (1,)), ((), ())), preferred_element_type=F32)


def _dot_tn(a, b):
    return lax.dot_general(a, b, (((0,), (0,)), ((), ())), preferred_element_type=F32)


def _split_bf16(t):
    hi = t.astype(BF16)
    lo = (t - hi.astype(F32)).astype(BF16)
    return hi, lo


def _mem_kv_kernel(mem_ref, g_ref, w_ref, kn_ref, k_ref, v_ref, *, d, heads):
    mn = _rms(mem_ref[0], g_ref[...]).astype(BF16)
    kv = _dot(mn, w_ref[...])
    hd = d // heads
    scale = hd ** -0.5
    for h in range(heads):
        kh = _rms(kv[:, h * hd:(h + 1) * hd], kn_ref[...]) * scale
        k_ref[0, :, h * hd:(h + 1) * hd] = kh.astype(BF16)
    v_ref[0] = kv[:, d:].astype(BF16)


def _mem_kv(mem, norm_m, w_ckv, kn):
    b, m, d = mem.shape
    return pl.pallas_call(
        functools.partial(_mem_kv_kernel, d=d, heads=CROSS_HEADS),
        grid=(b,),
        in_specs=[
            pl.BlockSpec((1, m, d), lambda i: (i, 0, 0)),
            pl.BlockSpec((1, d), lambda i: (0, 0)),
            pl.BlockSpec((d, 2 * d), lambda i: (0, 0)),
            pl.BlockSpec((1, d // CROSS_HEADS), lambda i: (0, 0)),
        ],
        out_specs=[
            pl.BlockSpec((1, m, d), lambda i: (i, 0, 0)),
            pl.BlockSpec((1, m, d), lambda i: (i, 0, 0)),
        ],
        out_shape=[jax.ShapeDtypeStruct((b, m, d), BF16)] * 2,
        compiler_params=_params("parallel"),
        name="mem_kv",
    )(mem, norm_m.reshape(1, d), w_ckv.astype(BF16), kn.reshape(1, -1))


_QK = DA_HEADS * 2 * DA_QK_DIM
_DAW = DA_HEADS * DA_V_DIM
_GQK = GLA_HEADS * GLA_K_DIM
_GW = GLA_HEADS * GLA_V_DIM
_OFF_DQ = 0
_OFF_DK = _OFF_DQ + _QK
_OFF_DV = _OFF_DK + _QK
_OFF_GQ = _OFF_DV + _DAW
_OFF_GK = _OFF_GQ + _GQK
_OFF_GV = _OFF_GK + _GQK
_OFF_GG = _OFF_GV + _GW
_OFF_GR = _OFF_GG + _GW
_IN_PAD = _OFF_GR + LANES


def _in_proj_kernel(x_ref, g_ref, w_ref, qg_ref, kg_ref, grp_ref, gw_ref, gb_ref,
                    qt_ref, k_ref, vt_ref, gq_ref, gk_ref, gv_ref, gg_ref, la_ref):
    u = _rms(x_ref[0], g_ref[...]).astype(BF16)

    def proj(off, width):
        return _dot(u, w_ref[:, off:off + width])

    def group_norm(p, gain):
        ms = _dot((p * p).astype(BF16), grp_ref[...])
        return p * lax.rsqrt(ms + EPS) * gain

    qn = group_norm(proj(_OFF_DQ, _QK), qg_ref[...]) * (DA_QK_DIM ** -0.5 * math.log2(math.e))
    qt_ref[0, 0] = qn.T.astype(BF16)
    k_ref[0] = group_norm(proj(_OFF_DK, _QK), kg_ref[...]).astype(BF16)
    vt_ref[0, 0] = proj(_OFF_DV, _DAW).T.astype(BF16)
    gq_ref[0] = proj(_OFF_GQ, _GQK) * (GLA_K_DIM ** -0.5)
    gk_ref[0] = proj(_OFF_GK, _GQK)
    gv_ref[0] = proj(_OFF_GV, _GW).astype(BF16)
    gg_ref[0] = proj(_OFF_GG, _GW)
    z = _dot(proj(_OFF_GR, LANES).astype(BF16), gw_ref[...]) + gb_ref[...]
    log_sig = jnp.minimum(z, 0.0) - jnp.log(1.0 + jnp.exp(-jnp.abs(z)))
    la_ref[0] = log_sig * (math.log2(math.e) / GLA_TAU)


def _in_proj(x, norm_g, w_in, da_qn, da_kn, gate_w, gate_b):
    b, s, d = x.shape
    tm = TM_PROJ
    ns = s // tm
    w = jnp.pad(w_in, ((0, 0), (0, _IN_PAD - w_in.shape[1]))).astype(BF16)
    gw = jnp.pad(gate_w, ((0, LANES - GLA_GATE_RANK), (0, 0))).astype(BF16)
    lane = jnp.arange(_QK)
    grp = jnp.where((lane[:, None] // DA_QK_DIM) == (lane[None, :] // DA_QK_DIM),
                    1.0 / DA_QK_DIM, 0.0).astype(BF16)
    const = lambda shape: pl.BlockSpec(shape, lambda i, j: (0,) * len(shape))
    tile = lambda width: pl.BlockSpec((1, tm, width), lambda i, j: (i, j, 0))
    tile_t = lambda width: pl.BlockSpec((1, 1, width, tm), lambda i, j: (i, j, 0, 0))
    return pl.pallas_call(
        _in_proj_kernel,
        grid=(b, ns),
        in_specs=[tile(d), const((1, d)), const((d, _IN_PAD)), const((1, _QK)), const((1, _QK)),
                  const((_QK, _QK)), const((LANES, _GQK)), const((1, _GQK))],
        out_specs=[tile_t(_QK), tile(_QK), tile_t(_DAW), tile(_GQK), tile(_GQK), tile(_GW), tile(_GW),
                   tile(_GQK)],
        out_shape=[
            jax.ShapeDtypeStruct((b, ns, _QK, tm), BF16),
            jax.ShapeDtypeStruct((b, s, _QK), BF16),
            jax.ShapeDtypeStruct((b, ns, _DAW, tm), BF16),
            jax.ShapeDtypeStruct((b, s, _GQK), F32),
            jax.ShapeDtypeStruct((b, s, _GQK), F32),
            jax.ShapeDtypeStruct((b, s, _GW), BF16),
            jax.ShapeDtypeStruct((b, s, _GW), F32),
            jax.ShapeDtypeStruct((b, s, _GQK), F32),
        ],
        compiler_params=_params("parallel", "parallel"),
        name="in_proj",
    )(x, norm_g.reshape(1, d), w, jnp.tile(da_qn, 2 * DA_HEADS).reshape(1, _QK),
      jnp.tile(da_kn, 2 * DA_HEADS).reshape(1, _QK), grp, gw, gate_b.reshape(1, _GQK))


def _split_q(qt):
    row = lax.broadcasted_iota(I32, qt.shape, 0)
    zero = jnp.zeros_like(qt)
    return jnp.where(row < DA_QK_DIM, qt, zero), jnp.where(row >= DA_QK_DIM, qt, zero)


def _chunk_causal_mask(blk):
    key_chunk = lax.broadcasted_iota(I32, (blk, blk), 0) // CHUNK
    qry_chunk = lax.broadcasted_iota(I32, (blk, blk), 1) // CHUNK
    return key_chunk <= qry_chunk


def _diff_attn_finish(lq1_ref, lk1_ref, lq2_ref, lk2_ref, gain_ref, a1, l1, a2, l2):
    lam = (jnp.exp(jnp.sum(lq1_ref[...] * lk1_ref[...], axis=-1, keepdims=True))
           - jnp.exp(jnp.sum(lq2_ref[...] * lk2_ref[...], axis=-1, keepdims=True)) + LAM_INIT)
    o = a1 / l1 - lam * (a2 / l2)
    ms = jnp.mean(o * o, axis=0, keepdims=True)
    o = o * lax.rsqrt(ms + EPS) * gain_ref[...] * (1.0 - LAM_INIT)
    return o.T.astype(BF16)


def _diff_attn_bounded_kernel(lq1_ref, lk1_ref, lq2_ref, lk2_ref, gain_ref, qt_ref, k_ref, vt_ref, out_ref,
                              s_ref, l1_ref, a1_ref, l2_ref, a2_ref, *, blk, nb):
    stats = ((l1_ref, a1_ref), (l2_ref, a2_ref))
    mask = _chunk_causal_mask(blk)

    def reset():
        for l_ref, a_ref in stats:
            l_ref[...] = jnp.zeros(l_ref.shape, F32)
            a_ref[...] = jnp.zeros(a_ref.shape, F32)

    def scores(q, j, slot):
        kb = k_ref[0, pl.ds(pl.multiple_of(j * blk, blk), blk), :]
        s_ref[slot, 0] = _dot(kb, q[0])
        s_ref[slot, 1] = _dot(kb, q[1])

    def consume(j, slot, masked):
        vb = vt_ref[0, j]
        for m, (l_ref, a_ref) in enumerate(stats):
            s = s_ref[slot, m]
            if masked:
                s = jnp.where(mask, s, NEG_INF)
            p = jnp.exp2(s)
            l_ref[...] += jnp.sum(p, axis=0, keepdims=True)
            a_ref[...] += _dot(vb, p.astype(BF16))

    def step(q, j, slot):
        scores(q, j + 1, 1 - slot)
        consume(j, slot, False)

    reset()
    q = _split_q(qt_ref[0, 0])
    slot = 0
    scores(q, 0, slot)
    for qi in range(nb):
        def pair(i, carry, q=q, slot=slot):
            step(q, 2 * i, slot)
            step(q, 2 * i + 1, 1 - slot)
            return carry

        if qi // 2:
            lax.fori_loop(0, qi // 2, pair, 0)
        if qi % 2:
            step(q, qi - 1, slot)
            slot = 1 - slot
        if qi + 1 < nb:
            q = _split_q(qt_ref[0, qi + 1])
            scores(q, 0, 1 - slot)
        consume(qi, slot, True)
        out_ref[0, qi * blk:(qi + 1) * blk, :] = _diff_attn_finish(
            lq1_ref, lk1_ref, lq2_ref, lk2_ref, gain_ref, a1_ref[...], l1_ref[...], a2_ref[...], l2_ref[...])
        if qi + 1 < nb:
            reset()
        slot = 1 - slot


def _diff_attn_online_kernel(lq1_ref, lk1_ref, lq2_ref, lk2_ref, gain_ref, qt_ref, k_ref, vt_ref, out_ref,
                             m1_ref, l1_ref, a1_ref, m2_ref, l2_ref, a2_ref, *, blk):
    qi = pl.program_id(2)
    q1, q2 = _split_q(qt_ref[0, 0])

    for m_ref, l_ref, a_ref in ((m1_ref, l1_ref, a1_ref), (m2_ref, l2_ref, a2_ref)):
        m_ref[...] = jnp.full(m_ref.shape, NEG_INF, F32)
        l_ref[...] = jnp.zeros(l_ref.shape, F32)
        a_ref[...] = jnp.zeros(a_ref.shape, F32)

    def update(s, vb, m_ref, l_ref, a_ref):
        m_old = m_ref[...]
        m_new = jnp.maximum(m_old, jnp.max(s, axis=0, keepdims=True))
        alpha = jnp.exp2(m_old - m_new)
        p = jnp.exp2(s - m_new)
        l_ref[...] = alpha * l_ref[...] + jnp.sum(p, axis=0, keepdims=True)
        a_ref[...] = alpha * a_ref[...] + _dot(vb, p.astype(BF16))
        m_ref[...] = m_new

    def block(j, mask):
        kb = k_ref[0, pl.ds(pl.multiple_of(j * blk, blk), blk), :]
        vb = vt_ref[0, j]
        s1 = _dot(kb, q1)
        s2 = _dot(kb, q2)
        if mask is not None:
            s1 = jnp.where(mask, s1, NEG_INF)
            s2 = jnp.where(mask, s2, NEG_INF)
        update(s1, vb, m1_ref, l1_ref, a1_ref)
        update(s2, vb, m2_ref, l2_ref, a2_ref)

    def body(j, carry):
        block(j, None)
        return carry

    lax.fori_loop(0, qi, body, 0)
    block(qi, _chunk_causal_mask(blk))
    out_ref[0] = _diff_attn_finish(lq1_ref, lk1_ref, lq2_ref, lk2_ref, gain_ref,
                                   a1_ref[...], l1_ref[...], a2_ref[...], l2_ref[...])


SCORE_BOUND = 60.0


def _diff_attn(qt, k, vt, lq1, lk1, lq2, lk2, da_on, da_qn, da_kn):
    b, nb, _, blk = qt.shape
    s = nb * blk
    stat = lambda: pltpu.VMEM((1, blk), F32)
    acc = lambda: pltpu.VMEM((DA_V_DIM, blk), F32)

    args = (lq1.reshape(1, -1), lk1.reshape(1, -1), lq2.reshape(1, -1), lk2.reshape(1, -1),
            da_on.reshape(-1, 1), qt, k, vt)
    out_shape = jax.ShapeDtypeStruct((b, s, _DAW), BF16)
    head = lambda *trailing: (lambda i, h: (i, 0, h) + trailing)
    vec2 = lambda: pl.BlockSpec((1, DA_QK_DIM), lambda i, h: (0, 0))
    bounded = pl.pallas_call(
        functools.partial(_diff_attn_bounded_kernel, blk=blk, nb=nb),
        grid=(b, DA_HEADS),
        in_specs=[
            vec2(), vec2(), vec2(), vec2(),
            pl.BlockSpec((DA_V_DIM, 1), lambda i, h: (0, 0)),
            pl.BlockSpec((1, nb, 2 * DA_QK_DIM, blk), head(0)),
            pl.BlockSpec((1, s, 2 * DA_QK_DIM), head()),
            pl.BlockSpec((1, nb, DA_V_DIM, blk), head(0)),
        ],
        out_specs=pl.BlockSpec((1, s, DA_V_DIM), head()),
        out_shape=out_shape,
        scratch_shapes=[pltpu.VMEM((2, 2, blk, blk), F32), stat(), acc(), stat(), acc()],
        compiler_params=_params("parallel", "parallel"),
        name="diff_attn",
    )
    vec3 = lambda: pl.BlockSpec((1, DA_QK_DIM), lambda i, h, q: (0, 0))
    online = pl.pallas_call(
        functools.partial(_diff_attn_online_kernel, blk=blk),
        grid=(b, DA_HEADS, nb),
        in_specs=[
            vec3(), vec3(), vec3(), vec3(),
            pl.BlockSpec((DA_V_DIM, 1), lambda i, h, q: (0, 0)),
            pl.BlockSpec((1, 1, 2 * DA_QK_DIM, blk), lambda i, h, q: (i, q, h, 0)),
            pl.BlockSpec((1, s, 2 * DA_QK_DIM), lambda i, h, q: (i, 0, h)),
            pl.BlockSpec((1, nb, DA_V_DIM, blk), lambda i, h, q: (i, 0, h, 0)),
        ],
        out_specs=pl.BlockSpec((1, blk, DA_V_DIM), lambda i, h, q: (i, q, h)),
        out_shape=out_shape,
        scratch_shapes=[stat(), stat(), acc(), stat(), stat(), acc()],
        compiler_params=_params("parallel", "parallel", "parallel"),
        name="diff_attn_online",
    )
    bound = (1.01 * DA_QK_DIM ** 0.5 * math.log2(math.e)) * jnp.max(jnp.abs(da_qn)) * jnp.max(jnp.abs(da_kn))
    return lax.cond(bound <= SCORE_BOUND, bounded, online, *args)


def _gla_kernel(q_ref, k_ref, la_ref, v_ref, g_ref, gain_ref, out_ref, st_ref, *, ts, group):
    @pl.when(pl.program_id(1) == 0)
    def _():
        st_ref[...] = jnp.zeros(st_ref.shape, F32)

    c = CHUNK
    rows = group * c
    hk, hv = _GQK, _GW
    r = lax.broadcasted_iota(I32, (rows, rows), 0)
    cc = lax.broadcasted_iota(I32, (rows, rows), 1)
    tri = jnp.where(jnp.logical_and(r // c == cc // c, r >= cc), 1.0, 0.0).astype(BF16)
    bd_k = (lax.broadcasted_iota(I32, (hk, hk), 0) // GLA_K_DIM
            == lax.broadcasted_iota(I32, (hk, hk), 1) // GLA_K_DIM)
    bd_v = (lax.broadcasted_iota(I32, (hk, hv), 0) // GLA_K_DIM
            == lax.broadcasted_iota(I32, (hk, hv), 1) // GLA_V_DIM)
    bd_vt = (lax.broadcasted_iota(I32, (hv, hk), 0) // GLA_V_DIM
             == lax.broadcasted_iota(I32, (hv, hk), 1) // GLA_K_DIM)
    lower = (lax.broadcasted_iota(I32, (c, hk), 0)
             >= lax.broadcasted_iota(I32, (c, hk), 1) % c)

    def tiled(t, mask):
        t4 = jnp.concatenate([t] * GLA_HEADS, axis=0)
        return jnp.where(mask, t4, jnp.zeros_like(t4))

    def chunk_row(t, row):
        return jnp.concatenate([jnp.broadcast_to(t[i * c + row:i * c + row + 1, :], (c, hk)) for i in range(group)],
                               axis=0)

    def body(gi, carry):
        sl = pl.ds(pl.multiple_of(gi * rows, rows), rows)
        la_hi, la_lo = _split_bf16(la_ref[0, sl, :])
        big_l = _dot(tri, la_hi) + _dot(tri, la_lo)
        l_end = chunk_row(big_l, c - 1)
        lc = big_l - chunk_row(big_l, c // 2 - 1)
        e_pos = jnp.exp2(lc)
        e_neg = jnp.exp2(-lc)
        q = q_ref[0, sl, :]
        k = k_ref[0, sl, :]
        v = v_ref[0, sl, :]
        q_pos = (q * e_pos).astype(BF16)
        q_neg = (q * e_neg).astype(BF16)
        k_pos = (k * e_pos).astype(BF16)
        k_neg = (k * e_neg).astype(BF16)
        q_in = (q * jnp.exp2(big_l)).astype(BF16)
        k_out = (k * jnp.exp2(l_end - big_l)).astype(BF16)
        decay = jnp.exp2(l_end)

        o_intra, u_t = [], []
        for i in range(group):
            cs = slice(i * c, (i + 1) * c)
            a_past = _dot_nt(q_pos[cs], tiled(k_neg[cs], bd_k))
            a_fut = _dot_nt(q_neg[cs], tiled(k_pos[cs], bd_k))
            a = jnp.where(lower, a_past, a_fut).astype(BF16)
            o_intra.append(_dot(a, tiled(v[cs], bd_v)))
            u_t.append(jnp.where(bd_vt, _dot_tn(v[cs], k_out[cs]), 0.0))

        st = st_ref[...]
        o_inter = []
        for i in range(group):
            cs = slice(i * c, (i + 1) * c)
            o_inter.append(_dot_nt(q_in[cs], st.astype(BF16)))
            st = st * decay[i * c:i * c + 1, :] + u_t[i]
        st_ref[...] = st

        o = jnp.concatenate(o_intra, axis=0) + jnp.concatenate(o_inter, axis=0)
        g = g_ref[0, sl, :]
        silu = g / (1.0 + jnp.exp(-g))
        for h in range(GLA_HEADS):
            hs = slice(h * GLA_V_DIM, (h + 1) * GLA_V_DIM)
            out_ref[0, sl, hs] = (_rms(o[:, hs], gain_ref[...]) * silu[:, hs]).astype(BF16)
        return carry

    lax.fori_loop(0, ts // rows, body, 0)


def _gla(gq, gk, la, gv, gg, gla_on):
    b, s, _ = gq.shape
    ts = TS_GLA
    tile = lambda width: pl.BlockSpec((1, ts, width), lambda i, j: (i, j, 0))
    return pl.pallas_call(
        functools.partial(_gla_kernel, ts=ts, group=GLA_GROUP),
        grid=(b, s // ts),
        in_specs=[tile(_GQK), tile(_GQK), tile(_GQK), tile(_GW), tile(_GW),
                  pl.BlockSpec((1, GLA_V_DIM), lambda i, j: (0, 0))],
        out_specs=tile(_GW),
        out_shape=jax.ShapeDtypeStruct((b, s, _GW), BF16),
        scratch_shapes=[pltpu.VMEM((_GW, _GQK), F32)],
        compiler_params=_params("parallel", "arbitrary"),
        name="gla",
    )(gq, gk, la, gv, gg, gla_on.reshape(1, -1))


_META_E0, _META_E1, _META_G0, _META_G1, _META_P0, _META_P1 = range(6)
_EXP_LANE0 = N_GROUPS


def _post_kernel(*refs, d, tm, sub, rider):
    if rider:
        (size_ref, pend_ref, x_ref, da_ref, gla_ref, wo_ref, gc_ref, wq_ref, qn_ref, km_ref, vm_ref, wco_ref,
         gf_ref, wr_ref, br_ref, rdest_ref, rsrc_ref,
         h_ref, xn_ref, meta_ref, ids_ref, cnt_ref, rpad_ref, zero_ref, rsem, zsem) = refs
    else:
        (x_ref, da_ref, gla_ref, wo_ref, gc_ref, wq_ref, qn_ref, km_ref, vm_ref, wco_ref,
         gf_ref, wr_ref, br_ref, h_ref, xn_ref, meta_ref, ids_ref, cnt_ref) = refs
    first = jnp.logical_and(pl.program_id(0) == 0, pl.program_id(1) == 0)

    @pl.when(first)
    def _():
        cnt_ref[...] = jnp.zeros(cnt_ref.shape, F32)
        if rider:
            _zero_fill(size_ref, pend_ref, rpad_ref, zero_ref, zsem)

    if rider:
        _scatter_rows(rdest_ref, rsrc_ref, rpad_ref, rsem)

    half = d // 2
    hd = d // CROSS_HEADS
    lane = lax.broadcasted_iota(I32, (sub, LANES), 1)
    big = jnp.int32(LANES)
    strict_lower = jnp.where(lax.broadcasted_iota(I32, (sub, sub), 0) > lax.broadcasted_iota(I32, (sub, sub), 1),
                             1.0, 0.0).astype(BF16)

    def lane_argmax(vals):
        m = jnp.max(vals, axis=-1, keepdims=True)
        idx = jnp.min(jnp.where(vals == m, lane, big), axis=-1, keepdims=True)
        return m, idx

    def rows(rs, base):
        h1 = x_ref[0, rs, :] + _dot(da_ref[0, rs, :], wo_ref[:half, :]) + _dot(gla_ref[0, rs, :], wo_ref[half:, :])

        u = _rms(h1, gc_ref[...]).astype(BF16)
        q = _dot(u, wq_ref[...])
        heads = []
        for h in range(CROSS_HEADS):
            hs = slice(h * hd, (h + 1) * hd)
            qh = _rms(q[:, hs], qn_ref[...]).astype(BF16)
            sc = _dot_nt(qh, km_ref[0, :, hs])
            sc = sc - jnp.max(sc, axis=-1, keepdims=True)
            p = jnp.exp(sc)
            p = p / jnp.sum(p, axis=-1, keepdims=True)
            heads.append(_dot(p.astype(BF16), vm_ref[0, :, hs]))
        o = jnp.concatenate(heads, axis=-1).astype(BF16)
        h2 = h1 + _dot(o, wco_ref[...])
        h_ref[0, rs, :] = h2

        xn = _rms(h2, gf_ref[...]).astype(BF16)
        bits = lax.bitcast_convert_type(xn.astype(F32), U32)
        xn_ref[0, rs, :] = (bits[:, :half] >> 16) | (bits[:, half:] & HI16)
        logits = _dot(xn, wr_ref[...]) + br_ref[...]

        lg = jnp.where(lane < N_GROUPS, logits, NEG_INF)
        g_max, g_sel = lane_argmax(lg)
        p_g = 1.0 / jnp.sum(jnp.exp(lg - g_max), axis=-1, keepdims=True)
        e_lo = _EXP_LANE0 + g_sel * EXPERTS_PER_GROUP
        in_group = jnp.logical_and(lane >= e_lo, lane < e_lo + EXPERTS_PER_GROUP)
        le = jnp.where(in_group, logits, NEG_INF)
        m1, i1 = lane_argmax(le)
        m2, i2 = lane_argmax(jnp.where(lane == i1, NEG_INF, le))
        e2 = jnp.exp(m2 - m1)
        gate0 = p_g / (1.0 + e2)
        gate1 = p_g * e2 / (1.0 + e2)
        e0 = i1 - _EXP_LANE0
        e1 = i2 - _EXP_LANE0

        hot0 = lane == e0
        hot1 = lane == e1
        onehot = jnp.where(jnp.logical_or(hot0, hot1), 1.0, 0.0)
        before = _dot(strict_lower, onehot.astype(BF16)) + base
        pos0 = jnp.sum(jnp.where(hot0, before, 0.0), axis=-1, keepdims=True)
        pos1 = jnp.sum(jnp.where(hot1, before, 0.0), axis=-1, keepdims=True)

        meta = jnp.zeros(logits.shape, F32)
        for idx, val in ((_META_E0, e0.astype(F32)), (_META_E1, e1.astype(F32)), (_META_G0, gate0),
                         (_META_G1, gate1), (_META_P0, pos0), (_META_P1, pos1)):
            meta = jnp.where(lane == idx, val, meta)
        meta_ref[0, rs, :] = meta
        ids_ref[0, 0, :, rs] = meta.T[:ids_ref.shape[2], :]
        return base + jnp.sum(onehot, axis=0, keepdims=True)

    base = cnt_ref[0:1, :]
    for r0 in range(0, tm, sub):
        base = rows(slice(r0, r0 + sub), base)
    cnt_ref[...] = jnp.broadcast_to(base, cnt_ref.shape)
    if rider:
        _scatter_wait(rsrc_ref, rpad_ref, rsem)


def _post(x, da, gla, w_o, norm_cross, w_cq, cross_qn, k_mem, v_mem, w_co, norm_ffn, w_group, b_group,
          w_expert, b_expert, *, half, rider=None):
    b, s, d = x.shape
    bh = b // 2
    b0 = half * bh
    tm = TM_POST
    ns = s // tm
    m = k_mem.shape[1]
    w_r = jnp.pad(jnp.concatenate([w_group, w_expert], axis=1), ((0, 0), (0, LANES - N_GROUPS - N_EXPERTS)))
    b_r = jnp.pad(jnp.concatenate([b_group, b_expert]), (0, LANES - N_GROUPS - N_EXPERTS)).reshape(1, LANES)
    const = lambda shape: pl.BlockSpec(shape, lambda i, j, *_: (0,) * len(shape))
    tile_in = lambda width: pl.BlockSpec((1, tm, width), lambda i, j, *_: (i + b0, j, 0))
    tile_out = lambda width: pl.BlockSpec((1, tm, width), lambda i, j, *_: (i, j, 0))
    per_b = lambda: pl.BlockSpec((1, m, d), lambda i, j, *_: (i + b0, 0, 0))
    in_specs = [tile_in(d), tile_in(d // 2), tile_in(d // 2), const((d, d)), const((1, d)), const((d, d)),
                const((1, d // CROSS_HEADS)), per_b(), per_b(), const((d, d)), const((1, d)),
                const((d, LANES)), const((1, LANES))]
    out_specs = [tile_out(d), tile_out(d // 2), tile_out(LANES),
                 pl.BlockSpec((1, 1, 8, tm), lambda i, j, *_: (i, j, 0, 0)), const((8, LANES))]
    out_shape = [
        jax.ShapeDtypeStruct((bh, s, d), F32),
        jax.ShapeDtypeStruct((bh, s, d // 2), U32),
        jax.ShapeDtypeStruct((bh, s, LANES), F32),
        jax.ShapeDtypeStruct((bh, ns, 8, tm), F32),
        jax.ShapeDtypeStruct((8, LANES), F32),
    ]
    args = (x, da, gla, w_o.astype(BF16), norm_cross.reshape(1, d), w_cq.astype(BF16), cross_qn.reshape(1, -1),
            k_mem, v_mem, w_co.astype(BF16), norm_ffn.reshape(1, d), w_r.astype(BF16), b_r)
    scalars, scratch = (), []
    if rider is not None:
        sizes, pend, dest, xn_src, n_rows = rider
        scalars = (sizes, pend)
        in_specs += [pl.BlockSpec((1, 2, tm), lambda i, j, *_: (i * ns + j, 0, 0), memory_space=pltpu.SMEM),
                     pl.BlockSpec((tm, d // 2), lambda i, j, *_: (i * ns + j, 0))]
        out_specs.append(pl.BlockSpec(memory_space=pl.ANY))
        out_shape.append(jax.ShapeDtypeStruct((n_rows, d // 2), U32))
        scratch = [pltpu.VMEM((EXPERT_ROWS, d // 2), U32), pltpu.SemaphoreType.DMA, pltpu.SemaphoreType.DMA]
        args += (dest, xn_src)
    return pl.pallas_call(
        functools.partial(_post_kernel, d=d, tm=tm, sub=SUB_POST, rider=rider is not None),
        grid_spec=pltpu.PrefetchScalarGridSpec(
            num_scalar_prefetch=len(scalars), grid=(bh, ns), in_specs=in_specs, out_specs=out_specs,
            scratch_shapes=scratch),
        out_shape=out_shape,
        compiler_params=_params("arbitrary", "arbitrary"),
        name="post_scatter" if rider is not None else "post",
    )(*scalars, *args)


def _row_copy(src_ref, src_row, dst_ref, dst_row, sem):
    return pltpu.make_async_copy(src_ref.at[pl.ds(src_row, 1)], dst_ref.at[pl.ds(dst_row, 1)], sem)


def _zero_fill(size_ref, pend_ref, xpad_ref, zero_ref, zsem):
    zero_ref[...] = jnp.zeros(zero_ref.shape, zero_ref.dtype)
    rows = zero_ref.shape[0]
    n_blocks = xpad_ref.shape[0] // rows
    n_used = pend_ref[N_EXPERTS - 1] // rows

    def zero_block(blk):
        return pltpu.make_async_copy(zero_ref, xpad_ref.at[pl.ds(pl.multiple_of(blk * rows, rows), rows)], zsem)

    def last_block(e, fn):
        @pl.when(size_ref[e] > 0)
        def _():
            fn(zero_block(pend_ref[e] // rows - 1))

    for fn in (lambda cp: cp.start(), lambda cp: cp.wait()):
        lax.fori_loop(0, N_EXPERTS, lambda e, c: (last_block(e, fn), c)[1], 0)
        lax.fori_loop(n_used, n_blocks, lambda blk, c: (fn(zero_block(blk)), c)[1], 0)


def _scatter_rows(dest_ref, src_ref, xpad_ref, sem):
    for t in range(src_ref.shape[0]):
        for k in range(2):
            _row_copy(src_ref, t, xpad_ref, dest_ref[0, k, t], sem).start(priority=k)


def _scatter_wait(src_ref, xpad_ref, sem):
    for _ in range(2):
        pltpu.make_async_copy(src_ref, xpad_ref.at[pl.ds(0, src_ref.shape[0])], sem).wait()


def _cast_expert_weights(fresh, wg_ref, wu_ref, wd_ref, wg_s, wu_s, wd_s):
    @pl.when(fresh)
    def _():
        wg_s[...] = wg_ref[0].astype(BF16)
        wu_s[...] = wu_ref[0].astype(BF16)
        wd_s[...] = wd_ref[0].astype(BF16)


def _expert_mlp(x_ref, wg_s, wu_s, wd_s, out_ref):
    words = x_ref[...]
    half = words.shape[1]
    lo = lax.bitcast_convert_type(words << 16, F32).astype(BF16)
    hi = lax.bitcast_convert_type(words & HI16, F32).astype(BF16)
    gate = _dot(lo, wg_s[:half, :]) + _dot(hi, wg_s[half:, :])
    up = _dot(lo, wu_s[:half, :]) + _dot(hi, wu_s[half:, :])
    hid = gate / (1.0 + jnp.exp(-gate)) * up
    out_ref[...] = _dot(hid.astype(BF16), wd_s[...])


def _experts_scatter_kernel(be_ref, nused_ref, size_ref, pend_ref, x_ref, wg_ref, wu_ref, wd_ref, rdest_ref, rsrc_ref,
                            out_ref, rpad_ref, wg_s, wu_s, wd_s, zero_ref, rsem, zsem, *, rider_steps):
    i = pl.program_id(0)
    used = i < nused_ref[0]
    new_expert = jnp.logical_or(i == 0, be_ref[i] != be_ref[jnp.maximum(i - 1, 0)])
    mlp = functools.partial(_expert_mlp, x_ref, wg_s, wu_s, wd_s, out_ref)

    @pl.when(i == 0)
    def _():
        _zero_fill(size_ref, pend_ref, rpad_ref, zero_ref, zsem)

    _cast_expert_weights(jnp.logical_and(used, new_expert), wg_ref, wu_ref, wd_ref, wg_s, wu_s, wd_s)
    riding = i < rider_steps

    @pl.when(riding)
    def _():
        _scatter_rows(rdest_ref, rsrc_ref, rpad_ref, rsem)
        mlp()
        _scatter_wait(rsrc_ref, rpad_ref, rsem)

    pl.when(jnp.logical_and(used, jnp.logical_not(riding)))(mlp)

    @pl.when(jnp.logical_not(used))
    def _():
        out_ref[...] = jnp.zeros(out_ref.shape, F32)


def _gather_rows(dest_ref, src_ref, buf_ref, sem):
    for t in range(buf_ref.shape[1]):
        for k in range(2):
            _row_copy(src_ref, dest_ref[0, k, t], buf_ref.at[k], t, sem).start(priority=k)


def _combine_rows(h_ref, meta_ref, src_ref, buf_ref, sem):
    for k in range(2):
        pltpu.make_async_copy(src_ref.at[pl.ds(0, buf_ref.shape[1])], buf_ref.at[k], sem).wait()
    meta = meta_ref[...]
    g0 = meta[:, _META_G0:_META_G0 + 1]
    g1 = meta[:, _META_G1:_META_G1 + 1]
    return h_ref[...] + g0 * buf_ref[0] + g1 * buf_ref[1]


def _experts_gather_kernel(be_ref, nused_ref, x_ref, wg_ref, wu_ref, wd_ref, gdest_ref, h_ref, meta_ref, opad_ref,
                           out_ref, y_ref, wg_s, wu_s, wd_s, gbuf_ref, gsem, *, rider_steps):
    i = pl.program_id(0)
    used = i < nused_ref[0]
    new_expert = jnp.logical_or(i == 0, be_ref[i] != be_ref[jnp.maximum(i - 1, 0)])
    mlp = functools.partial(_expert_mlp, x_ref, wg_s, wu_s, wd_s, out_ref)
    _cast_expert_weights(jnp.logical_and(used, new_expert), wg_ref, wu_ref, wd_ref, wg_s, wu_s, wd_s)

    def gather():
        _gather_rows(gdest_ref, opad_ref, gbuf_ref.at[i % 2], gsem.at[i % 2])

    def combine():
        slot = (i - 1) % 2
        y_ref[...] = _combine_rows(h_ref, meta_ref, opad_ref, gbuf_ref.at[slot], gsem.at[slot])

    @pl.when(i == 0)
    def _():
        gather()
        mlp()

    @pl.when(jnp.logical_and(i > 0, i < rider_steps))
    def _():
        gather()
        mlp()
        combine()

    @pl.when(i == rider_steps)
    def _():
        combine()
        pl.when(used)(mlp)

    pl.when(jnp.logical_and(i > rider_steps, used))(mlp)

    @pl.when(jnp.logical_not(used))
    def _():
        out_ref[...] = jnp.zeros(out_ref.shape, F32)


def _experts_kernel(be_ref, nused_ref, x_ref, wg_ref, wu_ref, wd_ref, out_ref, wg_s, wu_s, wd_s):
    i = pl.program_id(0)
    used = i < nused_ref[0]
    new_expert = jnp.logical_or(i == 0, be_ref[i] != be_ref[jnp.maximum(i - 1, 0)])
    _cast_expert_weights(jnp.logical_and(used, new_expert), wg_ref, wu_ref, wd_ref, wg_s, wu_s, wd_s)
    pl.when(used)(functools.partial(_expert_mlp, x_ref, wg_s, wu_s, wd_s, out_ref))

    @pl.when(jnp.logical_not(used))
    def _():
        out_ref[...] = jnp.zeros(out_ref.shape, F32)


def _retile(dest, tr):
    tiles, _, tm = dest.shape
    return dest.reshape(tiles, 2, tm // tr, tr).transpose(0, 2, 1, 3).reshape(tiles * (tm // tr), 2, tr)


def _experts(x_pad, block_expert, n_used, w_gate, w_up, w_down, scatter=None, gather=None):
    n_rows = x_pad.shape[0]
    _, d, f = w_gate.shape
    rows, tr = EXPERT_ROWS, RIDER_ROWS
    row_blk = lambda i, be, nu, *_: (jnp.minimum(i, nu[0] - 1), 0)
    weights = lambda shape: pl.BlockSpec(shape, lambda i, be, *_: (be[i], 0, 0))
    in_specs = [pl.BlockSpec((rows, d // 2), row_blk), weights((1, d, f)), weights((1, d, f)), weights((1, f, d))]
    out_specs = [pl.BlockSpec((rows, d), lambda i, *_: (i, 0))]
    out_shape = [jax.ShapeDtypeStruct((n_rows, d), F32)]
    scratch = [pltpu.VMEM((d, f), BF16), pltpu.VMEM((d, f), BF16), pltpu.VMEM((f, d), BF16)]
    scalars, args = (block_expert, n_used), (x_pad, w_gate, w_up, w_down)
    body, name = _experts_kernel, "experts"
    if scatter is not None or gather is not None:
        n_tokens = (scatter[3] if scatter is not None else gather[1]).shape[0]
        steps = n_tokens // tr
        assert 2 * n_tokens >= steps * rows, "riding steps must all be used expert blocks"
        tile = lambda width, shift: pl.BlockSpec((tr, width), lambda i, *_: (jnp.clip(i - shift, 0, steps - 1), 0))
        ids = lambda: pl.BlockSpec((1, 2, tr), lambda i, *_: (jnp.minimum(i, steps - 1), 0, 0),
                                   memory_space=pltpu.SMEM)
    if scatter is not None:
        sizes, pend, dest, xn_src, other_rows = scatter
        scalars += (sizes, pend)
        in_specs += [ids(), tile(d // 2, 0)]
        out_specs.append(pl.BlockSpec(memory_space=pl.ANY))
        out_shape.append(jax.ShapeDtypeStruct((other_rows, d // 2), U32))
        scratch += [pltpu.VMEM((rows, d // 2), U32), pltpu.SemaphoreType.DMA, pltpu.SemaphoreType.DMA]
        args += (_retile(dest, tr), xn_src)
        body, name = functools.partial(_experts_scatter_kernel, rider_steps=steps), "experts_scatter"
    elif gather is not None:
        dest, h2, meta, opad = gather
        in_specs += [ids(), tile(d, 1), tile(LANES, 1), pl.BlockSpec(memory_space=pl.ANY)]
        out_specs.append(tile(d, 1))
        out_shape.append(jax.ShapeDtypeStruct((n_tokens, d), F32))
        scratch += [pltpu.VMEM((2, 2, tr, d), F32), pltpu.SemaphoreType.DMA((2,))]
        args += (_retile(dest, tr), h2, meta, opad)
        body, name = functools.partial(_experts_gather_kernel, rider_steps=steps), "experts_gather"
    out = pl.pallas_call(
        body,
        grid_spec=pltpu.PrefetchScalarGridSpec(
            num_scalar_prefetch=len(scalars), grid=(n_rows // rows,), in_specs=in_specs, out_specs=out_specs,
            scratch_shapes=scratch),
        out_shape=out_shape,
        compiler_params=_params("arbitrary"),
        name=name,
    )(*scalars, *args)
    return out if len(out) > 1 else out[0]


def _combine_tail_kernel(dest_ref, y0_ref, h_ref, meta_ref, opad_ref, y_ref, buf_ref, sem, *, nt):
    s = pl.program_id(0)

    @pl.when(s < nt)
    def _():
        _gather_rows(dest_ref, opad_ref, buf_ref.at[s % 2], sem.at[s % 2])

    @pl.when(s > 0)
    def _():
        slot = (s - 1) % 2
        y_ref[0] = y0_ref[...]
        y_ref[1] = _combine_rows(h_ref, meta_ref, opad_ref, buf_ref.at[slot], sem.at[slot])


def _combine_tail(y0, h2, meta, dest, out_pad):
    th, d = h2.shape
    tm = TM_TAIL
    nt = th // tm
    prev = lambda s: (jnp.maximum(s - 1, 0), 0)
    return pl.pallas_call(
        functools.partial(_combine_tail_kernel, nt=nt),
        grid=(nt + 1,),
        in_specs=[
            pl.BlockSpec((1, 2, tm), lambda s: (jnp.minimum(s, nt - 1), 0, 0), memory_space=pltpu.SMEM),
            pl.BlockSpec((tm, d), prev), pl.BlockSpec((tm, d), prev), pl.BlockSpec((tm, LANES), prev),
            pl.BlockSpec(memory_space=pl.ANY),
        ],
        out_specs=pl.BlockSpec((2, tm, d), lambda s: (0, jnp.maximum(s - 1, 0), 0)),
        scratch_shapes=[pltpu.VMEM((2, 2, tm, d), F32), pltpu.SemaphoreType.DMA((2,))],
        out_shape=jax.ShapeDtypeStruct((2, th, d), F32),
        compiler_params=_params("arbitrary"),
        name="combine_tail",
    )(_retile(dest, tm), y0, h2, meta, out_pad)


def _route_tables(counts, ids, n_tokens):
    rows = EXPERT_ROWS
    sizes = counts[0, :N_EXPERTS].astype(I32)
    padded = (sizes + rows - 1) // rows * rows
    pend = jnp.cumsum(padded)
    pstart = pend - padded
    n_rows = 2 * n_tokens + N_EXPERTS * rows
    block_start = jnp.arange(n_rows // rows, dtype=I32) * rows
    block_expert = jnp.minimum(jnp.sum(pend[None, :] <= block_start[:, None], axis=1), N_EXPERTS - 1).astype(I32)
    n_used = (pend[-1:] // rows).astype(I32)
    expert = ids[:, _META_E0:_META_E1 + 1].astype(I32)
    rank = ids[:, _META_P0:_META_P1 + 1].astype(I32)
    dest = sum(jnp.where(expert == e, pstart[e], 0) for e in range(N_EXPERTS)) + rank
    return sizes, pend, block_expert, n_used, dest, n_rows


def kernel(x, mem, norm_mix, w_in, da_q_norm, da_k_norm, lambda_q1, lambda_k1, lambda_q2, lambda_k2,
           da_out_norm, gla_gate_w, gla_gate_b, gla_out_norm, w_o, norm_cross, norm_mem, w_cq, w_ckv,
           cross_q_norm, cross_k_norm, w_co, norm_ffn, w_group, b_group, w_expert, b_expert,
           w_e_gate, w_e_up, w_e_down):
    b, s, d = x.shape
    th = b // 2 * s
    h = x
    for l in range(norm_mix.shape[0]):
        assert l == 0, "lam_init is fixed for a single layer"
        qt, kda, vt, gq, gk, gv, gg, la = _in_proj(h, norm_mix[l], w_in[l], da_q_norm[l], da_k_norm[l],
                                                   gla_gate_w[l], gla_gate_b[l])
        da = _diff_attn(qt, kda, vt, lambda_q1[l], lambda_k1[l], lambda_q2[l], lambda_k2[l], da_out_norm[l],
                        da_q_norm[l], da_k_norm[l])
        gla = _gla(gq, gk, la, gv, gg, gla_out_norm[l])
        k_mem, v_mem = _mem_kv(mem, norm_mem[l], w_ckv[l], cross_k_norm[l])
        post = functools.partial(_post, h, da, gla, w_o[l], norm_cross[l], w_cq[l], cross_q_norm[l], k_mem, v_mem,
                                 w_co[l], norm_ffn[l], w_group[l], b_group[l], w_expert[l], b_expert[l])
        experts = functools.partial(_experts, w_gate=w_e_gate[l], w_up=w_e_up[l], w_down=w_e_down[l])

        h2_0, xn_0, meta_0, ids_0, counts_0 = post(half=0)
        sizes_0, pend_0, be_0, used_0, dest_0, n_rows = _route_tables(counts_0, ids_0.reshape(-1, 8, TM_POST), th)
        h2_1, xn_1, meta_1, ids_1, counts_1, xpad_0 = post(
            half=1, rider=(sizes_0, pend_0, dest_0, xn_0.reshape(th, d // 2), n_rows))
        sizes_1, pend_1, be_1, used_1, dest_1, _ = _route_tables(counts_1, ids_1.reshape(-1, 8, TM_POST), th)
        opad_0, xpad_1 = experts(xpad_0, be_0, used_0,
                                 scatter=(sizes_1, pend_1, dest_1, xn_1.reshape(th, d // 2), n_rows))
        opad_1, y_0 = experts(xpad_1, be_1, used_1,
                              gather=(dest_0, h2_0.reshape(th, d), meta_0.reshape(th, LANES), opad_0))
        h = _combine_tail(y_0, h2_1.reshape(th, d), meta_1.reshape(th, LANES), dest_1, opad_1).reshape(b, s, d)
    return h
```

```python
import functools
import math

import jax
import jax.numpy as jnp
import numpy as np
from jax import lax
from jax.experimental import pallas as pl
from jax.experimental.pallas import tpu as pltpu

F32 = jnp.float32
BF16 = jnp.bfloat16
I32 = jnp.int32
U32 = jnp.uint32
HI16 = np.uint32(0xFFFF0000)

EPS = 1e-6
CHUNK = 64

DA_HEADS = 4
DA_QK_DIM = 64
DA_V_DIM = 128
GLA_HEADS = 4
GLA_K_DIM = 64
GLA_V_DIM = 128
GLA_GATE_RANK = 16
GLA_TAU = 16.0
CROSS_HEADS = 4
N_GROUPS = 4
EXPERTS_PER_GROUP = 8
N_EXPERTS = N_GROUPS * EXPERTS_PER_GROUP
LAM_INIT = 0.8 - 0.6 * math.exp(-0.3 * 0)

LANES = 128
VMEM_LIMIT = 56 * 1024 * 1024

TM_PROJ = 1024
ATT_BLK = 512
TS_GLA = 1024
GLA_GROUP = 4
TM_POST = 1024
SUB_POST = 1024
TM_TAIL = 512
RIDER_ROWS = 256
EXPERT_ROWS = 512

NEG_INF = float("-inf")


def _params(*sem):
    return pltpu.CompilerParams(dimension_semantics=sem, vmem_limit_bytes=VMEM_LIMIT)


def _rms(t, g):
    ms = jnp.mean(t * t, axis=-1, keepdims=True)
    return t * lax.rsqrt(ms + EPS) * g


def _dot(a, b):
    return jnp.dot(a, b, preferred_element_type=F32)


def _dot_nt(a, b):
    return lax.dot_general(a, b, (((1,), (1,)), ((), ())), preferred_element_type=F32)


def _dot_tn(a, b):
    return lax.dot_general(a, b, (((0,), (0,)), ((), ())), preferred_element_type=F32)


def _split_bf16(t):
    hi = t.astype(BF16)
    lo = (t - hi.astype(F32)).astype(BF16)
    return hi, lo


def _mem_kv_kernel(mem_ref, g_ref, w_ref, kn_ref, k_ref, v_ref, *, d, heads):
    mn = _rms(mem_ref[0], g_ref[...]).astype(BF16)
    kv = _dot(mn, w_ref[...])
    hd = d // heads
    scale = hd ** -0.5
    for h in range(heads):
        kh = _rms(kv[:, h * hd:(h + 1) * hd], kn_ref[...]) * scale
        k_ref[0, :, h * hd:(h + 1) * hd] = kh.astype(BF16)
    v_ref[0] = kv[:, d:].astype(BF16)


def _mem_kv(mem, norm_m, w_ckv, kn):
    b, m, d = mem.shape
    return pl.pallas_call(
        functools.partial(_mem_kv_kernel, d=d, heads=CROSS_HEADS),
        grid=(b,),
        in_specs=[
            pl.BlockSpec((1, m, d), lambda i: (i, 0, 0)),
            pl.BlockSpec((1, d), lambda i: (0, 0)),
            pl.BlockSpec((d, 2 * d), lambda i: (0, 0)),
            pl.BlockSpec((1, d // CROSS_HEADS), lambda i: (0, 0)),
        ],
        out_specs=[
            pl.BlockSpec((1, m, d), lambda i: (i, 0, 0)),
            pl.BlockSpec((1, m, d), lambda i: (i, 0, 0)),
        ],
        out_shape=[jax.ShapeDtypeStruct((b, m, d), BF16)] * 2,
        compiler_params=_params("parallel"),
        name="mem_kv",
    )(mem, norm_m.reshape(1, d), w_ckv.astype(BF16), kn.reshape(1, -1))


_QK = DA_HEADS * 2 * DA_QK_DIM
_DAW = DA_HEADS * DA_V_DIM
_GQK = GLA_HEADS * GLA_K_DIM
_GW = GLA_HEADS * GLA_V_DIM
_OFF_DQ = 0
_OFF_DK = _OFF_DQ + _QK
_OFF_DV = _OFF_DK + _QK
_OFF_GQ = _OFF_DV + _DAW
_OFF_GK = _OFF_GQ + _GQK
_OFF_GV = _OFF_GK + _GQK
_OFF_GG = _OFF_GV + _GW
_OFF_GR = _OFF_GG + _GW
_IN_PAD = _OFF_GR + LANES


def _in_proj_kernel(x_ref, g_ref, w_ref, qg_ref, kg_ref, grp_ref, gw_ref, gb_ref,
                    qt_ref, k_ref, vt_ref, gq_ref, gk_ref, gv_ref, gg_ref, la_ref):
    u = _rms(x_ref[0], g_ref[...]).astype(BF16)

    def proj(off, width):
        return _dot(u, w_ref[:, off:off + width])

    def group_norm(p, gain):
        ms = _dot((p * p).astype(BF16), grp_ref[...])
        return p * lax.rsqrt(ms + EPS) * gain

    qn = group_norm(proj(_OFF_DQ, _QK), qg_ref[...]) * (DA_QK_DIM ** -0.5 * math.log2(math.e))
    blk = qt_ref.shape[3]
    for t in range(qt_ref.shape[1]):
        qt_ref[0, t] = qn[t * blk:(t + 1) * blk].T.astype(BF16)
    k_ref[0] = group_norm(proj(_OFF_DK, _QK), kg_ref[...]).astype(BF16)
    dv = proj(_OFF_DV, _DAW)
    for t in range(vt_ref.shape[1]):
        vt_ref[0, t] = dv[t * blk:(t + 1) * blk].T.astype(BF16)
    gq_ref[0] = proj(_OFF_GQ, _GQK) * (GLA_K_DIM ** -0.5)
    gk_ref[0] = proj(_OFF_GK, _GQK)
    gv_ref[0] = proj(_OFF_GV, _GW).astype(BF16)
    gg_ref[0] = proj(_OFF_GG, _GW)
    z = _dot(proj(_OFF_GR, LANES).astype(BF16), gw_ref[...]) + gb_ref[...]
    log_sig = jnp.minimum(z, 0.0) - jnp.log(1.0 + jnp.exp(-jnp.abs(z)))
    la_ref[0] = log_sig * (math.log2(math.e) / GLA_TAU)


def _in_proj(x, norm_g, w_in, da_qn, da_kn, gate_w, gate_b):
    b, s, d = x.shape
    tm, blk = TM_PROJ, ATT_BLK
    ns, nb = s // tm, tm // blk
    w = jnp.pad(w_in, ((0, 0), (0, _IN_PAD - w_in.shape[1]))).astype(BF16)
    gw = jnp.pad(gate_w, ((0, LANES - GLA_GATE_RANK), (0, 0))).astype(BF16)
    lane = jnp.arange(_QK)
    grp = jnp.where((lane[:, None] // DA_QK_DIM) == (lane[None, :] // DA_QK_DIM),
                    1.0 / DA_QK_DIM, 0.0).astype(BF16)
    const = lambda shape: pl.BlockSpec(shape, lambda i, j: (0,) * len(shape))
    tile = lambda width: pl.BlockSpec((1, tm, width), lambda i, j: (i, j, 0))
    tile_t = lambda width: pl.BlockSpec((1, nb, width, blk), lambda i, j: (i, j, 0, 0))
    return pl.pallas_call(
        _in_proj_kernel,
        grid=(b, ns),
        in_specs=[tile(d), const((1, d)), const((d, _IN_PAD)), const((1, _QK)), const((1, _QK)),
                  const((_QK, _QK)), const((LANES, _GQK)), const((1, _GQK))],
        out_specs=[tile_t(_QK), tile(_QK), tile_t(_DAW), tile(_GQK), tile(_GQK), tile(_GW), tile(_GW),
                   tile(_GQK)],
        out_shape=[
            jax.ShapeDtypeStruct((b, s // blk, _QK, blk), BF16),
            jax.ShapeDtypeStruct((b, s, _QK), BF16),
            jax.ShapeDtypeStruct((b, s // blk, _DAW, blk), BF16),
            jax.ShapeDtypeStruct((b, s, _GQK), F32),
            jax.ShapeDtypeStruct((b, s, _GQK), F32),
            jax.ShapeDtypeStruct((b, s, _GW), BF16),
            jax.ShapeDtypeStruct((b, s, _GW), F32),
            jax.ShapeDtypeStruct((b, s, _GQK), F32),
        ],
        compiler_params=_params("parallel", "parallel"),
        name="in_proj",
    )(x, norm_g.reshape(1, d), w, jnp.tile(da_qn, 2 * DA_HEADS).reshape(1, _QK),
      jnp.tile(da_kn, 2 * DA_HEADS).reshape(1, _QK), grp, gw, gate_b.reshape(1, _GQK))


def _split_q(qt):
    row = lax.broadcasted_iota(I32, qt.shape, 0)
    zero = jnp.zeros_like(qt)
    return jnp.where(row < DA_QK_DIM, qt, zero), jnp.where(row >= DA_QK_DIM, qt, zero)


def _chunk_causal_mask(blk):
    key_chunk = lax.broadcasted_iota(I32, (blk, blk), 0) // CHUNK
    qry_chunk = lax.broadcasted_iota(I32, (blk, blk), 1) // CHUNK
    return key_chunk <= qry_chunk


def _diff_attn_finish(lq1_ref, lk1_ref, lq2_ref, lk2_ref, gain_ref, a1, l1, a2, l2):
    lam = (jnp.exp(jnp.sum(lq1_ref[...] * lk1_ref[...], axis=-1, keepdims=True))
           - jnp.exp(jnp.sum(lq2_ref[...] * lk2_ref[...], axis=-1, keepdims=True)) + LAM_INIT)
    o = a1 / l1 - lam * (a2 / l2)
    ms = jnp.mean(o * o, axis=0, keepdims=True)
    o = o * lax.rsqrt(ms + EPS) * gain_ref[...] * (1.0 - LAM_INIT)
    return o.T.astype(BF16)


def _diff_attn_bounded_kernel(lq1_ref, lk1_ref, lq2_ref, lk2_ref, gain_ref, qt_ref, k_ref, vt_ref, out_ref,
                              s_ref, l1_ref, a1_ref, l2_ref, a2_ref, *, blk, nb):
    stats = ((l1_ref, a1_ref), (l2_ref, a2_ref))
    mask = _chunk_causal_mask(blk)

    def reset():
        for l_ref, a_ref in stats:
            l_ref[...] = jnp.zeros(l_ref.shape, F32)
            a_ref[...] = jnp.zeros(a_ref.shape, F32)

    def scores(q, j, slot):
        kb = k_ref[0, pl.ds(pl.multiple_of(j * blk, blk), blk), :]
        s_ref[slot, 0] = _dot(kb, q[0])
        s_ref[slot, 1] = _dot(kb, q[1])

    def consume(j, slot, masked):
        vb = vt_ref[0, j]
        for m, (l_ref, a_ref) in enumerate(stats):
            s = s_ref[slot, m]
            if masked:
                s = jnp.where(mask, s, NEG_INF)
            p = jnp.exp2(s)
            l_ref[...] += jnp.sum(p, axis=0, keepdims=True)
            a_ref[...] += _dot(vb, p.astype(BF16))

    def step(q, j, slot):
        scores(q, j + 1, 1 - slot)
        consume(j, slot, False)

    reset()
    q = _split_q(qt_ref[0, 0])
    slot = 0
    scores(q, 0, slot)
    for qi in range(nb):
        def pair(i, carry, q=q, slot=slot):
            step(q, 2 * i, slot)
            step(q, 2 * i + 1, 1 - slot)
            return carry

        if qi // 2:
            lax.fori_loop(0, qi // 2, pair, 0)
        if qi % 2:
            step(q, qi - 1, slot)
            slot = 1 - slot
        if qi + 1 < nb:
            q = _split_q(qt_ref[0, qi + 1])
            scores(q, 0, 1 - slot)
        consume(qi, slot, True)
        out_ref[0, qi * blk:(qi + 1) * blk, :] = _diff_attn_finish(
            lq1_ref, lk1_ref, lq2_ref, lk2_ref, gain_ref, a1_ref[...], l1_ref[...], a2_ref[...], l2_ref[...])
        if qi + 1 < nb:
            reset()
        slot = 1 - slot


def _diff_attn_online_kernel(lq1_ref, lk1_ref, lq2_ref, lk2_ref, gain_ref, qt_ref, k_ref, vt_ref, out_ref,
                             m1_ref, l1_ref, a1_ref, m2_ref, l2_ref, a2_ref, *, blk):
    qi = pl.program_id(2)
    q1, q2 = _split_q(qt_ref[0, 0])

    for m_ref, l_ref, a_ref in ((m1_ref, l1_ref, a1_ref), (m2_ref, l2_ref, a2_ref)):
        m_ref[...] = jnp.full(m_ref.shape, NEG_INF, F32)
        l_ref[...] = jnp.zeros(l_ref.shape, F32)
        a_ref[...] = jnp.zeros(a_ref.shape, F32)

    def update(s, vb, m_ref, l_ref, a_ref):
        m_old = m_ref[...]
        m_new = jnp.maximum(m_old, jnp.max(s, axis=0, keepdims=True))
        alpha = jnp.exp2(m_old - m_new)
        p = jnp.exp2(s - m_new)
        l_ref[...] = alpha * l_ref[...] + jnp.sum(p, axis=0, keepdims=True)
        a_ref[...] = alpha * a_ref[...] + _dot(vb, p.astype(BF16))
        m_ref[...] = m_new

    def block(j, mask):
        kb = k_ref[0, pl.ds(pl.multiple_of(j * blk, blk), blk), :]
        vb = vt_ref[0, j]
        s1 = _dot(kb, q1)
        s2 = _dot(kb, q2)
        if mask is not None:
            s1 = jnp.where(mask, s1, NEG_INF)
            s2 = jnp.where(mask, s2, NEG_INF)
        update(s1, vb, m1_ref, l1_ref, a1_ref)
        update(s2, vb, m2_ref, l2_ref, a2_ref)

    def body(j, carry):
        block(j, None)
        return carry

    lax.fori_loop(0, qi, body, 0)
    block(qi, _chunk_causal_mask(blk))
    out_ref[0] = _diff_attn_finish(lq1_ref, lk1_ref, lq2_ref, lk2_ref, gain_ref,
                                   a1_ref[...], l1_ref[...], a2_ref[...], l2_ref[...])


SCORE_BOUND = 60.0


def _diff_attn(qt, k, vt, lq1, lk1, lq2, lk2, da_on, da_qn, da_kn):
    b, nb, _, blk = qt.shape
    s = nb * blk
    stat = lambda: pltpu.VMEM((1, blk), F32)
    acc = lambda: pltpu.VMEM((DA_V_DIM, blk), F32)

    args = (lq1.reshape(1, -1), lk1.reshape(1, -1), lq2.reshape(1, -1), lk2.reshape(1, -1),
            da_on.reshape(-1, 1), qt, k, vt)
    out_shape = jax.ShapeDtypeStruct((b, s, _DAW), BF16)
    head = lambda *trailing: (lambda i, h: (i, 0, h) + trailing)
    vec2 = lambda: pl.BlockSpec((1, DA_QK_DIM), lambda i, h: (0, 0))
    bounded = pl.pallas_call(
        functools.partial(_diff_attn_bounded_kernel, blk=blk, nb=nb),
        grid=(b, DA_HEADS),
        in_specs=[
            vec2(), vec2(), vec2(), vec2(),
            pl.BlockSpec((DA_V_DIM, 1), lambda i, h: (0, 0)),
            pl.BlockSpec((1, nb, 2 * DA_QK_DIM, blk), head(0)),
            pl.BlockSpec((1, s, 2 * DA_QK_DIM), head()),
            pl.BlockSpec((1, nb, DA_V_DIM, blk), head(0)),
        ],
        out_specs=pl.BlockSpec((1, s, DA_V_DIM), head()),
        out_shape=out_shape,
        scratch_shapes=[pltpu.VMEM((2, 2, blk, blk), F32), stat(), acc(), stat(), acc()],
        compiler_params=_params("parallel", "parallel"),
        name="diff_attn",
    )
    vec3 = lambda: pl.BlockSpec((1, DA_QK_DIM), lambda i, h, q: (0, 0))
    online = pl.pallas_call(
        functools.partial(_diff_attn_online_kernel, blk=blk),
        grid=(b, DA_HEADS, nb),
        in_specs=[
            vec3(), vec3(), vec3(), vec3(),
            pl.BlockSpec((DA_V_DIM, 1), lambda i, h, q: (0, 0)),
            pl.BlockSpec((1, 1, 2 * DA_QK_DIM, blk), lambda i, h, q: (i, q, h, 0)),
            pl.BlockSpec((1, s, 2 * DA_QK_DIM), lambda i, h, q: (i, 0, h)),
            pl.BlockSpec((1, nb, DA_V_DIM, blk), lambda i, h, q: (i, 0, h, 0)),
        ],
        out_specs=pl.BlockSpec((1, blk, DA_V_DIM), lambda i, h, q: (i, q, h)),
        out_shape=out_shape,
        scratch_shapes=[stat(), stat(), acc(), stat(), stat(), acc()],
        compiler_params=_params("parallel", "parallel", "parallel"),
        name="diff_attn_online",
    )
    bound = (1.01 * DA_QK_DIM ** 0.5 * math.log2(math.e)) * jnp.max(jnp.abs(da_qn)) * jnp.max(jnp.abs(da_kn))
    return lax.cond(bound <= SCORE_BOUND, bounded, online, *args)


def _gla_kernel(q_ref, k_ref, la_ref, v_ref, g_ref, gain_ref, out_ref, st_ref, *, ts, group):
    @pl.when(pl.program_id(1) == 0)
    def _():
        st_ref[...] = jnp.zeros(st_ref.shape, F32)

    c = CHUNK
    rows = group * c
    hk, hv = _GQK, _GW
    r = lax.broadcasted_iota(I32, (rows, rows), 0)
    cc = lax.broadcasted_iota(I32, (rows, rows), 1)
    tri = jnp.where(jnp.logical_and(r // c == cc // c, r >= cc), 1.0, 0.0).astype(BF16)
    bd_k = (lax.broadcasted_iota(I32, (hk, hk), 0) // GLA_K_DIM
            == lax.broadcasted_iota(I32, (hk, hk), 1) // GLA_K_DIM)
    bd_v = (lax.broadcasted_iota(I32, (hk, hv), 0) // GLA_K_DIM
            == lax.broadcasted_iota(I32, (hk, hv), 1) // GLA_V_DIM)
    bd_vt = (lax.broadcasted_iota(I32, (hv, hk), 0) // GLA_V_DIM
             == lax.broadcasted_iota(I32, (hv, hk), 1) // GLA_K_DIM)
    lower = (lax.broadcasted_iota(I32, (c, hk), 0)
             >= lax.broadcasted_iota(I32, (c, hk), 1) % c)

    def tiled(t, mask):
        t4 = jnp.concatenate([t] * GLA_HEADS, axis=0)
        return jnp.where(mask, t4, jnp.zeros_like(t4))

    def chunk_row(t, row):
        return jnp.concatenate([jnp.broadcast_to(t[i * c + row:i * c + row + 1, :], (c, hk)) for i in range(group)],
                               axis=0)

    def body(gi, carry):
        sl = pl.ds(pl.multiple_of(gi * rows, rows), rows)
        la_hi, la_lo = _split_bf16(la_ref[0, sl, :])
        big_l = _dot(tri, la_hi) + _dot(tri, la_lo)
        l_end = chunk_row(big_l, c - 1)
        lc = big_l - chunk_row(big_l, c // 2 - 1)
        e_pos = jnp.exp2(lc)
        e_neg = jnp.exp2(-lc)
        q = q_ref[0, sl, :]
        k = k_ref[0, sl, :]
        v = v_ref[0, sl, :]
        q_pos = (q * e_pos).astype(BF16)
        q_neg = (q * e_neg).astype(BF16)
        k_pos = (k * e_pos).astype(BF16)
        k_neg = (k * e_neg).astype(BF16)
        q_in = (q * jnp.exp2(big_l)).astype(BF16)
        k_out = (k * jnp.exp2(l_end - big_l)).astype(BF16)
        decay = jnp.exp2(l_end)

        o_intra, u_t = [], []
        for i in range(group):
            cs = slice(i * c, (i + 1) * c)
            a_past = _dot_nt(q_pos[cs], tiled(k_neg[cs], bd_k))
            a_fut = _dot_nt(q_neg[cs], tiled(k_pos[cs], bd_k))
            a = jnp.where(lower, a_past, a_fut).astype(BF16)
            o_intra.append(_dot(a, tiled(v[cs], bd_v)))
            u_t.append(jnp.where(bd_vt, _dot_tn(v[cs], k_out[cs]), 0.0))

        st = st_ref[...]
        o_inter = []
        for i in range(group):
            cs = slice(i * c, (i + 1) * c)
            o_inter.append(_dot_nt(q_in[cs], st.astype(BF16)))
            st = st * decay[i * c:i * c + 1, :] + u_t[i]
        st_ref[...] = st

        o = jnp.concatenate(o_intra, axis=0) + jnp.concatenate(o_inter, axis=0)
        g = g_ref[0, sl, :]
        silu = g / (1.0 + jnp.exp(-g))
        for h in range(GLA_HEADS):
            hs = slice(h * GLA_V_DIM, (h + 1) * GLA_V_DIM)
            out_ref[0, sl, hs] = (_rms(o[:, hs], gain_ref[...]) * silu[:, hs]).astype(BF16)
        return carry

    lax.fori_loop(0, ts // rows, body, 0)


def _gla(gq, gk, la, gv, gg, gla_on):
    b, s, _ = gq.shape
    ts = TS_GLA
    tile = lambda width: pl.BlockSpec((1, ts, width), lambda i, j: (i, j, 0))
    return pl.pallas_call(
        functools.partial(_gla_kernel, ts=ts, group=GLA_GROUP),
        grid=(b, s // ts),
        in_specs=[tile(_GQK), tile(_GQK), tile(_GQK), tile(_GW), tile(_GW),
                  pl.BlockSpec((1, GLA_V_DIM), lambda i, j: (0, 0))],
        out_specs=tile(_GW),
        out_shape=jax.ShapeDtypeStruct((b, s, _GW), BF16),
        scratch_shapes=[pltpu.VMEM((_GW, _GQK), F32)],
        compiler_params=_params("parallel", "arbitrary"),
        name="gla",
    )(gq, gk, la, gv, gg, gla_on.reshape(1, -1))


_META_E0, _META_E1, _META_G0, _META_G1, _META_P0, _META_P1 = range(6)
_EXP_LANE0 = N_GROUPS


def _post_kernel(*refs, d, tm, sub, rider):
    if rider:
        (size_ref, pend_ref, x_ref, da_ref, gla_ref, wo_ref, gc_ref, wq_ref, qn_ref, km_ref, vm_ref, wco_ref,
         gf_ref, wr_ref, br_ref, rdest_ref, rsrc_ref,
         h_ref, xn_ref, meta_ref, ids_ref, cnt_ref, rpad_ref, zero_ref, rsem, zsem) = refs
    else:
        (x_ref, da_ref, gla_ref, wo_ref, gc_ref, wq_ref, qn_ref, km_ref, vm_ref, wco_ref,
         gf_ref, wr_ref, br_ref, h_ref, xn_ref, meta_ref, ids_ref, cnt_ref) = refs
    first = jnp.logical_and(pl.program_id(0) == 0, pl.program_id(1) == 0)

    @pl.when(first)
    def _():
        cnt_ref[...] = jnp.zeros(cnt_ref.shape, F32)
        if rider:
            _zero_fill(size_ref, pend_ref, rpad_ref, zero_ref, zsem)

    if rider:
        _scatter_rows(rdest_ref, rsrc_ref, rpad_ref, rsem)

    half = d // 2
    hd = d // CROSS_HEADS
    lane = lax.broadcasted_iota(I32, (sub, LANES), 1)
    big = jnp.int32(LANES)
    strict_lower = jnp.where(lax.broadcasted_iota(I32, (sub, sub), 0) > lax.broadcasted_iota(I32, (sub, sub), 1),
                             1.0, 0.0).astype(BF16)

    def lane_argmax(vals):
        m = jnp.max(vals, axis=-1, keepdims=True)
        idx = jnp.min(jnp.where(vals == m, lane, big), axis=-1, keepdims=True)
        return m, idx

    def rows(rs, base):
        h1 = x_ref[0, rs, :] + _dot(da_ref[0, rs, :], wo_ref[:half, :]) + _dot(gla_ref[0, rs, :], wo_ref[half:, :])

        u = _rms(h1, gc_ref[...]).astype(BF16)
        q = _dot(u, wq_ref[...])
        heads = []
        for h in range(CROSS_HEADS):
            hs = slice(h * hd, (h + 1) * hd)
            qh = _rms(q[:, hs], qn_ref[...]).astype(BF16)
            sc = _dot_nt(qh, km_ref[0, :, hs])
            sc = sc - jnp.max(sc, axis=-1, keepdims=True)
            p = jnp.exp(sc)
            p = p / jnp.sum(p, axis=-1, keepdims=True)
            heads.append(_dot(p.astype(BF16), vm_ref[0, :, hs]))
        o = jnp.concatenate(heads, axis=-1).astype(BF16)
        h2 = h1 + _dot(o, wco_ref[...])
        h_ref[0, rs, :] = h2

        xn = _rms(h2, gf_ref[...]).astype(BF16)
        bits = lax.bitcast_convert_type(xn.astype(F32), U32)
        xn_ref[0, rs, :] = (bits[:, :half] >> 16) | (bits[:, half:] & HI16)
        logits = _dot(xn, wr_ref[...]) + br_ref[...]

        lg = jnp.where(lane < N_GROUPS, logits, NEG_INF)
        g_max, g_sel = lane_argmax(lg)
        p_g = 1.0 / jnp.sum(jnp.exp(lg - g_max), axis=-1, keepdims=True)
        e_lo = _EXP_LANE0 + g_sel * EXPERTS_PER_GROUP
        in_group = jnp.logical_and(lane >= e_lo, lane < e_lo + EXPERTS_PER_GROUP)
        le = jnp.where(in_group, logits, NEG_INF)
        m1, i1 = lane_argmax(le)
        m2, i2 = lane_argmax(jnp.where(lane == i1, NEG_INF, le))
        e2 = jnp.exp(m2 - m1)
        gate0 = p_g / (1.0 + e2)
        gate1 = p_g * e2 / (1.0 + e2)
        e0 = i1 - _EXP_LANE0
        e1 = i2 - _EXP_LANE0

        hot0 = lane == e0
        hot1 = lane == e1
        onehot = jnp.where(jnp.logical_or(hot0, hot1), 1.0, 0.0)
        before = _dot(strict_lower, onehot.astype(BF16)) + base
        pos0 = jnp.sum(jnp.where(hot0, before, 0.0), axis=-1, keepdims=True)
        pos1 = jnp.sum(jnp.where(hot1, before, 0.0), axis=-1, keepdims=True)

        meta = jnp.zeros(logits.shape, F32)
        for idx, val in ((_META_E0, e0.astype(F32)), (_META_E1, e1.astype(F32)), (_META_G0, gate0),
                         (_META_G1, gate1), (_META_P0, pos0), (_META_P1, pos1)):
            meta = jnp.where(lane == idx, val, meta)
        meta_ref[0, rs, :] = meta
        ids_ref[0, 0, :, rs] = meta.T[:ids_ref.shape[2], :]
        return base + jnp.sum(onehot, axis=0, keepdims=True)

    base = cnt_ref[0:1, :]
    for r0 in range(0, tm, sub):
        base = rows(slice(r0, r0 + sub), base)
    cnt_ref[...] = jnp.broadcast_to(base, cnt_ref.shape)
    if rider:
        _scatter_wait(rsrc_ref, rpad_ref, rsem)


def _post(x, da, gla, w_o, norm_cross, w_cq, cross_qn, k_mem, v_mem, w_co, norm_ffn, w_group, b_group,
          w_expert, b_expert, *, half, rider=None):
    b, s, d = x.shape
    bh = b // 2
    b0 = half * bh
    tm = TM_POST
    ns = s // tm
    m = k_mem.shape[1]
    w_r = jnp.pad(jnp.concatenate([w_group, w_expert], axis=1), ((0, 0), (0, LANES - N_GROUPS - N_EXPERTS)))
    b_r = jnp.pad(jnp.concatenate([b_group, b_expert]), (0, LANES - N_GROUPS - N_EXPERTS)).reshape(1, LANES)
    const = lambda shape: pl.BlockSpec(shape, lambda i, j, *_: (0,) * len(shape))
    tile_in = lambda width: pl.BlockSpec((1, tm, width), lambda i, j, *_: (i + b0, j, 0))
    tile_out = lambda width: pl.BlockSpec((1, tm, width), lambda i, j, *_: (i, j, 0))
    per_b = lambda: pl.BlockSpec((1, m, d), lambda i, j, *_: (i + b0, 0, 0))
    in_specs = [tile_in(d), tile_in(d // 2), tile_in(d // 2), const((d, d)), const((1, d)), const((d, d)),
                const((1, d // CROSS_HEADS)), per_b(), per_b(), const((d, d)), const((1, d)),
                const((d, LANES)), const((1, LANES))]
    out_specs = [tile_out(d), tile_out(d // 2), tile_out(LANES),
                 pl.BlockSpec((1, 1, 8, tm), lambda i, j, *_: (i, j, 0, 0)), const((8, LANES))]
    out_shape = [
        jax.ShapeDtypeStruct((bh, s, d), F32),
        jax.ShapeDtypeStruct((bh, s, d // 2), U32),
        jax.ShapeDtypeStruct((bh, s, LANES), F32),
        jax.ShapeDtypeStruct((bh, ns, 8, tm), F32),
        jax.ShapeDtypeStruct((8, LANES), F32),
    ]
    args = (x, da, gla, w_o.astype(BF16), norm_cross.reshape(1, d), w_cq.astype(BF16), cross_qn.reshape(1, -1),
            k_mem, v_mem, w_co.astype(BF16), norm_ffn.reshape(1, d), w_r.astype(BF16), b_r)
    scalars, scratch = (), []
    if rider is not None:
        sizes, pend, dest, xn_src, n_rows = rider
        scalars = (sizes, pend)
        in_specs += [pl.BlockSpec((1, 2, tm), lambda i, j, *_: (i * ns + j, 0, 0), memory_space=pltpu.SMEM),
                     pl.BlockSpec((tm, d // 2), lambda i, j, *_: (i * ns + j, 0))]
        out_specs.append(pl.BlockSpec(memory_space=pl.ANY))
        out_shape.append(jax.ShapeDtypeStruct((n_rows, d // 2), U32))
        scratch = [pltpu.VMEM((EXPERT_ROWS, d // 2), U32), pltpu.SemaphoreType.DMA, pltpu.SemaphoreType.DMA]
        args += (dest, xn_src)
    return pl.pallas_call(
        functools.partial(_post_kernel, d=d, tm=tm, sub=SUB_POST, rider=rider is not None),
        grid_spec=pltpu.PrefetchScalarGridSpec(
            num_scalar_prefetch=len(scalars), grid=(bh, ns), in_specs=in_specs, out_specs=out_specs,
            scratch_shapes=scratch),
        out_shape=out_shape,
        compiler_params=_params("arbitrary", "arbitrary"),
        name="post_scatter" if rider is not None else "post",
    )(*scalars, *args)


def _row_copy(src_ref, src_row, dst_ref, dst_row, sem):
    return pltpu.make_async_copy(src_ref.at[pl.ds(src_row, 1)], dst_ref.at[pl.ds(dst_row, 1)], sem)


def _zero_fill(size_ref, pend_ref, xpad_ref, zero_ref, zsem):
    zero_ref[...] = jnp.zeros(zero_ref.shape, zero_ref.dtype)
    rows = zero_ref.shape[0]
    n_blocks = xpad_ref.shape[0] // rows
    n_used = pend_ref[N_EXPERTS - 1] // rows

    def zero_block(blk):
        return pltpu.make_async_copy(zero_ref, xpad_ref.at[pl.ds(pl.multiple_of(blk * rows, rows), rows)], zsem)

    def last_block(e, fn):
        @pl.when(size_ref[e] > 0)
        def _():
            fn(zero_block(pend_ref[e] // rows - 1))

    for fn in (lambda cp: cp.start(), lambda cp: cp.wait()):
        lax.fori_loop(0, N_EXPERTS, lambda e, c: (last_block(e, fn), c)[1], 0)
        lax.fori_loop(n_used, n_blocks, lambda blk, c: (fn(zero_block(blk)), c)[1], 0)


def _scatter_rows(dest_ref, src_ref, xpad_ref, sem):
    for t in range(src_ref.shape[0]):
        for k in range(2):
            _row_copy(src_ref, t, xpad_ref, dest_ref[0, k, t], sem).start(priority=k)


def _scatter_wait(src_ref, xpad_ref, sem):
    for _ in range(2):
        pltpu.make_async_copy(src_ref, xpad_ref.at[pl.ds(0, src_ref.shape[0])], sem).wait()


def _cast_expert_weights(fresh, wg_ref, wu_ref, wd_ref, wg_s, wu_s, wd_s):
    @pl.when(fresh)
    def _():
        wg_s[...] = wg_ref[0].astype(BF16)
        wu_s[...] = wu_ref[0].astype(BF16)
        wd_s[...] = wd_ref[0].astype(BF16)


def _expert_mlp(x_ref, wg_s, wu_s, wd_s, out_ref):
    words = x_ref[...]
    half = words.shape[1]
    lo = lax.bitcast_convert_type(words << 16, F32).astype(BF16)
    hi = lax.bitcast_convert_type(words & HI16, F32).astype(BF16)
    gate = _dot(lo, wg_s[:half, :]) + _dot(hi, wg_s[half:, :])
    up = _dot(lo, wu_s[:half, :]) + _dot(hi, wu_s[half:, :])
    hid = gate / (1.0 + jnp.exp(-gate)) * up
    out_ref[...] = _dot(hid.astype(BF16), wd_s[...])


def _experts_scatter_kernel(be_ref, nused_ref, size_ref, pend_ref, x_ref, wg_ref, wu_ref, wd_ref, rdest_ref, rsrc_ref,
                            out_ref, rpad_ref, wg_s, wu_s, wd_s, zero_ref, rsem, zsem, *, rider_steps):
    i = pl.program_id(0)
    used = i < nused_ref[0]
    new_expert = jnp.logical_or(i == 0, be_ref[i] != be_ref[jnp.maximum(i - 1, 0)])
    mlp = functools.partial(_expert_mlp, x_ref, wg_s, wu_s, wd_s, out_ref)

    @pl.when(i == 0)
    def _():
        _zero_fill(size_ref, pend_ref, rpad_ref, zero_ref, zsem)

    _cast_expert_weights(jnp.logical_and(used, new_expert), wg_ref, wu_ref, wd_ref, wg_s, wu_s, wd_s)
    riding = i < rider_steps

    @pl.when(riding)
    def _():
        _scatter_rows(rdest_ref, rsrc_ref, rpad_ref, rsem)
        mlp()
        _scatter_wait(rsrc_ref, rpad_ref, rsem)

    pl.when(jnp.logical_and(used, jnp.logical_not(riding)))(mlp)

    @pl.when(jnp.logical_not(used))
    def _():
        out_ref[...] = jnp.zeros(out_ref.shape, F32)


def _gather_rows(dest_ref, src_ref, buf_ref, sem):
    for t in range(buf_ref.shape[1]):
        for k in range(2):
            _row_copy(src_ref, dest_ref[0, k, t], buf_ref.at[k], t, sem).start(priority=k)


def _combine_rows(h_ref, meta_ref, src_ref, buf_ref, sem):
    for k in range(2):
        pltpu.make_async_copy(src_ref.at[pl.ds(0, buf_ref.shape[1])], buf_ref.at[k], sem).wait()
    meta = meta_ref[...]
    g0 = meta[:, _META_G0:_META_G0 + 1]
    g1 = meta[:, _META_G1:_META_G1 + 1]
    return h_ref[...] + g0 * buf_ref[0] + g1 * buf_ref[1]


def _experts_gather_kernel(be_ref, nused_ref, x_ref, wg_ref, wu_ref, wd_ref, gdest_ref, h_ref, meta_ref, opad_ref,
                           out_ref, y_ref, wg_s, wu_s, wd_s, gbuf_ref, gsem, *, rider_steps):
    i = pl.program_id(0)
    used = i < nused_ref[0]
    new_expert = jnp.logical_or(i == 0, be_ref[i] != be_ref[jnp.maximum(i - 1, 0)])
    mlp = functools.partial(_expert_mlp, x_ref, wg_s, wu_s, wd_s, out_ref)
    _cast_expert_weights(jnp.logical_and(used, new_expert), wg_ref, wu_ref, wd_ref, wg_s, wu_s, wd_s)

    def gather():
        _gather_rows(gdest_ref, opad_ref, gbuf_ref.at[i % 2], gsem.at[i % 2])

    def combine():
        slot = (i - 1) % 2
        y_ref[...] = _combine_rows(h_ref, meta_ref, opad_ref, gbuf_ref.at[slot], gsem.at[slot])

    @pl.when(i == 0)
    def _():
        gather()
        mlp()

    @pl.when(jnp.logical_and(i > 0, i < rider_steps))
    def _():
        gather()
        mlp()
        combine()

    @pl.when(i == rider_steps)
    def _():
        combine()
        pl.when(used)(mlp)

    pl.when(jnp.logical_and(i > rider_steps, used))(mlp)

    @pl.when(jnp.logical_not(used))
    def _():
        out_ref[...] = jnp.zeros(out_ref.shape, F32)


def _experts_kernel(be_ref, nused_ref, x_ref, wg_ref, wu_ref, wd_ref, out_ref, wg_s, wu_s, wd_s):
    i = pl.program_id(0)
    used = i < nused_ref[0]
    new_expert = jnp.logical_or(i == 0, be_ref[i] != be_ref[jnp.maximum(i - 1, 0)])
    _cast_expert_weights(jnp.logical_and(used, new_expert), wg_ref, wu_ref, wd_ref, wg_s, wu_s, wd_s)
    pl.when(used)(functools.partial(_expert_mlp, x_ref, wg_s, wu_s, wd_s, out_ref))

    @pl.when(jnp.logical_not(used))
    def _():
        out_ref[...] = jnp.zeros(out_ref.shape, F32)


def _retile(dest, tr):
    tiles, _, tm = dest.shape
    return dest.reshape(tiles, 2, tm // tr, tr).transpose(0, 2, 1, 3).reshape(tiles * (tm // tr), 2, tr)


def _experts(x_pad, block_expert, n_used, w_gate, w_up, w_down, scatter=None, gather=None):
    n_rows = x_pad.shape[0]
    _, d, f = w_gate.shape
    rows, tr = EXPERT_ROWS, RIDER_ROWS
    row_blk = lambda i, be, nu, *_: (jnp.minimum(i, nu[0] - 1), 0)
    weights = lambda shape: pl.BlockSpec(shape, lambda i, be, *_: (be[i], 0, 0))
    in_specs = [pl.BlockSpec((rows, d // 2), row_blk), weights((1, d, f)), weights((1, d, f)), weights((1, f, d))]
    out_specs = [pl.BlockSpec((rows, d), lambda i, *_: (i, 0))]
    out_shape = [jax.ShapeDtypeStruct((n_rows, d), F32)]
    scratch = [pltpu.VMEM((d, f), BF16), pltpu.VMEM((d, f), BF16), pltpu.VMEM((f, d), BF16)]
    scalars, args = (block_expert, n_used), (x_pad, w_gate, w_up, w_down)
    body, name = _experts_kernel, "experts"
    if scatter is not None or gather is not None:
        n_tokens = (scatter[3] if scatter is not None else gather[1]).shape[0]
        steps = n_tokens // tr
        assert 2 * n_tokens >= steps * rows, "riding steps must all be used expert blocks"
        tile = lambda width, shift: pl.BlockSpec((tr, width), lambda i, *_: (jnp.clip(i - shift, 0, steps - 1), 0))
        ids = lambda: pl.BlockSpec((1, 2, tr), lambda i, *_: (jnp.minimum(i, steps - 1), 0, 0),
                                   memory_space=pltpu.SMEM)
    if scatter is not None:
        sizes, pend, dest, xn_src, other_rows = scatter
        scalars += (sizes, pend)
        in_specs += [ids(), tile(d // 2, 0)]
        out_specs.append(pl.BlockSpec(memory_space=pl.ANY))
        out_shape.append(jax.ShapeDtypeStruct((other_rows, d // 2), U32))
        scratch += [pltpu.VMEM((rows, d // 2), U32), pltpu.SemaphoreType.DMA, pltpu.SemaphoreType.DMA]
        args += (_retile(dest, tr), xn_src)
        body, name = functools.partial(_experts_scatter_kernel, rider_steps=steps), "experts_scatter"
    elif gather is not None:
        dest, h2, meta, opad = gather
        in_specs += [ids(), tile(d, 1), tile(LANES, 1), pl.BlockSpec(memory_space=pl.ANY)]
        out_specs.append(tile(d, 1))
        out_shape.append(jax.ShapeDtypeStruct((n_tokens, d), F32))
        scratch += [pltpu.VMEM((2, 2, tr, d), F32), pltpu.SemaphoreType.DMA((2,))]
        args += (_retile(dest, tr), h2, meta, opad)
        body, name = functools.partial(_experts_gather_kernel, rider_steps=steps), "experts_gather"
    out = pl.pallas_call(
        body,
        grid_spec=pltpu.PrefetchScalarGridSpec(
            num_scalar_prefetch=len(scalars), grid=(n_rows // rows,), in_specs=in_specs, out_specs=out_specs,
            scratch_shapes=scratch),
        out_shape=out_shape,
        compiler_params=_params("arbitrary"),
        name=name,
    )(*scalars, *args)
    return out if len(out) > 1 else out[0]


def _combine_tail_kernel(dest_ref, y0_ref, h_ref, meta_ref, opad_ref, y_ref, buf_ref, sem, *, nt):
    s = pl.program_id(0)

    @pl.when(s < nt)
    def _():
        _gather_rows(dest_ref, opad_ref, buf_ref.at[s % 2], sem.at[s % 2])

    @pl.when(s > 0)
    def _():
        slot = (s - 1) % 2
        y_ref[0] = y0_ref[...]
        y_ref[1] = _combine_rows(h_ref, meta_ref, opad_ref, buf_ref.at[slot], sem.at[slot])


def _combine_tail(y0, h2, meta, dest, out_pad):
    th, d = h2.shape
    tm = TM_TAIL
    nt = th // tm
    prev = lambda s: (jnp.maximum(s - 1, 0), 0)
    return pl.pallas_call(
        functools.partial(_combine_tail_kernel, nt=nt),
        grid=(nt + 1,),
        in_specs=[
            pl.BlockSpec((1, 2, tm), lambda s: (jnp.minimum(s, nt - 1), 0, 0), memory_space=pltpu.SMEM),
            pl.BlockSpec((tm, d), prev), pl.BlockSpec((tm, d), prev), pl.BlockSpec((tm, LANES), prev),
            pl.BlockSpec(memory_space=pl.ANY),
        ],
        out_specs=pl.BlockSpec((2, tm, d), lambda s: (0, jnp.maximum(s - 1, 0), 0)),
        scratch_shapes=[pltpu.VMEM((2, 2, tm, d), F32), pltpu.SemaphoreType.DMA((2,))],
        out_shape=jax.ShapeDtypeStruct((2, th, d), F32),
        compiler_params=_params("arbitrary"),
        name="combine_tail",
    )(_retile(dest, tm), y0, h2, meta, out_pad)


def _route_tables(counts, ids, n_tokens):
    rows = EXPERT_ROWS
    sizes = counts[0, :N_EXPERTS].astype(I32)
    padded = (sizes + rows - 1) // rows * rows
    pend = jnp.cumsum(padded)
    pstart = pend - padded
    n_rows = 2 * n_tokens + N_EXPERTS * rows
    block_start = jnp.arange(n_rows // rows, dtype=I32) * rows
    block_expert = jnp.minimum(jnp.sum(pend[None, :] <= block_start[:, None], axis=1), N_EXPERTS - 1).astype(I32)
    n_used = (pend[-1:] // rows).astype(I32)
    expert = ids[:, _META_E0:_META_E1 + 1].astype(I32)
    rank = ids[:, _META_P0:_META_P1 + 1].astype(I32)
    experts = jnp.arange(N_EXPERTS, dtype=I32).reshape(-1, 1, 1, 1)
    dest = jnp.sum(jnp.where(expert[None] == experts, pstart.reshape(-1, 1, 1, 1), 0), axis=0) + rank
    return sizes, pend, block_expert, n_used, dest, n_rows


def kernel(x, mem, norm_mix, w_in, da_q_norm, da_k_norm, lambda_q1, lambda_k1, lambda_q2, lambda_k2,
           da_out_norm, gla_gate_w, gla_gate_b, gla_out_norm, w_o, norm_cross, norm_mem, w_cq, w_ckv,
           cross_q_norm, cross_k_norm, w_co, norm_ffn, w_group, b_group, w_expert, b_expert,
           w_e_gate, w_e_up, w_e_down):
    b, s, d = x.shape
    th = b // 2 * s
    h = x
    for l in range(norm_mix.shape[0]):
        assert l == 0, "lam_init is fixed for a single layer"
        qt, kda, vt, gq, gk, gv, gg, la = _in_proj(h, norm_mix[l], w_in[l], da_q_norm[l], da_k_norm[l],
                                                   gla_gate_w[l], gla_gate_b[l])
        da = _diff_attn(qt, kda, vt, lambda_q1[l], lambda_k1[l], lambda_q2[l], lambda_k2[l], da_out_norm[l],
                        da_q_norm[l], da_k_norm[l])
        gla = _gla(gq, gk, la, gv, gg, gla_out_norm[l])
        k_mem, v_mem = _mem_kv(mem, norm_mem[l], w_ckv[l], cross_k_norm[l])
        post = functools.partial(_post, h, da, gla, w_o[l], norm_cross[l], w_cq[l], cross_q_norm[l], k_mem, v_mem,
                                 w_co[l], norm_ffn[l], w_group[l], b_group[l], w_expert[l], b_expert[l])
        experts = functools.partial(_experts, w_gate=w_e_gate[l], w_up=w_e_up[l], w_down=w_e_down[l])

        h2_0, xn_0, meta_0, ids_0, counts_0 = post(half=0)
        sizes_0, pend_0, be_0, used_0, dest_0, n_rows = _route_tables(counts_0, ids_0.reshape(-1, 8, TM_POST), th)
        h2_1, xn_1, meta_1, ids_1, counts_1, xpad_0 = post(
            half=1, rider=(sizes_0, pend_0, dest_0, xn_0.reshape(th, d // 2), n_rows))
        sizes_1, pend_1, be_1, used_1, dest_1, _ = _route_tables(counts_1, ids_1.reshape(-1, 8, TM_POST), th)
        opad_0, xpad_1 = experts(xpad_0, be_0, used_0,
                                 scatter=(sizes_1, pend_1, dest_1, xn_1.reshape(th, d // 2), n_rows))
        opad_1, y_0 = experts(xpad_1, be_1, used_1,
                              gather=(dest_0, h2_0.reshape(th, d), meta_0.reshape(th, LANES), opad_0))
        h = _combine_tail(y_0, h2_1.reshape(th, d), meta_1.reshape(th, LANES), dest_1, opad_1).reshape(b, s, d)
    return h
```

```python
import functools
import math

import jax
import jax.numpy as jnp
import numpy as np
from jax import lax
from jax.experimental import pallas as pl
from jax.experimental.pallas import tpu as pltpu

F32 = jnp.float32
BF16 = jnp.bfloat16
I32 = jnp.int32
U32 = jnp.uint32
HI16 = np.uint32(0xFFFF0000)

EPS = 1e-6
CHUNK = 64

DA_HEADS = 4
DA_QK_DIM = 64
DA_V_DIM = 128
GLA_HEADS = 4
GLA_K_DIM = 64
GLA_V_DIM = 128
GLA_GATE_RANK = 16
GLA_TAU = 16.0
CROSS_HEADS = 4
N_GROUPS = 4
EXPERTS_PER_GROUP = 8
N_EXPERTS = N_GROUPS * EXPERTS_PER_GROUP
LAM_INIT = 0.8 - 0.6 * math.exp(-0.3 * 0)

LANES = 128
VMEM_LIMIT = 56 * 1024 * 1024

TM_PROJ = 1024
ATT_BLK = 512
TS_GLA = 1024
GLA_GROUP = 4
TM_POST = 1024
SUB_POST = 1024
TM_TAIL = 512
RIDER_ROWS = 256
EXPERT_ROWS = 512

NEG_INF = float("-inf")


def _params(*sem):
    return pltpu.CompilerParams(dimension_semantics=sem, vmem_limit_bytes=VMEM_LIMIT)


def _rms(t, g):
    ms = jnp.mean(t * t, axis=-1, keepdims=True)
    return t * lax.rsqrt(ms + EPS) * g


def _dot(a, b):
    return jnp.dot(a, b, preferred_element_type=F32)


def _dot_nt(a, b):
    return lax.dot_general(a, b, (((1,), (1,)), ((), ())), preferred_element_type=F32)


def _dot_tn(a, b):
    return lax.dot_general(a, b, (((0,), (0,)), ((), ())), preferred_element_type=F32)


def _split_bf16(t):
    hi = t.astype(BF16)
    lo = (t - hi.astype(F32)).astype(BF16)
    return hi, lo


def _mem_kv_kernel(mem_ref, g_ref, w_ref, kn_ref, k_ref, v_ref, *, d, heads):
    mn = _rms(mem_ref[0], g_ref[...]).astype(BF16)
    kv = _dot(mn, w_ref[...])
    hd = d // heads
    scale = hd ** -0.5
    for h in range(heads):
        kh = _rms(kv[:, h * hd:(h + 1) * hd], kn_ref[...]) * scale
        k_ref[0, :, h * hd:(h + 1) * hd] = kh.astype(BF16)
    v_ref[0] = kv[:, d:].astype(BF16)


def _mem_kv(mem, norm_m, w_ckv, kn):
    b, m, d = mem.shape
    return pl.pallas_call(
        functools.partial(_mem_kv_kernel, d=d, heads=CROSS_HEADS),
        grid=(b,),
        in_specs=[
            pl.BlockSpec((1, m, d), lambda i: (i, 0, 0)),
            pl.BlockSpec((1, d), lambda i: (0, 0)),
            pl.BlockSpec((d, 2 * d), lambda i: (0, 0)),
            pl.BlockSpec((1, d // CROSS_HEADS), lambda i: (0, 0)),
        ],
        out_specs=[
            pl.BlockSpec((1, m, d), lambda i: (i, 0, 0)),
            pl.BlockSpec((1, m, d), lambda i: (i, 0, 0)),
        ],
        out_shape=[jax.ShapeDtypeStruct((b, m, d), BF16)] * 2,
        compiler_params=_params("parallel"),
        name="mem_kv",
    )(mem, norm_m.reshape(1, d), w_ckv.astype(BF16), kn.reshape(1, -1))


_QK = DA_HEADS * 2 * DA_QK_DIM
_DAW = DA_HEADS * DA_V_DIM
_GQK = GLA_HEADS * GLA_K_DIM
_GW = GLA_HEADS * GLA_V_DIM
_OFF_DQ = 0
_OFF_DK = _OFF_DQ + _QK
_OFF_DV = _OFF_DK + _QK
_OFF_GQ = _OFF_DV + _DAW
_OFF_GK = _OFF_GQ + _GQK
_OFF_GV = _OFF_GK + _GQK
_OFF_GG = _OFF_GV + _GW
_OFF_GR = _OFF_GG + _GW
_IN_PAD = _OFF_GR + LANES


def _in_proj_kernel(x_ref, g_ref, w_ref, qg_ref, kg_ref, grp_ref, gw_ref, gb_ref,
                    qt_ref, k_ref, vt_ref, gq_ref, gk_ref, gv_ref, gg_ref, la_ref):
    u = _rms(x_ref[0], g_ref[...]).astype(BF16)

    def proj(off, width):
        return _dot(u, w_ref[:, off:off + width])

    def group_norm(p, gain):
        ms = _dot((p * p).astype(BF16), grp_ref[...])
        return p * lax.rsqrt(ms + EPS) * gain

    qn = group_norm(proj(_OFF_DQ, _QK), qg_ref[...]) * (DA_QK_DIM ** -0.5 * math.log2(math.e))
    blk = qt_ref.shape[3]
    for t in range(qt_ref.shape[1]):
        qt_ref[0, t] = qn[t * blk:(t + 1) * blk].T.astype(BF16)
    kn = group_norm(proj(_OFF_DK, _QK), kg_ref[...]).astype(BF16)
    for h in range(DA_HEADS):
        k_ref[0, h] = kn[:, h * 2 * DA_QK_DIM:(h + 1) * 2 * DA_QK_DIM]
    dv = proj(_OFF_DV, _DAW)
    for t in range(vt_ref.shape[1]):
        vt_ref[0, t] = dv[t * blk:(t + 1) * blk].T.astype(BF16)
    gq_ref[0] = proj(_OFF_GQ, _GQK) * (GLA_K_DIM ** -0.5)
    gk_ref[0] = proj(_OFF_GK, _GQK)
    gv_ref[0] = proj(_OFF_GV, _GW).astype(BF16)
    gg_ref[0] = proj(_OFF_GG, _GW)
    z = _dot(proj(_OFF_GR, LANES).astype(BF16), gw_ref[...]) + gb_ref[...]
    log_sig = jnp.minimum(z, 0.0) - jnp.log(1.0 + jnp.exp(-jnp.abs(z)))
    la_ref[0] = log_sig * (math.log2(math.e) / GLA_TAU)


def _in_proj(x, norm_g, w_in, da_qn, da_kn, gate_w, gate_b):
    b, s, d = x.shape
    tm, blk = TM_PROJ, ATT_BLK
    ns, nb = s // tm, tm // blk
    w = jnp.pad(w_in, ((0, 0), (0, _IN_PAD - w_in.shape[1]))).astype(BF16)
    gw = jnp.pad(gate_w, ((0, LANES - GLA_GATE_RANK), (0, 0))).astype(BF16)
    lane = jnp.arange(_QK)
    grp = jnp.where((lane[:, None] // DA_QK_DIM) == (lane[None, :] // DA_QK_DIM),
                    1.0 / DA_QK_DIM, 0.0).astype(BF16)
    const = lambda shape: pl.BlockSpec(shape, lambda i, j: (0,) * len(shape))
    tile = lambda width: pl.BlockSpec((1, tm, width), lambda i, j: (i, j, 0))
    tile_t = lambda width: pl.BlockSpec((1, nb, width, blk), lambda i, j: (i, j, 0, 0))
    return pl.pallas_call(
        _in_proj_kernel,
        grid=(b, ns),
        in_specs=[tile(d), const((1, d)), const((d, _IN_PAD)), const((1, _QK)), const((1, _QK)),
                  const((_QK, _QK)), const((LANES, _GQK)), const((1, _GQK))],
        out_specs=[tile_t(_QK), pl.BlockSpec((1, DA_HEADS, tm, 2 * DA_QK_DIM), lambda i, j: (i, 0, j, 0)),
                   tile_t(_DAW), tile(_GQK), tile(_GQK), tile(_GW), tile(_GW), tile(_GQK)],
        out_shape=[
            jax.ShapeDtypeStruct((b, s // blk, _QK, blk), BF16),
            jax.ShapeDtypeStruct((b, DA_HEADS, s, 2 * DA_QK_DIM), BF16),
            jax.ShapeDtypeStruct((b, s // blk, _DAW, blk), BF16),
            jax.ShapeDtypeStruct((b, s, _GQK), F32),
            jax.ShapeDtypeStruct((b, s, _GQK), F32),
            jax.ShapeDtypeStruct((b, s, _GW), BF16),
            jax.ShapeDtypeStruct((b, s, _GW), F32),
            jax.ShapeDtypeStruct((b, s, _GQK), F32),
        ],
        compiler_params=_params("parallel", "parallel"),
        name="in_proj",
    )(x, norm_g.reshape(1, d), w, jnp.tile(da_qn, 2 * DA_HEADS).reshape(1, _QK),
      jnp.tile(da_kn, 2 * DA_HEADS).reshape(1, _QK), grp, gw, gate_b.reshape(1, _GQK))


def _split_q(qt):
    row = lax.broadcasted_iota(I32, qt.shape, 0)
    zero = jnp.zeros_like(qt)
    return jnp.where(row < DA_QK_DIM, qt, zero), jnp.where(row >= DA_QK_DIM, qt, zero)


def _chunk_causal_mask(blk):
    key_chunk = lax.broadcasted_iota(I32, (blk, blk), 0) // CHUNK
    qry_chunk = lax.broadcasted_iota(I32, (blk, blk), 1) // CHUNK
    return key_chunk <= qry_chunk


def _diff_attn_finish(lq1_ref, lk1_ref, lq2_ref, lk2_ref, gain_ref, a1, l1, a2, l2):
    lam = (jnp.exp(jnp.sum(lq1_ref[...] * lk1_ref[...], axis=-1, keepdims=True))
           - jnp.exp(jnp.sum(lq2_ref[...] * lk2_ref[...], axis=-1, keepdims=True)) + LAM_INIT)
    o = a1 / l1 - lam * (a2 / l2)
    ms = jnp.mean(o * o, axis=0, keepdims=True)
    o = o * lax.rsqrt(ms + EPS) * gain_ref[...] * (1.0 - LAM_INIT)
    return o.T.astype(BF16)


def _diff_attn_bounded_kernel(lq1_ref, lk1_ref, lq2_ref, lk2_ref, gain_ref, qt_ref, k_ref, vt_ref, out_ref,
                              s_ref, l1_ref, a1_ref, l2_ref, a2_ref, *, blk, nb):
    stats = ((l1_ref, a1_ref), (l2_ref, a2_ref))
    mask = _chunk_causal_mask(blk)

    def reset():
        for l_ref, a_ref in stats:
            l_ref[...] = jnp.zeros(l_ref.shape, F32)
            a_ref[...] = jnp.zeros(a_ref.shape, F32)

    def scores(q, j, slot):
        kb = k_ref[0, 0, pl.ds(pl.multiple_of(j * blk, blk), blk), :]
        s_ref[slot, 0] = _dot(kb, q[0])
        s_ref[slot, 1] = _dot(kb, q[1])

    def consume(j, slot, masked):
        vb = vt_ref[0, j]
        for m, (l_ref, a_ref) in enumerate(stats):
            s = s_ref[slot, m]
            if masked:
                s = jnp.where(mask, s, NEG_INF)
            p = jnp.exp2(s)
            l_ref[...] += jnp.sum(p, axis=0, keepdims=True)
            a_ref[...] += _dot(vb, p.astype(BF16))

    def step(q, j, slot):
        scores(q, j + 1, 1 - slot)
        consume(j, slot, False)

    reset()
    q = _split_q(qt_ref[0, 0])
    slot = 0
    scores(q, 0, slot)
    for qi in range(nb):
        def pair(i, carry, q=q, slot=slot):
            step(q, 2 * i, slot)
            step(q, 2 * i + 1, 1 - slot)
            return carry

        if qi // 2:
            lax.fori_loop(0, qi // 2, pair, 0)
        if qi % 2:
            step(q, qi - 1, slot)
            slot = 1 - slot
        if qi + 1 < nb:
            q = _split_q(qt_ref[0, qi + 1])
            scores(q, 0, 1 - slot)
        consume(qi, slot, True)
        out_ref[0, 0, qi * blk:(qi + 1) * blk, :] = _diff_attn_finish(
            lq1_ref, lk1_ref, lq2_ref, lk2_ref, gain_ref, a1_ref[...], l1_ref[...], a2_ref[...], l2_ref[...])
        if qi + 1 < nb:
            reset()
        slot = 1 - slot


def _diff_attn_online_kernel(lq1_ref, lk1_ref, lq2_ref, lk2_ref, gain_ref, qt_ref, k_ref, vt_ref, out_ref,
                             m1_ref, l1_ref, a1_ref, m2_ref, l2_ref, a2_ref, *, blk):
    qi = pl.program_id(2)
    q1, q2 = _split_q(qt_ref[0, 0])

    for m_ref, l_ref, a_ref in ((m1_ref, l1_ref, a1_ref), (m2_ref, l2_ref, a2_ref)):
        m_ref[...] = jnp.full(m_ref.shape, NEG_INF, F32)
        l_ref[...] = jnp.zeros(l_ref.shape, F32)
        a_ref[...] = jnp.zeros(a_ref.shape, F32)

    def update(s, vb, m_ref, l_ref, a_ref):
        m_old = m_ref[...]
        m_new = jnp.maximum(m_old, jnp.max(s, axis=0, keepdims=True))
        alpha = jnp.exp2(m_old - m_new)
        p = jnp.exp2(s - m_new)
        l_ref[...] = alpha * l_ref[...] + jnp.sum(p, axis=0, keepdims=True)
        a_ref[...] = alpha * a_ref[...] + _dot(vb, p.astype(BF16))
        m_ref[...] = m_new

    def block(j, mask):
        kb = k_ref[0, 0, pl.ds(pl.multiple_of(j * blk, blk), blk), :]
        vb = vt_ref[0, j]
        s1 = _dot(kb, q1)
        s2 = _dot(kb, q2)
        if mask is not None:
            s1 = jnp.where(mask, s1, NEG_INF)
            s2 = jnp.where(mask, s2, NEG_INF)
        update(s1, vb, m1_ref, l1_ref, a1_ref)
        update(s2, vb, m2_ref, l2_ref, a2_ref)

    def body(j, carry):
        block(j, None)
        return carry

    lax.fori_loop(0, qi, body, 0)
    block(qi, _chunk_causal_mask(blk))
    out_ref[0, 0] = _diff_attn_finish(lq1_ref, lk1_ref, lq2_ref, lk2_ref, gain_ref,
                                      a1_ref[...], l1_ref[...], a2_ref[...], l2_ref[...])


SCORE_BOUND = 60.0


def _diff_attn(qt, k, vt, lq1, lk1, lq2, lk2, da_on, da_qn, da_kn):
    b, nb, _, blk = qt.shape
    s = nb * blk
    stat = lambda: pltpu.VMEM((1, blk), F32)
    acc = lambda: pltpu.VMEM((DA_V_DIM, blk), F32)

    args = (lq1.reshape(1, -1), lk1.reshape(1, -1), lq2.reshape(1, -1), lk2.reshape(1, -1),
            da_on.reshape(-1, 1), qt, k, vt)
    out_shape = jax.ShapeDtypeStruct((b, DA_HEADS, s, DA_V_DIM), BF16)
    head = lambda *trailing: (lambda i, h: (i, 0, h) + trailing)
    vec2 = lambda: pl.BlockSpec((1, DA_QK_DIM), lambda i, h: (0, 0))
    bounded = pl.pallas_call(
        functools.partial(_diff_attn_bounded_kernel, blk=blk, nb=nb),
        grid=(b, DA_HEADS),
        in_specs=[
            vec2(), vec2(), vec2(), vec2(),
            pl.BlockSpec((DA_V_DIM, 1), lambda i, h: (0, 0)),
            pl.BlockSpec((1, nb, 2 * DA_QK_DIM, blk), head(0)),
            pl.BlockSpec((1, 1, s, 2 * DA_QK_DIM), lambda i, h: (i, h, 0, 0)),
            pl.BlockSpec((1, nb, DA_V_DIM, blk), head(0)),
        ],
        out_specs=pl.BlockSpec((1, 1, s, DA_V_DIM), lambda i, h: (i, h, 0, 0)),
        out_shape=out_shape,
        scratch_shapes=[pltpu.VMEM((2, 2, blk, blk), F32), stat(), acc(), stat(), acc()],
        compiler_params=_params("parallel", "parallel"),
        name="diff_attn",
    )
    vec3 = lambda: pl.BlockSpec((1, DA_QK_DIM), lambda i, h, q: (0, 0))
    online = pl.pallas_call(
        functools.partial(_diff_attn_online_kernel, blk=blk),
        grid=(b, DA_HEADS, nb),
        in_specs=[
            vec3(), vec3(), vec3(), vec3(),
            pl.BlockSpec((DA_V_DIM, 1), lambda i, h, q: (0, 0)),
            pl.BlockSpec((1, 1, 2 * DA_QK_DIM, blk), lambda i, h, q: (i, q, h, 0)),
            pl.BlockSpec((1, 1, s, 2 * DA_QK_DIM), lambda i, h, q: (i, h, 0, 0)),
            pl.BlockSpec((1, nb, DA_V_DIM, blk), lambda i, h, q: (i, 0, h, 0)),
        ],
        out_specs=pl.BlockSpec((1, 1, blk, DA_V_DIM), lambda i, h, q: (i, h, q, 0)),
        out_shape=out_shape,
        scratch_shapes=[stat(), stat(), acc(), stat(), stat(), acc()],
        compiler_params=_params("parallel", "parallel", "parallel"),
        name="diff_attn_online",
    )
    bound = (1.01 * DA_QK_DIM ** 0.5 * math.log2(math.e)) * jnp.max(jnp.abs(da_qn)) * jnp.max(jnp.abs(da_kn))
    return lax.cond(bound <= SCORE_BOUND, bounded, online, *args)


def _gla_kernel(q_ref, k_ref, la_ref, v_ref, g_ref, gain_ref, out_ref, st_ref, *, ts, group):
    @pl.when(pl.program_id(1) == 0)
    def _():
        st_ref[...] = jnp.zeros(st_ref.shape, F32)

    c = CHUNK
    rows = group * c
    hk, hv = _GQK, _GW
    r = lax.broadcasted_iota(I32, (rows, rows), 0)
    cc = lax.broadcasted_iota(I32, (rows, rows), 1)
    tri = jnp.where(jnp.logical_and(r // c == cc // c, r >= cc), 1.0, 0.0).astype(BF16)
    bd_k = (lax.broadcasted_iota(I32, (hk, hk), 0) // GLA_K_DIM
            == lax.broadcasted_iota(I32, (hk, hk), 1) // GLA_K_DIM)
    bd_v = (lax.broadcasted_iota(I32, (hk, hv), 0) // GLA_K_DIM
            == lax.broadcasted_iota(I32, (hk, hv), 1) // GLA_V_DIM)
    bd_vt = (lax.broadcasted_iota(I32, (hv, hk), 0) // GLA_V_DIM
             == lax.broadcasted_iota(I32, (hv, hk), 1) // GLA_K_DIM)
    lower = (lax.broadcasted_iota(I32, (c, hk), 0)
             >= lax.broadcasted_iota(I32, (c, hk), 1) % c)

    def tiled(t, mask):
        t4 = jnp.concatenate([t] * GLA_HEADS, axis=0)
        return jnp.where(mask, t4, jnp.zeros_like(t4))

    def chunk_row(t, row):
        return jnp.concatenate([jnp.broadcast_to(t[i * c + row:i * c + row + 1, :], (c, hk)) for i in range(group)],
                               axis=0)

    def body(gi, carry):
        sl = pl.ds(pl.multiple_of(gi * rows, rows), rows)
        la_hi, la_lo = _split_bf16(la_ref[0, sl, :])
        big_l = _dot(tri, la_hi) + _dot(tri, la_lo)
        l_end = chunk_row(big_l, c - 1)
        lc = big_l - chunk_row(big_l, c // 2 - 1)
        e_pos = jnp.exp2(lc)
        e_neg = jnp.exp2(-lc)
        q = q_ref[0, sl, :]
        k = k_ref[0, sl, :]
        v = v_ref[0, sl, :]
        q_pos = (q * e_pos).astype(BF16)
        q_neg = (q * e_neg).astype(BF16)
        k_pos = (k * e_pos).astype(BF16)
        k_neg = (k * e_neg).astype(BF16)
        q_in = (q * jnp.exp2(big_l)).astype(BF16)
        k_out = (k * jnp.exp2(l_end - big_l)).astype(BF16)
        decay = jnp.exp2(l_end)

        o_intra, u_t = [], []
        for i in range(group):
            cs = slice(i * c, (i + 1) * c)
            a_past = _dot_nt(q_pos[cs], tiled(k_neg[cs], bd_k))
            a_fut = _dot_nt(q_neg[cs], tiled(k_pos[cs], bd_k))
            a = jnp.where(lower, a_past, a_fut).astype(BF16)
            o_intra.append(_dot(a, tiled(v[cs], bd_v)))
            u_t.append(jnp.where(bd_vt, _dot_tn(v[cs], k_out[cs]), 0.0))

        st = st_ref[...]
        o_inter = []
        for i in range(group):
            cs = slice(i * c, (i + 1) * c)
            o_inter.append(_dot_nt(q_in[cs], st.astype(BF16)))
            st = st * decay[i * c:i * c + 1, :] + u_t[i]
        st_ref[...] = st

        o = jnp.concatenate(o_intra, axis=0) + jnp.concatenate(o_inter, axis=0)
        g = g_ref[0, sl, :]
        silu = g / (1.0 + jnp.exp(-g))
        for h in range(GLA_HEADS):
            hs = slice(h * GLA_V_DIM, (h + 1) * GLA_V_DIM)
            out_ref[0, sl, hs] = (_rms(o[:, hs], gain_ref[...]) * silu[:, hs]).astype(BF16)
        return carry

    lax.fori_loop(0, ts // rows, body, 0)


def _gla(gq, gk, la, gv, gg, gla_on):
    b, s, _ = gq.shape
    ts = TS_GLA
    tile = lambda width: pl.BlockSpec((1, ts, width), lambda i, j: (i, j, 0))
    return pl.pallas_call(
        functools.partial(_gla_kernel, ts=ts, group=GLA_GROUP),
        grid=(b, s // ts),
        in_specs=[tile(_GQK), tile(_GQK), tile(_GQK), tile(_GW), tile(_GW),
                  pl.BlockSpec((1, GLA_V_DIM), lambda i, j: (0, 0))],
        out_specs=tile(_GW),
        out_shape=jax.ShapeDtypeStruct((b, s, _GW), BF16),
        scratch_shapes=[pltpu.VMEM((_GW, _GQK), F32)],
        compiler_params=_params("parallel", "arbitrary"),
        name="gla",
    )(gq, gk, la, gv, gg, gla_on.reshape(1, -1))


_META_E0, _META_E1, _META_G0, _META_G1, _META_P0, _META_P1 = range(6)
_EXP_LANE0 = N_GROUPS


def _post_kernel(*refs, d, tm, sub, rider):
    if rider:
        (size_ref, pend_ref, x_ref, da_ref, gla_ref, wo_ref, gc_ref, wq_ref, qn_ref, km_ref, vm_ref, wco_ref,
         gf_ref, wr_ref, br_ref, rdest_ref, rsrc_ref,
         h_ref, xn_ref, meta_ref, ids_ref, cnt_ref, rpad_ref, zero_ref, rsem, zsem) = refs
    else:
        (x_ref, da_ref, gla_ref, wo_ref, gc_ref, wq_ref, qn_ref, km_ref, vm_ref, wco_ref,
         gf_ref, wr_ref, br_ref, h_ref, xn_ref, meta_ref, ids_ref, cnt_ref) = refs
    first = jnp.logical_and(pl.program_id(0) == 0, pl.program_id(1) == 0)

    @pl.when(first)
    def _():
        cnt_ref[...] = jnp.zeros(cnt_ref.shape, F32)
        if rider:
            _zero_fill(size_ref, pend_ref, rpad_ref, zero_ref, zsem)

    if rider:
        _scatter_rows(rdest_ref, rsrc_ref, rpad_ref, rsem)

    half = d // 2
    hd = d // CROSS_HEADS
    lane = lax.broadcasted_iota(I32, (sub, LANES), 1)
    big = jnp.int32(LANES)
    strict_lower = jnp.where(lax.broadcasted_iota(I32, (sub, sub), 0) > lax.broadcasted_iota(I32, (sub, sub), 1),
                             1.0, 0.0).astype(BF16)

    def lane_argmax(vals):
        m = jnp.max(vals, axis=-1, keepdims=True)
        idx = jnp.min(jnp.where(vals == m, lane, big), axis=-1, keepdims=True)
        return m, idx

    def rows(rs, base):
        da = jnp.concatenate([da_ref[0, h, rs, :] for h in range(DA_HEADS)], axis=-1)
        h1 = x_ref[0, rs, :] + _dot(da, wo_ref[:half, :]) + _dot(gla_ref[0, rs, :], wo_ref[half:, :])

        u = _rms(h1, gc_ref[...]).astype(BF16)
        q = _dot(u, wq_ref[...])
        heads = []
        for h in range(CROSS_HEADS):
            hs = slice(h * hd, (h + 1) * hd)
            qh = _rms(q[:, hs], qn_ref[...]).astype(BF16)
            sc = _dot_nt(qh, km_ref[0, :, hs])
            sc = sc - jnp.max(sc, axis=-1, keepdims=True)
            p = jnp.exp(sc)
            p = p / jnp.sum(p, axis=-1, keepdims=True)
            heads.append(_dot(p.astype(BF16), vm_ref[0, :, hs]))
        o = jnp.concatenate(heads, axis=-1).astype(BF16)
        h2 = h1 + _dot(o, wco_ref[...])
        h_ref[0, rs, :] = h2

        xn = _rms(h2, gf_ref[...]).astype(BF16)
        bits = lax.bitcast_convert_type(xn.astype(F32), U32)
        xn_ref[0, rs, :] = (bits[:, :half] >> 16) | (bits[:, half:] & HI16)
        logits = _dot(xn, wr_ref[...]) + br_ref[...]

        lg = jnp.where(lane < N_GROUPS, logits, NEG_INF)
        g_max, g_sel = lane_argmax(lg)
        p_g = 1.0 / jnp.sum(jnp.exp(lg - g_max), axis=-1, keepdims=True)
        e_lo = _EXP_LANE0 + g_sel * EXPERTS_PER_GROUP
        in_group = jnp.logical_and(lane >= e_lo, lane < e_lo + EXPERTS_PER_GROUP)
        le = jnp.where(in_group, logits, NEG_INF)
        m1, i1 = lane_argmax(le)
        m2, i2 = lane_argmax(jnp.where(lane == i1, NEG_INF, le))
        e2 = jnp.exp(m2 - m1)
        gate0 = p_g / (1.0 + e2)
        gate1 = p_g * e2 / (1.0 + e2)
        e0 = i1 - _EXP_LANE0
        e1 = i2 - _EXP_LANE0

        hot0 = lane == e0
        hot1 = lane == e1
        onehot = jnp.where(jnp.logical_or(hot0, hot1), 1.0, 0.0)
        before = _dot(strict_lower, onehot.astype(BF16)) + base
        pos0 = jnp.sum(jnp.where(hot0, before, 0.0), axis=-1, keepdims=True)
        pos1 = jnp.sum(jnp.where(hot1, before, 0.0), axis=-1, keepdims=True)

        meta = jnp.zeros(logits.shape, F32)
        for idx, val in ((_META_E0, e0.astype(F32)), (_META_E1, e1.astype(F32)), (_META_G0, gate0),
                         (_META_G1, gate1), (_META_P0, pos0), (_META_P1, pos1)):
            meta = jnp.where(lane == idx, val, meta)
        meta_ref[0, rs, :] = meta
        ids_ref[0, 0, :, rs] = meta.T[:ids_ref.shape[2], :]
        return base + jnp.sum(onehot, axis=0, keepdims=True)

    base = cnt_ref[0:1, :]
    for r0 in range(0, tm, sub):
        base = rows(slice(r0, r0 + sub), base)
    cnt_ref[...] = jnp.broadcast_to(base, cnt_ref.shape)
    if rider:
        _scatter_wait(rsrc_ref, rpad_ref, rsem)


def _post(x, da, gla, w_o, norm_cross, w_cq, cross_qn, k_mem, v_mem, w_co, norm_ffn, w_group, b_group,
          w_expert, b_expert, *, half, rider=None):
    b, s, d = x.shape
    bh = b // 2
    b0 = half * bh
    tm = TM_POST
    ns = s // tm
    m = k_mem.shape[1]
    w_r = jnp.pad(jnp.concatenate([w_group, w_expert], axis=1), ((0, 0), (0, LANES - N_GROUPS - N_EXPERTS)))
    b_r = jnp.pad(jnp.concatenate([b_group, b_expert]), (0, LANES - N_GROUPS - N_EXPERTS)).reshape(1, LANES)
    const = lambda shape: pl.BlockSpec(shape, lambda i, j, *_: (0,) * len(shape))
    tile_in = lambda width: pl.BlockSpec((1, tm, width), lambda i, j, *_: (i + b0, j, 0))
    tile_out = lambda width: pl.BlockSpec((1, tm, width), lambda i, j, *_: (i, j, 0))
    per_b = lambda: pl.BlockSpec((1, m, d), lambda i, j, *_: (i + b0, 0, 0))
    da_spec = pl.BlockSpec((1, DA_HEADS, tm, DA_V_DIM), lambda i, j, *_: (i + b0, 0, j, 0))
    in_specs = [tile_in(d), da_spec, tile_in(d // 2), const((d, d)), const((1, d)), const((d, d)),
                const((1, d // CROSS_HEADS)), per_b(), per_b(), const((d, d)), const((1, d)),
                const((d, LANES)), const((1, LANES))]
    out_specs = [tile_out(d), tile_out(d // 2), tile_out(LANES),
                 pl.BlockSpec((1, 1, 8, tm), lambda i, j, *_: (i, j, 0, 0)), const((8, LANES))]
    out_shape = [
        jax.ShapeDtypeStruct((bh, s, d), F32),
        jax.ShapeDtypeStruct((bh, s, d // 2), U32),
        jax.ShapeDtypeStruct((bh, s, LANES), F32),
        jax.ShapeDtypeStruct((bh, ns, 8, tm), F32),
        jax.ShapeDtypeStruct((8, LANES), F32),
    ]
    args = (x, da, gla, w_o.astype(BF16), norm_cross.reshape(1, d), w_cq.astype(BF16), cross_qn.reshape(1, -1),
            k_mem, v_mem, w_co.astype(BF16), norm_ffn.reshape(1, d), w_r.astype(BF16), b_r)
    scalars, scratch = (), []
    if rider is not None:
        sizes, pend, dest, xn_src, n_rows = rider
        scalars = (sizes, pend)
        in_specs += [pl.BlockSpec((1, 2, tm), lambda i, j, *_: (i * ns + j, 0, 0), memory_space=pltpu.SMEM),
                     pl.BlockSpec((tm, d // 2), lambda i, j, *_: (i * ns + j, 0))]
        out_specs.append(pl.BlockSpec(memory_space=pl.ANY))
        out_shape.append(jax.ShapeDtypeStruct((n_rows, d // 2), U32))
        scratch = [pltpu.VMEM((EXPERT_ROWS, d // 2), U32), pltpu.SemaphoreType.DMA, pltpu.SemaphoreType.DMA]
        args += (dest, xn_src)
    return pl.pallas_call(
        functools.partial(_post_kernel, d=d, tm=tm, sub=SUB_POST, rider=rider is not None),
        grid_spec=pltpu.PrefetchScalarGridSpec(
            num_scalar_prefetch=len(scalars), grid=(bh, ns), in_specs=in_specs, out_specs=out_specs,
            scratch_shapes=scratch),
        out_shape=out_shape,
        compiler_params=_params("arbitrary", "arbitrary"),
        name="post_scatter" if rider is not None else "post",
    )(*scalars, *args)


def _row_copy(src_ref, src_row, dst_ref, dst_row, sem):
    return pltpu.make_async_copy(src_ref.at[pl.ds(src_row, 1)], dst_ref.at[pl.ds(dst_row, 1)], sem)


def _zero_fill(size_ref, pend_ref, xpad_ref, zero_ref, zsem):
    zero_ref[...] = jnp.zeros(zero_ref.shape, zero_ref.dtype)
    rows = zero_ref.shape[0]
    n_blocks = xpad_ref.shape[0] // rows
    n_used = pend_ref[N_EXPERTS - 1] // rows

    def zero_block(blk):
        return pltpu.make_async_copy(zero_ref, xpad_ref.at[pl.ds(pl.multiple_of(blk * rows, rows), rows)], zsem)

    def last_block(e, fn):
        @pl.when(size_ref[e] > 0)
        def _():
            fn(zero_block(pend_ref[e] // rows - 1))

    for fn in (lambda cp: cp.start(), lambda cp: cp.wait()):
        lax.fori_loop(0, N_EXPERTS, lambda e, c: (last_block(e, fn), c)[1], 0)
        lax.fori_loop(n_used, n_blocks, lambda blk, c: (fn(zero_block(blk)), c)[1], 0)


def _scatter_rows(dest_ref, src_ref, xpad_ref, sem):
    for t in range(src_ref.shape[0]):
        for k in range(2):
            _row_copy(src_ref, t, xpad_ref, dest_ref[0, k, t], sem).start(priority=k)


def _scatter_wait(src_ref, xpad_ref, sem):
    for _ in range(2):
        pltpu.make_async_copy(src_ref, xpad_ref.at[pl.ds(0, src_ref.shape[0])], sem).wait()


def _cast_expert_weights(fresh, wg_ref, wu_ref, wd_ref, wg_s, wu_s, wd_s):
    @pl.when(fresh)
    def _():
        wg_s[...] = wg_ref[0].astype(BF16)
        wu_s[...] = wu_ref[0].astype(BF16)
        wd_s[...] = wd_ref[0].astype(BF16)


def _expert_mlp(x_ref, wg_s, wu_s, wd_s, out_ref):
    words = x_ref[...]
    half = words.shape[1]
    lo = lax.bitcast_convert_type(words << 16, F32).astype(BF16)
    hi = lax.bitcast_convert_type(words & HI16, F32).astype(BF16)
    gate = _dot(lo, wg_s[:half, :]) + _dot(hi, wg_s[half:, :])
    up = _dot(lo, wu_s[:half, :]) + _dot(hi, wu_s[half:, :])
    hid = gate / (1.0 + jnp.exp(-gate)) * up
    out_ref[...] = _dot(hid.astype(BF16), wd_s[...])


def _experts_scatter_kernel(be_ref, nused_ref, size_ref, pend_ref, x_ref, wg_ref, wu_ref, wd_ref, rdest_ref, rsrc_ref,
                            out_ref, rpad_ref, wg_s, wu_s, wd_s, zero_ref, rsem, zsem, *, rider_steps):
    i = pl.program_id(0)
    used = i < nused_ref[0]
    new_expert = jnp.logical_or(i == 0, be_ref[i] != be_ref[jnp.maximum(i - 1, 0)])
    mlp = functools.partial(_expert_mlp, x_ref, wg_s, wu_s, wd_s, out_ref)

    @pl.when(i == 0)
    def _():
        _zero_fill(size_ref, pend_ref, rpad_ref, zero_ref, zsem)

    _cast_expert_weights(jnp.logical_and(used, new_expert), wg_ref, wu_ref, wd_ref, wg_s, wu_s, wd_s)
    riding = i < rider_steps

    @pl.when(riding)
    def _():
        _scatter_rows(rdest_ref, rsrc_ref, rpad_ref, rsem)
        mlp()
        _scatter_wait(rsrc_ref, rpad_ref, rsem)

    pl.when(jnp.logical_and(used, jnp.logical_not(riding)))(mlp)

    @pl.when(jnp.logical_not(used))
    def _():
        out_ref[...] = jnp.zeros(out_ref.shape, F32)


def _gather_rows(dest_ref, src_ref, buf_ref, sem):
    for t in range(buf_ref.shape[1]):
        for k in range(2):
            _row_copy(src_ref, dest_ref[0, k, t], buf_ref.at[k], t, sem).start(priority=k)


def _combine_rows(h_ref, meta_ref, src_ref, buf_ref, sem):
    for k in range(2):
        pltpu.make_async_copy(src_ref.at[pl.ds(0, buf_ref.shape[1])], buf_ref.at[k], sem).wait()
    meta = meta_ref[...]
    g0 = meta[:, _META_G0:_META_G0 + 1]
    g1 = meta[:, _META_G1:_META_G1 + 1]
    return h_ref[...] + g0 * buf_ref[0] + g1 * buf_ref[1]


def _experts_gather_kernel(be_ref, nused_ref, x_ref, wg_ref, wu_ref, wd_ref, gdest_ref, h_ref, meta_ref, opad_ref,
                           out_ref, y_ref, wg_s, wu_s, wd_s, gbuf_ref, gsem, *, rider_steps):
    i = pl.program_id(0)
    used = i < nused_ref[0]
    new_expert = jnp.logical_or(i == 0, be_ref[i] != be_ref[jnp.maximum(i - 1, 0)])
    mlp = functools.partial(_expert_mlp, x_ref, wg_s, wu_s, wd_s, out_ref)
    _cast_expert_weights(jnp.logical_and(used, new_expert), wg_ref, wu_ref, wd_ref, wg_s, wu_s, wd_s)

    def gather():
        _gather_rows(gdest_ref, opad_ref, gbuf_ref.at[i % 2], gsem.at[i % 2])

    def combine():
        slot = (i - 1) % 2
        y_ref[...] = _combine_rows(h_ref, meta_ref, opad_ref, gbuf_ref.at[slot], gsem.at[slot])

    @pl.when(i == 0)
    def _():
        gather()
        mlp()

    @pl.when(jnp.logical_and(i > 0, i < rider_steps))
    def _():
        gather()
        mlp()
        combine()

    @pl.when(i == rider_steps)
    def _():
        combine()
        pl.when(used)(mlp)

    pl.when(jnp.logical_and(i > rider_steps, used))(mlp)

    @pl.when(jnp.logical_not(used))
    def _():
        out_ref[...] = jnp.zeros(out_ref.shape, F32)


def _experts_kernel(be_ref, nused_ref, x_ref, wg_ref, wu_ref, wd_ref, out_ref, wg_s, wu_s, wd_s):
    i = pl.program_id(0)
    used = i < nused_ref[0]
    new_expert = jnp.logical_or(i == 0, be_ref[i] != be_ref[jnp.maximum(i - 1, 0)])
    _cast_expert_weights(jnp.logical_and(used, new_expert), wg_ref, wu_ref, wd_ref, wg_s, wu_s, wd_s)
    pl.when(used)(functools.partial(_expert_mlp, x_ref, wg_s, wu_s, wd_s, out_ref))

    @pl.when(jnp.logical_not(used))
    def _():
        out_ref[...] = jnp.zeros(out_ref.shape, F32)


def _retile(dest, tr):
    tiles, _, tm = dest.shape
    return dest.reshape(tiles, 2, tm // tr, tr).transpose(0, 2, 1, 3).reshape(tiles * (tm // tr), 2, tr)


def _experts(x_pad, block_expert, n_used, w_gate, w_up, w_down, scatter=None, gather=None):
    n_rows = x_pad.shape[0]
    _, d, f = w_gate.shape
    rows, tr = EXPERT_ROWS, RIDER_ROWS
    row_blk = lambda i, be, nu, *_: (jnp.minimum(i, nu[0] - 1), 0)
    weights = lambda shape: pl.BlockSpec(shape, lambda i, be, *_: (be[i], 0, 0))
    in_specs = [pl.BlockSpec((rows, d // 2), row_blk), weights((1, d, f)), weights((1, d, f)), weights((1, f, d))]
    out_specs = [pl.BlockSpec((rows, d), lambda i, *_: (i, 0))]
    out_shape = [jax.ShapeDtypeStruct((n_rows, d), F32)]
    scratch = [pltpu.VMEM((d, f), BF16), pltpu.VMEM((d, f), BF16), pltpu.VMEM((f, d), BF16)]
    scalars, args = (block_expert, n_used), (x_pad, w_gate, w_up, w_down)
    body, name = _experts_kernel, "experts"
    if scatter is not None or gather is not None:
        n_tokens = (scatter[3] if scatter is not None else gather[1]).shape[0]
        steps = n_tokens // tr
        assert 2 * n_tokens >= steps * rows, "riding steps must all be used expert blocks"
        tile = lambda width, shift: pl.BlockSpec((tr, width), lambda i, *_: (jnp.clip(i - shift, 0, steps - 1), 0))
        ids = lambda: pl.BlockSpec((1, 2, tr), lambda i, *_: (jnp.minimum(i, steps - 1), 0, 0),
                                   memory_space=pltpu.SMEM)
    if scatter is not None:
        sizes, pend, dest, xn_src, other_rows = scatter
        scalars += (sizes, pend)
        in_specs += [ids(), tile(d // 2, 0)]
        out_specs.append(pl.BlockSpec(memory_space=pl.ANY))
        out_shape.append(jax.ShapeDtypeStruct((other_rows, d // 2), U32))
        scratch += [pltpu.VMEM((rows, d // 2), U32), pltpu.SemaphoreType.DMA, pltpu.SemaphoreType.DMA]
        args += (_retile(dest, tr), xn_src)
        body, name = functools.partial(_experts_scatter_kernel, rider_steps=steps), "experts_scatter"
    elif gather is not None:
        dest, h2, meta, opad = gather
        in_specs += [ids(), tile(d, 1), tile(LANES, 1), pl.BlockSpec(memory_space=pl.ANY)]
        out_specs.append(tile(d, 1))
        out_shape.append(jax.ShapeDtypeStruct((n_tokens, d), F32))
        scratch += [pltpu.VMEM((2, 2, tr, d), F32), pltpu.SemaphoreType.DMA((2,))]
        args += (_retile(dest, tr), h2, meta, opad)
        body, name = functools.partial(_experts_gather_kernel, rider_steps=steps), "experts_gather"
    out = pl.pallas_call(
        body,
        grid_spec=pltpu.PrefetchScalarGridSpec(
            num_scalar_prefetch=len(scalars), grid=(n_rows // rows,), in_specs=in_specs, out_specs=out_specs,
            scratch_shapes=scratch),
        out_shape=out_shape,
        compiler_params=_params("arbitrary"),
        name=name,
    )(*scalars, *args)
    return out if len(out) > 1 else out[0]


def _combine_tail_kernel(dest_ref, y0_ref, h_ref, meta_ref, opad_ref, y_ref, buf_ref, sem, *, nt):
    s = pl.program_id(0)

    @pl.when(s < nt)
    def _():
        _gather_rows(dest_ref, opad_ref, buf_ref.at[s % 2], sem.at[s % 2])

    @pl.when(s > 0)
    def _():
        slot = (s - 1) % 2
        y_ref[0] = y0_ref[...]
        y_ref[1] = _combine_rows(h_ref, meta_ref, opad_ref, buf_ref.at[slot], sem.at[slot])


def _combine_tail(y0, h2, meta, dest, out_pad):
    th, d = h2.shape
    tm = TM_TAIL
    nt = th // tm
    prev = lambda s: (jnp.maximum(s - 1, 0), 0)
    return pl.pallas_call(
        functools.partial(_combine_tail_kernel, nt=nt),
        grid=(nt + 1,),
        in_specs=[
            pl.BlockSpec((1, 2, tm), lambda s: (jnp.minimum(s, nt - 1), 0, 0), memory_space=pltpu.SMEM),
            pl.BlockSpec((tm, d), prev), pl.BlockSpec((tm, d), prev), pl.BlockSpec((tm, LANES), prev),
            pl.BlockSpec(memory_space=pl.ANY),
        ],
        out_specs=pl.BlockSpec((2, tm, d), lambda s: (0, jnp.maximum(s - 1, 0), 0)),
        scratch_shapes=[pltpu.VMEM((2, 2, tm, d), F32), pltpu.SemaphoreType.DMA((2,))],
        out_shape=jax.ShapeDtypeStruct((2, th, d), F32),
        compiler_params=_params("arbitrary"),
        name="combine_tail",
    )(_retile(dest, tm), y0, h2, meta, out_pad)


def _route_tables(counts, ids, n_tokens):
    rows = EXPERT_ROWS
    sizes = counts[0, :N_EXPERTS].astype(I32)
    padded = (sizes + rows - 1) // rows * rows
    pend = jnp.cumsum(padded)
    pstart = pend - padded
    n_rows = 2 * n_tokens + N_EXPERTS * rows
    block_start = jnp.arange(n_rows // rows, dtype=I32) * rows
    block_expert = jnp.minimum(jnp.sum(pend[None, :] <= block_start[:, None], axis=1), N_EXPERTS - 1).astype(I32)
    n_used = (pend[-1:] // rows).astype(I32)
    expert = ids[:, _META_E0:_META_E1 + 1].astype(I32)
    rank = ids[:, _META_P0:_META_P1 + 1].astype(I32)
    experts = jnp.arange(N_EXPERTS, dtype=I32).reshape(-1, 1, 1, 1)
    dest = jnp.sum(jnp.where(expert[None] == experts, pstart.reshape(-1, 1, 1, 1), 0), axis=0) + rank
    return sizes, pend, block_expert, n_used, dest, n_rows


def kernel(x, mem, norm_mix, w_in, da_q_norm, da_k_norm, lambda_q1, lambda_k1, lambda_q2, lambda_k2,
           da_out_norm, gla_gate_w, gla_gate_b, gla_out_norm, w_o, norm_cross, norm_mem, w_cq, w_ckv,
           cross_q_norm, cross_k_norm, w_co, norm_ffn, w_group, b_group, w_expert, b_expert,
           w_e_gate, w_e_up, w_e_down):
    b, s, d = x.shape
    th = b // 2 * s
    h = x
    for l in range(norm_mix.shape[0]):
        assert l == 0, "lam_init is fixed for a single layer"
        qt, kda, vt, gq, gk, gv, gg, la = _in_proj(h, norm_mix[l], w_in[l], da_q_norm[l], da_k_norm[l],
                                                   gla_gate_w[l], gla_gate_b[l])
        da = _diff_attn(qt, kda, vt, lambda_q1[l], lambda_k1[l], lambda_q2[l], lambda_k2[l], da_out_norm[l],
                        da_q_norm[l], da_k_norm[l])
        gla = _gla(gq, gk, la, gv, gg, gla_out_norm[l])
        k_mem, v_mem = _mem_kv(mem, norm_mem[l], w_ckv[l], cross_k_norm[l])
        post = functools.partial(_post, h, da, gla, w_o[l], norm_cross[l], w_cq[l], cross_q_norm[l], k_mem, v_mem,
                                 w_co[l], norm_ffn[l], w_group[l], b_group[l], w_expert[l], b_expert[l])
        experts = functools.partial(_experts, w_gate=w_e_gate[l], w_up=w_e_up[l], w_down=w_e_down[l])

        h2_0, xn_0, meta_0, ids_0, counts_0 = post(half=0)
        sizes_0, pend_0, be_0, used_0, dest_0, n_rows = _route_tables(counts_0, ids_0.reshape(-1, 8, TM_POST), th)
        h2_1, xn_1, meta_1, ids_1, counts_1, xpad_0 = post(
            half=1, rider=(sizes_0, pend_0, dest_0, xn_0.reshape(th, d // 2), n_rows))
        sizes_1, pend_1, be_1, used_1, dest_1, _ = _route_tables(counts_1, ids_1.reshape(-1, 8, TM_POST), th)
        opad_0, xpad_1 = experts(xpad_0, be_0, used_0,
                                 scatter=(sizes_1, pend_1, dest_1, xn_1.reshape(th, d // 2), n_rows))
        opad_1, y_0 = experts(xpad_1, be_1, used_1,
                              gather=(dest_0, h2_0.reshape(th, d), meta_0.reshape(th, LANES), opad_0))
        h = _combine_tail(y_0, h2_1.reshape(th, d), meta_1.reshape(th, LANES), dest_1, opad_1).reshape(b, s, d)
    return h
```

```python
import functools
import math

import jax
import jax.numpy as jnp
import numpy as np
from jax import lax
from jax.experimental import pallas as pl
from jax.experimental.pallas import tpu as pltpu

F32 = jnp.float32
BF16 = jnp.bfloat16
I32 = jnp.int32
U32 = jnp.uint32
HI16 = np.uint32(0xFFFF0000)

EPS = 1e-6
CHUNK = 64

DA_HEADS = 4
DA_QK_DIM = 64
DA_V_DIM = 128
GLA_HEADS = 4
GLA_K_DIM = 64
GLA_V_DIM = 128
GLA_GATE_RANK = 16
GLA_TAU = 16.0
CROSS_HEADS = 4
N_GROUPS = 4
EXPERTS_PER_GROUP = 8
N_EXPERTS = N_GROUPS * EXPERTS_PER_GROUP
LAM_INIT = 0.8 - 0.6 * math.exp(-0.3 * 0)

LANES = 128
VMEM_LIMIT = 56 * 1024 * 1024

TM_PROJ = 1024
ATT_BLK = 512
TS_GLA = 1024
GLA_GROUP = 4
TM_POST = 1024
SUB_POST = 1024
TM_TAIL = 512
RIDER_ROWS = 256
EXPERT_ROWS = 512

NEG_INF = float("-inf")


def _params(*sem):
    return pltpu.CompilerParams(dimension_semantics=sem, vmem_limit_bytes=VMEM_LIMIT)


def _rms(t, g):
    ms = jnp.mean(t * t, axis=-1, keepdims=True)
    return t * lax.rsqrt(ms + EPS) * g


def _dot(a, b):
    return jnp.dot(a, b, preferred_element_type=F32)


def _dot_nt(a, b):
    return lax.dot_general(a, b, (((1,), (1,)), ((), ())), preferred_element_type=F32)


def _dot_tn(a, b):
    return lax.dot_general(a, b, (((0,), (0,)), ((), ())), preferred_element_type=F32)


def _split_bf16(t):
    hi = t.astype(BF16)
    lo = (t - hi.astype(F32)).astype(BF16)
    return hi, lo


def _mem_kv_kernel(mem_ref, g_ref, w_ref, kn_ref, k_ref, v_ref, *, d, heads):
    mn = _rms(mem_ref[0], g_ref[...]).astype(BF16)
    kv = _dot(mn, w_ref[...])
    hd = d // heads
    scale = hd ** -0.5
    for h in range(heads):
        kh = _rms(kv[:, h * hd:(h + 1) * hd], kn_ref[...]) * scale
        k_ref[0, :, h * hd:(h + 1) * hd] = kh.astype(BF16)
    v_ref[0] = kv[:, d:].astype(BF16)


def _mem_kv(mem, norm_m, w_ckv, kn):
    b, m, d = mem.shape
    return pl.pallas_call(
        functools.partial(_mem_kv_kernel, d=d, heads=CROSS_HEADS),
        grid=(b,),
        in_specs=[
            pl.BlockSpec((1, m, d), lambda i: (i, 0, 0)),
            pl.BlockSpec((1, d), lambda i: (0, 0)),
            pl.BlockSpec((d, 2 * d), lambda i: (0, 0)),
            pl.BlockSpec((1, d // CROSS_HEADS), lambda i: (0, 0)),
        ],
        out_specs=[
            pl.BlockSpec((1, m, d), lambda i: (i, 0, 0)),
            pl.BlockSpec((1, m, d), lambda i: (i, 0, 0)),
        ],
        out_shape=[jax.ShapeDtypeStruct((b, m, d), BF16)] * 2,
        compiler_params=_params("parallel"),
        name="mem_kv",
    )(mem, norm_m.reshape(1, d), w_ckv.astype(BF16), kn.reshape(1, -1))


_QK = DA_HEADS * 2 * DA_QK_DIM
_DAW = DA_HEADS * DA_V_DIM
_GQK = GLA_HEADS * GLA_K_DIM
_GW = GLA_HEADS * GLA_V_DIM
_OFF_DQ = 0
_OFF_DK = _OFF_DQ + _QK
_OFF_DV = _OFF_DK + _QK
_OFF_GQ = _OFF_DV + _DAW
_OFF_GK = _OFF_GQ + _GQK
_OFF_GV = _OFF_GK + _GQK
_OFF_GG = _OFF_GV + _GW
_OFF_GR = _OFF_GG + _GW
_IN_PAD = _OFF_GR + LANES


def _in_proj_kernel(x_ref, g_ref, w_ref, qg_ref, kg_ref, grp_ref, gw_ref, gb_ref,
                    qt_ref, k_ref, vt_ref, gq_ref, gk_ref, gv_ref, gg_ref, la_ref):
    u = _rms(x_ref[0], g_ref[...]).astype(BF16)

    def proj(off, width):
        return _dot(u, w_ref[:, off:off + width])

    def group_norm(p, gain):
        ms = _dot((p * p).astype(BF16), grp_ref[...])
        return p * lax.rsqrt(ms + EPS) * gain

    qn = group_norm(proj(_OFF_DQ, _QK), qg_ref[...]) * (DA_QK_DIM ** -0.5 * math.log2(math.e))
    blk = qt_ref.shape[3]
    for t in range(qt_ref.shape[1]):
        qt_ref[0, t] = qn[t * blk:(t + 1) * blk].T.astype(BF16)
    kn = group_norm(proj(_OFF_DK, _QK), kg_ref[...]).astype(BF16)
    for h in range(DA_HEADS):
        k_ref[0, h] = kn[:, h * 2 * DA_QK_DIM:(h + 1) * 2 * DA_QK_DIM]
    dv = proj(_OFF_DV, _DAW)
    for t in range(vt_ref.shape[1]):
        vt_ref[0, t] = dv[t * blk:(t + 1) * blk].T.astype(BF16)
    gq_ref[0] = proj(_OFF_GQ, _GQK) * (GLA_K_DIM ** -0.5)
    gk_ref[0] = proj(_OFF_GK, _GQK)
    gv_ref[0] = proj(_OFF_GV, _GW).astype(BF16)
    gg_ref[0] = proj(_OFF_GG, _GW)
    z = _dot(proj(_OFF_GR, LANES).astype(BF16), gw_ref[...]) + gb_ref[...]
    log_sig = jnp.minimum(z, 0.0) - jnp.log(1.0 + jnp.exp(-jnp.abs(z)))
    la_ref[0] = log_sig * (math.log2(math.e) / GLA_TAU)


def _in_proj(x, norm_g, w_in, da_qn, da_kn, gate_w, gate_b):
    b, s, d = x.shape
    tm, blk = TM_PROJ, ATT_BLK
    ns, nb = s // tm, tm // blk
    w = jnp.pad(w_in, ((0, 0), (0, _IN_PAD - w_in.shape[1]))).astype(BF16)
    gw = jnp.pad(gate_w, ((0, LANES - GLA_GATE_RANK), (0, 0))).astype(BF16)
    lane = jnp.arange(_QK)
    grp = jnp.where((lane[:, None] // DA_QK_DIM) == (lane[None, :] // DA_QK_DIM),
                    1.0 / DA_QK_DIM, 0.0).astype(BF16)
    const = lambda shape: pl.BlockSpec(shape, lambda i, j: (0,) * len(shape))
    tile = lambda width: pl.BlockSpec((1, tm, width), lambda i, j: (i, j, 0))
    tile_t = lambda width: pl.BlockSpec((1, nb, width, blk), lambda i, j: (i, j, 0, 0))
    return pl.pallas_call(
        _in_proj_kernel,
        grid=(b, ns),
        in_specs=[tile(d), const((1, d)), const((d, _IN_PAD)), const((1, _QK)), const((1, _QK)),
                  const((_QK, _QK)), const((LANES, _GQK)), const((1, _GQK))],
        out_specs=[tile_t(_QK), pl.BlockSpec((1, DA_HEADS, tm, 2 * DA_QK_DIM), lambda i, j: (i, 0, j, 0)),
                   tile_t(_DAW), tile(_GQK), tile(_GQK), tile(_GW), tile(_GW), tile(_GQK)],
        out_shape=[
            jax.ShapeDtypeStruct((b, s // blk, _QK, blk), BF16),
            jax.ShapeDtypeStruct((b, DA_HEADS, s, 2 * DA_QK_DIM), BF16),
            jax.ShapeDtypeStruct((b, s // blk, _DAW, blk), BF16),
            jax.ShapeDtypeStruct((b, s, _GQK), F32),
            jax.ShapeDtypeStruct((b, s, _GQK), F32),
            jax.ShapeDtypeStruct((b, s, _GW), BF16),
            jax.ShapeDtypeStruct((b, s, _GW), F32),
            jax.ShapeDtypeStruct((b, s, _GQK), F32),
        ],
        compiler_params=_params("parallel", "parallel"),
        name="in_proj",
    )(x, norm_g.reshape(1, d), w, jnp.tile(da_qn, 2 * DA_HEADS).reshape(1, _QK),
      jnp.tile(da_kn, 2 * DA_HEADS).reshape(1, _QK), grp, gw, gate_b.reshape(1, _GQK))


def _split_q(qt):
    row = lax.broadcasted_iota(I32, qt.shape, 0)
    zero = jnp.zeros_like(qt)
    return jnp.where(row < DA_QK_DIM, qt, zero), jnp.where(row >= DA_QK_DIM, qt, zero)


def _chunk_causal_mask(blk):
    key_chunk = lax.broadcasted_iota(I32, (blk, blk), 0) // CHUNK
    qry_chunk = lax.broadcasted_iota(I32, (blk, blk), 1) // CHUNK
    return key_chunk <= qry_chunk


def _diff_attn_finish(lq1_ref, lk1_ref, lq2_ref, lk2_ref, gain_ref, a1, l1, a2, l2):
    lam = (jnp.exp(jnp.sum(lq1_ref[...] * lk1_ref[...], axis=-1, keepdims=True))
           - jnp.exp(jnp.sum(lq2_ref[...] * lk2_ref[...], axis=-1, keepdims=True)) + LAM_INIT)
    o = a1 / l1 - lam * (a2 / l2)
    ms = jnp.mean(o * o, axis=0, keepdims=True)
    o = o * lax.rsqrt(ms + EPS) * gain_ref[...] * (1.0 - LAM_INIT)
    return o.T.astype(BF16)


def _diff_attn_bounded_kernel(lq1_ref, lk1_ref, lq2_ref, lk2_ref, gain_ref, qt_ref, k_ref, vt_ref, out_ref,
                              s_ref, l1_ref, a1_ref, l2_ref, a2_ref, *, blk, nb):
    stats = ((l1_ref, a1_ref), (l2_ref, a2_ref))
    mask = _chunk_causal_mask(blk)

    def reset():
        for l_ref, a_ref in stats:
            l_ref[...] = jnp.zeros(l_ref.shape, F32)
            a_ref[...] = jnp.zeros(a_ref.shape, F32)

    def scores(q, j, slot):
        kb = k_ref[0, 0, pl.ds(pl.multiple_of(j * blk, blk), blk), :]
        s_ref[slot, 0] = _dot(kb, q[0])
        s_ref[slot, 1] = _dot(kb, q[1])

    def consume(j, slot, masked):
        vb = vt_ref[0, j]
        for m, (l_ref, a_ref) in enumerate(stats):
            s = s_ref[slot, m]
            if masked:
                s = jnp.where(mask, s, NEG_INF)
            p = jnp.exp2(s)
            l_ref[...] += jnp.sum(p, axis=0, keepdims=True)
            a_ref[...] += _dot(vb, p.astype(BF16))

    def step(q, j, slot):
        scores(q, j + 1, 1 - slot)
        consume(j, slot, False)

    reset()
    q = _split_q(qt_ref[0, 0])
    slot = 0
    scores(q, 0, slot)
    for qi in range(nb):
        def pair(i, carry, q=q, slot=slot):
            step(q, 2 * i, slot)
            step(q, 2 * i + 1, 1 - slot)
            return carry

        if qi // 2:
            lax.fori_loop(0, qi // 2, pair, 0)
        if qi % 2:
            step(q, qi - 1, slot)
            slot = 1 - slot
        if qi + 1 < nb:
            q = _split_q(qt_ref[0, qi + 1])
            scores(q, 0, 1 - slot)
        consume(qi, slot, True)
        out_ref[0, 0, qi * blk:(qi + 1) * blk, :] = _diff_attn_finish(
            lq1_ref, lk1_ref, lq2_ref, lk2_ref, gain_ref, a1_ref[...], l1_ref[...], a2_ref[...], l2_ref[...])
        if qi + 1 < nb:
            reset()
        slot = 1 - slot


def _diff_attn_online_kernel(lq1_ref, lk1_ref, lq2_ref, lk2_ref, gain_ref, qt_ref, k_ref, vt_ref, out_ref,
                             m1_ref, l1_ref, a1_ref, m2_ref, l2_ref, a2_ref, *, blk):
    qi = pl.program_id(2)
    q1, q2 = _split_q(qt_ref[0, 0])

    for m_ref, l_ref, a_ref in ((m1_ref, l1_ref, a1_ref), (m2_ref, l2_ref, a2_ref)):
        m_ref[...] = jnp.full(m_ref.shape, NEG_INF, F32)
        l_ref[...] = jnp.zeros(l_ref.shape, F32)
        a_ref[...] = jnp.zeros(a_ref.shape, F32)

    def update(s, vb, m_ref, l_ref, a_ref):
        m_old = m_ref[...]
        m_new = jnp.maximum(m_old, jnp.max(s, axis=0, keepdims=True))
        alpha = jnp.exp2(m_old - m_new)
        p = jnp.exp2(s - m_new)
        l_ref[...] = alpha * l_ref[...] + jnp.sum(p, axis=0, keepdims=True)
        a_ref[...] = alpha * a_ref[...] + _dot(vb, p.astype(BF16))
        m_ref[...] = m_new

    def block(j, mask):
        kb = k_ref[0, 0, pl.ds(pl.multiple_of(j * blk, blk), blk), :]
        vb = vt_ref[0, j]
        s1 = _dot(kb, q1)
        s2 = _dot(kb, q2)
        if mask is not None:
            s1 = jnp.where(mask, s1, NEG_INF)
            s2 = jnp.where(mask, s2, NEG_INF)
        update(s1, vb, m1_ref, l1_ref, a1_ref)
        update(s2, vb, m2_ref, l2_ref, a2_ref)

    def body(j, carry):
        block(j, None)
        return carry

    lax.fori_loop(0, qi, body, 0)
    block(qi, _chunk_causal_mask(blk))
    out_ref[0, 0] = _diff_attn_finish(lq1_ref, lk1_ref, lq2_ref, lk2_ref, gain_ref,
                                      a1_ref[...], l1_ref[...], a2_ref[...], l2_ref[...])


SCORE_BOUND = 60.0


def _diff_attn(qt, k, vt, lq1, lk1, lq2, lk2, da_on, da_qn, da_kn):
    b, nb, _, blk = qt.shape
    s = nb * blk
    stat = lambda: pltpu.VMEM((1, blk), F32)
    acc = lambda: pltpu.VMEM((DA_V_DIM, blk), F32)

    args = (lq1.reshape(1, -1), lk1.reshape(1, -1), lq2.reshape(1, -1), lk2.reshape(1, -1),
            da_on.reshape(-1, 1), qt, k, vt)
    out_shape = jax.ShapeDtypeStruct((b, DA_HEADS, s, DA_V_DIM), BF16)
    head = lambda *trailing: (lambda i, h: (i, 0, h) + trailing)
    vec2 = lambda: pl.BlockSpec((1, DA_QK_DIM), lambda i, h: (0, 0))
    bounded = pl.pallas_call(
        functools.partial(_diff_attn_bounded_kernel, blk=blk, nb=nb),
        grid=(b, DA_HEADS),
        in_specs=[
            vec2(), vec2(), vec2(), vec2(),
            pl.BlockSpec((DA_V_DIM, 1), lambda i, h: (0, 0)),
            pl.BlockSpec((1, nb, 2 * DA_QK_DIM, blk), head(0)),
            pl.BlockSpec((1, 1, s, 2 * DA_QK_DIM), lambda i, h: (i, h, 0, 0)),
            pl.BlockSpec((1, nb, DA_V_DIM, blk), head(0)),
        ],
        out_specs=pl.BlockSpec((1, 1, s, DA_V_DIM), lambda i, h: (i, h, 0, 0)),
        out_shape=out_shape,
        scratch_shapes=[pltpu.VMEM((2, 2, blk, blk), F32), stat(), acc(), stat(), acc()],
        compiler_params=_params("parallel", "parallel"),
        name="diff_attn",
    )
    vec3 = lambda: pl.BlockSpec((1, DA_QK_DIM), lambda i, h, q: (0, 0))
    online = pl.pallas_call(
        functools.partial(_diff_attn_online_kernel, blk=blk),
        grid=(b, DA_HEADS, nb),
        in_specs=[
            vec3(), vec3(), vec3(), vec3(),
            pl.BlockSpec((DA_V_DIM, 1), lambda i, h, q: (0, 0)),
            pl.BlockSpec((1, 1, 2 * DA_QK_DIM, blk), lambda i, h, q: (i, q, h, 0)),
            pl.BlockSpec((1, 1, s, 2 * DA_QK_DIM), lambda i, h, q: (i, h, 0, 0)),
            pl.BlockSpec((1, nb, DA_V_DIM, blk), lambda i, h, q: (i, 0, h, 0)),
        ],
        out_specs=pl.BlockSpec((1, 1, blk, DA_V_DIM), lambda i, h, q: (i, h, q, 0)),
        out_shape=out_shape,
        scratch_shapes=[stat(), stat(), acc(), stat(), stat(), acc()],
        compiler_params=_params("parallel", "parallel", "parallel"),
        name="diff_attn_online",
    )
    bound = (1.01 * DA_QK_DIM ** 0.5 * math.log2(math.e)) * jnp.max(jnp.abs(da_qn)) * jnp.max(jnp.abs(da_kn))
    return lax.cond(bound <= SCORE_BOUND, bounded, online, *args)


def _gla_kernel(q_ref, k_ref, la_ref, v_ref, g_ref, gain_ref, out_ref, st_ref, *, ts, group):
    @pl.when(pl.program_id(1) == 0)
    def _():
        st_ref[...] = jnp.zeros(st_ref.shape, F32)

    c = CHUNK
    rows = group * c
    hk, hv = _GQK, _GW
    r = lax.broadcasted_iota(I32, (rows, rows), 0)
    cc = lax.broadcasted_iota(I32, (rows, rows), 1)
    tri = jnp.where(jnp.logical_and(r // c == cc // c, r >= cc), 1.0, 0.0).astype(BF16)
    bd_k = (lax.broadcasted_iota(I32, (hk, hk), 0) // GLA_K_DIM
            == lax.broadcasted_iota(I32, (hk, hk), 1) // GLA_K_DIM)
    bd_v = (lax.broadcasted_iota(I32, (hk, hv), 0) // GLA_K_DIM
            == lax.broadcasted_iota(I32, (hk, hv), 1) // GLA_V_DIM)
    bd_vt = (lax.broadcasted_iota(I32, (hv, hk), 0) // GLA_V_DIM
             == lax.broadcasted_iota(I32, (hv, hk), 1) // GLA_K_DIM)
    lower = (lax.broadcasted_iota(I32, (c, hk), 0)
             >= lax.broadcasted_iota(I32, (c, hk), 1) % c)

    def tiled(t, mask):
        t4 = jnp.concatenate([t] * GLA_HEADS, axis=0)
        return jnp.where(mask, t4, jnp.zeros_like(t4))

    def chunk_row(t, row):
        return jnp.concatenate([jnp.broadcast_to(t[i * c + row:i * c + row + 1, :], (c, hk)) for i in range(group)],
                               axis=0)

    def body(gi, carry):
        sl = pl.ds(pl.multiple_of(gi * rows, rows), rows)
        la_hi, la_lo = _split_bf16(la_ref[0, sl, :])
        big_l = _dot(tri, la_hi) + _dot(tri, la_lo)
        l_end = chunk_row(big_l, c - 1)
        lc = big_l - chunk_row(big_l, c // 2 - 1)
        e_pos = jnp.exp2(lc)
        e_neg = jnp.exp2(-lc)
        q = q_ref[0, sl, :]
        k = k_ref[0, sl, :]
        v = v_ref[0, sl, :]
        q_pos = (q * e_pos).astype(BF16)
        q_neg = (q * e_neg).astype(BF16)
        k_pos = (k * e_pos).astype(BF16)
        k_neg = (k * e_neg).astype(BF16)
        q_in = (q * jnp.exp2(big_l)).astype(BF16)
        k_out = (k * jnp.exp2(l_end - big_l)).astype(BF16)
        decay = jnp.exp2(l_end)

        o_intra, u_t = [], []
        for i in range(group):
            cs = slice(i * c, (i + 1) * c)
            a_past = _dot_nt(q_pos[cs], tiled(k_neg[cs], bd_k))
            a_fut = _dot_nt(q_neg[cs], tiled(k_pos[cs], bd_k))
            a = jnp.where(lower, a_past, a_fut).astype(BF16)
            o_intra.append(_dot(a, tiled(v[cs], bd_v)))
            u_t.append(jnp.where(bd_vt, _dot_tn(v[cs], k_out[cs]), 0.0))

        st = st_ref[...]
        o_inter = []
        for i in range(group):
            cs = slice(i * c, (i + 1) * c)
            o_inter.append(_dot_nt(q_in[cs], st.astype(BF16)))
            st = st * decay[i * c:i * c + 1, :] + u_t[i]
        st_ref[...] = st

        o = jnp.concatenate(o_intra, axis=0) + jnp.concatenate(o_inter, axis=0)
        g = g_ref[0, sl, :]
        silu = g / (1.0 + jnp.exp(-g))
        for h in range(GLA_HEADS):
            hs = slice(h * GLA_V_DIM, (h + 1) * GLA_V_DIM)
            out_ref[0, sl, hs] = (_rms(o[:, hs], gain_ref[...]) * silu[:, hs]).astype(BF16)
        return carry

    lax.fori_loop(0, ts // rows, body, 0)


def _gla(gq, gk, la, gv, gg, gla_on):
    b, s, _ = gq.shape
    ts = TS_GLA
    tile = lambda width: pl.BlockSpec((1, ts, width), lambda i, j: (i, j, 0))
    return pl.pallas_call(
        functools.partial(_gla_kernel, ts=ts, group=GLA_GROUP),
        grid=(b, s // ts),
        in_specs=[tile(_GQK), tile(_GQK), tile(_GQK), tile(_GW), tile(_GW),
                  pl.BlockSpec((1, GLA_V_DIM), lambda i, j: (0, 0))],
        out_specs=tile(_GW),
        out_shape=jax.ShapeDtypeStruct((b, s, _GW), BF16),
        scratch_shapes=[pltpu.VMEM((_GW, _GQK), F32)],
        compiler_params=_params("parallel", "arbitrary"),
        name="gla",
    )(gq, gk, la, gv, gg, gla_on.reshape(1, -1))


_META_E0, _META_E1, _META_G0, _META_G1, _META_P0, _META_P1 = range(6)
_EXP_LANE0 = N_GROUPS


def _post_kernel(*refs, d, tm, sub, rider):
    if rider:
        (size_ref, pend_ref, x_ref, da_ref, gla_ref, wo_ref, gc_ref, wq_ref, qn_ref, km_ref, vm_ref, wco_ref,
         gf_ref, wr_ref, br_ref, rdest_ref, rsrc_ref,
         h_ref, xn_ref, meta_ref, ids_ref, cnt_ref, rpad_ref, zero_ref, rsem, zsem) = refs
    else:
        (x_ref, da_ref, gla_ref, wo_ref, gc_ref, wq_ref, qn_ref, km_ref, vm_ref, wco_ref,
         gf_ref, wr_ref, br_ref, h_ref, xn_ref, meta_ref, ids_ref, cnt_ref) = refs
    first = jnp.logical_and(pl.program_id(0) == 0, pl.program_id(1) == 0)

    @pl.when(first)
    def _():
        cnt_ref[...] = jnp.zeros(cnt_ref.shape, F32)
        if rider:
            _zero_fill(size_ref, pend_ref, rpad_ref, zero_ref, zsem)

    if rider:
        _scatter_rows(rdest_ref, rsrc_ref, rpad_ref, rsem)

    half = d // 2
    hd = d // CROSS_HEADS
    lane = lax.broadcasted_iota(I32, (sub, LANES), 1)
    big = jnp.int32(LANES)
    strict_lower = jnp.where(lax.broadcasted_iota(I32, (sub, sub), 0) > lax.broadcasted_iota(I32, (sub, sub), 1),
                             1.0, 0.0).astype(BF16)

    def lane_argmax(vals):
        m = jnp.max(vals, axis=-1, keepdims=True)
        idx = jnp.min(jnp.where(vals == m, lane, big), axis=-1, keepdims=True)
        return m, idx

    def rows(rs, base):
        da = jnp.concatenate([da_ref[0, h, rs, :] for h in range(DA_HEADS)], axis=-1)
        h1 = x_ref[0, rs, :] + _dot(da, wo_ref[:half, :]) + _dot(gla_ref[0, rs, :], wo_ref[half:, :])

        u = _rms(h1, gc_ref[...]).astype(BF16)
        q = _dot(u, wq_ref[...])
        heads = []
        for h in range(CROSS_HEADS):
            hs = slice(h * hd, (h + 1) * hd)
            qh = _rms(q[:, hs], qn_ref[...]).astype(BF16)
            sc = _dot_nt(qh, km_ref[0, :, hs])
            sc = sc - jnp.max(sc, axis=-1, keepdims=True)
            p = jnp.exp(sc)
            p = p / jnp.sum(p, axis=-1, keepdims=True)
            heads.append(_dot(p.astype(BF16), vm_ref[0, :, hs]))
        o = jnp.concatenate(heads, axis=-1).astype(BF16)
        h2 = h1 + _dot(o, wco_ref[...])
        h_ref[0, rs, :] = h2

        xn = _rms(h2, gf_ref[...]).astype(BF16)
        bits = lax.bitcast_convert_type(xn.astype(F32), U32)
        xn_ref[0, rs, :] = (bits[:, :half] >> 16) | (bits[:, half:] & HI16)
        logits = _dot(xn, wr_ref[...]) + br_ref[...]

        lg = jnp.where(lane < N_GROUPS, logits, NEG_INF)
        g_max, g_sel = lane_argmax(lg)
        p_g = 1.0 / jnp.sum(jnp.exp(lg - g_max), axis=-1, keepdims=True)
        e_lo = _EXP_LANE0 + g_sel * EXPERTS_PER_GROUP
        in_group = jnp.logical_and(lane >= e_lo, lane < e_lo + EXPERTS_PER_GROUP)
        le = jnp.where(in_group, logits, NEG_INF)
        m1, i1 = lane_argmax(le)
        m2, i2 = lane_argmax(jnp.where(lane == i1, NEG_INF, le))
        e2 = jnp.exp(m2 - m1)
        gate0 = p_g / (1.0 + e2)
        gate1 = p_g * e2 / (1.0 + e2)
        e0 = i1 - _EXP_LANE0
        e1 = i2 - _EXP_LANE0

        hot0 = lane == e0
        hot1 = lane == e1
        onehot = jnp.where(jnp.logical_or(hot0, hot1), 1.0, 0.0)
        before = _dot(strict_lower, onehot.astype(BF16)) + base
        pos0 = jnp.sum(jnp.where(hot0, before, 0.0), axis=-1, keepdims=True)
        pos1 = jnp.sum(jnp.where(hot1, before, 0.0), axis=-1, keepdims=True)

        meta = jnp.zeros(logits.shape, F32)
        for idx, val in ((_META_E0, e0.astype(F32)), (_META_E1, e1.astype(F32)), (_META_G0, gate0),
                         (_META_G1, gate1), (_META_P0, pos0), (_META_P1, pos1)):
            meta = jnp.where(lane == idx, val, meta)
        meta_ref[0, rs, :] = meta
        ids_ref[0, 0, :, rs] = meta.T[:ids_ref.shape[2], :]
        return base + jnp.sum(onehot, axis=0, keepdims=True)

    base = cnt_ref[0:1, :]
    for r0 in range(0, tm, sub):
        base = rows(slice(r0, r0 + sub), base)
    cnt_ref[...] = jnp.broadcast_to(base, cnt_ref.shape)
    if rider:
        _scatter_wait(rsrc_ref, rpad_ref, rsem)


def _post(x, da, gla, w_o, norm_cross, w_cq, cross_qn, k_mem, v_mem, w_co, norm_ffn, w_group, b_group,
          w_expert, b_expert, *, half, rider=None):
    b, s, d = x.shape
    bh = b // 2
    b0 = half * bh
    tm = TM_POST
    ns = s // tm
    m = k_mem.shape[1]
    w_r = jnp.pad(jnp.concatenate([w_group, w_expert], axis=1), ((0, 0), (0, LANES - N_GROUPS - N_EXPERTS)))
    b_r = jnp.pad(jnp.concatenate([b_group, b_expert]), (0, LANES - N_GROUPS - N_EXPERTS)).reshape(1, LANES)
    const = lambda shape: pl.BlockSpec(shape, lambda i, j, *_: (0,) * len(shape))
    tile_in = lambda width: pl.BlockSpec((1, tm, width), lambda i, j, *_: (i + b0, j, 0))
    tile_out = lambda width: pl.BlockSpec((1, tm, width), lambda i, j, *_: (i, j, 0))
    per_b = lambda: pl.BlockSpec((1, m, d), lambda i, j, *_: (i + b0, 0, 0))
    da_spec = pl.BlockSpec((1, DA_HEADS, tm, DA_V_DIM), lambda i, j, *_: (i + b0, 0, j, 0))
    in_specs = [tile_in(d), da_spec, tile_in(d // 2), const((d, d)), const((1, d)), const((d, d)),
                const((1, d // CROSS_HEADS)), per_b(), per_b(), const((d, d)), const((1, d)),
                const((d, LANES)), const((1, LANES))]
    out_specs = [tile_out(d), tile_out(d // 2), tile_out(LANES),
                 pl.BlockSpec((1, 1, 8, tm), lambda i, j, *_: (i, j, 0, 0)), const((8, LANES))]
    out_shape = [
        jax.ShapeDtypeStruct((bh, s, d), F32),
        jax.ShapeDtypeStruct((bh, s, d // 2), U32),
        jax.ShapeDtypeStruct((bh, s, LANES), F32),
        jax.ShapeDtypeStruct((bh, ns, 8, tm), F32),
        jax.ShapeDtypeStruct((8, LANES), F32),
    ]
    args = (x, da, gla, w_o.astype(BF16), norm_cross.reshape(1, d), w_cq.astype(BF16), cross_qn.reshape(1, -1),
            k_mem, v_mem, w_co.astype(BF16), norm_ffn.reshape(1, d), w_r.astype(BF16), b_r)
    scalars, scratch = (), []
    if rider is not None:
        sizes, pend, dest, xn_src, n_rows = rider
        scalars = (sizes, pend)
        in_specs += [pl.BlockSpec((1, 2, tm), lambda i, j, *_: (i * ns + j, 0, 0), memory_space=pltpu.SMEM),
                     pl.BlockSpec((tm, d // 2), lambda i, j, *_: (i * ns + j, 0))]
        out_specs.append(pl.BlockSpec(memory_space=pl.ANY))
        out_shape.append(jax.ShapeDtypeStruct((n_rows, d // 2), U32))
        scratch = [pltpu.VMEM((EXPERT_ROWS, d // 2), U32), pltpu.SemaphoreType.DMA, pltpu.SemaphoreType.DMA]
        args += (dest, xn_src)
    return pl.pallas_call(
        functools.partial(_post_kernel, d=d, tm=tm, sub=SUB_POST, rider=rider is not None),
        grid_spec=pltpu.PrefetchScalarGridSpec(
            num_scalar_prefetch=len(scalars), grid=(bh, ns), in_specs=in_specs, out_specs=out_specs,
            scratch_shapes=scratch),
        out_shape=out_shape,
        compiler_params=_params("arbitrary", "arbitrary"),
        name="post_scatter" if rider is not None else "post",
    )(*scalars, *args)


def _row_copy(src_ref, src_row, dst_ref, dst_row, sem):
    return pltpu.make_async_copy(src_ref.at[pl.ds(src_row, 1)], dst_ref.at[pl.ds(dst_row, 1)], sem)


def _zero_fill(size_ref, pend_ref, xpad_ref, zero_ref, zsem):
    zero_ref[...] = jnp.zeros(zero_ref.shape, zero_ref.dtype)
    rows = zero_ref.shape[0]
    n_blocks = xpad_ref.shape[0] // rows
    n_used = pend_ref[N_EXPERTS - 1] // rows

    def zero_block(blk):
        return pltpu.make_async_copy(zero_ref, xpad_ref.at[pl.ds(pl.multiple_of(blk * rows, rows), rows)], zsem)

    def last_block(e, fn):
        @pl.when(size_ref[e] > 0)
        def _():
            fn(zero_block(pend_ref[e] // rows - 1))

    for fn in (lambda cp: cp.start(), lambda cp: cp.wait()):
        lax.fori_loop(0, N_EXPERTS, lambda e, c: (last_block(e, fn), c)[1], 0)
        lax.fori_loop(n_used, n_blocks, lambda blk, c: (fn(zero_block(blk)), c)[1], 0)


def _scatter_rows(dest_ref, src_ref, xpad_ref, sem):
    for t in range(src_ref.shape[0]):
        for k in range(2):
            _row_copy(src_ref, t, xpad_ref, dest_ref[0, k, t], sem).start(priority=k)


def _scatter_wait(src_ref, xpad_ref, sem):
    for _ in range(2):
        pltpu.make_async_copy(src_ref, xpad_ref.at[pl.ds(0, src_ref.shape[0])], sem).wait()


def _cast_expert_weights(fresh, wg_ref, wu_ref, wd_ref, wg_s, wu_s, wd_s):
    @pl.when(fresh)
    def _():
        wg_s[...] = wg_ref[0].astype(BF16)
        wu_s[...] = wu_ref[0].astype(BF16)
        wd_s[...] = wd_ref[0].astype(BF16)


def _expert_mlp(x_ref, wg_s, wu_s, wd_s, out_ref):
    words = x_ref[...]
    half = words.shape[1]
    lo = lax.bitcast_convert_type(words << 16, F32).astype(BF16)
    hi = lax.bitcast_convert_type(words & HI16, F32).astype(BF16)
    gate = _dot(lo, wg_s[:half, :]) + _dot(hi, wg_s[half:, :])
    up = _dot(lo, wu_s[:half, :]) + _dot(hi, wu_s[half:, :])
    hid = gate / (1.0 + jnp.exp(-gate)) * up
    out_ref[...] = _dot(hid.astype(BF16), wd_s[...])


def _experts_scatter_kernel(be_ref, nused_ref, size_ref, pend_ref, x_ref, wg_ref, wu_ref, wd_ref, rdest_ref, rsrc_ref,
                            out_ref, rpad_ref, wg_s, wu_s, wd_s, zero_ref, rsem, zsem, *, rider_steps):
    i = pl.program_id(0)
    used = i < nused_ref[0]
    new_expert = jnp.logical_or(i == 0, be_ref[i] != be_ref[jnp.maximum(i - 1, 0)])
    mlp = functools.partial(_expert_mlp, x_ref, wg_s, wu_s, wd_s, out_ref)

    @pl.when(i == 0)
    def _():
        _zero_fill(size_ref, pend_ref, rpad_ref, zero_ref, zsem)

    _cast_expert_weights(jnp.logical_and(used, new_expert), wg_ref, wu_ref, wd_ref, wg_s, wu_s, wd_s)
    riding = i < rider_steps

    @pl.when(riding)
    def _():
        _scatter_rows(rdest_ref, rsrc_ref, rpad_ref, rsem)
        mlp()
        _scatter_wait(rsrc_ref, rpad_ref, rsem)

    pl.when(jnp.logical_and(used, jnp.logical_not(riding)))(mlp)

    @pl.when(jnp.logical_not(used))
    def _():
        out_ref[...] = jnp.zeros(out_ref.shape, F32)


def _gather_rows(dest_ref, src_ref, buf_ref, sem):
    for t in range(buf_ref.shape[1]):
        for k in range(2):
            _row_copy(src_ref, dest_ref[0, k, t], buf_ref.at[k], t, sem).start(priority=k)


def _combine_rows(h_ref, meta_ref, src_ref, buf_ref, sem):
    for k in range(2):
        pltpu.make_async_copy(src_ref.at[pl.ds(0, buf_ref.shape[1])], buf_ref.at[k], sem).wait()
    meta = meta_ref[...]
    g0 = meta[:, _META_G0:_META_G0 + 1]
    g1 = meta[:, _META_G1:_META_G1 + 1]
    return h_ref[...] + g0 * buf_ref[0] + g1 * buf_ref[1]


def _experts_gather_kernel(be_ref, nused_ref, x_ref, wg_ref, wu_ref, wd_ref, gdest_ref, h_ref, meta_ref, opad_ref,
                           out_ref, y_ref, wg_s, wu_s, wd_s, gbuf_ref, gsem, *, rider_steps):
    i = pl.program_id(0)
    used = i < nused_ref[0]
    new_expert = jnp.logical_or(i == 0, be_ref[i] != be_ref[jnp.maximum(i - 1, 0)])
    mlp = functools.partial(_expert_mlp, x_ref, wg_s, wu_s, wd_s, out_ref)
    _cast_expert_weights(jnp.logical_and(used, new_expert), wg_ref, wu_ref, wd_ref, wg_s, wu_s, wd_s)

    def gather():
        _gather_rows(gdest_ref, opad_ref, gbuf_ref.at[i % 2], gsem.at[i % 2])

    def combine():
        slot = (i - 1) % 2
        y_ref[...] = _combine_rows(h_ref, meta_ref, opad_ref, gbuf_ref.at[slot], gsem.at[slot])

    @pl.when(i == 0)
    def _():
        gather()
        mlp()

    @pl.when(jnp.logical_and(i > 0, i < rider_steps))
    def _():
        gather()
        mlp()
        combine()

    @pl.when(i == rider_steps)
    def _():
        combine()
        pl.when(used)(mlp)

    pl.when(jnp.logical_and(i > rider_steps, used))(mlp)

    @pl.when(jnp.logical_not(used))
    def _():
        out_ref[...] = jnp.zeros(out_ref.shape, F32)


def _experts_kernel(be_ref, nused_ref, x_ref, wg_ref, wu_ref, wd_ref, out_ref, wg_s, wu_s, wd_s):
    i = pl.program_id(0)
    used = i < nused_ref[0]
    new_expert = jnp.logical_or(i == 0, be_ref[i] != be_ref[jnp.maximum(i - 1, 0)])
    _cast_expert_weights(jnp.logical_and(used, new_expert), wg_ref, wu_ref, wd_ref, wg_s, wu_s, wd_s)
    pl.when(used)(functools.partial(_expert_mlp, x_ref, wg_s, wu_s, wd_s, out_ref))

    @pl.when(jnp.logical_not(used))
    def _():
        out_ref[...] = jnp.zeros(out_ref.shape, F32)


def _retile(dest, tr):
    tiles, _, tm = dest.shape
    return dest.reshape(tiles, 2, tm // tr, tr).transpose(0, 2, 1, 3).reshape(tiles * (tm // tr), 2, tr)


def _experts(x_pad, block_expert, n_used, w_gate, w_up, w_down, scatter=None, gather=None):
    n_rows = x_pad.shape[0]
    _, d, f = w_gate.shape
    rows, tr = EXPERT_ROWS, RIDER_ROWS
    row_blk = lambda i, be, nu, *_: (jnp.minimum(i, nu[0] - 1), 0)
    weights = lambda shape: pl.BlockSpec(shape, lambda i, be, *_: (be[i], 0, 0))
    in_specs = [pl.BlockSpec((rows, d // 2), row_blk), weights((1, d, f)), weights((1, d, f)), weights((1, f, d))]
    out_specs = [pl.BlockSpec((rows, d), lambda i, *_: (i, 0))]
    out_shape = [jax.ShapeDtypeStruct((n_rows, d), F32)]
    scratch = [pltpu.VMEM((d, f), BF16), pltpu.VMEM((d, f), BF16), pltpu.VMEM((f, d), BF16)]
    scalars, args = (block_expert, n_used), (x_pad, w_gate, w_up, w_down)
    body, name = _experts_kernel, "experts"
    if scatter is not None or gather is not None:
        n_tokens = (scatter[3] if scatter is not None else gather[1]).shape[0]
        steps = n_tokens // tr
        assert 2 * n_tokens >= steps * rows, "riding steps must all be used expert blocks"
        tile = lambda width, shift: pl.BlockSpec((tr, width), lambda i, *_: (jnp.clip(i - shift, 0, steps - 1), 0))
        ids = lambda: pl.BlockSpec((1, 2, tr), lambda i, *_: (jnp.minimum(i, steps - 1), 0, 0),
                                   memory_space=pltpu.SMEM)
    if scatter is not None:
        sizes, pend, dest, xn_src, other_rows = scatter
        scalars += (sizes, pend)
        in_specs += [ids(), tile(d // 2, 0)]
        out_specs.append(pl.BlockSpec(memory_space=pl.ANY))
        out_shape.append(jax.ShapeDtypeStruct((other_rows, d // 2), U32))
        scratch += [pltpu.VMEM((rows, d // 2), U32), pltpu.SemaphoreType.DMA, pltpu.SemaphoreType.DMA]
        args += (_retile(dest, tr), xn_src)
        body, name = functools.partial(_experts_scatter_kernel, rider_steps=steps), "experts_scatter"
    elif gather is not None:
        dest, h2, meta, opad = gather
        in_specs += [ids(), tile(d, 1), tile(LANES, 1), pl.BlockSpec(memory_space=pl.ANY)]
        out_specs.append(tile(d, 1))
        out_shape.append(jax.ShapeDtypeStruct((n_tokens, d), F32))
        scratch += [pltpu.VMEM((2, 2, tr, d), F32), pltpu.SemaphoreType.DMA((2,))]
        args += (_retile(dest, tr), h2, meta, opad)
        body, name = functools.partial(_experts_gather_kernel, rider_steps=steps), "experts_gather"
    out = pl.pallas_call(
        body,
        grid_spec=pltpu.PrefetchScalarGridSpec(
            num_scalar_prefetch=len(scalars), grid=(n_rows // rows,), in_specs=in_specs, out_specs=out_specs,
            scratch_shapes=scratch),
        out_shape=out_shape,
        compiler_params=_params("arbitrary"),
        name=name,
    )(*scalars, *args)
    return out if len(out) > 1 else out[0]


def _combine_tail_kernel(dest_ref, y0_ref, h_ref, meta_ref, opad_ref, y_ref, buf_ref, sem, *, nt):
    s = pl.program_id(0)

    @pl.when(s < nt)
    def _():
        _gather_rows(dest_ref, opad_ref, buf_ref.at[s % 2], sem.at[s % 2])

    @pl.when(s > 0)
    def _():
        slot = (s - 1) % 2
        y_ref[0] = y0_ref[...]
        y_ref[1] = _combine_rows(h_ref, meta_ref, opad_ref, buf_ref.at[slot], sem.at[slot])


def _combine_tail(y0, h2, meta, dest, out_pad):
    th, d = h2.shape
    tm = TM_TAIL
    nt = th // tm
    prev = lambda s: (jnp.maximum(s - 1, 0), 0)
    return pl.pallas_call(
        functools.partial(_combine_tail_kernel, nt=nt),
        grid=(nt + 1,),
        in_specs=[
            pl.BlockSpec((1, 2, tm), lambda s: (jnp.minimum(s, nt - 1), 0, 0), memory_space=pltpu.SMEM),
            pl.BlockSpec((tm, d), prev), pl.BlockSpec((tm, d), prev), pl.BlockSpec((tm, LANES), prev),
            pl.BlockSpec(memory_space=pl.ANY),
        ],
        out_specs=pl.BlockSpec((2, tm, d), lambda s: (0, jnp.maximum(s - 1, 0), 0)),
        scratch_shapes=[pltpu.VMEM((2, 2, tm, d), F32), pltpu.SemaphoreType.DMA((2,))],
        out_shape=jax.ShapeDtypeStruct((2, th, d), F32),
        compiler_params=_params("arbitrary"),
        name="combine_tail",
    )(_retile(dest, tm), y0, h2, meta, out_pad)


def _route_tables(counts, ids, n_tokens):
    rows = EXPERT_ROWS
    sizes = counts[0, :N_EXPERTS].astype(I32)
    padded = (sizes + rows - 1) // rows * rows
    pend = jnp.cumsum(padded)
    pstart = pend - padded
    n_rows = 2 * n_tokens + N_EXPERTS * rows
    block_start = jnp.arange(n_rows // rows, dtype=I32) * rows
    block_expert = jnp.minimum(jnp.sum(pend[None, :] <= block_start[:, None], axis=1), N_EXPERTS - 1).astype(I32)
    n_used = (pend[-1:] // rows).astype(I32)
    expert = ids[:, _META_E0:_META_E1 + 1].astype(I32)
    rank = ids[:, _META_P0:_META_P1 + 1].astype(I32)
    experts = jnp.arange(N_EXPERTS, dtype=I32).reshape(-1, 1, 1, 1)
    dest = jnp.sum(jnp.where(expert[None] == experts, pstart.reshape(-1, 1, 1, 1), 0), axis=0) + rank
    return sizes, pend, block_expert, n_used, dest, n_rows


def kernel(x, mem, norm_mix, w_in, da_q_norm, da_k_norm, lambda_q1, lambda_k1, lambda_q2, lambda_k2,
           da_out_norm, gla_gate_w, gla_gate_b, gla_out_norm, w_o, norm_cross, norm_mem, w_cq, w_ckv,
           cross_q_norm, cross_k_norm, w_co, norm_ffn, w_group, b_group, w_expert, b_expert,
           w_e_gate, w_e_up, w_e_down):
    b, s, d = x.shape
    th = b // 2 * s
    h = x
    for l in range(norm_mix.shape[0]):
        assert l == 0, "lam_init is fixed for a single layer"
        qt, kda, vt, gq, gk, gv, gg, la = _in_proj(h, norm_mix[l], w_in[l], da_q_norm[l], da_k_norm[l],
                                                   gla_gate_w[l], gla_gate_b[l])
        da = _diff_attn(qt, kda, vt, lambda_q1[l], lambda_k1[l], lambda_q2[l], lambda_k2[l], da_out_norm[l],
                        da_q_norm[l], da_k_norm[l])
        gla = _gla(gq, gk, la, gv, gg, gla_out_norm[l])
        k_mem, v_mem = _mem_kv(mem, norm_mem[l], w_ckv[l], cross_k_norm[l])
        post = functools.partial(_post, h, da, gla, w_o[l], norm_cross[l], w_cq[l], cross_q_norm[l], k_mem, v_mem,
                                 w_co[l], norm_ffn[l], w_group[l], b_group[l], w_expert[l], b_expert[l])
        experts = functools.partial(_experts, w_gate=w_e_gate[l].astype(BF16), w_up=w_e_up[l].astype(BF16),
                                    w_down=w_e_down[l].astype(BF16))

        h2_0, xn_0, meta_0, ids_0, counts_0 = post(half=0)
        sizes_0, pend_0, be_0, used_0, dest_0, n_rows = _route_tables(counts_0, ids_0.reshape(-1, 8, TM_POST), th)
        h2_1, xn_1, meta_1, ids_1, counts_1, xpad_0 = post(
            half=1, rider=(sizes_0, pend_0, dest_0, xn_0.reshape(th, d // 2), n_rows))
        sizes_1, pend_1, be_1, used_1, dest_1, _ = _route_tables(counts_1, ids_1.reshape(-1, 8, TM_POST), th)
        opad_0, xpad_1 = experts(xpad_0, be_0, used_0,
                                 scatter=(sizes_1, pend_1, dest_1, xn_1.reshape(th, d // 2), n_rows))
        opad_1, y_0 = experts(xpad_1, be_1, used_1,
                              gather=(dest_0, h2_0.reshape(th, d), meta_0.reshape(th, LANES), opad_0))
        h = _combine_tail(y_0, h2_1.reshape(th, d), meta_1.reshape(th, LANES), dest_1, opad_1).reshape(b, s, d)
    return h
```

```python
import functools
import math

import jax
import jax.numpy as jnp
import numpy as np
from jax import lax
from jax.experimental import pallas as pl
from jax.experimental.pallas import tpu as pltpu

F32 = jnp.float32
BF16 = jnp.bfloat16
I32 = jnp.int32
U32 = jnp.uint32
HI16 = np.uint32(0xFFFF0000)

EPS = 1e-6
CHUNK = 64

DA_HEADS = 4
DA_QK_DIM = 64
DA_V_DIM = 128
GLA_HEADS = 4
GLA_K_DIM = 64
GLA_V_DIM = 128
GLA_GATE_RANK = 16
GLA_TAU = 16.0
CROSS_HEADS = 4
N_GROUPS = 4
EXPERTS_PER_GROUP = 8
N_EXPERTS = N_GROUPS * EXPERTS_PER_GROUP
LAM_INIT = 0.8 - 0.6 * math.exp(-0.3 * 0)

LANES = 128
VMEM_LIMIT = 56 * 1024 * 1024

TM_PROJ = 1024
ATT_BLK = 512
TS_GLA = 1024
GLA_GROUP = 4
TM_POST = 1024
SUB_POST = 256
TM_TAIL = 512
RIDER_ROWS = 256
EXPERT_ROWS = 512

NEG_INF = float("-inf")


def _params(*sem):
    return pltpu.CompilerParams(dimension_semantics=sem, vmem_limit_bytes=VMEM_LIMIT)


def _rms(t, g):
    ms = jnp.mean(t * t, axis=-1, keepdims=True)
    return t * lax.rsqrt(ms + EPS) * g


def _dot(a, b):
    return jnp.dot(a, b, preferred_element_type=F32)


def _dot_nt(a, b):
    return lax.dot_general(a, b, (((1,), (1,)), ((), ())), preferred_element_type=F32)


def _dot_tn(a, b):
    return lax.dot_general(a, b, (((0,), (0,)), ((), ())), preferred_element_type=F32)


def _split_bf16(t):
    hi = t.astype(BF16)
    lo = (t - hi.astype(F32)).astype(BF16)
    return hi, lo


def _mem_kv_kernel(mem_ref, g_ref, w_ref, kn_ref, k_ref, v_ref, *, d, heads):
    mn = _rms(mem_ref[0], g_ref[...]).astype(BF16)
    kv = _dot(mn, w_ref[...])
    hd = d // heads
    scale = hd ** -0.5
    for h in range(heads):
        kh = _rms(kv[:, h * hd:(h + 1) * hd], kn_ref[...]) * scale
        k_ref[0, :, h * hd:(h + 1) * hd] = kh.astype(BF16)
    v_ref[0] = kv[:, d:].astype(BF16)


def _mem_kv(mem, norm_m, w_ckv, kn):
    b, m, d = mem.shape
    return pl.pallas_call(
        functools.partial(_mem_kv_kernel, d=d, heads=CROSS_HEADS),
        grid=(b,),
        in_specs=[
            pl.BlockSpec((1, m, d), lambda i: (i, 0, 0)),
            pl.BlockSpec((1, d), lambda i: (0, 0)),
            pl.BlockSpec((d, 2 * d), lambda i: (0, 0)),
            pl.BlockSpec((1, d // CROSS_HEADS), lambda i: (0, 0)),
        ],
        out_specs=[
            pl.BlockSpec((1, m, d), lambda i: (i, 0, 0)),
            pl.BlockSpec((1, m, d), lambda i: (i, 0, 0)),
        ],
        out_shape=[jax.ShapeDtypeStruct((b, m, d), BF16)] * 2,
        compiler_params=_params("parallel"),
        name="mem_kv",
    )(mem, norm_m.reshape(1, d), w_ckv.astype(BF16), kn.reshape(1, -1))


_QK = DA_HEADS * 2 * DA_QK_DIM
_DAW = DA_HEADS * DA_V_DIM
_GQK = GLA_HEADS * GLA_K_DIM
_GW = GLA_HEADS * GLA_V_DIM
_OFF_DQ = 0
_OFF_DK = _OFF_DQ + _QK
_OFF_DV = _OFF_DK + _QK
_OFF_GQ = _OFF_DV + _DAW
_OFF_GK = _OFF_GQ + _GQK
_OFF_GV = _OFF_GK + _GQK
_OFF_GG = _OFF_GV + _GW
_OFF_GR = _OFF_GG + _GW
_IN_PAD = _OFF_GR + LANES


def _in_proj_kernel(x_ref, g_ref, w_ref, qg_ref, kg_ref, grp_ref, gw_ref, gb_ref,
                    qt_ref, k_ref, vt_ref, gq_ref, gk_ref, gv_ref, gg_ref, la_ref):
    u = _rms(x_ref[0], g_ref[...]).astype(BF16)

    def proj(off, width):
        return _dot(u, w_ref[:, off:off + width])

    def group_norm(p, gain):
        ms = _dot((p * p).astype(BF16), grp_ref[...])
        return p * lax.rsqrt(ms + EPS) * gain

    qn = group_norm(proj(_OFF_DQ, _QK), qg_ref[...]) * (DA_QK_DIM ** -0.5 * math.log2(math.e))
    blk = qt_ref.shape[3]
    for t in range(qt_ref.shape[1]):
        qt_ref[0, t] = qn[t * blk:(t + 1) * blk].T.astype(BF16)
    kn = group_norm(proj(_OFF_DK, _QK), kg_ref[...]).astype(BF16)
    for h in range(DA_HEADS):
        k_ref[0, h] = kn[:, h * 2 * DA_QK_DIM:(h + 1) * 2 * DA_QK_DIM]
    dv = proj(_OFF_DV, _DAW)
    for t in range(vt_ref.shape[1]):
        vt_ref[0, t] = dv[t * blk:(t + 1) * blk].T.astype(BF16)
    gq_ref[0] = proj(_OFF_GQ, _GQK) * (GLA_K_DIM ** -0.5)
    gk_ref[0] = proj(_OFF_GK, _GQK)
    gv_ref[0] = proj(_OFF_GV, _GW).astype(BF16)
    gg_ref[0] = proj(_OFF_GG, _GW)
    z = _dot(proj(_OFF_GR, LANES).astype(BF16), gw_ref[...]) + gb_ref[...]
    log_sig = jnp.minimum(z, 0.0) - jnp.log(1.0 + jnp.exp(-jnp.abs(z)))
    la_ref[0] = log_sig * (math.log2(math.e) / GLA_TAU)


def _in_proj(x, norm_g, w_in, da_qn, da_kn, gate_w, gate_b):
    b, s, d = x.shape
    tm, blk = TM_PROJ, ATT_BLK
    ns, nb = s // tm, tm // blk
    w = jnp.pad(w_in, ((0, 0), (0, _IN_PAD - w_in.shape[1]))).astype(BF16)
    gw = jnp.pad(gate_w, ((0, LANES - GLA_GATE_RANK), (0, 0))).astype(BF16)
    lane = jnp.arange(_QK)
    grp = jnp.where((lane[:, None] // DA_QK_DIM) == (lane[None, :] // DA_QK_DIM),
                    1.0 / DA_QK_DIM, 0.0).astype(BF16)
    const = lambda shape: pl.BlockSpec(shape, lambda i, j: (0,) * len(shape))
    tile = lambda width: pl.BlockSpec((1, tm, width), lambda i, j: (i, j, 0))
    tile_t = lambda width: pl.BlockSpec((1, nb, width, blk), lambda i, j: (i, j, 0, 0))
    return pl.pallas_call(
        _in_proj_kernel,
        grid=(b, ns),
        in_specs=[tile(d), const((1, d)), const((d, _IN_PAD)), const((1, _QK)), const((1, _QK)),
                  const((_QK, _QK)), const((LANES, _GQK)), const((1, _GQK))],
        out_specs=[tile_t(_QK), pl.BlockSpec((1, DA_HEADS, tm, 2 * DA_QK_DIM), lambda i, j: (i, 0, j, 0)),
                   tile_t(_DAW), tile(_GQK), tile(_GQK), tile(_GW), tile(_GW), tile(_GQK)],
        out_shape=[
            jax.ShapeDtypeStruct((b, s // blk, _QK, blk), BF16),
            jax.ShapeDtypeStruct((b, DA_HEADS, s, 2 * DA_QK_DIM), BF16),
            jax.ShapeDtypeStruct((b, s // blk, _DAW, blk), BF16),
            jax.ShapeDtypeStruct((b, s, _GQK), F32),
            jax.ShapeDtypeStruct((b, s, _GQK), F32),
            jax.ShapeDtypeStruct((b, s, _GW), BF16),
            jax.ShapeDtypeStruct((b, s, _GW), F32),
            jax.ShapeDtypeStruct((b, s, _GQK), F32),
        ],
        compiler_params=_params("parallel", "parallel"),
        name="in_proj",
    )(x, norm_g.reshape(1, d), w, jnp.tile(da_qn, 2 * DA_HEADS).reshape(1, _QK),
      jnp.tile(da_kn, 2 * DA_HEADS).reshape(1, _QK), grp, gw, gate_b.reshape(1, _GQK))


def _split_q(qt):
    row = lax.broadcasted_iota(I32, qt.shape, 0)
    zero = jnp.zeros_like(qt)
    return jnp.where(row < DA_QK_DIM, qt, zero), jnp.where(row >= DA_QK_DIM, qt, zero)


def _chunk_causal_mask(blk):
    key_chunk = lax.broadcasted_iota(I32, (blk, blk), 0) // CHUNK
    qry_chunk = lax.broadcasted_iota(I32, (blk, blk), 1) // CHUNK
    return key_chunk <= qry_chunk


def _diff_attn_finish(lq1_ref, lk1_ref, lq2_ref, lk2_ref, gain_ref, a1, l1, a2, l2):
    lam = (jnp.exp(jnp.sum(lq1_ref[...] * lk1_ref[...], axis=-1, keepdims=True))
           - jnp.exp(jnp.sum(lq2_ref[...] * lk2_ref[...], axis=-1, keepdims=True)) + LAM_INIT)
    o = a1 / l1 - lam * (a2 / l2)
    ms = jnp.mean(o * o, axis=0, keepdims=True)
    o = o * lax.rsqrt(ms + EPS) * gain_ref[...] * (1.0 - LAM_INIT)
    return o.T.astype(BF16)


def _diff_attn_bounded_kernel(lq1_ref, lk1_ref, lq2_ref, lk2_ref, gain_ref, qt_ref, k_ref, vt_ref, out_ref,
                              s_ref, l1_ref, a1_ref, l2_ref, a2_ref, *, blk, nb):
    stats = ((l1_ref, a1_ref), (l2_ref, a2_ref))
    mask = _chunk_causal_mask(blk)

    def reset():
        for l_ref, a_ref in stats:
            l_ref[...] = jnp.zeros(l_ref.shape, F32)
            a_ref[...] = jnp.zeros(a_ref.shape, F32)

    def scores(q, j, slot):
        kb = k_ref[0, 0, pl.ds(pl.multiple_of(j * blk, blk), blk), :]
        s_ref[slot, 0] = _dot(kb, q[0])
        s_ref[slot, 1] = _dot(kb, q[1])

    def consume(j, slot, masked):
        vb = vt_ref[0, j]
        for m, (l_ref, a_ref) in enumerate(stats):
            s = s_ref[slot, m]
            if masked:
                s = jnp.where(mask, s, NEG_INF)
            p = jnp.exp2(s)
            l_ref[...] += jnp.sum(p, axis=0, keepdims=True)
            a_ref[...] += _dot(vb, p.astype(BF16))

    def step(q, j, slot):
        scores(q, j + 1, 1 - slot)
        consume(j, slot, False)

    reset()
    q = _split_q(qt_ref[0, 0])
    slot = 0
    scores(q, 0, slot)
    for qi in range(nb):
        def pair(i, carry, q=q, slot=slot):
            step(q, 2 * i, slot)
            step(q, 2 * i + 1, 1 - slot)
            return carry

        if qi // 2:
            lax.fori_loop(0, qi // 2, pair, 0)
        if qi % 2:
            step(q, qi - 1, slot)
            slot = 1 - slot
        if qi + 1 < nb:
            q = _split_q(qt_ref[0, qi + 1])
            scores(q, 0, 1 - slot)
        consume(qi, slot, True)
        out_ref[0, 0, qi * blk:(qi + 1) * blk, :] = _diff_attn_finish(
            lq1_ref, lk1_ref, lq2_ref, lk2_ref, gain_ref, a1_ref[...], l1_ref[...], a2_ref[...], l2_ref[...])
        if qi + 1 < nb:
            reset()
        slot = 1 - slot


def _diff_attn_online_kernel(lq1_ref, lk1_ref, lq2_ref, lk2_ref, gain_ref, qt_ref, k_ref, vt_ref, out_ref,
                             m1_ref, l1_ref, a1_ref, m2_ref, l2_ref, a2_ref, *, blk):
    qi = pl.program_id(2)
    q1, q2 = _split_q(qt_ref[0, 0])

    for m_ref, l_ref, a_ref in ((m1_ref, l1_ref, a1_ref), (m2_ref, l2_ref, a2_ref)):
        m_ref[...] = jnp.full(m_ref.shape, NEG_INF, F32)
        l_ref[...] = jnp.zeros(l_ref.shape, F32)
        a_ref[...] = jnp.zeros(a_ref.shape, F32)

    def update(s, vb, m_ref, l_ref, a_ref):
        m_old = m_ref[...]
        m_new = jnp.maximum(m_old, jnp.max(s, axis=0, keepdims=True))
        alpha = jnp.exp2(m_old - m_new)
        p = jnp.exp2(s - m_new)
        l_ref[...] = alpha * l_ref[...] + jnp.sum(p, axis=0, keepdims=True)
        a_ref[...] = alpha * a_ref[...] + _dot(vb, p.astype(BF16))
        m_ref[...] = m_new

    def block(j, mask):
        kb = k_ref[0, 0, pl.ds(pl.multiple_of(j * blk, blk), blk), :]
        vb = vt_ref[0, j]
        s1 = _dot(kb, q1)
        s2 = _dot(kb, q2)
        if mask is not None:
            s1 = jnp.where(mask, s1, NEG_INF)
            s2 = jnp.where(mask, s2, NEG_INF)
        update(s1, vb, m1_ref, l1_ref, a1_ref)
        update(s2, vb, m2_ref, l2_ref, a2_ref)

    def body(j, carry):
        block(j, None)
        return carry

    lax.fori_loop(0, qi, body, 0)
    block(qi, _chunk_causal_mask(blk))
    out_ref[0, 0] = _diff_attn_finish(lq1_ref, lk1_ref, lq2_ref, lk2_ref, gain_ref,
                                      a1_ref[...], l1_ref[...], a2_ref[...], l2_ref[...])


SCORE_BOUND = 60.0


def _diff_attn(qt, k, vt, lq1, lk1, lq2, lk2, da_on, da_qn, da_kn):
    b, nb, _, blk = qt.shape
    s = nb * blk
    stat = lambda: pltpu.VMEM((1, blk), F32)
    acc = lambda: pltpu.VMEM((DA_V_DIM, blk), F32)

    args = (lq1.reshape(1, -1), lk1.reshape(1, -1), lq2.reshape(1, -1), lk2.reshape(1, -1),
            da_on.reshape(-1, 1), qt, k, vt)
    out_shape = jax.ShapeDtypeStruct((b, DA_HEADS, s, DA_V_DIM), BF16)
    head = lambda *trailing: (lambda i, h: (i, 0, h) + trailing)
    vec2 = lambda: pl.BlockSpec((1, DA_QK_DIM), lambda i, h: (0, 0))
    bounded = pl.pallas_call(
        functools.partial(_diff_attn_bounded_kernel, blk=blk, nb=nb),
        grid=(b, DA_HEADS),
        in_specs=[
            vec2(), vec2(), vec2(), vec2(),
            pl.BlockSpec((DA_V_DIM, 1), lambda i, h: (0, 0)),
            pl.BlockSpec((1, nb, 2 * DA_QK_DIM, blk), head(0)),
            pl.BlockSpec((1, 1, s, 2 * DA_QK_DIM), lambda i, h: (i, h, 0, 0)),
            pl.BlockSpec((1, nb, DA_V_DIM, blk), head(0)),
        ],
        out_specs=pl.BlockSpec((1, 1, s, DA_V_DIM), lambda i, h: (i, h, 0, 0)),
        out_shape=out_shape,
        scratch_shapes=[pltpu.VMEM((2, 2, blk, blk), F32), stat(), acc(), stat(), acc()],
        compiler_params=_params("parallel", "parallel"),
        name="diff_attn",
    )
    vec3 = lambda: pl.BlockSpec((1, DA_QK_DIM), lambda i, h, q: (0, 0))
    online = pl.pallas_call(
        functools.partial(_diff_attn_online_kernel, blk=blk),
        grid=(b, DA_HEADS, nb),
        in_specs=[
            vec3(), vec3(), vec3(), vec3(),
            pl.BlockSpec((DA_V_DIM, 1), lambda i, h, q: (0, 0)),
            pl.BlockSpec((1, 1, 2 * DA_QK_DIM, blk), lambda i, h, q: (i, q, h, 0)),
            pl.BlockSpec((1, 1, s, 2 * DA_QK_DIM), lambda i, h, q: (i, h, 0, 0)),
            pl.BlockSpec((1, nb, DA_V_DIM, blk), lambda i, h, q: (i, 0, h, 0)),
        ],
        out_specs=pl.BlockSpec((1, 1, blk, DA_V_DIM), lambda i, h, q: (i, h, q, 0)),
        out_shape=out_shape,
        scratch_shapes=[stat(), stat(), acc(), stat(), stat(), acc()],
        compiler_params=_params("parallel", "parallel", "parallel"),
        name="diff_attn_online",
    )
    bound = (1.01 * DA_QK_DIM ** 0.5 * math.log2(math.e)) * jnp.max(jnp.abs(da_qn)) * jnp.max(jnp.abs(da_kn))
    return lax.cond(bound <= SCORE_BOUND, bounded, online, *args)


def _gla_kernel(q_ref, k_ref, la_ref, v_ref, g_ref, gain_ref, out_ref, st_ref, *, ts, group):
    @pl.when(pl.program_id(1) == 0)
    def _():
        st_ref[...] = jnp.zeros(st_ref.shape, F32)

    c = CHUNK
    rows = group * c
    hk, hv = _GQK, _GW
    r = lax.broadcasted_iota(I32, (rows, rows), 0)
    cc = lax.broadcasted_iota(I32, (rows, rows), 1)
    tri = jnp.where(jnp.logical_and(r // c == cc // c, r >= cc), 1.0, 0.0).astype(BF16)
    bd_k = (lax.broadcasted_iota(I32, (hk, hk), 0) // GLA_K_DIM
            == lax.broadcasted_iota(I32, (hk, hk), 1) // GLA_K_DIM)
    bd_v = (lax.broadcasted_iota(I32, (hk, hv), 0) // GLA_K_DIM
            == lax.broadcasted_iota(I32, (hk, hv), 1) // GLA_V_DIM)
    pair_w = 2 * GLA_K_DIM
    low_sq = lax.broadcasted_iota(I32, (GLA_V_DIM, pair_w), 1) < GLA_K_DIM
    low_q = lax.broadcasted_iota(I32, (c, pair_w), 1) < GLA_K_DIM
    lower = (lax.broadcasted_iota(I32, (c, hk), 0)
             >= lax.broadcasted_iota(I32, (c, hk), 1) % c)

    def tiled(t, mask):
        t4 = jnp.concatenate([t] * GLA_HEADS, axis=0)
        return jnp.where(mask, t4, jnp.zeros_like(t4))

    def chunk_row(t, row):
        return jnp.concatenate([jnp.broadcast_to(t[i * c + row:i * c + row + 1, :], (c, hk)) for i in range(group)],
                               axis=0)

    def body(gi, carry):
        sl = pl.ds(pl.multiple_of(gi * rows, rows), rows)
        la_hi, la_lo = _split_bf16(la_ref[0, sl, :])
        big_l = _dot(tri, la_hi) + _dot(tri, la_lo)
        l_end = chunk_row(big_l, c - 1)
        lc = big_l - chunk_row(big_l, c // 2 - 1)
        e_pos = jnp.exp2(lc)
        e_neg = jnp.exp2(-lc)
        q = q_ref[0, sl, :]
        k = k_ref[0, sl, :]
        v = v_ref[0, sl, :]
        q_pos = (q * e_pos).astype(BF16)
        q_neg = (q * e_neg).astype(BF16)
        k_pos = (k * e_pos).astype(BF16)
        k_neg = (k * e_neg).astype(BF16)
        q_in = (q * jnp.exp2(big_l)).astype(BF16)
        k_out = (k * jnp.exp2(l_end - big_l)).astype(BF16)
        decay = jnp.exp2(l_end)

        chunks = [slice(i * c, (i + 1) * c) for i in range(group)]
        a_past = [_dot_nt(q_pos[cs], tiled(k_neg[cs], bd_k)) for cs in chunks]
        a_fut = [_dot_nt(q_neg[cs], tiled(k_pos[cs], bd_k)) for cs in chunks]
        inc = [[_dot_tn(v[cs, h * GLA_V_DIM:(h + 1) * GLA_V_DIM], k_out[cs, h // 2 * pair_w:(h // 2 + 1) * pair_w])
                for h in range(GLA_HEADS)] for cs in chunks]
        u_t = [[jnp.where(low_sq, r[2 * j], r[2 * j + 1]) for j in range(GLA_HEADS // 2)] for r in inc]
        a = [jnp.where(lower, p, f).astype(BF16) for p, f in zip(a_past, a_fut)]
        o_intra = [_dot(a[i], tiled(v[cs], bd_v)) for i, cs in enumerate(chunks)]

        st = [st_ref[j] for j in range(GLA_HEADS // 2)]
        o_inter = []
        for i in range(group):
            cs = slice(i * c, (i + 1) * c)
            st_bf = [t.astype(BF16) for t in st]
            heads = []
            for h in range(GLA_HEADS):
                j = h // 2
                qp = q_in[cs, j * pair_w:(j + 1) * pair_w]
                qh = jnp.where(low_q if h % 2 == 0 else jnp.logical_not(low_q), qp, jnp.zeros_like(qp))
                heads.append(_dot_nt(qh, st_bf[j]))
            o_inter.append(jnp.concatenate(heads, axis=-1))
            st = [st[j] * decay[i * c:i * c + 1, j * pair_w:(j + 1) * pair_w] + u_t[i][j]
                  for j in range(GLA_HEADS // 2)]
        for j in range(GLA_HEADS // 2):
            st_ref[j] = st[j]

        o = jnp.concatenate(o_intra, axis=0) + jnp.concatenate(o_inter, axis=0)
        g = g_ref[0, sl, :]
        silu = g / (1.0 + jnp.exp(-g))
        for h in range(GLA_HEADS):
            hs = slice(h * GLA_V_DIM, (h + 1) * GLA_V_DIM)
            out_ref[0, sl, hs] = (_rms(o[:, hs], gain_ref[...]) * silu[:, hs]).astype(BF16)
        return carry

    lax.fori_loop(0, ts // rows, body, 0)


def _gla(gq, gk, la, gv, gg, gla_on):
    b, s, _ = gq.shape
    ts = TS_GLA
    tile = lambda width: pl.BlockSpec((1, ts, width), lambda i, j: (i, j, 0))
    return pl.pallas_call(
        functools.partial(_gla_kernel, ts=ts, group=GLA_GROUP),
        grid=(b, s // ts),
        in_specs=[tile(_GQK), tile(_GQK), tile(_GQK), tile(_GW), tile(_GW),
                  pl.BlockSpec((1, GLA_V_DIM), lambda i, j: (0, 0))],
        out_specs=tile(_GW),
        out_shape=jax.ShapeDtypeStruct((b, s, _GW), BF16),
        scratch_shapes=[pltpu.VMEM((GLA_HEADS // 2, GLA_V_DIM, 2 * GLA_K_DIM), F32)],
        compiler_params=_params("parallel", "arbitrary"),
        name="gla",
    )(gq, gk, la, gv, gg, gla_on.reshape(1, -1))


_META_E0, _META_E1, _META_G0, _META_G1, _META_P0, _META_P1 = range(6)
_EXP_LANE0 = N_GROUPS


def _post_kernel(*refs, d, tm, sub, rider):
    if rider:
        (size_ref, pend_ref, x_ref, da_ref, gla_ref, wo_ref, gc_ref, wq_ref, qn_ref, km_ref, vm_ref, wco_ref,
         gf_ref, wr_ref, br_ref, rdest_ref, rsrc_ref,
         h_ref, xn_ref, meta_ref, ids_ref, cnt_ref, rpad_ref, zero_ref, rsem, zsem) = refs
    else:
        (x_ref, da_ref, gla_ref, wo_ref, gc_ref, wq_ref, qn_ref, km_ref, vm_ref, wco_ref,
         gf_ref, wr_ref, br_ref, h_ref, xn_ref, meta_ref, ids_ref, cnt_ref) = refs
    first = jnp.logical_and(pl.program_id(0) == 0, pl.program_id(1) == 0)

    @pl.when(first)
    def _():
        cnt_ref[...] = jnp.zeros(cnt_ref.shape, F32)
        if rider:
            _zero_fill(size_ref, pend_ref, rpad_ref, zero_ref, zsem)

    if rider:
        _scatter_rows(rdest_ref, rsrc_ref, rpad_ref, rsem)

    half = d // 2
    hd = d // CROSS_HEADS
    lane = lax.broadcasted_iota(I32, (sub, LANES), 1)
    big = jnp.int32(LANES)
    strict_lower = jnp.where(lax.broadcasted_iota(I32, (sub, sub), 0) > lax.broadcasted_iota(I32, (sub, sub), 1),
                             1.0, 0.0).astype(BF16)

    def lane_argmax(vals):
        m = jnp.max(vals, axis=-1, keepdims=True)
        idx = jnp.min(jnp.where(vals == m, lane, big), axis=-1, keepdims=True)
        return m, idx

    groups = [slice(r0, r0 + sub) for r0 in range(0, tm, sub)]
    heads = [slice(h * hd, (h + 1) * hd) for h in range(CROSS_HEADS)]
    da = [jnp.concatenate([da_ref[0, h, rs, :] for h in range(DA_HEADS)], axis=-1) for rs in groups]
    h1 = [x_ref[0, rs, :] + _dot(da[g], wo_ref[:half, :]) + _dot(gla_ref[0, rs, :], wo_ref[half:, :])
          for g, rs in enumerate(groups)]

    u = [_rms(t, gc_ref[...]).astype(BF16) for t in h1]
    q = [_dot(t, wq_ref[...]) for t in u]
    qh = [[_rms(t[:, hs], qn_ref[...]).astype(BF16) for hs in heads] for t in q]
    sc = [[_dot_nt(t[h], km_ref[0, :, hs]) for h, hs in enumerate(heads)] for t in qh]
    pr = []
    for t in sc:
        e = [jnp.exp(v - jnp.max(v, axis=-1, keepdims=True)) for v in t]
        pr.append([(v / jnp.sum(v, axis=-1, keepdims=True)).astype(BF16) for v in e])
    o = [jnp.concatenate([_dot(t[h], vm_ref[0, :, hs]) for h, hs in enumerate(heads)], axis=-1).astype(BF16)
         for t in pr]
    h2 = [h1[g] + _dot(o[g], wco_ref[...]) for g in range(len(groups))]
    for g, rs in enumerate(groups):
        h_ref[0, rs, :] = h2[g]

    xn = [_rms(t, gf_ref[...]).astype(BF16) for t in h2]
    logits = [_dot(t, wr_ref[...]) + br_ref[...] for t in xn]
    base = cnt_ref[0:1, :]
    for g, rs in enumerate(groups):
        bits = lax.bitcast_convert_type(xn[g].astype(F32), U32)
        xn_ref[0, rs, :] = (bits[:, :half] >> 16) | (bits[:, half:] & HI16)

        lg = jnp.where(lane < N_GROUPS, logits[g], NEG_INF)
        g_max, g_sel = lane_argmax(lg)
        p_g = 1.0 / jnp.sum(jnp.exp(lg - g_max), axis=-1, keepdims=True)
        e_lo = _EXP_LANE0 + g_sel * EXPERTS_PER_GROUP
        in_group = jnp.logical_and(lane >= e_lo, lane < e_lo + EXPERTS_PER_GROUP)
        le = jnp.where(in_group, logits[g], NEG_INF)
        m1, i1 = lane_argmax(le)
        m2, i2 = lane_argmax(jnp.where(lane == i1, NEG_INF, le))
        e2 = jnp.exp(m2 - m1)
        gate0 = p_g / (1.0 + e2)
        gate1 = p_g * e2 / (1.0 + e2)
        e0 = i1 - _EXP_LANE0
        e1 = i2 - _EXP_LANE0

        hot0 = lane == e0
        hot1 = lane == e1
        onehot = jnp.where(jnp.logical_or(hot0, hot1), 1.0, 0.0)
        before = _dot(strict_lower, onehot.astype(BF16)) + base
        pos0 = jnp.sum(jnp.where(hot0, before, 0.0), axis=-1, keepdims=True)
        pos1 = jnp.sum(jnp.where(hot1, before, 0.0), axis=-1, keepdims=True)

        meta = jnp.zeros(logits[g].shape, F32)
        for idx, val in ((_META_E0, e0.astype(F32)), (_META_E1, e1.astype(F32)), (_META_G0, gate0),
                         (_META_G1, gate1), (_META_P0, pos0), (_META_P1, pos1)):
            meta = jnp.where(lane == idx, val, meta)
        meta_ref[0, rs, :] = meta
        ids_ref[0, 0, :, rs] = meta.T[:ids_ref.shape[2], :]
        base = base + jnp.sum(onehot, axis=0, keepdims=True)
    cnt_ref[...] = jnp.broadcast_to(base, cnt_ref.shape)
    if rider:
        _scatter_wait(rsrc_ref, rpad_ref, rsem)


def _post(x, da, gla, w_o, norm_cross, w_cq, cross_qn, k_mem, v_mem, w_co, norm_ffn, w_group, b_group,
          w_expert, b_expert, *, half, rider=None):
    b, s, d = x.shape
    bh = b // 2
    b0 = half * bh
    tm = TM_POST
    ns = s // tm
    m = k_mem.shape[1]
    w_r = jnp.pad(jnp.concatenate([w_group, w_expert], axis=1), ((0, 0), (0, LANES - N_GROUPS - N_EXPERTS)))
    b_r = jnp.pad(jnp.concatenate([b_group, b_expert]), (0, LANES - N_GROUPS - N_EXPERTS)).reshape(1, LANES)
    const = lambda shape: pl.BlockSpec(shape, lambda i, j, *_: (0,) * len(shape))
    tile_in = lambda width: pl.BlockSpec((1, tm, width), lambda i, j, *_: (i + b0, j, 0))
    tile_out = lambda width: pl.BlockSpec((1, tm, width), lambda i, j, *_: (i, j, 0))
    per_b = lambda: pl.BlockSpec((1, m, d), lambda i, j, *_: (i + b0, 0, 0))
    da_spec = pl.BlockSpec((1, DA_HEADS, tm, DA_V_DIM), lambda i, j, *_: (i + b0, 0, j, 0))
    in_specs = [tile_in(d), da_spec, tile_in(d // 2), const((d, d)), const((1, d)), const((d, d)),
                const((1, d // CROSS_HEADS)), per_b(), per_b(), const((d, d)), const((1, d)),
                const((d, LANES)), const((1, LANES))]
    out_specs = [tile_out(d), tile_out(d // 2), tile_out(LANES),
                 pl.BlockSpec((1, 1, 8, tm), lambda i, j, *_: (i, j, 0, 0)), const((8, LANES))]
    out_shape = [
        jax.ShapeDtypeStruct((bh, s, d), F32),
        jax.ShapeDtypeStruct((bh, s, d // 2), U32),
        jax.ShapeDtypeStruct((bh, s, LANES), F32),
        jax.ShapeDtypeStruct((bh, ns, 8, tm), F32),
        jax.ShapeDtypeStruct((8, LANES), F32),
    ]
    args = (x, da, gla, w_o.astype(BF16), norm_cross.reshape(1, d), w_cq.astype(BF16), cross_qn.reshape(1, -1),
            k_mem, v_mem, w_co.astype(BF16), norm_ffn.reshape(1, d), w_r.astype(BF16), b_r)
    scalars, scratch = (), []
    if rider is not None:
        sizes, pend, dest, xn_src, n_rows = rider
        scalars = (sizes, pend)
        in_specs += [pl.BlockSpec((1, 2, tm), lambda i, j, *_: (i * ns + j, 0, 0), memory_space=pltpu.SMEM),
                     pl.BlockSpec((tm, d // 2), lambda i, j, *_: (i * ns + j, 0))]
        out_specs.append(pl.BlockSpec(memory_space=pl.ANY))
        out_shape.append(jax.ShapeDtypeStruct((n_rows, d // 2), U32))
        scratch = [pltpu.VMEM((EXPERT_ROWS, d // 2), U32), pltpu.SemaphoreType.DMA, pltpu.SemaphoreType.DMA]
        args += (dest, xn_src)
    return pl.pallas_call(
        functools.partial(_post_kernel, d=d, tm=tm, sub=SUB_POST, rider=rider is not None),
        grid_spec=pltpu.PrefetchScalarGridSpec(
            num_scalar_prefetch=len(scalars), grid=(bh, ns), in_specs=in_specs, out_specs=out_specs,
            scratch_shapes=scratch),
        out_shape=out_shape,
        compiler_params=_params("arbitrary", "arbitrary"),
        name="post_scatter" if rider is not None else "post",
    )(*scalars, *args)


def _row_copy(src_ref, src_row, dst_ref, dst_row, sem):
    return pltpu.make_async_copy(src_ref.at[pl.ds(src_row, 1)], dst_ref.at[pl.ds(dst_row, 1)], sem)


def _zero_fill(size_ref, pend_ref, xpad_ref, zero_ref, zsem):
    zero_ref[...] = jnp.zeros(zero_ref.shape, zero_ref.dtype)
    rows = zero_ref.shape[0]
    n_blocks = xpad_ref.shape[0] // rows
    n_used = pend_ref[N_EXPERTS - 1] // rows

    def zero_block(blk):
        return pltpu.make_async_copy(zero_ref, xpad_ref.at[pl.ds(pl.multiple_of(blk * rows, rows), rows)], zsem)

    def last_block(e, fn):
        @pl.when(size_ref[e] > 0)
        def _():
            fn(zero_block(pend_ref[e] // rows - 1))

    for fn in (lambda cp: cp.start(), lambda cp: cp.wait()):
        lax.fori_loop(0, N_EXPERTS, lambda e, c: (last_block(e, fn), c)[1], 0)
        lax.fori_loop(n_used, n_blocks, lambda blk, c: (fn(zero_block(blk)), c)[1], 0)


def _scatter_rows(dest_ref, src_ref, xpad_ref, sem):
    for t in range(src_ref.shape[0]):
        for k in range(2):
            _row_copy(src_ref, t, xpad_ref, dest_ref[0, k, t], sem).start(priority=k)


def _scatter_wait(src_ref, xpad_ref, sem):
    for _ in range(2):
        pltpu.make_async_copy(src_ref, xpad_ref.at[pl.ds(0, src_ref.shape[0])], sem).wait()


def _cast_expert_weights(fresh, wg_ref, wu_ref, wd_ref, wg_s, wu_s, wd_s):
    @pl.when(fresh)
    def _():
        wg_s[...] = wg_ref[0].astype(BF16)
        wu_s[...] = wu_ref[0].astype(BF16)
        wd_s[...] = wd_ref[0].astype(BF16)


def _expert_mlp(x_ref, wg_s, wu_s, wd_s, out_ref):
    words = x_ref[...]
    half = words.shape[1]
    lo = lax.bitcast_convert_type(words << 16, F32).astype(BF16)
    hi = lax.bitcast_convert_type(words & HI16, F32).astype(BF16)
    gate = _dot(lo, wg_s[:half, :]) + _dot(hi, wg_s[half:, :])
    up = _dot(lo, wu_s[:half, :]) + _dot(hi, wu_s[half:, :])
    hid = gate / (1.0 + jnp.exp(-gate)) * up
    out_ref[...] = _dot(hid.astype(BF16), wd_s[...])


def _experts_scatter_kernel(be_ref, nused_ref, size_ref, pend_ref, x_ref, wg_ref, wu_ref, wd_ref, rdest_ref, rsrc_ref,
                            out_ref, rpad_ref, wg_s, wu_s, wd_s, zero_ref, rsem, zsem, *, rider_steps):
    i = pl.program_id(0)
    used = i < nused_ref[0]
    new_expert = jnp.logical_or(i == 0, be_ref[i] != be_ref[jnp.maximum(i - 1, 0)])
    mlp = functools.partial(_expert_mlp, x_ref, wg_s, wu_s, wd_s, out_ref)

    @pl.when(i == 0)
    def _():
        _zero_fill(size_ref, pend_ref, rpad_ref, zero_ref, zsem)

    _cast_expert_weights(jnp.logical_and(used, new_expert), wg_ref, wu_ref, wd_ref, wg_s, wu_s, wd_s)
    riding = i < rider_steps

    @pl.when(riding)
    def _():
        _scatter_rows(rdest_ref, rsrc_ref, rpad_ref, rsem)
        mlp()
        _scatter_wait(rsrc_ref, rpad_ref, rsem)

    pl.when(jnp.logical_and(used, jnp.logical_not(riding)))(mlp)

    @pl.when(jnp.logical_not(used))
    def _():
        out_ref[...] = jnp.zeros(out_ref.shape, F32)


def _gather_rows(dest_ref, src_ref, buf_ref, sem):
    for t in range(buf_ref.shape[1]):
        for k in range(2):
            _row_copy(src_ref, dest_ref[0, k, t], buf_ref.at[k], t, sem).start(priority=k)


def _combine_rows(h_ref, meta_ref, src_ref, buf_ref, sem):
    for k in range(2):
        pltpu.make_async_copy(src_ref.at[pl.ds(0, buf_ref.shape[1])], buf_ref.at[k], sem).wait()
    meta = meta_ref[...]
    g0 = meta[:, _META_G0:_META_G0 + 1]
    g1 = meta[:, _META_G1:_META_G1 + 1]
    return h_ref[...] + g0 * buf_ref[0] + g1 * buf_ref[1]


def _experts_gather_kernel(be_ref, nused_ref, x_ref, wg_ref, wu_ref, wd_ref, gdest_ref, h_ref, meta_ref, opad_ref,
                           out_ref, y_ref, wg_s, wu_s, wd_s, gbuf_ref, gsem, *, rider_steps):
    i = pl.program_id(0)
    used = i < nused_ref[0]
    new_expert = jnp.logical_or(i == 0, be_ref[i] != be_ref[jnp.maximum(i - 1, 0)])
    mlp = functools.partial(_expert_mlp, x_ref, wg_s, wu_s, wd_s, out_ref)
    _cast_expert_weights(jnp.logical_and(used, new_expert), wg_ref, wu_ref, wd_ref, wg_s, wu_s, wd_s)

    def gather():
        _gather_rows(gdest_ref, opad_ref, gbuf_ref.at[i % 2], gsem.at[i % 2])

    def combine():
        slot = (i - 1) % 2
        y_ref[...] = _combine_rows(h_ref, meta_ref, opad_ref, gbuf_ref.at[slot], gsem.at[slot])

    @pl.when(i == 0)
    def _():
        gather()
        mlp()

    @pl.when(jnp.logical_and(i > 0, i < rider_steps))
    def _():
        gather()
        mlp()
        combine()

    @pl.when(i == rider_steps)
    def _():
        combine()
        pl.when(used)(mlp)

    pl.when(jnp.logical_and(i > rider_steps, used))(mlp)

    @pl.when(jnp.logical_not(used))
    def _():
        out_ref[...] = jnp.zeros(out_ref.shape, F32)


def _experts_kernel(be_ref, nused_ref, x_ref, wg_ref, wu_ref, wd_ref, out_ref, wg_s, wu_s, wd_s):
    i = pl.program_id(0)
    used = i < nused_ref[0]
    new_expert = jnp.logical_or(i == 0, be_ref[i] != be_ref[jnp.maximum(i - 1, 0)])
    _cast_expert_weights(jnp.logical_and(used, new_expert), wg_ref, wu_ref, wd_ref, wg_s, wu_s, wd_s)
    pl.when(used)(functools.partial(_expert_mlp, x_ref, wg_s, wu_s, wd_s, out_ref))

    @pl.when(jnp.logical_not(used))
    def _():
        out_ref[...] = jnp.zeros(out_ref.shape, F32)


def _retile(dest, tr):
    tiles, _, tm = dest.shape
    return dest.reshape(tiles, 2, tm // tr, tr).transpose(0, 2, 1, 3).reshape(tiles * (tm // tr), 2, tr)


def _experts(x_pad, block_expert, n_used, w_gate, w_up, w_down, scatter=None, gather=None):
    n_rows = x_pad.shape[0]
    _, d, f = w_gate.shape
    rows, tr = EXPERT_ROWS, RIDER_ROWS
    row_blk = lambda i, be, nu, *_: (jnp.minimum(i, nu[0] - 1), 0)
    weights = lambda shape: pl.BlockSpec(shape, lambda i, be, *_: (be[i], 0, 0))
    in_specs = [pl.BlockSpec((rows, d // 2), row_blk), weights((1, d, f)), weights((1, d, f)), weights((1, f, d))]
    out_specs = [pl.BlockSpec((rows, d), lambda i, *_: (i, 0))]
    out_shape = [jax.ShapeDtypeStruct((n_rows, d), F32)]
    scratch = [pltpu.VMEM((d, f), BF16), pltpu.VMEM((d, f), BF16), pltpu.VMEM((f, d), BF16)]
    scalars, args = (block_expert, n_used), (x_pad, w_gate, w_up, w_down)
    body, name = _experts_kernel, "experts"
    if scatter is not None or gather is not None:
        n_tokens = (scatter[3] if scatter is not None else gather[1]).shape[0]
        steps = n_tokens // tr
        assert 2 * n_tokens >= steps * rows, "riding steps must all be used expert blocks"
        tile = lambda width, shift: pl.BlockSpec((tr, width), lambda i, *_: (jnp.clip(i - shift, 0, steps - 1), 0))
        ids = lambda: pl.BlockSpec((1, 2, tr), lambda i, *_: (jnp.minimum(i, steps - 1), 0, 0),
                                   memory_space=pltpu.SMEM)
    if scatter is not None:
        sizes, pend, dest, xn_src, other_rows = scatter
        scalars += (sizes, pend)
        in_specs += [ids(), tile(d // 2, 0)]
        out_specs.append(pl.BlockSpec(memory_space=pl.ANY))
        out_shape.append(jax.ShapeDtypeStruct((other_rows, d // 2), U32))
        scratch += [pltpu.VMEM((rows, d // 2), U32), pltpu.SemaphoreType.DMA, pltpu.SemaphoreType.DMA]
        args += (_retile(dest, tr), xn_src)
        body, name = functools.partial(_experts_scatter_kernel, rider_steps=steps), "experts_scatter"
    elif gather is not None:
        dest, h2, meta, opad = gather
        in_specs += [ids(), tile(d, 1), tile(LANES, 1), pl.BlockSpec(memory_space=pl.ANY)]
        out_specs.append(tile(d, 1))
        out_shape.append(jax.ShapeDtypeStruct((n_tokens, d), F32))
        scratch += [pltpu.VMEM((2, 2, tr, d), F32), pltpu.SemaphoreType.DMA((2,))]
        args += (_retile(dest, tr), h2, meta, opad)
        body, name = functools.partial(_experts_gather_kernel, rider_steps=steps), "experts_gather"
    out = pl.pallas_call(
        body,
        grid_spec=pltpu.PrefetchScalarGridSpec(
            num_scalar_prefetch=len(scalars), grid=(n_rows // rows,), in_specs=in_specs, out_specs=out_specs,
            scratch_shapes=scratch),
        out_shape=out_shape,
        compiler_params=_params("arbitrary"),
        name=name,
    )(*scalars, *args)
    return out if len(out) > 1 else out[0]


def _combine_tail_kernel(dest_ref, y0_ref, h_ref, meta_ref, opad_ref, y_ref, buf_ref, sem, *, nt):
    s = pl.program_id(0)

    @pl.when(s < nt)
    def _():
        _gather_rows(dest_ref, opad_ref, buf_ref.at[s % 2], sem.at[s % 2])

    @pl.when(s > 0)
    def _():
        slot = (s - 1) % 2
        y_ref[0] = y0_ref[...]
        y_ref[1] = _combine_rows(h_ref, meta_ref, opad_ref, buf_ref.at[slot], sem.at[slot])


def _combine_tail(y0, h2, meta, dest, out_pad):
    th, d = h2.shape
    tm = TM_TAIL
    nt = th // tm
    prev = lambda s: (jnp.maximum(s - 1, 0), 0)
    return pl.pallas_call(
        functools.partial(_combine_tail_kernel, nt=nt),
        grid=(nt + 1,),
        in_specs=[
            pl.BlockSpec((1, 2, tm), lambda s: (jnp.minimum(s, nt - 1), 0, 0), memory_space=pltpu.SMEM),
            pl.BlockSpec((tm, d), prev), pl.BlockSpec((tm, d), prev), pl.BlockSpec((tm, LANES), prev),
            pl.BlockSpec(memory_space=pl.ANY),
        ],
        out_specs=pl.BlockSpec((2, tm, d), lambda s: (0, jnp.maximum(s - 1, 0), 0)),
        scratch_shapes=[pltpu.VMEM((2, 2, tm, d), F32), pltpu.SemaphoreType.DMA((2,))],
        out_shape=jax.ShapeDtypeStruct((2, th, d), F32),
        compiler_params=_params("arbitrary"),
        name="combine_tail",
    )(_retile(dest, tm), y0, h2, meta, out_pad)


def _route_tables(counts, ids, n_tokens):
    rows = EXPERT_ROWS
    sizes = counts[0, :N_EXPERTS].astype(I32)
    padded = (sizes + rows - 1) // rows * rows
    pend = jnp.cumsum(padded)
    pstart = pend - padded
    n_rows = 2 * n_tokens + N_EXPERTS * rows
    block_start = jnp.arange(n_rows // rows, dtype=I32) * rows
    block_expert = jnp.minimum(jnp.sum(pend[None, :] <= block_start[:, None], axis=1), N_EXPERTS - 1).astype(I32)
    n_used = (pend[-1:] // rows).astype(I32)
    expert = ids[:, _META_E0:_META_E1 + 1].astype(I32)
    rank = ids[:, _META_P0:_META_P1 + 1].astype(I32)
    experts = jnp.arange(N_EXPERTS, dtype=I32).reshape(-1, 1, 1, 1)
    dest = jnp.sum(jnp.where(expert[None] == experts, pstart.reshape(-1, 1, 1, 1), 0), axis=0) + rank
    return sizes, pend, block_expert, n_used, dest, n_rows


def kernel(x, mem, norm_mix, w_in, da_q_norm, da_k_norm, lambda_q1, lambda_k1, lambda_q2, lambda_k2,
           da_out_norm, gla_gate_w, gla_gate_b, gla_out_norm, w_o, norm_cross, norm_mem, w_cq, w_ckv,
           cross_q_norm, cross_k_norm, w_co, norm_ffn, w_group, b_group, w_expert, b_expert,
           w_e_gate, w_e_up, w_e_down):
    b, s, d = x.shape
    th = b // 2 * s
    h = x
    for l in range(norm_mix.shape[0]):
        assert l == 0, "lam_init is fixed for a single layer"
        qt, kda, vt, gq, gk, gv, gg, la = _in_proj(h, norm_mix[l], w_in[l], da_q_norm[l], da_k_norm[l],
                                                   gla_gate_w[l], gla_gate_b[l])
        da = _diff_attn(qt, kda, vt, lambda_q1[l], lambda_k1[l], lambda_q2[l], lambda_k2[l], da_out_norm[l],
                        da_q_norm[l], da_k_norm[l])
        gla = _gla(gq, gk, la, gv, gg, gla_out_norm[l])
        k_mem, v_mem = _mem_kv(mem, norm_mem[l], w_ckv[l], cross_k_norm[l])
        post = functools.partial(_post, h, da, gla, w_o[l], norm_cross[l], w_cq[l], cross_q_norm[l], k_mem, v_mem,
                                 w_co[l], norm_ffn[l], w_group[l], b_group[l], w_expert[l], b_expert[l])
        experts = functools.partial(_experts, w_gate=w_e_gate[l], w_up=w_e_up[l], w_down=w_e_down[l])

        h2_0, xn_0, meta_0, ids_0, counts_0 = post(half=0)
        sizes_0, pend_0, be_0, used_0, dest_0, n_rows = _route_tables(counts_0, ids_0.reshape(-1, 8, TM_POST), th)
        h2_1, xn_1, meta_1, ids_1, counts_1, xpad_0 = post(
            half=1, rider=(sizes_0, pend_0, dest_0, xn_0.reshape(th, d // 2), n_rows))
        sizes_1, pend_1, be_1, used_1, dest_1, _ = _route_tables(counts_1, ids_1.reshape(-1, 8, TM_POST), th)
        opad_0, xpad_1 = experts(xpad_0, be_0, used_0,
                                 scatter=(sizes_1, pend_1, dest_1, xn_1.reshape(th, d // 2), n_rows))
        opad_1, y_0 = experts(xpad_1, be_1, used_1,
                              gather=(dest_0, h2_0.reshape(th, d), meta_0.reshape(th, LANES), opad_0))
        h = _combine_tail(y_0, h2_1.reshape(th, d), meta_1.reshape(th, LANES), dest_1, opad_1).reshape(b, s, d)
    return h
```

```python
import functools
import math

import jax
import jax.numpy as jnp
import numpy as np
from jax import lax
from jax.experimental import pallas as pl
from jax.experimental.pallas import tpu as pltpu

F32 = jnp.float32
BF16 = jnp.bfloat16
I32 = jnp.int32
U32 = jnp.uint32
HI16 = np.uint32(0xFFFF0000)

EPS = 1e-6
CHUNK = 64

DA_HEADS = 4
DA_QK_DIM = 64
DA_V_DIM = 128
GLA_HEADS = 4
GLA_K_DIM = 64
GLA_V_DIM = 128
GLA_GATE_RANK = 16
GLA_TAU = 16.0
CROSS_HEADS = 4
N_GROUPS = 4
EXPERTS_PER_GROUP = 8
N_EXPERTS = N_GROUPS * EXPERTS_PER_GROUP
LAM_INIT = 0.8 - 0.6 * math.exp(-0.3 * 0)

LANES = 128
VMEM_LIMIT = 56 * 1024 * 1024

TM_PROJ = 1024
ATT_BLK = 512
TS_GLA = 1024
GLA_GROUP = 4
TM_POST = 1024
SUB_POST = 256
TM_TAIL = 512
RIDER_ROWS = 256
EXPERT_ROWS = 512

NEG_INF = float("-inf")


def _params(*sem):
    return pltpu.CompilerParams(dimension_semantics=sem, vmem_limit_bytes=VMEM_LIMIT)


def _rms(t, g):
    ms = jnp.mean(t * t, axis=-1, keepdims=True)
    return t * lax.rsqrt(ms + EPS) * g


def _dot(a, b):
    return jnp.dot(a, b, preferred_element_type=F32)


def _dot_nt(a, b):
    return lax.dot_general(a, b, (((1,), (1,)), ((), ())), preferred_element_type=F32)


def _dot_tn(a, b):
    return lax.dot_general(a, b, (((0,), (0,)), ((), ())), preferred_element_type=F32)


def _split_bf16(t):
    hi = t.astype(BF16)
    lo = (t - hi.astype(F32)).astype(BF16)
    return hi, lo


def _mem_kv_kernel(mem_ref, g_ref, w_ref, kn_ref, k_ref, v_ref, *, d, heads):
    mn = _rms(mem_ref[0], g_ref[...]).astype(BF16)
    kv = _dot(mn, w_ref[...])
    hd = d // heads
    scale = hd ** -0.5
    for h in range(heads):
        kh = _rms(kv[:, h * hd:(h + 1) * hd], kn_ref[...]) * scale
        k_ref[0, :, h * hd:(h + 1) * hd] = kh.astype(BF16)
    v_ref[0] = kv[:, d:].astype(BF16)


def _mem_kv(mem, norm_m, w_ckv, kn):
    b, m, d = mem.shape
    return pl.pallas_call(
        functools.partial(_mem_kv_kernel, d=d, heads=CROSS_HEADS),
        grid=(b,),
        in_specs=[
            pl.BlockSpec((1, m, d), lambda i: (i, 0, 0)),
            pl.BlockSpec((1, d), lambda i: (0, 0)),
            pl.BlockSpec((d, 2 * d), lambda i: (0, 0)),
            pl.BlockSpec((1, d // CROSS_HEADS), lambda i: (0, 0)),
        ],
        out_specs=[
            pl.BlockSpec((1, m, d), lambda i: (i, 0, 0)),
            pl.BlockSpec((1, m, d), lambda i: (i, 0, 0)),
        ],
        out_shape=[jax.ShapeDtypeStruct((b, m, d), BF16)] * 2,
        compiler_params=_params("parallel"),
        name="mem_kv",
    )(mem, norm_m.reshape(1, d), w_ckv.astype(BF16), kn.reshape(1, -1))


_QK = DA_HEADS * 2 * DA_QK_DIM
_DAW = DA_HEADS * DA_V_DIM
_GQK = GLA_HEADS * GLA_K_DIM
_GW = GLA_HEADS * GLA_V_DIM
_OFF_DQ = 0
_OFF_DK = _OFF_DQ + _QK
_OFF_DV = _OFF_DK + _QK
_OFF_GQ = _OFF_DV + _DAW
_OFF_GK = _OFF_GQ + _GQK
_OFF_GV = _OFF_GK + _GQK
_OFF_GG = _OFF_GV + _GW
_OFF_GR = _OFF_GG + _GW
_IN_PAD = _OFF_GR + LANES


def _in_proj_kernel(x_ref, g_ref, w_ref, qg_ref, kg_ref, grp_ref, gw_ref, gb_ref,
                    qt_ref, k_ref, vt_ref, gq_ref, gk_ref, gv_ref, gg_ref, la_ref):
    u = _rms(x_ref[0], g_ref[...]).astype(BF16)

    def proj(off, width):
        return _dot(u, w_ref[:, off:off + width])

    def mean_square(p):
        return _dot((p * p).astype(BF16), grp_ref[...])

    p_q = proj(_OFF_DQ, _QK)
    p_k = proj(_OFF_DK, _QK)
    ms_q = mean_square(p_q)
    qn = p_q * lax.rsqrt(ms_q + EPS) * qg_ref[...] * (DA_QK_DIM ** -0.5 * math.log2(math.e))
    blk = qt_ref.shape[3]
    for t in range(qt_ref.shape[1]):
        qt_ref[0, t] = qn[t * blk:(t + 1) * blk].T.astype(BF16)
    dv = proj(_OFF_DV, _DAW)
    ms_k = mean_square(p_k)
    kn = (p_k * lax.rsqrt(ms_k + EPS) * kg_ref[...]).astype(BF16)
    for h in range(DA_HEADS):
        k_ref[0, h] = kn[:, h * 2 * DA_QK_DIM:(h + 1) * 2 * DA_QK_DIM]
    for t in range(vt_ref.shape[1]):
        vt_ref[0, t] = dv[t * blk:(t + 1) * blk].T.astype(BF16)
    g_r = proj(_OFF_GR, LANES)
    gq_ref[0] = proj(_OFF_GQ, _GQK) * (GLA_K_DIM ** -0.5)
    gk_ref[0] = proj(_OFF_GK, _GQK)
    z = _dot(g_r.astype(BF16), gw_ref[...]) + gb_ref[...]
    gv_ref[0] = proj(_OFF_GV, _GW).astype(BF16)
    gg_ref[0] = proj(_OFF_GG, _GW)
    log_sig = jnp.minimum(z, 0.0) - jnp.log(1.0 + jnp.exp(-jnp.abs(z)))
    la_ref[0] = log_sig * (math.log2(math.e) / GLA_TAU)


def _in_proj(x, norm_g, w_in, da_qn, da_kn, gate_w, gate_b):
    b, s, d = x.shape
    tm, blk = TM_PROJ, ATT_BLK
    ns, nb = s // tm, tm // blk
    w = jnp.pad(w_in, ((0, 0), (0, _IN_PAD - w_in.shape[1]))).astype(BF16)
    gw = jnp.pad(gate_w, ((0, LANES - GLA_GATE_RANK), (0, 0))).astype(BF16)
    lane = jnp.arange(_QK)
    grp = jnp.where((lane[:, None] // DA_QK_DIM) == (lane[None, :] // DA_QK_DIM),
                    1.0 / DA_QK_DIM, 0.0).astype(BF16)
    const = lambda shape: pl.BlockSpec(shape, lambda i, j: (0,) * len(shape))
    tile = lambda width: pl.BlockSpec((1, tm, width), lambda i, j: (i, j, 0))
    tile_t = lambda width: pl.BlockSpec((1, nb, width, blk), lambda i, j: (i, j, 0, 0))
    return pl.pallas_call(
        _in_proj_kernel,
        grid=(b, ns),
        in_specs=[tile(d), const((1, d)), const((d, _IN_PAD)), const((1, _QK)), const((1, _QK)),
                  const((_QK, _QK)), const((LANES, _GQK)), const((1, _GQK))],
        out_specs=[tile_t(_QK), pl.BlockSpec((1, DA_HEADS, tm, 2 * DA_QK_DIM), lambda i, j: (i, 0, j, 0)),
                   tile_t(_DAW), tile(_GQK), tile(_GQK), tile(_GW), tile(_GW), tile(_GQK)],
        out_shape=[
            jax.ShapeDtypeStruct((b, s // blk, _QK, blk), BF16),
            jax.ShapeDtypeStruct((b, DA_HEADS, s, 2 * DA_QK_DIM), BF16),
            jax.ShapeDtypeStruct((b, s // blk, _DAW, blk), BF16),
            jax.ShapeDtypeStruct((b, s, _GQK), F32),
            jax.ShapeDtypeStruct((b, s, _GQK), F32),
            jax.ShapeDtypeStruct((b, s, _GW), BF16),
            jax.ShapeDtypeStruct((b, s, _GW), F32),
            jax.ShapeDtypeStruct((b, s, _GQK), F32),
        ],
        compiler_params=_params("parallel", "parallel"),
        name="in_proj",
    )(x, norm_g.reshape(1, d), w, jnp.tile(da_qn, 2 * DA_HEADS).reshape(1, _QK),
      jnp.tile(da_kn, 2 * DA_HEADS).reshape(1, _QK), grp, gw, gate_b.reshape(1, _GQK))


def _split_q(qt):
    row = lax.broadcasted_iota(I32, qt.shape, 0)
    zero = jnp.zeros_like(qt)
    return jnp.where(row < DA_QK_DIM, qt, zero), jnp.where(row >= DA_QK_DIM, qt, zero)


def _chunk_causal_mask(blk):
    key_chunk = lax.broadcasted_iota(I32, (blk, blk), 0) // CHUNK
    qry_chunk = lax.broadcasted_iota(I32, (blk, blk), 1) // CHUNK
    return key_chunk <= qry_chunk


def _diff_attn_finish(lq1_ref, lk1_ref, lq2_ref, lk2_ref, gain_ref, a1, l1, a2, l2):
    lam = (jnp.exp(jnp.sum(lq1_ref[...] * lk1_ref[...], axis=-1, keepdims=True))
           - jnp.exp(jnp.sum(lq2_ref[...] * lk2_ref[...], axis=-1, keepdims=True)) + LAM_INIT)
    o = a1 / l1 - lam * (a2 / l2)
    ms = jnp.mean(o * o, axis=0, keepdims=True)
    o = o * lax.rsqrt(ms + EPS) * gain_ref[...] * (1.0 - LAM_INIT)
    return o.T.astype(BF16)


def _diff_attn_bounded_kernel(lq1_ref, lk1_ref, lq2_ref, lk2_ref, gain_ref, qt_ref, k_ref, vt_ref, out_ref,
                              s_ref, l1_ref, a1_ref, l2_ref, a2_ref, *, blk, nb):
    stats = ((l1_ref, a1_ref), (l2_ref, a2_ref))
    mask = _chunk_causal_mask(blk)

    def reset():
        for l_ref, a_ref in stats:
            l_ref[...] = jnp.zeros(l_ref.shape, F32)
            a_ref[...] = jnp.zeros(a_ref.shape, F32)

    def scores(q, j, slot):
        kb = k_ref[0, 0, pl.ds(pl.multiple_of(j * blk, blk), blk), :]
        s_ref[slot, 0] = _dot(kb, q[0])
        s_ref[slot, 1] = _dot(kb, q[1])

    def consume(j, slot, masked):
        vb = vt_ref[0, j]
        for m, (l_ref, a_ref) in enumerate(stats):
            s = s_ref[slot, m]
            if masked:
                s = jnp.where(mask, s, NEG_INF)
            p = jnp.exp2(s)
            l_ref[...] += jnp.sum(p, axis=0, keepdims=True)
            a_ref[...] += _dot(vb, p.astype(BF16))

    def step(q, j, slot):
        scores(q, j + 1, 1 - slot)
        consume(j, slot, False)

    reset()
    q = _split_q(qt_ref[0, 0])
    slot = 0
    scores(q, 0, slot)
    for qi in range(nb):
        def pair(i, carry, q=q, slot=slot):
            step(q, 2 * i, slot)
            step(q, 2 * i + 1, 1 - slot)
            return carry

        if qi // 2:
            lax.fori_loop(0, qi // 2, pair, 0)
        if qi % 2:
            step(q, qi - 1, slot)
            slot = 1 - slot
        if qi + 1 < nb:
            q = _split_q(qt_ref[0, qi + 1])
            scores(q, 0, 1 - slot)
        consume(qi, slot, True)
        out_ref[0, 0, qi * blk:(qi + 1) * blk, :] = _diff_attn_finish(
            lq1_ref, lk1_ref, lq2_ref, lk2_ref, gain_ref, a1_ref[...], l1_ref[...], a2_ref[...], l2_ref[...])
        if qi + 1 < nb:
            reset()
        slot = 1 - slot


def _diff_attn_online_kernel(lq1_ref, lk1_ref, lq2_ref, lk2_ref, gain_ref, qt_ref, k_ref, vt_ref, out_ref,
                             m1_ref, l1_ref, a1_ref, m2_ref, l2_ref, a2_ref, *, blk):
    qi = pl.program_id(2)
    q1, q2 = _split_q(qt_ref[0, 0])

    for m_ref, l_ref, a_ref in ((m1_ref, l1_ref, a1_ref), (m2_ref, l2_ref, a2_ref)):
        m_ref[...] = jnp.full(m_ref.shape, NEG_INF, F32)
        l_ref[...] = jnp.zeros(l_ref.shape, F32)
        a_ref[...] = jnp.zeros(a_ref.shape, F32)

    def update(s, vb, m_ref, l_ref, a_ref):
        m_old = m_ref[...]
        m_new = jnp.maximum(m_old, jnp.max(s, axis=0, keepdims=True))
        alpha = jnp.exp2(m_old - m_new)
        p = jnp.exp2(s - m_new)
        l_ref[...] = alpha * l_ref[...] + jnp.sum(p, axis=0, keepdims=True)
        a_ref[...] = alpha * a_ref[...] + _dot(vb, p.astype(BF16))
        m_ref[...] = m_new

    def block(j, mask):
        kb = k_ref[0, 0, pl.ds(pl.multiple_of(j * blk, blk), blk), :]
        vb = vt_ref[0, j]
        s1 = _dot(kb, q1)
        s2 = _dot(kb, q2)
        if mask is not None:
            s1 = jnp.where(mask, s1, NEG_INF)
            s2 = jnp.where(mask, s2, NEG_INF)
        update(s1, vb, m1_ref, l1_ref, a1_ref)
        update(s2, vb, m2_ref, l2_ref, a2_ref)

    def body(j, carry):
        block(j, None)
        return carry

    lax.fori_loop(0, qi, body, 0)
    block(qi, _chunk_causal_mask(blk))
    out_ref[0, 0] = _diff_attn_finish(lq1_ref, lk1_ref, lq2_ref, lk2_ref, gain_ref,
                                      a1_ref[...], l1_ref[...], a2_ref[...], l2_ref[...])


SCORE_BOUND = 60.0


def _diff_attn(qt, k, vt, lq1, lk1, lq2, lk2, da_on, da_qn, da_kn):
    b, nb, _, blk = qt.shape
    s = nb * blk
    stat = lambda: pltpu.VMEM((1, blk), F32)
    acc = lambda: pltpu.VMEM((DA_V_DIM, blk), F32)

    args = (lq1.reshape(1, -1), lk1.reshape(1, -1), lq2.reshape(1, -1), lk2.reshape(1, -1),
            da_on.reshape(-1, 1), qt, k, vt)
    out_shape = jax.ShapeDtypeStruct((b, DA_HEADS, s, DA_V_DIM), BF16)
    head = lambda *trailing: (lambda i, h: (i, 0, h) + trailing)
    vec2 = lambda: pl.BlockSpec((1, DA_QK_DIM), lambda i, h: (0, 0))
    bounded = pl.pallas_call(
        functools.partial(_diff_attn_bounded_kernel, blk=blk, nb=nb),
        grid=(b, DA_HEADS),
        in_specs=[
            vec2(), vec2(), vec2(), vec2(),
            pl.BlockSpec((DA_V_DIM, 1), lambda i, h: (0, 0)),
            pl.BlockSpec((1, nb, 2 * DA_QK_DIM, blk), head(0)),
            pl.BlockSpec((1, 1, s, 2 * DA_QK_DIM), lambda i, h: (i, h, 0, 0)),
            pl.BlockSpec((1, nb, DA_V_DIM, blk), head(0)),
        ],
        out_specs=pl.BlockSpec((1, 1, s, DA_V_DIM), lambda i, h: (i, h, 0, 0)),
        out_shape=out_shape,
        scratch_shapes=[pltpu.VMEM((2, 2, blk, blk), F32), stat(), acc(), stat(), acc()],
        compiler_params=_params("parallel", "parallel"),
        name="diff_attn",
    )
    vec3 = lambda: pl.BlockSpec((1, DA_QK_DIM), lambda i, h, q: (0, 0))
    online = pl.pallas_call(
        functools.partial(_diff_attn_online_kernel, blk=blk),
        grid=(b, DA_HEADS, nb),
        in_specs=[
            vec3(), vec3(), vec3(), vec3(),
            pl.BlockSpec((DA_V_DIM, 1), lambda i, h, q: (0, 0)),
            pl.BlockSpec((1, 1, 2 * DA_QK_DIM, blk), lambda i, h, q: (i, q, h, 0)),
            pl.BlockSpec((1, 1, s, 2 * DA_QK_DIM), lambda i, h, q: (i, h, 0, 0)),
            pl.BlockSpec((1, nb, DA_V_DIM, blk), lambda i, h, q: (i, 0, h, 0)),
        ],
        out_specs=pl.BlockSpec((1, 1, blk, DA_V_DIM), lambda i, h, q: (i, h, q, 0)),
        out_shape=out_shape,
        scratch_shapes=[stat(), stat(), acc(), stat(), stat(), acc()],
        compiler_params=_params("parallel", "parallel", "parallel"),
        name="diff_attn_online",
    )
    bound = (1.01 * DA_QK_DIM ** 0.5 * math.log2(math.e)) * jnp.max(jnp.abs(da_qn)) * jnp.max(jnp.abs(da_kn))
    return lax.cond(bound <= SCORE_BOUND, bounded, online, *args)


def _gla_kernel(q_ref, k_ref, la_ref, v_ref, g_ref, gain_ref, out_ref, st_ref, *, ts, group):
    @pl.when(pl.program_id(1) == 0)
    def _():
        st_ref[...] = jnp.zeros(st_ref.shape, F32)

    c = CHUNK
    rows = group * c
    hk, hv = _GQK, _GW
    r = lax.broadcasted_iota(I32, (rows, rows), 0)
    cc = lax.broadcasted_iota(I32, (rows, rows), 1)
    tri = jnp.where(jnp.logical_and(r // c == cc // c, r >= cc), 1.0, 0.0).astype(BF16)
    bd_k = (lax.broadcasted_iota(I32, (hk, hk), 0) // GLA_K_DIM
            == lax.broadcasted_iota(I32, (hk, hk), 1) // GLA_K_DIM)
    bd_v = (lax.broadcasted_iota(I32, (hk, hv), 0) // GLA_K_DIM
            == lax.broadcasted_iota(I32, (hk, hv), 1) // GLA_V_DIM)
    pair_w = 2 * GLA_K_DIM
    low_sq = lax.broadcasted_iota(I32, (GLA_V_DIM, pair_w), 1) < GLA_K_DIM
    low_q = lax.broadcasted_iota(I32, (c, pair_w), 1) < GLA_K_DIM
    lower = (lax.broadcasted_iota(I32, (c, hk), 0)
             >= lax.broadcasted_iota(I32, (c, hk), 1) % c)

    def tiled(t, mask):
        t4 = jnp.concatenate([t] * GLA_HEADS, axis=0)
        return jnp.where(mask, t4, jnp.zeros_like(t4))

    def chunk_row(t, row):
        return jnp.concatenate([jnp.broadcast_to(t[i * c + row:i * c + row + 1, :], (c, hk)) for i in range(group)],
                               axis=0)

    def cum_decay(gi):
        la_hi, la_lo = _split_bf16(la_ref[0, pl.ds(pl.multiple_of(gi * rows, rows), rows), :])
        return _dot(tri, la_hi) + _dot(tri, la_lo)

    n_groups = ts // rows

    def body(gi, big_l):
        sl = pl.ds(pl.multiple_of(gi * rows, rows), rows)
        l_end = chunk_row(big_l, c - 1)
        lc = big_l - chunk_row(big_l, c // 2 - 1)
        e_pos = jnp.exp2(lc)
        e_neg = jnp.exp2(-lc)
        q = q_ref[0, sl, :]
        k = k_ref[0, sl, :]
        v = v_ref[0, sl, :]
        q_pos = (q * e_pos).astype(BF16)
        q_neg = (q * e_neg).astype(BF16)
        k_pos = (k * e_pos).astype(BF16)
        k_neg = (k * e_neg).astype(BF16)
        q_in = (q * jnp.exp2(big_l)).astype(BF16)
        k_out = (k * jnp.exp2(l_end - big_l)).astype(BF16)
        decay = jnp.exp2(l_end)

        chunks = [slice(i * c, (i + 1) * c) for i in range(group)]
        a_past = [_dot_nt(q_pos[cs], tiled(k_neg[cs], bd_k)) for cs in chunks]
        a_fut = [_dot_nt(q_neg[cs], tiled(k_pos[cs], bd_k)) for cs in chunks]
        next_l = cum_decay(jnp.minimum(gi + 1, n_groups - 1))
        inc = [[_dot_tn(v[cs, h * GLA_V_DIM:(h + 1) * GLA_V_DIM], k_out[cs, h // 2 * pair_w:(h // 2 + 1) * pair_w])
                for h in range(GLA_HEADS)] for cs in chunks]
        u_t = [[jnp.where(low_sq, r[2 * j], r[2 * j + 1]) for j in range(GLA_HEADS // 2)] for r in inc]
        a = [jnp.where(lower, p, f).astype(BF16) for p, f in zip(a_past, a_fut)]
        o_intra = [_dot(a[i], tiled(v[cs], bd_v)) for i, cs in enumerate(chunks)]

        st = [st_ref[j] for j in range(GLA_HEADS // 2)]
        o_inter = []
        for i in range(group):
            cs = slice(i * c, (i + 1) * c)
            st_bf = [t.astype(BF16) for t in st]
            heads = []
            for h in range(GLA_HEADS):
                j = h // 2
                qp = q_in[cs, j * pair_w:(j + 1) * pair_w]
                qh = jnp.where(low_q if h % 2 == 0 else jnp.logical_not(low_q), qp, jnp.zeros_like(qp))
                heads.append(_dot_nt(qh, st_bf[j]))
            o_inter.append(jnp.concatenate(heads, axis=-1))
            st = [st[j] * decay[i * c:i * c + 1, j * pair_w:(j + 1) * pair_w] + u_t[i][j]
                  for j in range(GLA_HEADS // 2)]
        for j in range(GLA_HEADS // 2):
            st_ref[j] = st[j]

        o = jnp.concatenate(o_intra, axis=0) + jnp.concatenate(o_inter, axis=0)
        g = g_ref[0, sl, :]
        silu = g / (1.0 + jnp.exp(-g))
        for h in range(GLA_HEADS):
            hs = slice(h * GLA_V_DIM, (h + 1) * GLA_V_DIM)
            out_ref[0, sl, hs] = (_rms(o[:, hs], gain_ref[...]) * silu[:, hs]).astype(BF16)
        return next_l

    lax.fori_loop(0, n_groups, body, cum_decay(0))


def _gla(gq, gk, la, gv, gg, gla_on):
    b, s, _ = gq.shape
    ts = TS_GLA
    tile = lambda width: pl.BlockSpec((1, ts, width), lambda i, j: (i, j, 0))
    return pl.pallas_call(
        functools.partial(_gla_kernel, ts=ts, group=GLA_GROUP),
        grid=(b, s // ts),
        in_specs=[tile(_GQK), tile(_GQK), tile(_GQK), tile(_GW), tile(_GW),
                  pl.BlockSpec((1, GLA_V_DIM), lambda i, j: (0, 0))],
        out_specs=tile(_GW),
        out_shape=jax.ShapeDtypeStruct((b, s, _GW), BF16),
        scratch_shapes=[pltpu.VMEM((GLA_HEADS // 2, GLA_V_DIM, 2 * GLA_K_DIM), F32)],
        compiler_params=_params("parallel", "arbitrary"),
        name="gla",
    )(gq, gk, la, gv, gg, gla_on.reshape(1, -1))


_META_E0, _META_E1, _META_G0, _META_G1, _META_P0, _META_P1 = range(6)
_EXP_LANE0 = N_GROUPS


def _post_kernel(*refs, d, tm, sub, rider):
    if rider:
        (size_ref, pend_ref, x_ref, da_ref, gla_ref, wo_ref, gc_ref, wq_ref, qn_ref, km_ref, vm_ref, wco_ref,
         gf_ref, wr_ref, br_ref, rdest_ref, rsrc_ref,
         h_ref, xn_ref, meta_ref, ids_ref, cnt_ref, rpad_ref, zero_ref, rsem, zsem) = refs
    else:
        (x_ref, da_ref, gla_ref, wo_ref, gc_ref, wq_ref, qn_ref, km_ref, vm_ref, wco_ref,
         gf_ref, wr_ref, br_ref, h_ref, xn_ref, meta_ref, ids_ref, cnt_ref) = refs
    first = jnp.logical_and(pl.program_id(0) == 0, pl.program_id(1) == 0)

    @pl.when(first)
    def _():
        cnt_ref[...] = jnp.zeros(cnt_ref.shape, F32)
        if rider:
            _zero_fill(size_ref, pend_ref, rpad_ref, zero_ref, zsem)

    if rider:
        _scatter_rows(rdest_ref, rsrc_ref, rpad_ref, rsem)

    half = d // 2
    hd = d // CROSS_HEADS
    lane = lax.broadcasted_iota(I32, (sub, LANES), 1)
    big = jnp.int32(LANES)
    strict_lower = jnp.where(lax.broadcasted_iota(I32, (sub, sub), 0) > lax.broadcasted_iota(I32, (sub, sub), 1),
                             1.0, 0.0).astype(BF16)

    def lane_argmax(vals):
        m = jnp.max(vals, axis=-1, keepdims=True)
        idx = jnp.min(jnp.where(vals == m, lane, big), axis=-1, keepdims=True)
        return m, idx

    groups = [slice(r0, r0 + sub) for r0 in range(0, tm, sub)]
    heads = [slice(h * hd, (h + 1) * hd) for h in range(CROSS_HEADS)]
    da = [jnp.concatenate([da_ref[0, h, rs, :] for h in range(DA_HEADS)], axis=-1) for rs in groups]
    h1 = [x_ref[0, rs, :] + _dot(da[g], wo_ref[:half, :]) + _dot(gla_ref[0, rs, :], wo_ref[half:, :])
          for g, rs in enumerate(groups)]

    u = [_rms(t, gc_ref[...]).astype(BF16) for t in h1]
    q = [_dot(t, wq_ref[...]) for t in u]
    qh = [[_rms(t[:, hs], qn_ref[...]).astype(BF16) for hs in heads] for t in q]
    sc = [[_dot_nt(t[h], km_ref[0, :, hs]) for h, hs in enumerate(heads)] for t in qh]
    pr = []
    for t in sc:
        e = [jnp.exp(v - jnp.max(v, axis=-1, keepdims=True)) for v in t]
        pr.append([(v / jnp.sum(v, axis=-1, keepdims=True)).astype(BF16) for v in e])
    o = [jnp.concatenate([_dot(t[h], vm_ref[0, :, hs]) for h, hs in enumerate(heads)], axis=-1).astype(BF16)
         for t in pr]
    h2 = [h1[g] + _dot(o[g], wco_ref[...]) for g in range(len(groups))]
    for g, rs in enumerate(groups):
        h_ref[0, rs, :] = h2[g]

    xn = [_rms(t, gf_ref[...]).astype(BF16) for t in h2]
    logits = [_dot(t, wr_ref[...]) + br_ref[...] for t in xn]
    base = cnt_ref[0:1, :]
    for g, rs in enumerate(groups):
        bits = lax.bitcast_convert_type(xn[g].astype(F32), U32)
        xn_ref[0, rs, :] = (bits[:, :half] >> 16) | (bits[:, half:] & HI16)

        lg = jnp.where(lane < N_GROUPS, logits[g], NEG_INF)
        g_max, g_sel = lane_argmax(lg)
        p_g = 1.0 / jnp.sum(jnp.exp(lg - g_max), axis=-1, keepdims=True)
        e_lo = _EXP_LANE0 + g_sel * EXPERTS_PER_GROUP
        in_group = jnp.logical_and(lane >= e_lo, lane < e_lo + EXPERTS_PER_GROUP)
        le = jnp.where(in_group, logits[g], NEG_INF)
        m1, i1 = lane_argmax(le)
        m2, i2 = lane_argmax(jnp.where(lane == i1, NEG_INF, le))
        e2 = jnp.exp(m2 - m1)
        gate0 = p_g / (1.0 + e2)
        gate1 = p_g * e2 / (1.0 + e2)
        e0 = i1 - _EXP_LANE0
        e1 = i2 - _EXP_LANE0

        hot0 = lane == e0
        hot1 = lane == e1
        onehot = jnp.where(jnp.logical_or(hot0, hot1), 1.0, 0.0)
        before = _dot(strict_lower, onehot.astype(BF16)) + base
        pos0 = jnp.sum(jnp.where(hot0, before, 0.0), axis=-1, keepdims=True)
        pos1 = jnp.sum(jnp.where(hot1, before, 0.0), axis=-1, keepdims=True)

        meta = jnp.zeros(logits[g].shape, F32)
        for idx, val in ((_META_E0, e0.astype(F32)), (_META_E1, e1.astype(F32)), (_META_G0, gate0),
                         (_META_G1, gate1), (_META_P0, pos0), (_META_P1, pos1)):
            meta = jnp.where(lane == idx, val, meta)
        meta_ref[0, rs, :] = meta
        ids_ref[0, 0, :, rs] = meta.T[:ids_ref.shape[2], :]
        base = base + jnp.sum(onehot, axis=0, keepdims=True)
    cnt_ref[...] = jnp.broadcast_to(base, cnt_ref.shape)
    if rider:
        _scatter_wait(rsrc_ref, rpad_ref, rsem)


def _post(x, da, gla, w_o, norm_cross, w_cq, cross_qn, k_mem, v_mem, w_co, norm_ffn, w_group, b_group,
          w_expert, b_expert, *, half, rider=None):
    b, s, d = x.shape
    bh = b // 2
    b0 = half * bh
    tm = TM_POST
    ns = s // tm
    m = k_mem.shape[1]
    w_r = jnp.pad(jnp.concatenate([w_group, w_expert], axis=1), ((0, 0), (0, LANES - N_GROUPS - N_EXPERTS)))
    b_r = jnp.pad(jnp.concatenate([b_group, b_expert]), (0, LANES - N_GROUPS - N_EXPERTS)).reshape(1, LANES)
    const = lambda shape: pl.BlockSpec(shape, lambda i, j, *_: (0,) * len(shape))
    tile_in = lambda width: pl.BlockSpec((1, tm, width), lambda i, j, *_: (i + b0, j, 0))
    tile_out = lambda width: pl.BlockSpec((1, tm, width), lambda i, j, *_: (i, j, 0))
    per_b = lambda: pl.BlockSpec((1, m, d), lambda i, j, *_: (i + b0, 0, 0))
    da_spec = pl.BlockSpec((1, DA_HEADS, tm, DA_V_DIM), lambda i, j, *_: (i + b0, 0, j, 0))
    in_specs = [tile_in(d), da_spec, tile_in(d // 2), const((d, d)), const((1, d)), const((d, d)),
                const((1, d // CROSS_HEADS)), per_b(), per_b(), const((d, d)), const((1, d)),
                const((d, LANES)), const((1, LANES))]
    out_specs = [tile_out(d), tile_out(d // 2), tile_out(LANES),
                 pl.BlockSpec((1, 1, 8, tm), lambda i, j, *_: (i, j, 0, 0)), const((8, LANES))]
    out_shape = [
        jax.ShapeDtypeStruct((bh, s, d), F32),
        jax.ShapeDtypeStruct((bh, s, d // 2), U32),
        jax.ShapeDtypeStruct((bh, s, LANES), F32),
        jax.ShapeDtypeStruct((bh, ns, 8, tm), F32),
        jax.ShapeDtypeStruct((8, LANES), F32),
    ]
    args = (x, da, gla, w_o.astype(BF16), norm_cross.reshape(1, d), w_cq.astype(BF16), cross_qn.reshape(1, -1),
            k_mem, v_mem, w_co.astype(BF16), norm_ffn.reshape(1, d), w_r.astype(BF16), b_r)
    scalars, scratch = (), []
    if rider is not None:
        sizes, pend, dest, xn_src, n_rows = rider
        scalars = (sizes, pend)
        in_specs += [pl.BlockSpec((1, 2, tm), lambda i, j, *_: (i * ns + j, 0, 0), memory_space=pltpu.SMEM),
                     pl.BlockSpec((tm, d // 2), lambda i, j, *_: (i * ns + j, 0))]
        out_specs.append(pl.BlockSpec(memory_space=pl.ANY))
        out_shape.append(jax.ShapeDtypeStruct((n_rows, d // 2), U32))
        scratch = [pltpu.VMEM((EXPERT_ROWS, d // 2), U32), pltpu.SemaphoreType.DMA, pltpu.SemaphoreType.DMA]
        args += (dest, xn_src)
    return pl.pallas_call(
        functools.partial(_post_kernel, d=d, tm=tm, sub=SUB_POST, rider=rider is not None),
        grid_spec=pltpu.PrefetchScalarGridSpec(
            num_scalar_prefetch=len(scalars), grid=(bh, ns), in_specs=in_specs, out_specs=out_specs,
            scratch_shapes=scratch),
        out_shape=out_shape,
        compiler_params=_params("arbitrary", "arbitrary"),
        name="post_scatter" if rider is not None else "post",
    )(*scalars, *args)


def _row_copy(src_ref, src_row, dst_ref, dst_row, sem):
    return pltpu.make_async_copy(src_ref.at[pl.ds(src_row, 1)], dst_ref.at[pl.ds(dst_row, 1)], sem)


def _zero_fill(size_ref, pend_ref, xpad_ref, zero_ref, zsem):
    zero_ref[...] = jnp.zeros(zero_ref.shape, zero_ref.dtype)
    rows = zero_ref.shape[0]
    n_blocks = xpad_ref.shape[0] // rows
    n_used = pend_ref[N_EXPERTS - 1] // rows

    def zero_block(blk):
        return pltpu.make_async_copy(zero_ref, xpad_ref.at[pl.ds(pl.multiple_of(blk * rows, rows), rows)], zsem)

    def last_block(e, fn):
        @pl.when(size_ref[e] > 0)
        def _():
            fn(zero_block(pend_ref[e] // rows - 1))

    for fn in (lambda cp: cp.start(), lambda cp: cp.wait()):
        lax.fori_loop(0, N_EXPERTS, lambda e, c: (last_block(e, fn), c)[1], 0)
        lax.fori_loop(n_used, n_blocks, lambda blk, c: (fn(zero_block(blk)), c)[1], 0)


def _scatter_rows(dest_ref, src_ref, xpad_ref, sem):
    for t in range(src_ref.shape[0]):
        for k in range(2):
            _row_copy(src_ref, t, xpad_ref, dest_ref[0, k, t], sem).start(priority=k)


def _scatter_wait(src_ref, xpad_ref, sem):
    for _ in range(2):
        pltpu.make_async_copy(src_ref, xpad_ref.at[pl.ds(0, src_ref.shape[0])], sem).wait()


def _cast_expert_weights(fresh, wg_ref, wu_ref, wd_ref, wg_s, wu_s, wd_s):
    @pl.when(fresh)
    def _():
        wg_s[...] = wg_ref[0].astype(BF16)
        wu_s[...] = wu_ref[0].astype(BF16)
        wd_s[...] = wd_ref[0].astype(BF16)


def _expert_mlp(x_ref, wg_s, wu_s, wd_s, out_ref):
    words = x_ref[...]
    half = words.shape[1]
    lo = lax.bitcast_convert_type(words << 16, F32).astype(BF16)
    hi = lax.bitcast_convert_type(words & HI16, F32).astype(BF16)
    gate = _dot(lo, wg_s[:half, :]) + _dot(hi, wg_s[half:, :])
    up = _dot(lo, wu_s[:half, :]) + _dot(hi, wu_s[half:, :])
    hid = gate / (1.0 + jnp.exp(-gate)) * up
    out_ref[...] = _dot(hid.astype(BF16), wd_s[...])


def _experts_scatter_kernel(be_ref, nused_ref, size_ref, pend_ref, x_ref, wg_ref, wu_ref, wd_ref, rdest_ref, rsrc_ref,
                            out_ref, rpad_ref, wg_s, wu_s, wd_s, zero_ref, rsem, zsem, *, rider_steps):
    i = pl.program_id(0)
    used = i < nused_ref[0]
    new_expert = jnp.logical_or(i == 0, be_ref[i] != be_ref[jnp.maximum(i - 1, 0)])
    mlp = functools.partial(_expert_mlp, x_ref, wg_s, wu_s, wd_s, out_ref)

    @pl.when(i == 0)
    def _():
        _zero_fill(size_ref, pend_ref, rpad_ref, zero_ref, zsem)

    _cast_expert_weights(jnp.logical_and(used, new_expert), wg_ref, wu_ref, wd_ref, wg_s, wu_s, wd_s)
    riding = i < rider_steps

    @pl.when(riding)
    def _():
        _scatter_rows(rdest_ref, rsrc_ref, rpad_ref, rsem)
        mlp()
        _scatter_wait(rsrc_ref, rpad_ref, rsem)

    pl.when(jnp.logical_and(used, jnp.logical_not(riding)))(mlp)

    @pl.when(jnp.logical_not(used))
    def _():
        out_ref[...] = jnp.zeros(out_ref.shape, F32)


def _gather_rows(dest_ref, src_ref, buf_ref, sem):
    for t in range(buf_ref.shape[1]):
        for k in range(2):
            _row_copy(src_ref, dest_ref[0, k, t], buf_ref.at[k], t, sem).start(priority=k)


def _combine_rows(h_ref, meta_ref, src_ref, buf_ref, sem):
    for k in range(2):
        pltpu.make_async_copy(src_ref.at[pl.ds(0, buf_ref.shape[1])], buf_ref.at[k], sem).wait()
    meta = meta_ref[...]
    g0 = meta[:, _META_G0:_META_G0 + 1]
    g1 = meta[:, _META_G1:_META_G1 + 1]
    return h_ref[...] + g0 * buf_ref[0] + g1 * buf_ref[1]


def _experts_gather_kernel(be_ref, nused_ref, x_ref, wg_ref, wu_ref, wd_ref, gdest_ref, h_ref, meta_ref, opad_ref,
                           out_ref, y_ref, wg_s, wu_s, wd_s, gbuf_ref, gsem, *, rider_steps):
    i = pl.program_id(0)
    used = i < nused_ref[0]
    new_expert = jnp.logical_or(i == 0, be_ref[i] != be_ref[jnp.maximum(i - 1, 0)])
    mlp = functools.partial(_expert_mlp, x_ref, wg_s, wu_s, wd_s, out_ref)
    _cast_expert_weights(jnp.logical_and(used, new_expert), wg_ref, wu_ref, wd_ref, wg_s, wu_s, wd_s)

    def gather():
        _gather_rows(gdest_ref, opad_ref, gbuf_ref.at[i % 2], gsem.at[i % 2])

    def combine():
        slot = (i - 1) % 2
        y_ref[...] = _combine_rows(h_ref, meta_ref, opad_ref, gbuf_ref.at[slot], gsem.at[slot])

    @pl.when(i == 0)
    def _():
        gather()
        mlp()

    @pl.when(jnp.logical_and(i > 0, i < rider_steps))
    def _():
        gather()
        mlp()
        combine()

    @pl.when(i == rider_steps)
    def _():
        combine()
        pl.when(used)(mlp)

    pl.when(jnp.logical_and(i > rider_steps, used))(mlp)

    @pl.when(jnp.logical_not(used))
    def _():
        out_ref[...] = jnp.zeros(out_ref.shape, F32)


def _experts_kernel(be_ref, nused_ref, x_ref, wg_ref, wu_ref, wd_ref, out_ref, wg_s, wu_s, wd_s):
    i = pl.program_id(0)
    used = i < nused_ref[0]
    new_expert = jnp.logical_or(i == 0, be_ref[i] != be_ref[jnp.maximum(i - 1, 0)])
    _cast_expert_weights(jnp.logical_and(used, new_expert), wg_ref, wu_ref, wd_ref, wg_s, wu_s, wd_s)
    pl.when(used)(functools.partial(_expert_mlp, x_ref, wg_s, wu_s, wd_s, out_ref))

    @pl.when(jnp.logical_not(used))
    def _():
        out_ref[...] = jnp.zeros(out_ref.shape, F32)


def _retile(dest, tr):
    tiles, _, tm = dest.shape
    return dest.reshape(tiles, 2, tm // tr, tr).transpose(0, 2, 1, 3).reshape(tiles * (tm // tr), 2, tr)


def _experts(x_pad, block_expert, n_used, w_gate, w_up, w_down, scatter=None, gather=None):
    n_rows = x_pad.shape[0]
    _, d, f = w_gate.shape
    rows, tr = EXPERT_ROWS, RIDER_ROWS
    row_blk = lambda i, be, nu, *_: (jnp.minimum(i, nu[0] - 1), 0)
    weights = lambda shape: pl.BlockSpec(shape, lambda i, be, *_: (be[i], 0, 0))
    in_specs = [pl.BlockSpec((rows, d // 2), row_blk), weights((1, d, f)), weights((1, d, f)), weights((1, f, d))]
    out_specs = [pl.BlockSpec((rows, d), lambda i, *_: (i, 0))]
    out_shape = [jax.ShapeDtypeStruct((n_rows, d), F32)]
    scratch = [pltpu.VMEM((d, f), BF16), pltpu.VMEM((d, f), BF16), pltpu.VMEM((f, d), BF16)]
    scalars, args = (block_expert, n_used), (x_pad, w_gate, w_up, w_down)
    body, name = _experts_kernel, "experts"
    if scatter is not None or gather is not None:
        n_tokens = (scatter[3] if scatter is not None else gather[1]).shape[0]
        steps = n_tokens // tr
        assert 2 * n_tokens >= steps * rows, "riding steps must all be used expert blocks"
        tile = lambda width, shift: pl.BlockSpec((tr, width), lambda i, *_: (jnp.clip(i - shift, 0, steps - 1), 0))
        ids = lambda: pl.BlockSpec((1, 2, tr), lambda i, *_: (jnp.minimum(i, steps - 1), 0, 0),
                                   memory_space=pltpu.SMEM)
    if scatter is not None:
        sizes, pend, dest, xn_src, other_rows = scatter
        scalars += (sizes, pend)
        in_specs += [ids(), tile(d // 2, 0)]
        out_specs.append(pl.BlockSpec(memory_space=pl.ANY))
        out_shape.append(jax.ShapeDtypeStruct((other_rows, d // 2), U32))
        scratch += [pltpu.VMEM((rows, d // 2), U32), pltpu.SemaphoreType.DMA, pltpu.SemaphoreType.DMA]
        args += (_retile(dest, tr), xn_src)
        body, name = functools.partial(_experts_scatter_kernel, rider_steps=steps), "experts_scatter"
    elif gather is not None:
        dest, h2, meta, opad = gather
        in_specs += [ids(), tile(d, 1), tile(LANES, 1), pl.BlockSpec(memory_space=pl.ANY)]
        out_specs.append(tile(d, 1))
        out_shape.append(jax.ShapeDtypeStruct((n_tokens, d), F32))
        scratch += [pltpu.VMEM((2, 2, tr, d), F32), pltpu.SemaphoreType.DMA((2,))]
        args += (_retile(dest, tr), h2, meta, opad)
        body, name = functools.partial(_experts_gather_kernel, rider_steps=steps), "experts_gather"
    out = pl.pallas_call(
        body,
        grid_spec=pltpu.PrefetchScalarGridSpec(
            num_scalar_prefetch=len(scalars), grid=(n_rows // rows,), in_specs=in_specs, out_specs=out_specs,
            scratch_shapes=scratch),
        out_shape=out_shape,
        compiler_params=_params("arbitrary"),
        name=name,
    )(*scalars, *args)
    return out if len(out) > 1 else out[0]


def _combine_tail_kernel(dest_ref, y0_ref, h_ref, meta_ref, opad_ref, y_ref, buf_ref, sem, *, nt):
    s = pl.program_id(0)

    @pl.when(s < nt)
    def _():
        _gather_rows(dest_ref, opad_ref, buf_ref.at[s % 2], sem.at[s % 2])

    @pl.when(s > 0)
    def _():
        slot = (s - 1) % 2
        y_ref[0] = y0_ref[...]
        y_ref[1] = _combine_rows(h_ref, meta_ref, opad_ref, buf_ref.at[slot], sem.at[slot])


def _combine_tail(y0, h2, meta, dest, out_pad):
    th, d = h2.shape
    tm = TM_TAIL
    nt = th // tm
    prev = lambda s: (jnp.maximum(s - 1, 0), 0)
    return pl.pallas_call(
        functools.partial(_combine_tail_kernel, nt=nt),
        grid=(nt + 1,),
        in_specs=[
            pl.BlockSpec((1, 2, tm), lambda s: (jnp.minimum(s, nt - 1), 0, 0), memory_space=pltpu.SMEM),
            pl.BlockSpec((tm, d), prev), pl.BlockSpec((tm, d), prev), pl.BlockSpec((tm, LANES), prev),
            pl.BlockSpec(memory_space=pl.ANY),
        ],
        out_specs=pl.BlockSpec((2, tm, d), lambda s: (0, jnp.maximum(s - 1, 0), 0)),
        scratch_shapes=[pltpu.VMEM((2, 2, tm, d), F32), pltpu.SemaphoreType.DMA((2,))],
        out_shape=jax.ShapeDtypeStruct((2, th, d), F32),
        compiler_params=_params("arbitrary"),
        name="combine_tail",
    )(_retile(dest, tm), y0, h2, meta, out_pad)


def _route_tables(counts, ids, n_tokens):
    rows = EXPERT_ROWS
    sizes = counts[0, :N_EXPERTS].astype(I32)
    padded = (sizes + rows - 1) // rows * rows
    pend = jnp.cumsum(padded)
    pstart = pend - padded
    n_rows = 2 * n_tokens + N_EXPERTS * rows
    block_start = jnp.arange(n_rows // rows, dtype=I32) * rows
    block_expert = jnp.minimum(jnp.sum(pend[None, :] <= block_start[:, None], axis=1), N_EXPERTS - 1).astype(I32)
    n_used = (pend[-1:] // rows).astype(I32)
    expert = ids[:, _META_E0:_META_E1 + 1].astype(I32)
    rank = ids[:, _META_P0:_META_P1 + 1].astype(I32)
    experts = jnp.arange(N_EXPERTS, dtype=I32).reshape(-1, 1, 1, 1)
    dest = jnp.sum(jnp.where(expert[None] == experts, pstart.reshape(-1, 1, 1, 1), 0), axis=0) + rank
    return sizes, pend, block_expert, n_used, dest, n_rows


def kernel(x, mem, norm_mix, w_in, da_q_norm, da_k_norm, lambda_q1, lambda_k1, lambda_q2, lambda_k2,
           da_out_norm, gla_gate_w, gla_gate_b, gla_out_norm, w_o, norm_cross, norm_mem, w_cq, w_ckv,
           cross_q_norm, cross_k_norm, w_co, norm_ffn, w_group, b_group, w_expert, b_expert,
           w_e_gate, w_e_up, w_e_down):
    b, s, d = x.shape
    th = b // 2 * s
    h = x
    for l in range(norm_mix.shape[0]):
        assert l == 0, "lam_init is fixed for a single layer"
        qt, kda, vt, gq, gk, gv, gg, la = _in_proj(h, norm_mix[l], w_in[l], da_q_norm[l], da_k_norm[l],
                                                   gla_gate_w[l], gla_gate_b[l])
        da = _diff_attn(qt, kda, vt, lambda_q1[l], lambda_k1[l], lambda_q2[l], lambda_k2[l], da_out_norm[l],
                        da_q_norm[l], da_k_norm[l])
        gla = _gla(gq, gk, la, gv, gg, gla_out_norm[l])
        k_mem, v_mem = _mem_kv(mem, norm_mem[l], w_ckv[l], cross_k_norm[l])
        post = functools.partial(_post, h, da, gla, w_o[l], norm_cross[l], w_cq[l], cross_q_norm[l], k_mem, v_mem,
                                 w_co[l], norm_ffn[l], w_group[l], b_group[l], w_expert[l], b_expert[l])
        experts = functools.partial(_experts, w_gate=w_e_gate[l], w_up=w_e_up[l], w_down=w_e_down[l])

        h2_0, xn_0, meta_0, ids_0, counts_0 = post(half=0)
        sizes_0, pend_0, be_0, used_0, dest_0, n_rows = _route_tables(counts_0, ids_0.reshape(-1, 8, TM_POST), th)
        h2_1, xn_1, meta_1, ids_1, counts_1, xpad_0 = post(
            half=1, rider=(sizes_0, pend_0, dest_0, xn_0.reshape(th, d // 2), n_rows))
        sizes_1, pend_1, be_1, used_1, dest_1, _ = _route_tables(counts_1, ids_1.reshape(-1, 8, TM_POST), th)
        opad_0, xpad_1 = experts(xpad_0, be_0, used_0,
                                 scatter=(sizes_1, pend_1, dest_1, xn_1.reshape(th, d // 2), n_rows))
        opad_1, y_0 = experts(xpad_1, be_1, used_1,
                              gather=(dest_0, h2_0.reshape(th, d), meta_0.reshape(th, LANES), opad_0))
        h = _combine_tail(y_0, h2_1.reshape(th, d), meta_1.reshape(th, LANES), dest_1, opad_1).reshape(b, s, d)
    return h
```

```python
import functools
import math

import jax
import jax.numpy as jnp
import numpy as np
from jax import lax
from jax.experimental import pallas as pl
from jax.experimental.pallas import tpu as pltpu

F32 = jnp.float32
BF16 = jnp.bfloat16
I32 = jnp.int32
U32 = jnp.uint32
HI16 = np.uint32(0xFFFF0000)

EPS = 1e-6
CHUNK = 64

DA_HEADS = 4
DA_QK_DIM = 64
DA_V_DIM = 128
GLA_HEADS = 4
GLA_K_DIM = 64
GLA_V_DIM = 128
GLA_GATE_RANK = 16
GLA_TAU = 16.0
CROSS_HEADS = 4
N_GROUPS = 4
EXPERTS_PER_GROUP = 8
N_EXPERTS = N_GROUPS * EXPERTS_PER_GROUP
LAM_INIT = 0.8 - 0.6 * math.exp(-0.3 * 0)

LANES = 128
VMEM_LIMIT = 56 * 1024 * 1024

TM_PROJ = 1024
ATT_BLK = 512
TS_GLA = 1024
GLA_GROUP = 4
TM_POST = 1024
SUB_POST = 256
TM_TAIL = 512
RIDER_ROWS = 256
EXPERT_ROWS = 512

NEG_INF = float("-inf")


def _params(*sem):
    return pltpu.CompilerParams(dimension_semantics=sem, vmem_limit_bytes=VMEM_LIMIT)


def _rms(t, g):
    ms = jnp.mean(t * t, axis=-1, keepdims=True)
    return t * lax.rsqrt(ms + EPS) * g


def _dot(a, b):
    return jnp.dot(a, b, preferred_element_type=F32)


def _dot_nt(a, b):
    return lax.dot_general(a, b, (((1,), (1,)), ((), ())), preferred_element_type=F32)


def _dot_tn(a, b):
    return lax.dot_general(a, b, (((0,), (0,)), ((), ())), preferred_element_type=F32)


def _split_bf16(t):
    hi = t.astype(BF16)
    lo = (t - hi.astype(F32)).astype(BF16)
    return hi, lo


def _mem_kv_kernel(mem_ref, g_ref, w_ref, kn_ref, k_ref, v_ref, *, d, heads):
    mn = _rms(mem_ref[0], g_ref[...]).astype(BF16)
    kv = _dot(mn, w_ref[...])
    hd = d // heads
    scale = hd ** -0.5
    for h in range(heads):
        kh = _rms(kv[:, h * hd:(h + 1) * hd], kn_ref[...]) * scale
        k_ref[0, :, h * hd:(h + 1) * hd] = kh.astype(BF16)
    v_ref[0] = kv[:, d:].astype(BF16)


def _mem_kv(mem, norm_m, w_ckv, kn):
    b, m, d = mem.shape
    return pl.pallas_call(
        functools.partial(_mem_kv_kernel, d=d, heads=CROSS_HEADS),
        grid=(b,),
        in_specs=[
            pl.BlockSpec((1, m, d), lambda i: (i, 0, 0)),
            pl.BlockSpec((1, d), lambda i: (0, 0)),
            pl.BlockSpec((d, 2 * d), lambda i: (0, 0)),
            pl.BlockSpec((1, d // CROSS_HEADS), lambda i: (0, 0)),
        ],
        out_specs=[
            pl.BlockSpec((1, m, d), lambda i: (i, 0, 0)),
            pl.BlockSpec((1, m, d), lambda i: (i, 0, 0)),
        ],
        out_shape=[jax.ShapeDtypeStruct((b, m, d), BF16)] * 2,
        compiler_params=_params("parallel"),
        name="mem_kv",
    )(mem, norm_m.reshape(1, d), w_ckv.astype(BF16), kn.reshape(1, -1))


_QK = DA_HEADS * 2 * DA_QK_DIM
_DAW = DA_HEADS * DA_V_DIM
_GQK = GLA_HEADS * GLA_K_DIM
_GW = GLA_HEADS * GLA_V_DIM
_OFF_DQ = 0
_OFF_DK = _OFF_DQ + _QK
_OFF_DV = _OFF_DK + _QK
_OFF_GQ = _OFF_DV + _DAW
_OFF_GK = _OFF_GQ + _GQK
_OFF_GV = _OFF_GK + _GQK
_OFF_GG = _OFF_GV + _GW
_OFF_GR = _OFF_GG + _GW
_IN_PAD = _OFF_GR + LANES


def _in_proj_kernel(x_ref, g_ref, w_ref, qg_ref, kg_ref, grp_ref, gw_ref, gb_ref,
                    qt_ref, k_ref, vt_ref, gq_ref, gk_ref, gv_ref, gg_ref, la_ref):
    u = _rms(x_ref[0], g_ref[...]).astype(BF16)

    def proj(off, width):
        return _dot(u, w_ref[:, off:off + width])

    def mean_square(p):
        return _dot((p * p).astype(BF16), grp_ref[...])

    p_q = proj(_OFF_DQ, _QK)
    p_k = proj(_OFF_DK, _QK)
    ms_q = mean_square(p_q)
    qn = p_q * lax.rsqrt(ms_q + EPS) * qg_ref[...] * (DA_QK_DIM ** -0.5 * math.log2(math.e))
    blk = qt_ref.shape[3]
    for t in range(qt_ref.shape[1]):
        qt_ref[0, t] = qn[t * blk:(t + 1) * blk].T.astype(BF16)
    dv = proj(_OFF_DV, _DAW)
    ms_k = mean_square(p_k)
    kn = (p_k * lax.rsqrt(ms_k + EPS) * kg_ref[...]).astype(BF16)
    for h in range(DA_HEADS):
        k_ref[0, h] = kn[:, h * 2 * DA_QK_DIM:(h + 1) * 2 * DA_QK_DIM]
    for t in range(vt_ref.shape[1]):
        vt_ref[0, t] = dv[t * blk:(t + 1) * blk].T.astype(BF16)
    g_r = proj(_OFF_GR, LANES)
    gq_ref[0] = proj(_OFF_GQ, _GQK) * (GLA_K_DIM ** -0.5)
    gk_ref[0] = proj(_OFF_GK, _GQK)
    z = _dot(g_r.astype(BF16), gw_ref[...]) + gb_ref[...]
    gv_ref[0] = proj(_OFF_GV, _GW).astype(BF16)
    gg_ref[0] = proj(_OFF_GG, _GW)
    log_sig = jnp.minimum(z, 0.0) - jnp.log(1.0 + jnp.exp(-jnp.abs(z)))
    la_ref[0] = log_sig * (math.log2(math.e) / GLA_TAU)


def _in_proj(x, norm_g, w_in, da_qn, da_kn, gate_w, gate_b):
    b, s, d = x.shape
    tm, blk = TM_PROJ, ATT_BLK
    ns, nb = s // tm, tm // blk
    w = jnp.pad(w_in, ((0, 0), (0, _IN_PAD - w_in.shape[1]))).astype(BF16)
    gw = jnp.pad(gate_w, ((0, LANES - GLA_GATE_RANK), (0, 0))).astype(BF16)
    lane = jnp.arange(_QK)
    grp = jnp.where((lane[:, None] // DA_QK_DIM) == (lane[None, :] // DA_QK_DIM),
                    1.0 / DA_QK_DIM, 0.0).astype(BF16)
    const = lambda shape: pl.BlockSpec(shape, lambda i, j: (0,) * len(shape))
    tile = lambda width: pl.BlockSpec((1, tm, width), lambda i, j: (i, j, 0))
    tile_t = lambda width: pl.BlockSpec((1, nb, width, blk), lambda i, j: (i, j, 0, 0))
    return pl.pallas_call(
        _in_proj_kernel,
        grid=(b, ns),
        in_specs=[tile(d), const((1, d)), const((d, _IN_PAD)), const((1, _QK)), const((1, _QK)),
                  const((_QK, _QK)), const((LANES, _GQK)), const((1, _GQK))],
        out_specs=[tile_t(_QK), pl.BlockSpec((1, DA_HEADS, tm, 2 * DA_QK_DIM), lambda i, j: (i, 0, j, 0)),
                   tile_t(_DAW), tile(_GQK), tile(_GQK), tile(_GW), tile(_GW), tile(_GQK)],
        out_shape=[
            jax.ShapeDtypeStruct((b, s // blk, _QK, blk), BF16),
            jax.ShapeDtypeStruct((b, DA_HEADS, s, 2 * DA_QK_DIM), BF16),
            jax.ShapeDtypeStruct((b, s // blk, _DAW, blk), BF16),
            jax.ShapeDtypeStruct((b, s, _GQK), F32),
            jax.ShapeDtypeStruct((b, s, _GQK), F32),
            jax.ShapeDtypeStruct((b, s, _GW), BF16),
            jax.ShapeDtypeStruct((b, s, _GW), F32),
            jax.ShapeDtypeStruct((b, s, _GQK), F32),
        ],
        compiler_params=_params("parallel", "parallel"),
        name="in_proj",
    )(x, norm_g.reshape(1, d), w, jnp.tile(da_qn, 2 * DA_HEADS).reshape(1, _QK),
      jnp.tile(da_kn, 2 * DA_HEADS).reshape(1, _QK), grp, gw, gate_b.reshape(1, _GQK))


def _split_q(qt):
    row = lax.broadcasted_iota(I32, qt.shape, 0)
    zero = jnp.zeros_like(qt)
    return jnp.where(row < DA_QK_DIM, qt, zero), jnp.where(row >= DA_QK_DIM, qt, zero)


def _chunk_causal_mask(blk):
    key_chunk = lax.broadcasted_iota(I32, (blk, blk), 0) // CHUNK
    qry_chunk = lax.broadcasted_iota(I32, (blk, blk), 1) // CHUNK
    return key_chunk <= qry_chunk


def _diff_attn_finish(lq1_ref, lk1_ref, lq2_ref, lk2_ref, gain_ref, a1, l1, a2, l2):
    lam = (jnp.exp(jnp.sum(lq1_ref[...] * lk1_ref[...], axis=-1, keepdims=True))
           - jnp.exp(jnp.sum(lq2_ref[...] * lk2_ref[...], axis=-1, keepdims=True)) + LAM_INIT)
    o = a1 / l1 - lam * (a2 / l2)
    ms = jnp.mean(o * o, axis=0, keepdims=True)
    o = o * lax.rsqrt(ms + EPS) * gain_ref[...] * (1.0 - LAM_INIT)
    return o.T.astype(BF16)


def _diff_attn_bounded_kernel(lq1_ref, lk1_ref, lq2_ref, lk2_ref, gain_ref, qt_ref, k_ref, vt_ref, out_ref,
                              s_ref, l1_ref, a1_ref, l2_ref, a2_ref, *, blk, nb):
    stats = ((l1_ref, a1_ref), (l2_ref, a2_ref))
    mask = _chunk_causal_mask(blk)

    def reset():
        for l_ref, a_ref in stats:
            l_ref[...] = jnp.zeros(l_ref.shape, F32)
            a_ref[...] = jnp.zeros(a_ref.shape, F32)

    def scores(q, j, slot):
        kb = k_ref[0, 0, pl.ds(pl.multiple_of(j * blk, blk), blk), :]
        s_ref[slot, 0] = _dot(kb, q[0])
        s_ref[slot, 1] = _dot(kb, q[1])

    def consume(j, slot, masked):
        vb = vt_ref[0, j]
        for m, (l_ref, a_ref) in enumerate(stats):
            s = s_ref[slot, m]
            if masked:
                s = jnp.where(mask, s, NEG_INF)
            p = jnp.exp2(s)
            l_ref[...] += jnp.sum(p, axis=0, keepdims=True)
            a_ref[...] += _dot(vb, p.astype(BF16))

    def step(q, j, slot):
        scores(q, j + 1, 1 - slot)
        consume(j, slot, False)

    reset()
    q = _split_q(qt_ref[0, 0])
    slot = 0
    scores(q, 0, slot)
    for qi in range(nb):
        for j in range(qi):
            step(q, j, slot)
            slot = 1 - slot
        if qi + 1 < nb:
            q = _split_q(qt_ref[0, qi + 1])
            scores(q, 0, 1 - slot)
        consume(qi, slot, True)
        out_ref[0, 0, qi * blk:(qi + 1) * blk, :] = _diff_attn_finish(
            lq1_ref, lk1_ref, lq2_ref, lk2_ref, gain_ref, a1_ref[...], l1_ref[...], a2_ref[...], l2_ref[...])
        if qi + 1 < nb:
            reset()
        slot = 1 - slot


def _diff_attn_online_kernel(lq1_ref, lk1_ref, lq2_ref, lk2_ref, gain_ref, qt_ref, k_ref, vt_ref, out_ref,
                             m1_ref, l1_ref, a1_ref, m2_ref, l2_ref, a2_ref, *, blk):
    qi = pl.program_id(2)
    q1, q2 = _split_q(qt_ref[0, 0])

    for m_ref, l_ref, a_ref in ((m1_ref, l1_ref, a1_ref), (m2_ref, l2_ref, a2_ref)):
        m_ref[...] = jnp.full(m_ref.shape, NEG_INF, F32)
        l_ref[...] = jnp.zeros(l_ref.shape, F32)
        a_ref[...] = jnp.zeros(a_ref.shape, F32)

    def update(s, vb, m_ref, l_ref, a_ref):
        m_old = m_ref[...]
        m_new = jnp.maximum(m_old, jnp.max(s, axis=0, keepdims=True))
        alpha = jnp.exp2(m_old - m_new)
        p = jnp.exp2(s - m_new)
        l_ref[...] = alpha * l_ref[...] + jnp.sum(p, axis=0, keepdims=True)
        a_ref[...] = alpha * a_ref[...] + _dot(vb, p.astype(BF16))
        m_ref[...] = m_new

    def block(j, mask):
        kb = k_ref[0, 0, pl.ds(pl.multiple_of(j * blk, blk), blk), :]
        vb = vt_ref[0, j]
        s1 = _dot(kb, q1)
        s2 = _dot(kb, q2)
        if mask is not None:
            s1 = jnp.where(mask, s1, NEG_INF)
            s2 = jnp.where(mask, s2, NEG_INF)
        update(s1, vb, m1_ref, l1_ref, a1_ref)
        update(s2, vb, m2_ref, l2_ref, a2_ref)

    def body(j, carry):
        block(j, None)
        return carry

    lax.fori_loop(0, qi, body, 0)
    block(qi, _chunk_causal_mask(blk))
    out_ref[0, 0] = _diff_attn_finish(lq1_ref, lk1_ref, lq2_ref, lk2_ref, gain_ref,
                                      a1_ref[...], l1_ref[...], a2_ref[...], l2_ref[...])


SCORE_BOUND = 60.0


def _diff_attn(qt, k, vt, lq1, lk1, lq2, lk2, da_on, da_qn, da_kn):
    b, nb, _, blk = qt.shape
    s = nb * blk
    stat = lambda: pltpu.VMEM((1, blk), F32)
    acc = lambda: pltpu.VMEM((DA_V_DIM, blk), F32)

    args = (lq1.reshape(1, -1), lk1.reshape(1, -1), lq2.reshape(1, -1), lk2.reshape(1, -1),
            da_on.reshape(-1, 1), qt, k, vt)
    out_shape = jax.ShapeDtypeStruct((b, DA_HEADS, s, DA_V_DIM), BF16)
    head = lambda *trailing: (lambda i, h: (i, 0, h) + trailing)
    vec2 = lambda: pl.BlockSpec((1, DA_QK_DIM), lambda i, h: (0, 0))
    bounded = pl.pallas_call(
        functools.partial(_diff_attn_bounded_kernel, blk=blk, nb=nb),
        grid=(b, DA_HEADS),
        in_specs=[
            vec2(), vec2(), vec2(), vec2(),
            pl.BlockSpec((DA_V_DIM, 1), lambda i, h: (0, 0)),
            pl.BlockSpec((1, nb, 2 * DA_QK_DIM, blk), head(0)),
            pl.BlockSpec((1, 1, s, 2 * DA_QK_DIM), lambda i, h: (i, h, 0, 0)),
            pl.BlockSpec((1, nb, DA_V_DIM, blk), head(0)),
        ],
        out_specs=pl.BlockSpec((1, 1, s, DA_V_DIM), lambda i, h: (i, h, 0, 0)),
        out_shape=out_shape,
        scratch_shapes=[pltpu.VMEM((2, 2, blk, blk), F32), stat(), acc(), stat(), acc()],
        compiler_params=_params("parallel", "parallel"),
        name="diff_attn",
    )
    vec3 = lambda: pl.BlockSpec((1, DA_QK_DIM), lambda i, h, q: (0, 0))
    online = pl.pallas_call(
        functools.partial(_diff_attn_online_kernel, blk=blk),
        grid=(b, DA_HEADS, nb),
        in_specs=[
            vec3(), vec3(), vec3(), vec3(),
            pl.BlockSpec((DA_V_DIM, 1), lambda i, h, q: (0, 0)),
            pl.BlockSpec((1, 1, 2 * DA_QK_DIM, blk), lambda i, h, q: (i, q, h, 0)),
            pl.BlockSpec((1, 1, s, 2 * DA_QK_DIM), lambda i, h, q: (i, h, 0, 0)),
            pl.BlockSpec((1, nb, DA_V_DIM, blk), lambda i, h, q: (i, 0, h, 0)),
        ],
        out_specs=pl.BlockSpec((1, 1, blk, DA_V_DIM), lambda i, h, q: (i, h, q, 0)),
        out_shape=out_shape,
        scratch_shapes=[stat(), stat(), acc(), stat(), stat(), acc()],
        compiler_params=_params("parallel", "parallel", "parallel"),
        name="diff_attn_online",
    )
    bound = (1.01 * DA_QK_DIM ** 0.5 * math.log2(math.e)) * jnp.max(jnp.abs(da_qn)) * jnp.max(jnp.abs(da_kn))
    return lax.cond(bound <= SCORE_BOUND, bounded, online, *args)


def _gla_kernel(q_ref, k_ref, la_ref, v_ref, g_ref, gain_ref, out_ref, st_ref, *, ts, group):
    @pl.when(pl.program_id(1) == 0)
    def _():
        st_ref[...] = jnp.zeros(st_ref.shape, F32)

    c = CHUNK
    rows = group * c
    hk, hv = _GQK, _GW
    r = lax.broadcasted_iota(I32, (rows, rows), 0)
    cc = lax.broadcasted_iota(I32, (rows, rows), 1)
    tri = jnp.where(jnp.logical_and(r // c == cc // c, r >= cc), 1.0, 0.0).astype(BF16)
    bd_k = (lax.broadcasted_iota(I32, (hk, hk), 0) // GLA_K_DIM
            == lax.broadcasted_iota(I32, (hk, hk), 1) // GLA_K_DIM)
    bd_v = (lax.broadcasted_iota(I32, (hk, hv), 0) // GLA_K_DIM
            == lax.broadcasted_iota(I32, (hk, hv), 1) // GLA_V_DIM)
    pair_w = 2 * GLA_K_DIM
    low_sq = lax.broadcasted_iota(I32, (GLA_V_DIM, pair_w), 1) < GLA_K_DIM
    low_q = lax.broadcasted_iota(I32, (c, pair_w), 1) < GLA_K_DIM
    lower = (lax.broadcasted_iota(I32, (c, hk), 0)
             >= lax.broadcasted_iota(I32, (c, hk), 1) % c)

    def tiled(t, mask):
        t4 = jnp.concatenate([t] * GLA_HEADS, axis=0)
        return jnp.where(mask, t4, jnp.zeros_like(t4))

    def chunk_row(t, row):
        return jnp.concatenate([jnp.broadcast_to(t[i * c + row:i * c + row + 1, :], (c, hk)) for i in range(group)],
                               axis=0)

    def cum_decay(gi):
        la_hi, la_lo = _split_bf16(la_ref[0, pl.ds(pl.multiple_of(gi * rows, rows), rows), :])
        return _dot(tri, la_hi) + _dot(tri, la_lo)

    n_groups = ts // rows

    def body(gi, big_l):
        sl = pl.ds(pl.multiple_of(gi * rows, rows), rows)
        l_end = chunk_row(big_l, c - 1)
        lc = big_l - chunk_row(big_l, c // 2 - 1)
        e_pos = jnp.exp2(lc)
        e_neg = jnp.exp2(-lc)
        q = q_ref[0, sl, :]
        k = k_ref[0, sl, :]
        v = v_ref[0, sl, :]
        q_pos = (q * e_pos).astype(BF16)
        q_neg = (q * e_neg).astype(BF16)
        k_pos = (k * e_pos).astype(BF16)
        k_neg = (k * e_neg).astype(BF16)
        q_in = (q * jnp.exp2(big_l)).astype(BF16)
        k_out = (k * jnp.exp2(l_end - big_l)).astype(BF16)
        decay = jnp.exp2(l_end)

        chunks = [slice(i * c, (i + 1) * c) for i in range(group)]
        a_past = [_dot_nt(q_pos[cs], tiled(k_neg[cs], bd_k)) for cs in chunks]
        a_fut = [_dot_nt(q_neg[cs], tiled(k_pos[cs], bd_k)) for cs in chunks]
        next_l = cum_decay(jnp.minimum(gi + 1, n_groups - 1))
        inc = [[_dot_tn(v[cs, h * GLA_V_DIM:(h + 1) * GLA_V_DIM], k_out[cs, h // 2 * pair_w:(h // 2 + 1) * pair_w])
                for h in range(GLA_HEADS)] for cs in chunks]
        u_t = [[jnp.where(low_sq, r[2 * j], r[2 * j + 1]) for j in range(GLA_HEADS // 2)] for r in inc]
        a = [jnp.where(lower, p, f).astype(BF16) for p, f in zip(a_past, a_fut)]
        o_intra = [_dot(a[i], tiled(v[cs], bd_v)) for i, cs in enumerate(chunks)]

        st = [st_ref[j] for j in range(GLA_HEADS // 2)]
        o_inter = []
        for i in range(group):
            cs = slice(i * c, (i + 1) * c)
            st_bf = [t.astype(BF16) for t in st]
            heads = []
            for h in range(GLA_HEADS):
                j = h // 2
                qp = q_in[cs, j * pair_w:(j + 1) * pair_w]
                qh = jnp.where(low_q if h % 2 == 0 else jnp.logical_not(low_q), qp, jnp.zeros_like(qp))
                heads.append(_dot_nt(qh, st_bf[j]))
            o_inter.append(jnp.concatenate(heads, axis=-1))
            st = [st[j] * decay[i * c:i * c + 1, j * pair_w:(j + 1) * pair_w] + u_t[i][j]
                  for j in range(GLA_HEADS // 2)]
        for j in range(GLA_HEADS // 2):
            st_ref[j] = st[j]

        o = jnp.concatenate(o_intra, axis=0) + jnp.concatenate(o_inter, axis=0)
        g = g_ref[0, sl, :]
        silu = g / (1.0 + jnp.exp(-g))
        for h in range(GLA_HEADS):
            hs = slice(h * GLA_V_DIM, (h + 1) * GLA_V_DIM)
            out_ref[0, sl, hs] = (_rms(o[:, hs], gain_ref[...]) * silu[:, hs]).astype(BF16)
        return next_l

    lax.fori_loop(0, n_groups, body, cum_decay(0))


def _gla(gq, gk, la, gv, gg, gla_on):
    b, s, _ = gq.shape
    ts = TS_GLA
    tile = lambda width: pl.BlockSpec((1, ts, width), lambda i, j: (i, j, 0))
    return pl.pallas_call(
        functools.partial(_gla_kernel, ts=ts, group=GLA_GROUP),
        grid=(b, s // ts),
        in_specs=[tile(_GQK), tile(_GQK), tile(_GQK), tile(_GW), tile(_GW),
                  pl.BlockSpec((1, GLA_V_DIM), lambda i, j: (0, 0))],
        out_specs=tile(_GW),
        out_shape=jax.ShapeDtypeStruct((b, s, _GW), BF16),
        scratch_shapes=[pltpu.VMEM((GLA_HEADS // 2, GLA_V_DIM, 2 * GLA_K_DIM), F32)],
        compiler_params=_params("parallel", "arbitrary"),
        name="gla",
    )(gq, gk, la, gv, gg, gla_on.reshape(1, -1))


_META_E0, _META_E1, _META_G0, _META_G1, _META_P0, _META_P1 = range(6)
_EXP_LANE0 = N_GROUPS


def _post_kernel(*refs, d, tm, sub, rider):
    if rider:
        (size_ref, pend_ref, x_ref, da_ref, gla_ref, wo_ref, gc_ref, wq_ref, qn_ref, km_ref, vm_ref, wco_ref,
         gf_ref, wr_ref, br_ref, rdest_ref, rsrc_ref,
         h_ref, xn_ref, meta_ref, ids_ref, cnt_ref, rpad_ref, zero_ref, rsem, zsem) = refs
    else:
        (x_ref, da_ref, gla_ref, wo_ref, gc_ref, wq_ref, qn_ref, km_ref, vm_ref, wco_ref,
         gf_ref, wr_ref, br_ref, h_ref, xn_ref, meta_ref, ids_ref, cnt_ref) = refs
    first = jnp.logical_and(pl.program_id(0) == 0, pl.program_id(1) == 0)

    @pl.when(first)
    def _():
        cnt_ref[...] = jnp.zeros(cnt_ref.shape, F32)
        if rider:
            _zero_fill(size_ref, pend_ref, rpad_ref, zero_ref, zsem)

    if rider:
        _scatter_rows(rdest_ref, rsrc_ref, rpad_ref, rsem)

    half = d // 2
    hd = d // CROSS_HEADS
    lane = lax.broadcasted_iota(I32, (sub, LANES), 1)
    big = jnp.int32(LANES)
    strict_lower = jnp.where(lax.broadcasted_iota(I32, (sub, sub), 0) > lax.broadcasted_iota(I32, (sub, sub), 1),
                             1.0, 0.0).astype(BF16)

    def lane_argmax(vals):
        m = jnp.max(vals, axis=-1, keepdims=True)
        idx = jnp.min(jnp.where(vals == m, lane, big), axis=-1, keepdims=True)
        return m, idx

    groups = [slice(r0, r0 + sub) for r0 in range(0, tm, sub)]
    heads = [slice(h * hd, (h + 1) * hd) for h in range(CROSS_HEADS)]
    da = [jnp.concatenate([da_ref[0, h, rs, :] for h in range(DA_HEADS)], axis=-1) for rs in groups]
    h1 = [x_ref[0, rs, :] + _dot(da[g], wo_ref[:half, :]) + _dot(gla_ref[0, rs, :], wo_ref[half:, :])
          for g, rs in enumerate(groups)]

    u = [_rms(t, gc_ref[...]).astype(BF16) for t in h1]
    q = [_dot(t, wq_ref[...]) for t in u]
    qh = [[_rms(t[:, hs], qn_ref[...]).astype(BF16) for hs in heads] for t in q]
    sc = [[_dot_nt(t[h], km_ref[0, :, hs]) for h, hs in enumerate(heads)] for t in qh]
    pr = []
    for t in sc:
        e = [jnp.exp(v - jnp.max(v, axis=-1, keepdims=True)) for v in t]
        pr.append([(v / jnp.sum(v, axis=-1, keepdims=True)).astype(BF16) for v in e])
    o = [jnp.concatenate([_dot(t[h], vm_ref[0, :, hs]) for h, hs in enumerate(heads)], axis=-1).astype(BF16)
         for t in pr]
    h2 = [h1[g] + _dot(o[g], wco_ref[...]) for g in range(len(groups))]
    for g, rs in enumerate(groups):
        h_ref[0, rs, :] = h2[g]

    xn = [_rms(t, gf_ref[...]).astype(BF16) for t in h2]
    logits = [_dot(t, wr_ref[...]) + br_ref[...] for t in xn]
    base = cnt_ref[0:1, :]
    for g, rs in enumerate(groups):
        bits = lax.bitcast_convert_type(xn[g].astype(F32), U32)
        xn_ref[0, rs, :] = (bits[:, :half] >> 16) | (bits[:, half:] & HI16)

        lg = jnp.where(lane < N_GROUPS, logits[g], NEG_INF)
        g_max, g_sel = lane_argmax(lg)
        p_g = 1.0 / jnp.sum(jnp.exp(lg - g_max), axis=-1, keepdims=True)
        e_lo = _EXP_LANE0 + g_sel * EXPERTS_PER_GROUP
        in_group = jnp.logical_and(lane >= e_lo, lane < e_lo + EXPERTS_PER_GROUP)
        le = jnp.where(in_group, logits[g], NEG_INF)
        m1, i1 = lane_argmax(le)
        m2, i2 = lane_argmax(jnp.where(lane == i1, NEG_INF, le))
        e2 = jnp.exp(m2 - m1)
        gate0 = p_g / (1.0 + e2)
        gate1 = p_g * e2 / (1.0 + e2)
        e0 = i1 - _EXP_LANE0
        e1 = i2 - _EXP_LANE0

        hot0 = lane == e0
        hot1 = lane == e1
        onehot = jnp.where(jnp.logical_or(hot0, hot1), 1.0, 0.0)
        before = _dot(strict_lower, onehot.astype(BF16)) + base
        pos0 = jnp.sum(jnp.where(hot0, before, 0.0), axis=-1, keepdims=True)
        pos1 = jnp.sum(jnp.where(hot1, before, 0.0), axis=-1, keepdims=True)

        meta = jnp.zeros(logits[g].shape, F32)
        for idx, val in ((_META_E0, e0.astype(F32)), (_META_E1, e1.astype(F32)), (_META_G0, gate0),
                         (_META_G1, gate1), (_META_P0, pos0), (_META_P1, pos1)):
            meta = jnp.where(lane == idx, val, meta)
        meta_ref[0, rs, :] = meta
        ids_ref[0, 0, :, rs] = meta.T[:ids_ref.shape[2], :]
        base = base + jnp.sum(onehot, axis=0, keepdims=True)
    cnt_ref[...] = jnp.broadcast_to(base, cnt_ref.shape)
    if rider:
        _scatter_wait(rsrc_ref, rpad_ref, rsem)


def _post(x, da, gla, w_o, norm_cross, w_cq, cross_qn, k_mem, v_mem, w_co, norm_ffn, w_group, b_group,
          w_expert, b_expert, *, half, rider=None):
    b, s, d = x.shape
    bh = b // 2
    b0 = half * bh
    tm = TM_POST
    ns = s // tm
    m = k_mem.shape[1]
    w_r = jnp.pad(jnp.concatenate([w_group, w_expert], axis=1), ((0, 0), (0, LANES - N_GROUPS - N_EXPERTS)))
    b_r = jnp.pad(jnp.concatenate([b_group, b_expert]), (0, LANES - N_GROUPS - N_EXPERTS)).reshape(1, LANES)
    const = lambda shape: pl.BlockSpec(shape, lambda i, j, *_: (0,) * len(shape))
    tile_in = lambda width: pl.BlockSpec((1, tm, width), lambda i, j, *_: (i + b0, j, 0))
    tile_out = lambda width: pl.BlockSpec((1, tm, width), lambda i, j, *_: (i, j, 0))
    per_b = lambda: pl.BlockSpec((1, m, d), lambda i, j, *_: (i + b0, 0, 0))
    da_spec = pl.BlockSpec((1, DA_HEADS, tm, DA_V_DIM), lambda i, j, *_: (i + b0, 0, j, 0))
    in_specs = [tile_in(d), da_spec, tile_in(d // 2), const((d, d)), const((1, d)), const((d, d)),
                const((1, d // CROSS_HEADS)), per_b(), per_b(), const((d, d)), const((1, d)),
                const((d, LANES)), const((1, LANES))]
    out_specs = [tile_out(d), tile_out(d // 2), tile_out(LANES),
                 pl.BlockSpec((1, 1, 8, tm), lambda i, j, *_: (i, j, 0, 0)), const((8, LANES))]
    out_shape = [
        jax.ShapeDtypeStruct((bh, s, d), F32),
        jax.ShapeDtypeStruct((bh, s, d // 2), U32),
        jax.ShapeDtypeStruct((bh, s, LANES), F32),
        jax.ShapeDtypeStruct((bh, ns, 8, tm), F32),
        jax.ShapeDtypeStruct((8, LANES), F32),
    ]
    args = (x, da, gla, w_o.astype(BF16), norm_cross.reshape(1, d), w_cq.astype(BF16), cross_qn.reshape(1, -1),
            k_mem, v_mem, w_co.astype(BF16), norm_ffn.reshape(1, d), w_r.astype(BF16), b_r)
    scalars, scratch = (), []
    if rider is not None:
        sizes, pend, dest, xn_src, n_rows = rider
        scalars = (sizes, pend)
        in_specs += [pl.BlockSpec((1, 2, tm), lambda i, j, *_: (i * ns + j, 0, 0), memory_space=pltpu.SMEM),
                     pl.BlockSpec((tm, d // 2), lambda i, j, *_: (i * ns + j, 0))]
        out_specs.append(pl.BlockSpec(memory_space=pl.ANY))
        out_shape.append(jax.ShapeDtypeStruct((n_rows, d // 2), U32))
        scratch = [pltpu.VMEM((EXPERT_ROWS, d // 2), U32), pltpu.SemaphoreType.DMA, pltpu.SemaphoreType.DMA]
        args += (dest, xn_src)
    return pl.pallas_call(
        functools.partial(_post_kernel, d=d, tm=tm, sub=SUB_POST, rider=rider is not None),
        grid_spec=pltpu.PrefetchScalarGridSpec(
            num_scalar_prefetch=len(scalars), grid=(bh, ns), in_specs=in_specs, out_specs=out_specs,
            scratch_shapes=scratch),
        out_shape=out_shape,
        compiler_params=_params("arbitrary", "arbitrary"),
        name="post_scatter" if rider is not None else "post",
    )(*scalars, *args)


def _row_copy(src_ref, src_row, dst_ref, dst_row, sem):
    return pltpu.make_async_copy(src_ref.at[pl.ds(src_row, 1)], dst_ref.at[pl.ds(dst_row, 1)], sem)


def _zero_fill(size_ref, pend_ref, xpad_ref, zero_ref, zsem):
    zero_ref[...] = jnp.zeros(zero_ref.shape, zero_ref.dtype)
    rows = zero_ref.shape[0]
    n_blocks = xpad_ref.shape[0] // rows
    n_used = pend_ref[N_EXPERTS - 1] // rows

    def zero_block(blk):
        return pltpu.make_async_copy(zero_ref, xpad_ref.at[pl.ds(pl.multiple_of(blk * rows, rows), rows)], zsem)

    def last_block(e, fn):
        @pl.when(size_ref[e] > 0)
        def _():
            fn(zero_block(pend_ref[e] // rows - 1))

    for fn in (lambda cp: cp.start(), lambda cp: cp.wait()):
        lax.fori_loop(0, N_EXPERTS, lambda e, c: (last_block(e, fn), c)[1], 0)
        lax.fori_loop(n_used, n_blocks, lambda blk, c: (fn(zero_block(blk)), c)[1], 0)


def _scatter_rows(dest_ref, src_ref, xpad_ref, sem):
    for t in range(src_ref.shape[0]):
        for k in range(2):
            _row_copy(src_ref, t, xpad_ref, dest_ref[0, k, t], sem).start(priority=k)


def _scatter_wait(src_ref, xpad_ref, sem):
    for _ in range(2):
        pltpu.make_async_copy(src_ref, xpad_ref.at[pl.ds(0, src_ref.shape[0])], sem).wait()


def _cast_expert_weights(fresh, wg_ref, wu_ref, wd_ref, wg_s, wu_s, wd_s):
    @pl.when(fresh)
    def _():
        wg_s[...] = wg_ref[0].astype(BF16)
        wu_s[...] = wu_ref[0].astype(BF16)
        wd_s[...] = wd_ref[0].astype(BF16)


def _expert_mlp(x_ref, wg_s, wu_s, wd_s, out_ref):
    words = x_ref[...]
    half = words.shape[1]
    lo = lax.bitcast_convert_type(words << 16, F32).astype(BF16)
    hi = lax.bitcast_convert_type(words & HI16, F32).astype(BF16)
    gate = _dot(lo, wg_s[:half, :]) + _dot(hi, wg_s[half:, :])
    up = _dot(lo, wu_s[:half, :]) + _dot(hi, wu_s[half:, :])
    hid = gate / (1.0 + jnp.exp(-gate)) * up
    out_ref[...] = _dot(hid.astype(BF16), wd_s[...])


def _experts_scatter_kernel(be_ref, nused_ref, size_ref, pend_ref, x_ref, wg_ref, wu_ref, wd_ref, rdest_ref, rsrc_ref,
                            out_ref, rpad_ref, wg_s, wu_s, wd_s, zero_ref, rsem, zsem, *, rider_steps):
    i = pl.program_id(0)
    used = i < nused_ref[0]
    new_expert = jnp.logical_or(i == 0, be_ref[i] != be_ref[jnp.maximum(i - 1, 0)])
    mlp = functools.partial(_expert_mlp, x_ref, wg_s, wu_s, wd_s, out_ref)

    @pl.when(i == 0)
    def _():
        _zero_fill(size_ref, pend_ref, rpad_ref, zero_ref, zsem)

    _cast_expert_weights(jnp.logical_and(used, new_expert), wg_ref, wu_ref, wd_ref, wg_s, wu_s, wd_s)
    riding = i < rider_steps

    @pl.when(riding)
    def _():
        _scatter_rows(rdest_ref, rsrc_ref, rpad_ref, rsem)
        mlp()
        _scatter_wait(rsrc_ref, rpad_ref, rsem)

    pl.when(jnp.logical_and(used, jnp.logical_not(riding)))(mlp)

    @pl.when(jnp.logical_not(used))
    def _():
        out_ref[...] = jnp.zeros(out_ref.shape, F32)


def _gather_rows(dest_ref, src_ref, buf_ref, sem):
    for t in range(buf_ref.shape[1]):
        for k in range(2):
            _row_copy(src_ref, dest_ref[0, k, t], buf_ref.at[k], t, sem).start(priority=k)


def _combine_rows(h_ref, meta_ref, src_ref, buf_ref, sem):
    for k in range(2):
        pltpu.make_async_copy(src_ref.at[pl.ds(0, buf_ref.shape[1])], buf_ref.at[k], sem).wait()
    meta = meta_ref[...]
    g0 = meta[:, _META_G0:_META_G0 + 1]
    g1 = meta[:, _META_G1:_META_G1 + 1]
    return h_ref[...] + g0 * buf_ref[0] + g1 * buf_ref[1]


def _experts_gather_kernel(be_ref, nused_ref, x_ref, wg_ref, wu_ref, wd_ref, gdest_ref, h_ref, meta_ref, opad_ref,
                           out_ref, y_ref, wg_s, wu_s, wd_s, gbuf_ref, gsem, *, rider_steps):
    i = pl.program_id(0)
    used = i < nused_ref[0]
    new_expert = jnp.logical_or(i == 0, be_ref[i] != be_ref[jnp.maximum(i - 1, 0)])
    mlp = functools.partial(_expert_mlp, x_ref, wg_s, wu_s, wd_s, out_ref)
    _cast_expert_weights(jnp.logical_and(used, new_expert), wg_ref, wu_ref, wd_ref, wg_s, wu_s, wd_s)

    def gather():
        _gather_rows(gdest_ref, opad_ref, gbuf_ref.at[i % 2], gsem.at[i % 2])

    def combine():
        slot = (i - 1) % 2
        y_ref[...] = _combine_rows(h_ref, meta_ref, opad_ref, gbuf_ref.at[slot], gsem.at[slot])

    @pl.when(i == 0)
    def _():
        gather()
        mlp()

    @pl.when(jnp.logical_and(i > 0, i < rider_steps))
    def _():
        gather()
        mlp()
        combine()

    @pl.when(i == rider_steps)
    def _():
        combine()
        pl.when(used)(mlp)

    pl.when(jnp.logical_and(i > rider_steps, used))(mlp)

    @pl.when(jnp.logical_not(used))
    def _():
        out_ref[...] = jnp.zeros(out_ref.shape, F32)


def _experts_kernel(be_ref, nused_ref, x_ref, wg_ref, wu_ref, wd_ref, out_ref, wg_s, wu_s, wd_s):
    i = pl.program_id(0)
    used = i < nused_ref[0]
    new_expert = jnp.logical_or(i == 0, be_ref[i] != be_ref[jnp.maximum(i - 1, 0)])
    _cast_expert_weights(jnp.logical_and(used, new_expert), wg_ref, wu_ref, wd_ref, wg_s, wu_s, wd_s)
    pl.when(used)(functools.partial(_expert_mlp, x_ref, wg_s, wu_s, wd_s, out_ref))

    @pl.when(jnp.logical_not(used))
    def _():
        out_ref[...] = jnp.zeros(out_ref.shape, F32)


def _retile(dest, tr):
    tiles, _, tm = dest.shape
    return dest.reshape(tiles, 2, tm // tr, tr).transpose(0, 2, 1, 3).reshape(tiles * (tm // tr), 2, tr)


def _experts(x_pad, block_expert, n_used, w_gate, w_up, w_down, scatter=None, gather=None):
    n_rows = x_pad.shape[0]
    _, d, f = w_gate.shape
    rows, tr = EXPERT_ROWS, RIDER_ROWS
    row_blk = lambda i, be, nu, *_: (jnp.minimum(i, nu[0] - 1), 0)
    weights = lambda shape: pl.BlockSpec(shape, lambda i, be, *_: (be[i], 0, 0))
    in_specs = [pl.BlockSpec((rows, d // 2), row_blk), weights((1, d, f)), weights((1, d, f)), weights((1, f, d))]
    out_specs = [pl.BlockSpec((rows, d), lambda i, *_: (i, 0))]
    out_shape = [jax.ShapeDtypeStruct((n_rows, d), F32)]
    scratch = [pltpu.VMEM((d, f), BF16), pltpu.VMEM((d, f), BF16), pltpu.VMEM((f, d), BF16)]
    scalars, args = (block_expert, n_used), (x_pad, w_gate, w_up, w_down)
    body, name = _experts_kernel, "experts"
    if scatter is not None or gather is not None:
        n_tokens = (scatter[3] if scatter is not None else gather[1]).shape[0]
        steps = n_tokens // tr
        assert 2 * n_tokens >= steps * rows, "riding steps must all be used expert blocks"
        tile = lambda width, shift: pl.BlockSpec((tr, width), lambda i, *_: (jnp.clip(i - shift, 0, steps - 1), 0))
        ids = lambda: pl.BlockSpec((1, 2, tr), lambda i, *_: (jnp.minimum(i, steps - 1), 0, 0),
                                   memory_space=pltpu.SMEM)
    if scatter is not None:
        sizes, pend, dest, xn_src, other_rows = scatter
        scalars += (sizes, pend)
        in_specs += [ids(), tile(d // 2, 0)]
        out_specs.append(pl.BlockSpec(memory_space=pl.ANY))
        out_shape.append(jax.ShapeDtypeStruct((other_rows, d // 2), U32))
        scratch += [pltpu.VMEM((rows, d // 2), U32), pltpu.SemaphoreType.DMA, pltpu.SemaphoreType.DMA]
        args += (_retile(dest, tr), xn_src)
        body, name = functools.partial(_experts_scatter_kernel, rider_steps=steps), "experts_scatter"
    elif gather is not None:
        dest, h2, meta, opad = gather
        in_specs += [ids(), tile(d, 1), tile(LANES, 1), pl.BlockSpec(memory_space=pl.ANY)]
        out_specs.append(tile(d, 1))
        out_shape.append(jax.ShapeDtypeStruct((n_tokens, d), F32))
        scratch += [pltpu.VMEM((2, 2, tr, d), F32), pltpu.SemaphoreType.DMA((2,))]
        args += (_retile(dest, tr), h2, meta, opad)
        body, name = functools.partial(_experts_gather_kernel, rider_steps=steps), "experts_gather"
    out = pl.pallas_call(
        body,
        grid_spec=pltpu.PrefetchScalarGridSpec(
            num_scalar_prefetch=len(scalars), grid=(n_rows // rows,), in_specs=in_specs, out_specs=out_specs,
            scratch_shapes=scratch),
        out_shape=out_shape,
        compiler_params=_params("arbitrary"),
        name=name,
    )(*scalars, *args)
    return out if len(out) > 1 else out[0]


def _combine_tail_kernel(dest_ref, y0_ref, h_ref, meta_ref, opad_ref, y_ref, buf_ref, sem, *, nt):
    s = pl.program_id(0)

    @pl.when(s < nt)
    def _():
        _gather_rows(dest_ref, opad_ref, buf_ref.at[s % 2], sem.at[s % 2])

    @pl.when(s > 0)
    def _():
        slot = (s - 1) % 2
        y_ref[0] = y0_ref[...]
        y_ref[1] = _combine_rows(h_ref, meta_ref, opad_ref, buf_ref.at[slot], sem.at[slot])


def _combine_tail(y0, h2, meta, dest, out_pad):
    th, d = h2.shape
    tm = TM_TAIL
    nt = th // tm
    prev = lambda s: (jnp.maximum(s - 1, 0), 0)
    return pl.pallas_call(
        functools.partial(_combine_tail_kernel, nt=nt),
        grid=(nt + 1,),
        in_specs=[
            pl.BlockSpec((1, 2, tm), lambda s: (jnp.minimum(s, nt - 1), 0, 0), memory_space=pltpu.SMEM),
            pl.BlockSpec((tm, d), prev), pl.BlockSpec((tm, d), prev), pl.BlockSpec((tm, LANES), prev),
            pl.BlockSpec(memory_space=pl.ANY),
        ],
        out_specs=pl.BlockSpec((2, tm, d), lambda s: (0, jnp.maximum(s - 1, 0), 0)),
        scratch_shapes=[pltpu.VMEM((2, 2, tm, d), F32), pltpu.SemaphoreType.DMA((2,))],
        out_shape=jax.ShapeDtypeStruct((2, th, d), F32),
        compiler_params=_params("arbitrary"),
        name="combine_tail",
    )(_retile(dest, tm), y0, h2, meta, out_pad)


def _route_tables(counts, ids, n_tokens):
    rows = EXPERT_ROWS
    sizes = counts[0, :N_EXPERTS].astype(I32)
    padded = (sizes + rows - 1) // rows * rows
    pend = jnp.cumsum(padded)
    pstart = pend - padded
    n_rows = 2 * n_tokens + N_EXPERTS * rows
    block_start = jnp.arange(n_rows // rows, dtype=I32) * rows
    block_expert = jnp.minimum(jnp.sum(pend[None, :] <= block_start[:, None], axis=1), N_EXPERTS - 1).astype(I32)
    n_used = (pend[-1:] // rows).astype(I32)
    expert = ids[:, _META_E0:_META_E1 + 1].astype(I32)
    rank = ids[:, _META_P0:_META_P1 + 1].astype(I32)
    experts = jnp.arange(N_EXPERTS, dtype=I32).reshape(-1, 1, 1, 1)
    dest = jnp.sum(jnp.where(expert[None] == experts, pstart.reshape(-1, 1, 1, 1), 0), axis=0) + rank
    return sizes, pend, block_expert, n_used, dest, n_rows


def kernel(x, mem, norm_mix, w_in, da_q_norm, da_k_norm, lambda_q1, lambda_k1, lambda_q2, lambda_k2,
           da_out_norm, gla_gate_w, gla_gate_b, gla_out_norm, w_o, norm_cross, norm_mem, w_cq, w_ckv,
           cross_q_norm, cross_k_norm, w_co, norm_ffn, w_group, b_group, w_expert, b_expert,
           w_e_gate, w_e_up, w_e_down):
    b, s, d = x.shape
    th = b // 2 * s
    h = x
    for l in range(norm_mix.shape[0]):
        assert l == 0, "lam_init is fixed for a single layer"
        qt, kda, vt, gq, gk, gv, gg, la = _in_proj(h, norm_mix[l], w_in[l], da_q_norm[l], da_k_norm[l],
                                                   gla_gate_w[l], gla_gate_b[l])
        da = _diff_attn(qt, kda, vt, lambda_q1[l], lambda_k1[l], lambda_q2[l], lambda_k2[l], da_out_norm[l],
                        da_q_norm[l], da_k_norm[l])
        gla = _gla(gq, gk, la, gv, gg, gla_out_norm[l])
        k_mem, v_mem = _mem_kv(mem, norm_mem[l], w_ckv[l], cross_k_norm[l])
        post = functools.partial(_post, h, da, gla, w_o[l], norm_cross[l], w_cq[l], cross_q_norm[l], k_mem, v_mem,
                                 w_co[l], norm_ffn[l], w_group[l], b_group[l], w_expert[l], b_expert[l])
        experts = functools.partial(_experts, w_gate=w_e_gate[l], w_up=w_e_up[l], w_down=w_e_down[l])

        h2_0, xn_0, meta_0, ids_0, counts_0 = post(half=0)
        sizes_0, pend_0, be_0, used_0, dest_0, n_rows = _route_tables(counts_0, ids_0.reshape(-1, 8, TM_POST), th)
        h2_1, xn_1, meta_1, ids_1, counts_1, xpad_0 = post(
            half=1, rider=(sizes_0, pend_0, dest_0, xn_0.reshape(th, d // 2), n_rows))
        sizes_1, pend_1, be_1, used_1, dest_1, _ = _route_tables(counts_1, ids_1.reshape(-1, 8, TM_POST), th)
        opad_0, xpad_1 = experts(xpad_0, be_0, used_0,
                                 scatter=(sizes_1, pend_1, dest_1, xn_1.reshape(th, d // 2), n_rows))
        opad_1, y_0 = experts(xpad_1, be_1, used_1,
                              gather=(dest_0, h2_0.reshape(th, d), meta_0.reshape(th, LANES), opad_0))
        h = _combine_tail(y_0, h2_1.reshape(th, d), meta_1.reshape(th, LANES), dest_1, opad_1).reshape(b, s, d)
    return h
```

```python
import functools
import math

import jax
import jax.numpy as jnp
import numpy as np
from jax import lax
from jax.experimental import pallas as pl
from jax.experimental.pallas import tpu as pltpu

F32 = jnp.float32
BF16 = jnp.bfloat16
I32 = jnp.int32
U32 = jnp.uint32
HI16 = np.uint32(0xFFFF0000)

EPS = 1e-6
CHUNK = 64

DA_HEADS = 4
DA_QK_DIM = 64
DA_V_DIM = 128
GLA_HEADS = 4
GLA_K_DIM = 64
GLA_V_DIM = 128
GLA_GATE_RANK = 16
GLA_TAU = 16.0
CROSS_HEADS = 4
N_GROUPS = 4
EXPERTS_PER_GROUP = 8
N_EXPERTS = N_GROUPS * EXPERTS_PER_GROUP
LAM_INIT = 0.8 - 0.6 * math.exp(-0.3 * 0)

LANES = 128
VMEM_LIMIT = 56 * 1024 * 1024

TM_PROJ = 1024
ATT_BLK = 512
TS_GLA = 1024
GLA_GROUP = 4
TM_POST = 1024
SUB_POST = 256
TM_TAIL = 512
RIDER_ROWS = 256
EXPERT_ROWS = 512

NEG_INF = float("-inf")


def _params(*sem):
    return pltpu.CompilerParams(dimension_semantics=sem, vmem_limit_bytes=VMEM_LIMIT)


def _rms(t, g):
    ms = jnp.mean(t * t, axis=-1, keepdims=True)
    return t * lax.rsqrt(ms + EPS) * g


def _dot(a, b):
    return jnp.dot(a, b, preferred_element_type=F32)


def _dot_nt(a, b):
    return lax.dot_general(a, b, (((1,), (1,)), ((), ())), preferred_element_type=F32)


def _dot_tn(a, b):
    return lax.dot_general(a, b, (((0,), (0,)), ((), ())), preferred_element_type=F32)


def _split_bf16(t):
    hi = t.astype(BF16)
    lo = (t - hi.astype(F32)).astype(BF16)
    return hi, lo


def _mem_kv_kernel(mem_ref, g_ref, w_ref, kn_ref, k_ref, v_ref, *, d, heads):
    mn = _rms(mem_ref[0], g_ref[...]).astype(BF16)
    kv = _dot(mn, w_ref[...])
    hd = d // heads
    scale = hd ** -0.5
    for h in range(heads):
        kh = _rms(kv[:, h * hd:(h + 1) * hd], kn_ref[...]) * scale
        k_ref[0, :, h * hd:(h + 1) * hd] = kh.astype(BF16)
    v_ref[0] = kv[:, d:].astype(BF16)


def _mem_kv(mem, norm_m, w_ckv, kn):
    b, m, d = mem.shape
    return pl.pallas_call(
        functools.partial(_mem_kv_kernel, d=d, heads=CROSS_HEADS),
        grid=(b,),
        in_specs=[
            pl.BlockSpec((1, m, d), lambda i: (i, 0, 0)),
            pl.BlockSpec((1, d), lambda i: (0, 0)),
            pl.BlockSpec((d, 2 * d), lambda i: (0, 0)),
            pl.BlockSpec((1, d // CROSS_HEADS), lambda i: (0, 0)),
        ],
        out_specs=[
            pl.BlockSpec((1, m, d), lambda i: (i, 0, 0)),
            pl.BlockSpec((1, m, d), lambda i: (i, 0, 0)),
        ],
        out_shape=[jax.ShapeDtypeStruct((b, m, d), BF16)] * 2,
        compiler_params=_params("parallel"),
        name="mem_kv",
    )(mem, norm_m.reshape(1, d), w_ckv.astype(BF16), kn.reshape(1, -1))


_QK = DA_HEADS * 2 * DA_QK_DIM
_DAW = DA_HEADS * DA_V_DIM
_GQK = GLA_HEADS * GLA_K_DIM
_GW = GLA_HEADS * GLA_V_DIM
_OFF_DQ = 0
_OFF_DK = _OFF_DQ + _QK
_OFF_DV = _OFF_DK + _QK
_OFF_GQ = _OFF_DV + _DAW
_OFF_GK = _OFF_GQ + _GQK
_OFF_GV = _OFF_GK + _GQK
_OFF_GG = _OFF_GV + _GW
_OFF_GR = _OFF_GG + _GW
_IN_PAD = _OFF_GR + LANES


def _in_proj_kernel(x_ref, g_ref, w_ref, qg_ref, kg_ref, grp_ref, gw_ref, gb_ref,
                    qt_ref, k_ref, vt_ref, gq_ref, gk_ref, gv_ref, gg_ref, la_ref):
    u = _rms(x_ref[0], g_ref[...]).astype(BF16)

    def proj(off, width):
        return _dot(u, w_ref[:, off:off + width])

    def mean_square(p):
        return _dot((p * p).astype(BF16), grp_ref[...])

    p_q = proj(_OFF_DQ, _QK)
    p_k = proj(_OFF_DK, _QK)
    ms_q = mean_square(p_q)
    qn = p_q * lax.rsqrt(ms_q + EPS) * qg_ref[...] * (DA_QK_DIM ** -0.5 * math.log2(math.e))
    blk = qt_ref.shape[3]
    for t in range(qt_ref.shape[1]):
        qt_ref[0, t] = qn[t * blk:(t + 1) * blk].T.astype(BF16)
    dv = proj(_OFF_DV, _DAW)
    ms_k = mean_square(p_k)
    kn = (p_k * lax.rsqrt(ms_k + EPS) * kg_ref[...]).astype(BF16)
    for h in range(DA_HEADS):
        k_ref[0, h] = kn[:, h * 2 * DA_QK_DIM:(h + 1) * 2 * DA_QK_DIM]
    for t in range(vt_ref.shape[1]):
        vt_ref[0, t] = dv[t * blk:(t + 1) * blk].T.astype(BF16)
    g_r = proj(_OFF_GR, LANES)
    gq_ref[0] = proj(_OFF_GQ, _GQK) * (GLA_K_DIM ** -0.5)
    gk_ref[0] = proj(_OFF_GK, _GQK)
    z = _dot(g_r.astype(BF16), gw_ref[...]) + gb_ref[...]
    gv_ref[0] = proj(_OFF_GV, _GW).astype(BF16)
    gg_ref[0] = proj(_OFF_GG, _GW)
    log_sig = jnp.minimum(z, 0.0) - jnp.log(1.0 + jnp.exp(-jnp.abs(z)))
    la_ref[0] = log_sig * (math.log2(math.e) / GLA_TAU)


def _in_proj(x, norm_g, w_in, da_qn, da_kn, gate_w, gate_b):
    b, s, d = x.shape
    tm, blk = TM_PROJ, ATT_BLK
    ns, nb = s // tm, tm // blk
    w = jnp.pad(w_in, ((0, 0), (0, _IN_PAD - w_in.shape[1]))).astype(BF16)
    gw = jnp.pad(gate_w, ((0, LANES - GLA_GATE_RANK), (0, 0))).astype(BF16)
    lane = jnp.arange(_QK)
    grp = jnp.where((lane[:, None] // DA_QK_DIM) == (lane[None, :] // DA_QK_DIM),
                    1.0 / DA_QK_DIM, 0.0).astype(BF16)
    const = lambda shape: pl.BlockSpec(shape, lambda i, j: (0,) * len(shape))
    tile = lambda width: pl.BlockSpec((1, tm, width), lambda i, j: (i, j, 0))
    tile_t = lambda width: pl.BlockSpec((1, nb, width, blk), lambda i, j: (i, j, 0, 0))
    return pl.pallas_call(
        _in_proj_kernel,
        grid=(b, ns),
        in_specs=[tile(d), const((1, d)), const((d, _IN_PAD)), const((1, _QK)), const((1, _QK)),
                  const((_QK, _QK)), const((LANES, _GQK)), const((1, _GQK))],
        out_specs=[tile_t(_QK), pl.BlockSpec((1, DA_HEADS, tm, 2 * DA_QK_DIM), lambda i, j: (i, 0, j, 0)),
                   tile_t(_DAW), tile(_GQK), tile(_GQK), tile(_GW), tile(_GW), tile(_GQK)],
        out_shape=[
            jax.ShapeDtypeStruct((b, s // blk, _QK, blk), BF16),
            jax.ShapeDtypeStruct((b, DA_HEADS, s, 2 * DA_QK_DIM), BF16),
            jax.ShapeDtypeStruct((b, s // blk, _DAW, blk), BF16),
            jax.ShapeDtypeStruct((b, s, _GQK), F32),
            jax.ShapeDtypeStruct((b, s, _GQK), F32),
            jax.ShapeDtypeStruct((b, s, _GW), BF16),
            jax.ShapeDtypeStruct((b, s, _GW), F32),
            jax.ShapeDtypeStruct((b, s, _GQK), F32),
        ],
        compiler_params=_params("parallel", "parallel"),
        name="in_proj",
    )(x, norm_g.reshape(1, d), w, jnp.tile(da_qn, 2 * DA_HEADS).reshape(1, _QK),
      jnp.tile(da_kn, 2 * DA_HEADS).reshape(1, _QK), grp, gw, gate_b.reshape(1, _GQK))


def _split_q(qt):
    row = lax.broadcasted_iota(I32, qt.shape, 0)
    zero = jnp.zeros_like(qt)
    return jnp.where(row < DA_QK_DIM, qt, zero), jnp.where(row >= DA_QK_DIM, qt, zero)


def _chunk_causal_mask(blk):
    key_chunk = lax.broadcasted_iota(I32, (blk, blk), 0) // CHUNK
    qry_chunk = lax.broadcasted_iota(I32, (blk, blk), 1) // CHUNK
    return key_chunk <= qry_chunk


def _diff_attn_finish(lq1_ref, lk1_ref, lq2_ref, lk2_ref, gain_ref, a1, l1, a2, l2):
    lam = (jnp.exp(jnp.sum(lq1_ref[...] * lk1_ref[...], axis=-1, keepdims=True))
           - jnp.exp(jnp.sum(lq2_ref[...] * lk2_ref[...], axis=-1, keepdims=True)) + LAM_INIT)
    o = a1 / l1 - lam * (a2 / l2)
    ms = jnp.mean(o * o, axis=0, keepdims=True)
    o = o * lax.rsqrt(ms + EPS) * gain_ref[...] * (1.0 - LAM_INIT)
    return o.T.astype(BF16)


def _diff_attn_bounded_kernel(lq1_ref, lk1_ref, lq2_ref, lk2_ref, gain_ref, qt_ref, k_ref, vt_ref, out_ref,
                              s_ref, l1_ref, a1_ref, l2_ref, a2_ref, *, blk, nb):
    stats = ((l1_ref, a1_ref), (l2_ref, a2_ref))
    mask = _chunk_causal_mask(blk)

    def reset():
        for l_ref, a_ref in stats:
            l_ref[...] = jnp.zeros(l_ref.shape, F32)
            a_ref[...] = jnp.zeros(a_ref.shape, F32)

    def scores(q, j, slot):
        kb = k_ref[0, 0, pl.ds(pl.multiple_of(j * blk, blk), blk), :]
        s_ref[slot, 0] = _dot(kb, q[0])
        s_ref[slot, 1] = _dot(kb, q[1])

    def consume(j, slot, masked):
        vb = vt_ref[0, j]
        for m, (l_ref, a_ref) in enumerate(stats):
            s = s_ref[slot, m]
            if masked:
                s = jnp.where(mask, s, NEG_INF)
            p = jnp.exp2(s)
            l_ref[...] += jnp.sum(p, axis=0, keepdims=True)
            a_ref[...] += _dot(vb, p.astype(BF16))

    def step(q, j, slot):
        scores(q, j + 1, 1 - slot)
        consume(j, slot, False)

    reset()
    q = _split_q(qt_ref[0, 0])
    slot = 0
    scores(q, 0, slot)
    for qi in range(nb):
        for j in range(qi):
            step(q, j, slot)
            slot = 1 - slot
        if qi + 1 < nb:
            q = _split_q(qt_ref[0, qi + 1])
            scores(q, 0, 1 - slot)
        consume(qi, slot, True)
        out_ref[0, 0, qi * blk:(qi + 1) * blk, :] = _diff_attn_finish(
            lq1_ref, lk1_ref, lq2_ref, lk2_ref, gain_ref, a1_ref[...], l1_ref[...], a2_ref[...], l2_ref[...])
        if qi + 1 < nb:
            reset()
        slot = 1 - slot


def _diff_attn_online_kernel(lq1_ref, lk1_ref, lq2_ref, lk2_ref, gain_ref, qt_ref, k_ref, vt_ref, out_ref,
                             m1_ref, l1_ref, a1_ref, m2_ref, l2_ref, a2_ref, *, blk):
    qi = pl.program_id(2)
    q1, q2 = _split_q(qt_ref[0, 0])

    for m_ref, l_ref, a_ref in ((m1_ref, l1_ref, a1_ref), (m2_ref, l2_ref, a2_ref)):
        m_ref[...] = jnp.full(m_ref.shape, NEG_INF, F32)
        l_ref[...] = jnp.zeros(l_ref.shape, F32)
        a_ref[...] = jnp.zeros(a_ref.shape, F32)

    def update(s, vb, m_ref, l_ref, a_ref):
        m_old = m_ref[...]
        m_new = jnp.maximum(m_old, jnp.max(s, axis=0, keepdims=True))
        alpha = jnp.exp2(m_old - m_new)
        p = jnp.exp2(s - m_new)
        l_ref[...] = alpha * l_ref[...] + jnp.sum(p, axis=0, keepdims=True)
        a_ref[...] = alpha * a_ref[...] + _dot(vb, p.astype(BF16))
        m_ref[...] = m_new

    def block(j, mask):
        kb = k_ref[0, 0, pl.ds(pl.multiple_of(j * blk, blk), blk), :]
        vb = vt_ref[0, j]
        s1 = _dot(kb, q1)
        s2 = _dot(kb, q2)
        if mask is not None:
            s1 = jnp.where(mask, s1, NEG_INF)
            s2 = jnp.where(mask, s2, NEG_INF)
        update(s1, vb, m1_ref, l1_ref, a1_ref)
        update(s2, vb, m2_ref, l2_ref, a2_ref)

    def body(j, carry):
        block(j, None)
        return carry

    lax.fori_loop(0, qi, body, 0)
    block(qi, _chunk_causal_mask(blk))
    out_ref[0, 0] = _diff_attn_finish(lq1_ref, lk1_ref, lq2_ref, lk2_ref, gain_ref,
                                      a1_ref[...], l1_ref[...], a2_ref[...], l2_ref[...])


SCORE_BOUND = 60.0


def _diff_attn(qt, k, vt, lq1, lk1, lq2, lk2, da_on, da_qn, da_kn):
    b, nb, _, blk = qt.shape
    s = nb * blk
    stat = lambda: pltpu.VMEM((1, blk), F32)
    acc = lambda: pltpu.VMEM((DA_V_DIM, blk), F32)

    args = (lq1.reshape(1, -1), lk1.reshape(1, -1), lq2.reshape(1, -1), lk2.reshape(1, -1),
            da_on.reshape(-1, 1), qt, k, vt)
    out_shape = jax.ShapeDtypeStruct((b, DA_HEADS, s, DA_V_DIM), BF16)
    head = lambda *trailing: (lambda i, h: (i, 0, h) + trailing)
    vec2 = lambda: pl.BlockSpec((1, DA_QK_DIM), lambda i, h: (0, 0))
    bounded = pl.pallas_call(
        functools.partial(_diff_attn_bounded_kernel, blk=blk, nb=nb),
        grid=(b, DA_HEADS),
        in_specs=[
            vec2(), vec2(), vec2(), vec2(),
            pl.BlockSpec((DA_V_DIM, 1), lambda i, h: (0, 0)),
            pl.BlockSpec((1, nb, 2 * DA_QK_DIM, blk), head(0)),
            pl.BlockSpec((1, 1, s, 2 * DA_QK_DIM), lambda i, h: (i, h, 0, 0)),
            pl.BlockSpec((1, nb, DA_V_DIM, blk), head(0)),
        ],
        out_specs=pl.BlockSpec((1, 1, s, DA_V_DIM), lambda i, h: (i, h, 0, 0)),
        out_shape=out_shape,
        scratch_shapes=[pltpu.VMEM((2, 2, blk, blk), F32), stat(), acc(), stat(), acc()],
        compiler_params=_params("parallel", "parallel"),
        name="diff_attn",
    )
    vec3 = lambda: pl.BlockSpec((1, DA_QK_DIM), lambda i, h, q: (0, 0))
    online = pl.pallas_call(
        functools.partial(_diff_attn_online_kernel, blk=blk),
        grid=(b, DA_HEADS, nb),
        in_specs=[
            vec3(), vec3(), vec3(), vec3(),
            pl.BlockSpec((DA_V_DIM, 1), lambda i, h, q: (0, 0)),
            pl.BlockSpec((1, 1, 2 * DA_QK_DIM, blk), lambda i, h, q: (i, q, h, 0)),
            pl.BlockSpec((1, 1, s, 2 * DA_QK_DIM), lambda i, h, q: (i, h, 0, 0)),
            pl.BlockSpec((1, nb, DA_V_DIM, blk), lambda i, h, q: (i, 0, h, 0)),
        ],
        out_specs=pl.BlockSpec((1, 1, blk, DA_V_DIM), lambda i, h, q: (i, h, q, 0)),
        out_shape=out_shape,
        scratch_shapes=[stat(), stat(), acc(), stat(), stat(), acc()],
        compiler_params=_params("parallel", "parallel", "parallel"),
        name="diff_attn_online",
    )
    bound = (1.01 * DA_QK_DIM ** 0.5 * math.log2(math.e)) * jnp.max(jnp.abs(da_qn)) * jnp.max(jnp.abs(da_kn))
    return lax.cond(bound <= SCORE_BOUND, bounded, online, *args)


def _gla_kernel(q_ref, k_ref, la_ref, v_ref, g_ref, gain_ref, out_ref, st_ref, *, ts, group):
    @pl.when(pl.program_id(1) == 0)
    def _():
        st_ref[...] = jnp.zeros(st_ref.shape, F32)

    c = CHUNK
    rows = group * c
    hk, hv = _GQK, _GW
    r = lax.broadcasted_iota(I32, (rows, rows), 0)
    cc = lax.broadcasted_iota(I32, (rows, rows), 1)
    tri = jnp.where(jnp.logical_and(r // c == cc // c, r >= cc), 1.0, 0.0).astype(BF16)
    bd_k = (lax.broadcasted_iota(I32, (hk, hk), 0) // GLA_K_DIM
            == lax.broadcasted_iota(I32, (hk, hk), 1) // GLA_K_DIM)
    bd_v = (lax.broadcasted_iota(I32, (hk, hv), 0) // GLA_K_DIM
            == lax.broadcasted_iota(I32, (hk, hv), 1) // GLA_V_DIM)
    pair_w = 2 * GLA_K_DIM
    low_sq = lax.broadcasted_iota(I32, (GLA_V_DIM, pair_w), 1) < GLA_K_DIM
    low_q = lax.broadcasted_iota(I32, (c, pair_w), 1) < GLA_K_DIM
    lower = (lax.broadcasted_iota(I32, (c, hk), 0)
             >= lax.broadcasted_iota(I32, (c, hk), 1) % c)

    def tiled(t, mask):
        t4 = jnp.concatenate([t] * GLA_HEADS, axis=0)
        return jnp.where(mask, t4, jnp.zeros_like(t4))

    def chunk_row(t, row):
        return jnp.concatenate([jnp.broadcast_to(t[i * c + row:i * c + row + 1, :], (c, hk)) for i in range(group)],
                               axis=0)

    def cum_decay(gi):
        la_hi, la_lo = _split_bf16(la_ref[0, pl.ds(pl.multiple_of(gi * rows, rows), rows), :])
        return _dot(tri, la_hi) + _dot(tri, la_lo)

    n_groups = ts // rows

    def body(gi, big_l):
        sl = pl.ds(pl.multiple_of(gi * rows, rows), rows)
        l_end = chunk_row(big_l, c - 1)
        lc = big_l - chunk_row(big_l, c // 2 - 1)
        e_pos = jnp.exp2(lc)
        e_neg = jnp.exp2(-lc)
        q = q_ref[0, sl, :]
        k = k_ref[0, sl, :]
        v = v_ref[0, sl, :]
        q_pos = (q * e_pos).astype(BF16)
        q_neg = (q * e_neg).astype(BF16)
        k_pos = (k * e_pos).astype(BF16)
        k_neg = (k * e_neg).astype(BF16)
        q_in = (q * jnp.exp2(big_l)).astype(BF16)
        k_out = (k * jnp.exp2(l_end - big_l)).astype(BF16)
        decay = jnp.exp2(l_end)

        chunks = [slice(i * c, (i + 1) * c) for i in range(group)]
        a_past = [_dot_nt(q_pos[cs], tiled(k_neg[cs], bd_k)) for cs in chunks]
        a_fut = [_dot_nt(q_neg[cs], tiled(k_pos[cs], bd_k)) for cs in chunks]
        next_l = cum_decay(jnp.minimum(gi + 1, n_groups - 1))
        inc = [[_dot_tn(v[cs, h * GLA_V_DIM:(h + 1) * GLA_V_DIM], k_out[cs, h // 2 * pair_w:(h // 2 + 1) * pair_w])
                for h in range(GLA_HEADS)] for cs in chunks]
        u_t = [[jnp.where(low_sq, r[2 * j], r[2 * j + 1]) for j in range(GLA_HEADS // 2)] for r in inc]
        a = [jnp.where(lower, p, f).astype(BF16) for p, f in zip(a_past, a_fut)]
        o_intra = [_dot(a[i], tiled(v[cs], bd_v)) for i, cs in enumerate(chunks)]

        st = [st_ref[j] for j in range(GLA_HEADS // 2)]
        o_inter = []
        for i in range(group):
            cs = slice(i * c, (i + 1) * c)
            st_bf = [t.astype(BF16) for t in st]
            heads = []
            for h in range(GLA_HEADS):
                j = h // 2
                qp = q_in[cs, j * pair_w:(j + 1) * pair_w]
                qh = jnp.where(low_q if h % 2 == 0 else jnp.logical_not(low_q), qp, jnp.zeros_like(qp))
                heads.append(_dot_nt(qh, st_bf[j]))
            o_inter.append(jnp.concatenate(heads, axis=-1))
            st = [st[j] * decay[i * c:i * c + 1, j * pair_w:(j + 1) * pair_w] + u_t[i][j]
                  for j in range(GLA_HEADS // 2)]
        for j in range(GLA_HEADS // 2):
            st_ref[j] = st[j]

        o = jnp.concatenate(o_intra, axis=0) + jnp.concatenate(o_inter, axis=0)
        g = g_ref[0, sl, :]
        silu = g / (1.0 + jnp.exp(-g))
        for h in range(GLA_HEADS):
            hs = slice(h * GLA_V_DIM, (h + 1) * GLA_V_DIM)
            out_ref[0, sl, hs] = (_rms(o[:, hs], gain_ref[...]) * silu[:, hs]).astype(BF16)
        return next_l

    big_l = cum_decay(0)
    for gi in range(n_groups):
        big_l = body(gi, big_l)


def _gla(gq, gk, la, gv, gg, gla_on):
    b, s, _ = gq.shape
    ts = TS_GLA
    tile = lambda width: pl.BlockSpec((1, ts, width), lambda i, j: (i, j, 0))
    return pl.pallas_call(
        functools.partial(_gla_kernel, ts=ts, group=GLA_GROUP),
        grid=(b, s // ts),
        in_specs=[tile(_GQK), tile(_GQK), tile(_GQK), tile(_GW), tile(_GW),
                  pl.BlockSpec((1, GLA_V_DIM), lambda i, j: (0, 0))],
        out_specs=tile(_GW),
        out_shape=jax.ShapeDtypeStruct((b, s, _GW), BF16),
        scratch_shapes=[pltpu.VMEM((GLA_HEADS // 2, GLA_V_DIM, 2 * GLA_K_DIM), F32)],
        compiler_params=_params("parallel", "arbitrary"),
        name="gla",
    )(gq, gk, la, gv, gg, gla_on.reshape(1, -1))


_META_E0, _META_E1, _META_G0, _META_G1, _META_P0, _META_P1 = range(6)
_EXP_LANE0 = N_GROUPS


def _post_kernel(*refs, d, tm, sub, rider):
    if rider:
        (size_ref, pend_ref, x_ref, da_ref, gla_ref, wo_ref, gc_ref, wq_ref, qn_ref, km_ref, vm_ref, wco_ref,
         gf_ref, wr_ref, br_ref, rdest_ref, rsrc_ref,
         h_ref, xn_ref, meta_ref, ids_ref, cnt_ref, rpad_ref, zero_ref, rsem, zsem) = refs
    else:
        (x_ref, da_ref, gla_ref, wo_ref, gc_ref, wq_ref, qn_ref, km_ref, vm_ref, wco_ref,
         gf_ref, wr_ref, br_ref, h_ref, xn_ref, meta_ref, ids_ref, cnt_ref) = refs
    first = jnp.logical_and(pl.program_id(0) == 0, pl.program_id(1) == 0)

    @pl.when(first)
    def _():
        cnt_ref[...] = jnp.zeros(cnt_ref.shape, F32)
        if rider:
            _zero_fill(size_ref, pend_ref, rpad_ref, zero_ref, zsem)

    if rider:
        _scatter_rows(rdest_ref, rsrc_ref, rpad_ref, rsem)

    half = d // 2
    hd = d // CROSS_HEADS
    lane = lax.broadcasted_iota(I32, (sub, LANES), 1)
    big = jnp.int32(LANES)
    strict_lower = jnp.where(lax.broadcasted_iota(I32, (sub, sub), 0) > lax.broadcasted_iota(I32, (sub, sub), 1),
                             1.0, 0.0).astype(BF16)

    def lane_argmax(vals):
        m = jnp.max(vals, axis=-1, keepdims=True)
        idx = jnp.min(jnp.where(vals == m, lane, big), axis=-1, keepdims=True)
        return m, idx

    groups = [slice(r0, r0 + sub) for r0 in range(0, tm, sub)]
    heads = [slice(h * hd, (h + 1) * hd) for h in range(CROSS_HEADS)]
    da = [jnp.concatenate([da_ref[0, h, rs, :] for h in range(DA_HEADS)], axis=-1) for rs in groups]
    h1 = [x_ref[0, rs, :] + _dot(da[g], wo_ref[:half, :]) + _dot(gla_ref[0, rs, :], wo_ref[half:, :])
          for g, rs in enumerate(groups)]

    u = [_rms(t, gc_ref[...]).astype(BF16) for t in h1]
    q = [_dot(t, wq_ref[...]) for t in u]
    qh = [[_rms(t[:, hs], qn_ref[...]).astype(BF16) for hs in heads] for t in q]
    sc = [[_dot_nt(t[h], km_ref[0, :, hs]) for h, hs in enumerate(heads)] for t in qh]
    pr = []
    for t in sc:
        e = [jnp.exp(v - jnp.max(v, axis=-1, keepdims=True)) for v in t]
        pr.append([(v / jnp.sum(v, axis=-1, keepdims=True)).astype(BF16) for v in e])
    o = [jnp.concatenate([_dot(t[h], vm_ref[0, :, hs]) for h, hs in enumerate(heads)], axis=-1).astype(BF16)
         for t in pr]
    h2 = [h1[g] + _dot(o[g], wco_ref[...]) for g in range(len(groups))]
    for g, rs in enumerate(groups):
        h_ref[0, rs, :] = h2[g]

    xn = [_rms(t, gf_ref[...]).astype(BF16) for t in h2]
    logits = [_dot(t, wr_ref[...]) + br_ref[...] for t in xn]
    base = cnt_ref[0:1, :]
    for g, rs in enumerate(groups):
        bits = lax.bitcast_convert_type(xn[g].astype(F32), U32)
        xn_ref[0, rs, :] = (bits[:, :half] >> 16) | (bits[:, half:] & HI16)

        lg = jnp.where(lane < N_GROUPS, logits[g], NEG_INF)
        g_max, g_sel = lane_argmax(lg)
        p_g = 1.0 / jnp.sum(jnp.exp(lg - g_max), axis=-1, keepdims=True)
        e_lo = _EXP_LANE0 + g_sel * EXPERTS_PER_GROUP
        in_group = jnp.logical_and(lane >= e_lo, lane < e_lo + EXPERTS_PER_GROUP)
        le = jnp.where(in_group, logits[g], NEG_INF)
        m1, i1 = lane_argmax(le)
        m2, i2 = lane_argmax(jnp.where(lane == i1, NEG_INF, le))
        e2 = jnp.exp(m2 - m1)
        gate0 = p_g / (1.0 + e2)
        gate1 = p_g * e2 / (1.0 + e2)
        e0 = i1 - _EXP_LANE0
        e1 = i2 - _EXP_LANE0

        hot0 = lane == e0
        hot1 = lane == e1
        onehot = jnp.where(jnp.logical_or(hot0, hot1), 1.0, 0.0)
        before = _dot(strict_lower, onehot.astype(BF16)) + base
        pos0 = jnp.sum(jnp.where(hot0, before, 0.0), axis=-1, keepdims=True)
        pos1 = jnp.sum(jnp.where(hot1, before, 0.0), axis=-1, keepdims=True)

        meta = jnp.zeros(logits[g].shape, F32)
        for idx, val in ((_META_E0, e0.astype(F32)), (_META_E1, e1.astype(F32)), (_META_G0, gate0),
                         (_META_G1, gate1), (_META_P0, pos0), (_META_P1, pos1)):
            meta = jnp.where(lane == idx, val, meta)
        meta_ref[0, rs, :] = meta
        ids_ref[0, 0, :, rs] = meta.T[:ids_ref.shape[2], :]
        base = base + jnp.sum(onehot, axis=0, keepdims=True)
    cnt_ref[...] = jnp.broadcast_to(base, cnt_ref.shape)
    if rider:
        _scatter_wait(rsrc_ref, rpad_ref, rsem)


def _post(x, da, gla, w_o, norm_cross, w_cq, cross_qn, k_mem, v_mem, w_co, norm_ffn, w_group, b_group,
          w_expert, b_expert, *, half, rider=None):
    b, s, d = x.shape
    bh = b // 2
    b0 = half * bh
    tm = TM_POST
    ns = s // tm
    m = k_mem.shape[1]
    w_r = jnp.pad(jnp.concatenate([w_group, w_expert], axis=1), ((0, 0), (0, LANES - N_GROUPS - N_EXPERTS)))
    b_r = jnp.pad(jnp.concatenate([b_group, b_expert]), (0, LANES - N_GROUPS - N_EXPERTS)).reshape(1, LANES)
    const = lambda shape: pl.BlockSpec(shape, lambda i, j, *_: (0,) * len(shape))
    tile_in = lambda width: pl.BlockSpec((1, tm, width), lambda i, j, *_: (i + b0, j, 0))
    tile_out = lambda width: pl.BlockSpec((1, tm, width), lambda i, j, *_: (i, j, 0))
    per_b = lambda: pl.BlockSpec((1, m, d), lambda i, j, *_: (i + b0, 0, 0))
    da_spec = pl.BlockSpec((1, DA_HEADS, tm, DA_V_DIM), lambda i, j, *_: (i + b0, 0, j, 0))
    in_specs = [tile_in(d), da_spec, tile_in(d // 2), const((d, d)), const((1, d)), const((d, d)),
                const((1, d // CROSS_HEADS)), per_b(), per_b(), const((d, d)), const((1, d)),
                const((d, LANES)), const((1, LANES))]
    out_specs = [tile_out(d), tile_out(d // 2), tile_out(LANES),
                 pl.BlockSpec((1, 1, 8, tm), lambda i, j, *_: (i, j, 0, 0)), const((8, LANES))]
    out_shape = [
        jax.ShapeDtypeStruct((bh, s, d), F32),
        jax.ShapeDtypeStruct((bh, s, d // 2), U32),
        jax.ShapeDtypeStruct((bh, s, LANES), F32),
        jax.ShapeDtypeStruct((bh, ns, 8, tm), F32),
        jax.ShapeDtypeStruct((8, LANES), F32),
    ]
    args = (x, da, gla, w_o.astype(BF16), norm_cross.reshape(1, d), w_cq.astype(BF16), cross_qn.reshape(1, -1),
            k_mem, v_mem, w_co.astype(BF16), norm_ffn.reshape(1, d), w_r.astype(BF16), b_r)
    scalars, scratch = (), []
    if rider is not None:
        sizes, pend, dest, xn_src, n_rows = rider
        scalars = (sizes, pend)
        in_specs += [pl.BlockSpec((1, 2, tm), lambda i, j, *_: (i * ns + j, 0, 0), memory_space=pltpu.SMEM),
                     pl.BlockSpec((tm, d // 2), lambda i, j, *_: (i * ns + j, 0))]
        out_specs.append(pl.BlockSpec(memory_space=pl.ANY))
        out_shape.append(jax.ShapeDtypeStruct((n_rows, d // 2), U32))
        scratch = [pltpu.VMEM((EXPERT_ROWS, d // 2), U32), pltpu.SemaphoreType.DMA, pltpu.SemaphoreType.DMA]
        args += (dest, xn_src)
    return pl.pallas_call(
        functools.partial(_post_kernel, d=d, tm=tm, sub=SUB_POST, rider=rider is not None),
        grid_spec=pltpu.PrefetchScalarGridSpec(
            num_scalar_prefetch=len(scalars), grid=(bh, ns), in_specs=in_specs, out_specs=out_specs,
            scratch_shapes=scratch),
        out_shape=out_shape,
        compiler_params=_params("arbitrary", "arbitrary"),
        name="post_scatter" if rider is not None else "post",
    )(*scalars, *args)


def _row_copy(src_ref, src_row, dst_ref, dst_row, sem):
    return pltpu.make_async_copy(src_ref.at[pl.ds(src_row, 1)], dst_ref.at[pl.ds(dst_row, 1)], sem)


def _zero_fill(size_ref, pend_ref, xpad_ref, zero_ref, zsem):
    zero_ref[...] = jnp.zeros(zero_ref.shape, zero_ref.dtype)
    rows = zero_ref.shape[0]
    n_blocks = xpad_ref.shape[0] // rows
    n_used = pend_ref[N_EXPERTS - 1] // rows

    def zero_block(blk):
        return pltpu.make_async_copy(zero_ref, xpad_ref.at[pl.ds(pl.multiple_of(blk * rows, rows), rows)], zsem)

    def last_block(e, fn):
        @pl.when(size_ref[e] > 0)
        def _():
            fn(zero_block(pend_ref[e] // rows - 1))

    for fn in (lambda cp: cp.start(), lambda cp: cp.wait()):
        lax.fori_loop(0, N_EXPERTS, lambda e, c: (last_block(e, fn), c)[1], 0)
        lax.fori_loop(n_used, n_blocks, lambda blk, c: (fn(zero_block(blk)), c)[1], 0)


def _scatter_rows(dest_ref, src_ref, xpad_ref, sem):
    for t in range(src_ref.shape[0]):
        for k in range(2):
            _row_copy(src_ref, t, xpad_ref, dest_ref[0, k, t], sem).start(priority=k)


def _scatter_wait(src_ref, xpad_ref, sem):
    for _ in range(2):
        pltpu.make_async_copy(src_ref, xpad_ref.at[pl.ds(0, src_ref.shape[0])], sem).wait()


def _cast_expert_weights(fresh, wg_ref, wu_ref, wd_ref, wg_s, wu_s, wd_s):
    @pl.when(fresh)
    def _():
        wg_s[...] = wg_ref[0].astype(BF16)
        wu_s[...] = wu_ref[0].astype(BF16)
        wd_s[...] = wd_ref[0].astype(BF16)


def _expert_mlp(x_ref, wg_s, wu_s, wd_s, out_ref):
    words = x_ref[...]
    half = words.shape[1]
    lo = lax.bitcast_convert_type(words << 16, F32).astype(BF16)
    hi = lax.bitcast_convert_type(words & HI16, F32).astype(BF16)
    gate = _dot(lo, wg_s[:half, :]) + _dot(hi, wg_s[half:, :])
    up = _dot(lo, wu_s[:half, :]) + _dot(hi, wu_s[half:, :])
    hid = gate / (1.0 + jnp.exp(-gate)) * up
    out_ref[...] = _dot(hid.astype(BF16), wd_s[...])


def _experts_scatter_kernel(be_ref, nused_ref, size_ref, pend_ref, x_ref, wg_ref, wu_ref, wd_ref, rdest_ref, rsrc_ref,
                            out_ref, rpad_ref, wg_s, wu_s, wd_s, zero_ref, rsem, zsem, *, rider_steps):
    i = pl.program_id(0)
    used = i < nused_ref[0]
    new_expert = jnp.logical_or(i == 0, be_ref[i] != be_ref[jnp.maximum(i - 1, 0)])
    mlp = functools.partial(_expert_mlp, x_ref, wg_s, wu_s, wd_s, out_ref)

    @pl.when(i == 0)
    def _():
        _zero_fill(size_ref, pend_ref, rpad_ref, zero_ref, zsem)

    _cast_expert_weights(jnp.logical_and(used, new_expert), wg_ref, wu_ref, wd_ref, wg_s, wu_s, wd_s)
    riding = i < rider_steps

    @pl.when(riding)
    def _():
        _scatter_rows(rdest_ref, rsrc_ref, rpad_ref, rsem)
        mlp()
        _scatter_wait(rsrc_ref, rpad_ref, rsem)

    pl.when(jnp.logical_and(used, jnp.logical_not(riding)))(mlp)

    @pl.when(jnp.logical_not(used))
    def _():
        out_ref[...] = jnp.zeros(out_ref.shape, F32)


def _gather_rows(dest_ref, src_ref, buf_ref, sem):
    for t in range(buf_ref.shape[1]):
        for k in range(2):
            _row_copy(src_ref, dest_ref[0, k, t], buf_ref.at[k], t, sem).start(priority=k)


def _combine_rows(h_ref, meta_ref, src_ref, buf_ref, sem):
    for k in range(2):
        pltpu.make_async_copy(src_ref.at[pl.ds(0, buf_ref.shape[1])], buf_ref.at[k], sem).wait()
    meta = meta_ref[...]
    g0 = meta[:, _META_G0:_META_G0 + 1]
    g1 = meta[:, _META_G1:_META_G1 + 1]
    return h_ref[...] + g0 * buf_ref[0] + g1 * buf_ref[1]


def _experts_gather_kernel(be_ref, nused_ref, x_ref, wg_ref, wu_ref, wd_ref, gdest_ref, h_ref, meta_ref, opad_ref,
                           out_ref, y_ref, wg_s, wu_s, wd_s, gbuf_ref, gsem, *, rider_steps):
    i = pl.program_id(0)
    used = i < nused_ref[0]
    new_expert = jnp.logical_or(i == 0, be_ref[i] != be_ref[jnp.maximum(i - 1, 0)])
    mlp = functools.partial(_expert_mlp, x_ref, wg_s, wu_s, wd_s, out_ref)
    _cast_expert_weights(jnp.logical_and(used, new_expert), wg_ref, wu_ref, wd_ref, wg_s, wu_s, wd_s)

    def gather():
        _gather_rows(gdest_ref, opad_ref, gbuf_ref.at[i % 2], gsem.at[i % 2])

    def combine():
        slot = (i - 1) % 2
        y_ref[...] = _combine_rows(h_ref, meta_ref, opad_ref, gbuf_ref.at[slot], gsem.at[slot])

    @pl.when(i == 0)
    def _():
        gather()
        mlp()

    @pl.when(jnp.logical_and(i > 0, i < rider_steps))
    def _():
        gather()
        mlp()
        combine()

    @pl.when(i == rider_steps)
    def _():
        combine()
        pl.when(used)(mlp)

    pl.when(jnp.logical_and(i > rider_steps, used))(mlp)

    @pl.when(jnp.logical_not(used))
    def _():
        out_ref[...] = jnp.zeros(out_ref.shape, F32)


def _experts_kernel(be_ref, nused_ref, x_ref, wg_ref, wu_ref, wd_ref, out_ref, wg_s, wu_s, wd_s):
    i = pl.program_id(0)
    used = i < nused_ref[0]
    new_expert = jnp.logical_or(i == 0, be_ref[i] != be_ref[jnp.maximum(i - 1, 0)])
    _cast_expert_weights(jnp.logical_and(used, new_expert), wg_ref, wu_ref, wd_ref, wg_s, wu_s, wd_s)
    pl.when(used)(functools.partial(_expert_mlp, x_ref, wg_s, wu_s, wd_s, out_ref))

    @pl.when(jnp.logical_not(used))
    def _():
        out_ref[...] = jnp.zeros(out_ref.shape, F32)


def _retile(dest, tr):
    tiles, _, tm = dest.shape
    return dest.reshape(tiles, 2, tm // tr, tr).transpose(0, 2, 1, 3).reshape(tiles * (tm // tr), 2, tr)


def _experts(x_pad, block_expert, n_used, w_gate, w_up, w_down, scatter=None, gather=None):
    n_rows = x_pad.shape[0]
    _, d, f = w_gate.shape
    rows, tr = EXPERT_ROWS, RIDER_ROWS
    row_blk = lambda i, be, nu, *_: (jnp.minimum(i, nu[0] - 1), 0)
    weights = lambda shape: pl.BlockSpec(shape, lambda i, be, *_: (be[i], 0, 0))
    in_specs = [pl.BlockSpec((rows, d // 2), row_blk), weights((1, d, f)), weights((1, d, f)), weights((1, f, d))]
    out_specs = [pl.BlockSpec((rows, d), lambda i, *_: (i, 0))]
    out_shape = [jax.ShapeDtypeStruct((n_rows, d), F32)]
    scratch = [pltpu.VMEM((d, f), BF16), pltpu.VMEM((d, f), BF16), pltpu.VMEM((f, d), BF16)]
    scalars, args = (block_expert, n_used), (x_pad, w_gate, w_up, w_down)
    body, name = _experts_kernel, "experts"
    if scatter is not None or gather is not None:
        n_tokens = (scatter[3] if scatter is not None else gather[1]).shape[0]
        steps = n_tokens // tr
        assert 2 * n_tokens >= steps * rows, "riding steps must all be used expert blocks"
        tile = lambda width, shift: pl.BlockSpec((tr, width), lambda i, *_: (jnp.clip(i - shift, 0, steps - 1), 0))
        ids = lambda: pl.BlockSpec((1, 2, tr), lambda i, *_: (jnp.minimum(i, steps - 1), 0, 0),
                                   memory_space=pltpu.SMEM)
    if scatter is not None:
        sizes, pend, dest, xn_src, other_rows = scatter
        scalars += (sizes, pend)
        in_specs += [ids(), tile(d // 2, 0)]
        out_specs.append(pl.BlockSpec(memory_space=pl.ANY))
        out_shape.append(jax.ShapeDtypeStruct((other_rows, d // 2), U32))
        scratch += [pltpu.VMEM((rows, d // 2), U32), pltpu.SemaphoreType.DMA, pltpu.SemaphoreType.DMA]
        args += (_retile(dest, tr), xn_src)
        body, name = functools.partial(_experts_scatter_kernel, rider_steps=steps), "experts_scatter"
    elif gather is not None:
        dest, h2, meta, opad = gather
        in_specs += [ids(), tile(d, 1), tile(LANES, 1), pl.BlockSpec(memory_space=pl.ANY)]
        out_specs.append(tile(d, 1))
        out_shape.append(jax.ShapeDtypeStruct((n_tokens, d), F32))
        scratch += [pltpu.VMEM((2, 2, tr, d), F32), pltpu.SemaphoreType.DMA((2,))]
        args += (_retile(dest, tr), h2, meta, opad)
        body, name = functools.partial(_experts_gather_kernel, rider_steps=steps), "experts_gather"
    out = pl.pallas_call(
        body,
        grid_spec=pltpu.PrefetchScalarGridSpec(
            num_scalar_prefetch=len(scalars), grid=(n_rows // rows,), in_specs=in_specs, out_specs=out_specs,
            scratch_shapes=scratch),
        out_shape=out_shape,
        compiler_params=_params("arbitrary"),
        name=name,
    )(*scalars, *args)
    return out if len(out) > 1 else out[0]


def _combine_tail_kernel(dest_ref, y0_ref, h_ref, meta_ref, opad_ref, y_ref, buf_ref, sem, *, nt):
    s = pl.program_id(0)

    @pl.when(s < nt)
    def _():
        _gather_rows(dest_ref, opad_ref, buf_ref.at[s % 2], sem.at[s % 2])

    @pl.when(s > 0)
    def _():
        slot = (s - 1) % 2
        y_ref[0] = y0_ref[...]
        y_ref[1] = _combine_rows(h_ref, meta_ref, opad_ref, buf_ref.at[slot], sem.at[slot])


def _combine_tail(y0, h2, meta, dest, out_pad):
    th, d = h2.shape
    tm = TM_TAIL
    nt = th // tm
    prev = lambda s: (jnp.maximum(s - 1, 0), 0)
    return pl.pallas_call(
        functools.partial(_combine_tail_kernel, nt=nt),
        grid=(nt + 1,),
        in_specs=[
            pl.BlockSpec((1, 2, tm), lambda s: (jnp.minimum(s, nt - 1), 0, 0), memory_space=pltpu.SMEM),
            pl.BlockSpec((tm, d), prev), pl.BlockSpec((tm, d), prev), pl.BlockSpec((tm, LANES), prev),
            pl.BlockSpec(memory_space=pl.ANY),
        ],
        out_specs=pl.BlockSpec((2, tm, d), lambda s: (0, jnp.maximum(s - 1, 0), 0)),
        scratch_shapes=[pltpu.VMEM((2, 2, tm, d), F32), pltpu.SemaphoreType.DMA((2,))],
        out_shape=jax.ShapeDtypeStruct((2, th, d), F32),
        compiler_params=_params("arbitrary"),
        name="combine_tail",
    )(_retile(dest, tm), y0, h2, meta, out_pad)


def _route_tables(counts, ids, n_tokens):
    rows = EXPERT_ROWS
    sizes = counts[0, :N_EXPERTS].astype(I32)
    padded = (sizes + rows - 1) // rows * rows
    pend = jnp.cumsum(padded)
    pstart = pend - padded
    n_rows = 2 * n_tokens + N_EXPERTS * rows
    block_start = jnp.arange(n_rows // rows, dtype=I32) * rows
    block_expert = jnp.minimum(jnp.sum(pend[None, :] <= block_start[:, None], axis=1), N_EXPERTS - 1).astype(I32)
    n_used = (pend[-1:] // rows).astype(I32)
    expert = ids[:, _META_E0:_META_E1 + 1].astype(I32)
    rank = ids[:, _META_P0:_META_P1 + 1].astype(I32)
    experts = jnp.arange(N_EXPERTS, dtype=I32).reshape(-1, 1, 1, 1)
    dest = jnp.sum(jnp.where(expert[None] == experts, pstart.reshape(-1, 1, 1, 1), 0), axis=0) + rank
    return sizes, pend, block_expert, n_used, dest, n_rows


def kernel(x, mem, norm_mix, w_in, da_q_norm, da_k_norm, lambda_q1, lambda_k1, lambda_q2, lambda_k2,
           da_out_norm, gla_gate_w, gla_gate_b, gla_out_norm, w_o, norm_cross, norm_mem, w_cq, w_ckv,
           cross_q_norm, cross_k_norm, w_co, norm_ffn, w_group, b_group, w_expert, b_expert,
           w_e_gate, w_e_up, w_e_down):
    b, s, d = x.shape
    th = b // 2 * s
    h = x
    for l in range(norm_mix.shape[0]):
        assert l == 0, "lam_init is fixed for a single layer"
        qt, kda, vt, gq, gk, gv, gg, la = _in_proj(h, norm_mix[l], w_in[l], da_q_norm[l], da_k_norm[l],
                                                   gla_gate_w[l], gla_gate_b[l])
        da = _diff_attn(qt, kda, vt, lambda_q1[l], lambda_k1[l], lambda_q2[l], lambda_k2[l], da_out_norm[l],
                        da_q_norm[l], da_k_norm[l])
        gla = _gla(gq, gk, la, gv, gg, gla_out_norm[l])
        k_mem, v_mem = _mem_kv(mem, norm_mem[l], w_ckv[l], cross_k_norm[l])
        post = functools.partial(_post, h, da, gla, w_o[l], norm_cross[l], w_cq[l], cross_q_norm[l], k_mem, v_mem,
                                 w_co[l], norm_ffn[l], w_group[l], b_group[l], w_expert[l], b_expert[l])
        experts = functools.partial(_experts, w_gate=w_e_gate[l], w_up=w_e_up[l], w_down=w_e_down[l])

        h2_0, xn_0, meta_0, ids_0, counts_0 = post(half=0)
        sizes_0, pend_0, be_0, used_0, dest_0, n_rows = _route_tables(counts_0, ids_0.reshape(-1, 8, TM_POST), th)
        h2_1, xn_1, meta_1, ids_1, counts_1, xpad_0 = post(
            half=1, rider=(sizes_0, pend_0, dest_0, xn_0.reshape(th, d // 2), n_rows))
        sizes_1, pend_1, be_1, used_1, dest_1, _ = _route_tables(counts_1, ids_1.reshape(-1, 8, TM_POST), th)
        opad_0, xpad_1 = experts(xpad_0, be_0, used_0,
                                 scatter=(sizes_1, pend_1, dest_1, xn_1.reshape(th, d // 2), n_rows))
        opad_1, y_0 = experts(xpad_1, be_1, used_1,
                              gather=(dest_0, h2_0.reshape(th, d), meta_0.reshape(th, LANES), opad_0))
        h = _combine_tail(y_0, h2_1.reshape(th, d), meta_1.reshape(th, LANES), dest_1, opad_1).reshape(b, s, d)
    return h
```

```python
import functools
import math

import jax
import jax.numpy as jnp
import numpy as np
from jax import lax
from jax.experimental import pallas as pl
from jax.experimental.pallas import tpu as pltpu

F32 = jnp.float32
BF16 = jnp.bfloat16
I32 = jnp.int32
U32 = jnp.uint32
HI16 = np.uint32(0xFFFF0000)

EPS = 1e-6
CHUNK = 64

DA_HEADS = 4
DA_QK_DIM = 64
DA_V_DIM = 128
GLA_HEADS = 4
GLA_K_DIM = 64
GLA_V_DIM = 128
GLA_GATE_RANK = 16
GLA_TAU = 16.0
CROSS_HEADS = 4
N_GROUPS = 4
EXPERTS_PER_GROUP = 8
N_EXPERTS = N_GROUPS * EXPERTS_PER_GROUP
LAM_INIT = 0.8 - 0.6 * math.exp(-0.3 * 0)

LANES = 128
VMEM_LIMIT = 56 * 1024 * 1024

TM_PROJ = 1024
ATT_BLK = 512
TS_GLA = 2048
GLA_GROUP = 4
TM_POST = 1024
SUB_POST = 256
TM_TAIL = 512
RIDER_ROWS = 256
EXPERT_ROWS = 512

NEG_INF = float("-inf")


def _params(*sem):
    return pltpu.CompilerParams(dimension_semantics=sem, vmem_limit_bytes=VMEM_LIMIT)


def _rms(t, g):
    ms = jnp.mean(t * t, axis=-1, keepdims=True)
    return t * lax.rsqrt(ms + EPS) * g


def _dot(a, b):
    return jnp.dot(a, b, preferred_element_type=F32)


def _dot_nt(a, b):
    return lax.dot_general(a, b, (((1,), (1,)), ((), ())), preferred_element_type=F32)


def _dot_tn(a, b):
    return lax.dot_general(a, b, (((0,), (0,)), ((), ())), preferred_element_type=F32)


def _split_bf16(t):
    hi = t.astype(BF16)
    lo = (t - hi.astype(F32)).astype(BF16)
    return hi, lo


def _mem_kv_kernel(mem_ref, g_ref, w_ref, kn_ref, k_ref, v_ref, *, d, heads):
    mn = _rms(mem_ref[0], g_ref[...]).astype(BF16)
    kv = _dot(mn, w_ref[...])
    hd = d // heads
    scale = hd ** -0.5
    for h in range(heads):
        kh = _rms(kv[:, h * hd:(h + 1) * hd], kn_ref[...]) * scale
        k_ref[0, :, h * hd:(h + 1) * hd] = kh.astype(BF16)
    v_ref[0] = kv[:, d:].astype(BF16)


def _mem_kv(mem, norm_m, w_ckv, kn):
    b, m, d = mem.shape
    return pl.pallas_call(
        functools.partial(_mem_kv_kernel, d=d, heads=CROSS_HEADS),
        grid=(b,),
        in_specs=[
            pl.BlockSpec((1, m, d), lambda i: (i, 0, 0)),
            pl.BlockSpec((1, d), lambda i: (0, 0)),
            pl.BlockSpec((d, 2 * d), lambda i: (0, 0)),
            pl.BlockSpec((1, d // CROSS_HEADS), lambda i: (0, 0)),
        ],
        out_specs=[
            pl.BlockSpec((1, m, d), lambda i: (i, 0, 0)),
            pl.BlockSpec((1, m, d), lambda i: (i, 0, 0)),
        ],
        out_shape=[jax.ShapeDtypeStruct((b, m, d), BF16)] * 2,
        compiler_params=_params("parallel"),
        name="mem_kv",
    )(mem, norm_m.reshape(1, d), w_ckv.astype(BF16), kn.reshape(1, -1))


_QK = DA_HEADS * 2 * DA_QK_DIM
_DAW = DA_HEADS * DA_V_DIM
_GQK = GLA_HEADS * GLA_K_DIM
_GW = GLA_HEADS * GLA_V_DIM
_OFF_DQ = 0
_OFF_DK = _OFF_DQ + _QK
_OFF_DV = _OFF_DK + _QK
_OFF_GQ = _OFF_DV + _DAW
_OFF_GK = _OFF_GQ + _GQK
_OFF_GV = _OFF_GK + _GQK
_OFF_GG = _OFF_GV + _GW
_OFF_GR = _OFF_GG + _GW
_IN_PAD = _OFF_GR + LANES


def _in_proj_kernel(x_ref, g_ref, w_ref, qg_ref, kg_ref, grp_ref, gw_ref, gb_ref,
                    qt_ref, k_ref, vt_ref, gq_ref, gk_ref, gv_ref, gg_ref, la_ref):
    u = _rms(x_ref[0], g_ref[...]).astype(BF16)

    def proj(off, width):
        return _dot(u, w_ref[:, off:off + width])

    def mean_square(p):
        return _dot((p * p).astype(BF16), grp_ref[...])

    p_q = proj(_OFF_DQ, _QK)
    p_k = proj(_OFF_DK, _QK)
    ms_q = mean_square(p_q)
    qn = p_q * lax.rsqrt(ms_q + EPS) * qg_ref[...] * (DA_QK_DIM ** -0.5 * math.log2(math.e))
    blk = qt_ref.shape[3]
    for t in range(qt_ref.shape[1]):
        qt_ref[0, t] = qn[t * blk:(t + 1) * blk].T.astype(BF16)
    dv = proj(_OFF_DV, _DAW)
    ms_k = mean_square(p_k)
    kn = (p_k * lax.rsqrt(ms_k + EPS) * kg_ref[...]).astype(BF16)
    for h in range(DA_HEADS):
        k_ref[0, h] = kn[:, h * 2 * DA_QK_DIM:(h + 1) * 2 * DA_QK_DIM]
    for t in range(vt_ref.shape[1]):
        vt_ref[0, t] = dv[t * blk:(t + 1) * blk].T.astype(BF16)
    g_r = proj(_OFF_GR, LANES)
    gq_ref[0] = proj(_OFF_GQ, _GQK) * (GLA_K_DIM ** -0.5)
    gk_ref[0] = proj(_OFF_GK, _GQK)
    z = _dot(g_r.astype(BF16), gw_ref[...]) + gb_ref[...]
    gv_ref[0] = proj(_OFF_GV, _GW).astype(BF16)
    gg_ref[0] = proj(_OFF_GG, _GW)
    log_sig = jnp.minimum(z, 0.0) - jnp.log(1.0 + jnp.exp(-jnp.abs(z)))
    la_ref[0] = log_sig * (math.log2(math.e) / GLA_TAU)


def _in_proj(x, norm_g, w_in, da_qn, da_kn, gate_w, gate_b):
    b, s, d = x.shape
    tm, blk = TM_PROJ, ATT_BLK
    ns, nb = s // tm, tm // blk
    w = jnp.pad(w_in, ((0, 0), (0, _IN_PAD - w_in.shape[1]))).astype(BF16)
    gw = jnp.pad(gate_w, ((0, LANES - GLA_GATE_RANK), (0, 0))).astype(BF16)
    lane = jnp.arange(_QK)
    grp = jnp.where((lane[:, None] // DA_QK_DIM) == (lane[None, :] // DA_QK_DIM),
                    1.0 / DA_QK_DIM, 0.0).astype(BF16)
    const = lambda shape: pl.BlockSpec(shape, lambda i, j: (0,) * len(shape))
    tile = lambda width: pl.BlockSpec((1, tm, width), lambda i, j: (i, j, 0))
    tile_t = lambda width: pl.BlockSpec((1, nb, width, blk), lambda i, j: (i, j, 0, 0))
    return pl.pallas_call(
        _in_proj_kernel,
        grid=(b, ns),
        in_specs=[tile(d), const((1, d)), const((d, _IN_PAD)), const((1, _QK)), const((1, _QK)),
                  const((_QK, _QK)), const((LANES, _GQK)), const((1, _GQK))],
        out_specs=[tile_t(_QK), pl.BlockSpec((1, DA_HEADS, tm, 2 * DA_QK_DIM), lambda i, j: (i, 0, j, 0)),
                   tile_t(_DAW), tile(_GQK), tile(_GQK), tile(_GW), tile(_GW), tile(_GQK)],
        out_shape=[
            jax.ShapeDtypeStruct((b, s // blk, _QK, blk), BF16),
            jax.ShapeDtypeStruct((b, DA_HEADS, s, 2 * DA_QK_DIM), BF16),
            jax.ShapeDtypeStruct((b, s // blk, _DAW, blk), BF16),
            jax.ShapeDtypeStruct((b, s, _GQK), F32),
            jax.ShapeDtypeStruct((b, s, _GQK), F32),
            jax.ShapeDtypeStruct((b, s, _GW), BF16),
            jax.ShapeDtypeStruct((b, s, _GW), F32),
            jax.ShapeDtypeStruct((b, s, _GQK), F32),
        ],
        compiler_params=_params("parallel", "parallel"),
        name="in_proj",
    )(x, norm_g.reshape(1, d), w, jnp.tile(da_qn, 2 * DA_HEADS).reshape(1, _QK),
      jnp.tile(da_kn, 2 * DA_HEADS).reshape(1, _QK), grp, gw, gate_b.reshape(1, _GQK))


def _split_q(qt):
    row = lax.broadcasted_iota(I32, qt.shape, 0)
    zero = jnp.zeros_like(qt)
    return jnp.where(row < DA_QK_DIM, qt, zero), jnp.where(row >= DA_QK_DIM, qt, zero)


def _chunk_causal_mask(blk):
    key_chunk = lax.broadcasted_iota(I32, (blk, blk), 0) // CHUNK
    qry_chunk = lax.broadcasted_iota(I32, (blk, blk), 1) // CHUNK
    return key_chunk <= qry_chunk


def _diff_attn_finish(lq1_ref, lk1_ref, lq2_ref, lk2_ref, gain_ref, a1, l1, a2, l2):
    lam = (jnp.exp(jnp.sum(lq1_ref[...] * lk1_ref[...], axis=-1, keepdims=True))
           - jnp.exp(jnp.sum(lq2_ref[...] * lk2_ref[...], axis=-1, keepdims=True)) + LAM_INIT)
    o = a1 / l1 - lam * (a2 / l2)
    ms = jnp.mean(o * o, axis=0, keepdims=True)
    o = o * lax.rsqrt(ms + EPS) * gain_ref[...] * (1.0 - LAM_INIT)
    return o.T.astype(BF16)


def _diff_attn_bounded_kernel(lq1_ref, lk1_ref, lq2_ref, lk2_ref, gain_ref, qt_ref, k_ref, vt_ref, out_ref,
                              s_ref, l1_ref, a1_ref, l2_ref, a2_ref, *, blk, nb):
    stats = ((l1_ref, a1_ref), (l2_ref, a2_ref))
    mask = _chunk_causal_mask(blk)

    def reset():
        for l_ref, a_ref in stats:
            l_ref[...] = jnp.zeros(l_ref.shape, F32)
            a_ref[...] = jnp.zeros(a_ref.shape, F32)

    def scores(q, j, slot):
        kb = k_ref[0, 0, pl.ds(pl.multiple_of(j * blk, blk), blk), :]
        s_ref[slot, 0] = _dot(kb, q[0])
        s_ref[slot, 1] = _dot(kb, q[1])

    def consume(j, slot, masked):
        vb = vt_ref[0, j]
        for m, (l_ref, a_ref) in enumerate(stats):
            s = s_ref[slot, m]
            if masked:
                s = jnp.where(mask, s, NEG_INF)
            p = jnp.exp2(s)
            l_ref[...] += jnp.sum(p, axis=0, keepdims=True)
            a_ref[...] += _dot(vb, p.astype(BF16))

    def step(q, j, slot):
        scores(q, j + 1, 1 - slot)
        consume(j, slot, False)

    reset()
    q = _split_q(qt_ref[0, 0])
    slot = 0
    scores(q, 0, slot)
    for qi in range(nb):
        for j in range(qi):
            step(q, j, slot)
            slot = 1 - slot
        if qi + 1 < nb:
            q = _split_q(qt_ref[0, qi + 1])
            scores(q, 0, 1 - slot)
        consume(qi, slot, True)
        out_ref[0, 0, qi * blk:(qi + 1) * blk, :] = _diff_attn_finish(
            lq1_ref, lk1_ref, lq2_ref, lk2_ref, gain_ref, a1_ref[...], l1_ref[...], a2_ref[...], l2_ref[...])
        if qi + 1 < nb:
            reset()
        slot = 1 - slot


def _diff_attn_online_kernel(lq1_ref, lk1_ref, lq2_ref, lk2_ref, gain_ref, qt_ref, k_ref, vt_ref, out_ref,
                             m1_ref, l1_ref, a1_ref, m2_ref, l2_ref, a2_ref, *, blk):
    qi = pl.program_id(2)
    q1, q2 = _split_q(qt_ref[0, 0])

    for m_ref, l_ref, a_ref in ((m1_ref, l1_ref, a1_ref), (m2_ref, l2_ref, a2_ref)):
        m_ref[...] = jnp.full(m_ref.shape, NEG_INF, F32)
        l_ref[...] = jnp.zeros(l_ref.shape, F32)
        a_ref[...] = jnp.zeros(a_ref.shape, F32)

    def update(s, vb, m_ref, l_ref, a_ref):
        m_old = m_ref[...]
        m_new = jnp.maximum(m_old, jnp.max(s, axis=0, keepdims=True))
        alpha = jnp.exp2(m_old - m_new)
        p = jnp.exp2(s - m_new)
        l_ref[...] = alpha * l_ref[...] + jnp.sum(p, axis=0, keepdims=True)
        a_ref[...] = alpha * a_ref[...] + _dot(vb, p.astype(BF16))
        m_ref[...] = m_new

    def block(j, mask):
        kb = k_ref[0, 0, pl.ds(pl.multiple_of(j * blk, blk), blk), :]
        vb = vt_ref[0, j]
        s1 = _dot(kb, q1)
        s2 = _dot(kb, q2)
        if mask is not None:
            s1 = jnp.where(mask, s1, NEG_INF)
            s2 = jnp.where(mask, s2, NEG_INF)
        update(s1, vb, m1_ref, l1_ref, a1_ref)
        update(s2, vb, m2_ref, l2_ref, a2_ref)

    def body(j, carry):
        block(j, None)
        return carry

    lax.fori_loop(0, qi, body, 0)
    block(qi, _chunk_causal_mask(blk))
    out_ref[0, 0] = _diff_attn_finish(lq1_ref, lk1_ref, lq2_ref, lk2_ref, gain_ref,
                                      a1_ref[...], l1_ref[...], a2_ref[...], l2_ref[...])


SCORE_BOUND = 60.0


def _diff_attn(qt, k, vt, lq1, lk1, lq2, lk2, da_on, da_qn, da_kn):
    b, nb, _, blk = qt.shape
    s = nb * blk
    stat = lambda: pltpu.VMEM((1, blk), F32)
    acc = lambda: pltpu.VMEM((DA_V_DIM, blk), F32)

    args = (lq1.reshape(1, -1), lk1.reshape(1, -1), lq2.reshape(1, -1), lk2.reshape(1, -1),
            da_on.reshape(-1, 1), qt, k, vt)
    out_shape = jax.ShapeDtypeStruct((b, DA_HEADS, s, DA_V_DIM), BF16)
    head = lambda *trailing: (lambda i, h: (i, 0, h) + trailing)
    vec2 = lambda: pl.BlockSpec((1, DA_QK_DIM), lambda i, h: (0, 0))
    bounded = pl.pallas_call(
        functools.partial(_diff_attn_bounded_kernel, blk=blk, nb=nb),
        grid=(b, DA_HEADS),
        in_specs=[
            vec2(), vec2(), vec2(), vec2(),
            pl.BlockSpec((DA_V_DIM, 1), lambda i, h: (0, 0)),
            pl.BlockSpec((1, nb, 2 * DA_QK_DIM, blk), head(0)),
            pl.BlockSpec((1, 1, s, 2 * DA_QK_DIM), lambda i, h: (i, h, 0, 0)),
            pl.BlockSpec((1, nb, DA_V_DIM, blk), head(0)),
        ],
        out_specs=pl.BlockSpec((1, 1, s, DA_V_DIM), lambda i, h: (i, h, 0, 0)),
        out_shape=out_shape,
        scratch_shapes=[pltpu.VMEM((2, 2, blk, blk), F32), stat(), acc(), stat(), acc()],
        compiler_params=_params("parallel", "parallel"),
        name="diff_attn",
    )
    vec3 = lambda: pl.BlockSpec((1, DA_QK_DIM), lambda i, h, q: (0, 0))
    online = pl.pallas_call(
        functools.partial(_diff_attn_online_kernel, blk=blk),
        grid=(b, DA_HEADS, nb),
        in_specs=[
            vec3(), vec3(), vec3(), vec3(),
            pl.BlockSpec((DA_V_DIM, 1), lambda i, h, q: (0, 0)),
            pl.BlockSpec((1, 1, 2 * DA_QK_DIM, blk), lambda i, h, q: (i, q, h, 0)),
            pl.BlockSpec((1, 1, s, 2 * DA_QK_DIM), lambda i, h, q: (i, h, 0, 0)),
            pl.BlockSpec((1, nb, DA_V_DIM, blk), lambda i, h, q: (i, 0, h, 0)),
        ],
        out_specs=pl.BlockSpec((1, 1, blk, DA_V_DIM), lambda i, h, q: (i, h, q, 0)),
        out_shape=out_shape,
        scratch_shapes=[stat(), stat(), acc(), stat(), stat(), acc()],
        compiler_params=_params("parallel", "parallel", "parallel"),
        name="diff_attn_online",
    )
    bound = (1.01 * DA_QK_DIM ** 0.5 * math.log2(math.e)) * jnp.max(jnp.abs(da_qn)) * jnp.max(jnp.abs(da_kn))
    return lax.cond(bound <= SCORE_BOUND, bounded, online, *args)


def _gla_kernel(q_ref, k_ref, la_ref, v_ref, g_ref, gain_ref, out_ref, st_ref, *, ts, group):
    @pl.when(pl.program_id(1) == 0)
    def _():
        st_ref[...] = jnp.zeros(st_ref.shape, F32)

    c = CHUNK
    rows = group * c
    hk, hv = _GQK, _GW
    r = lax.broadcasted_iota(I32, (rows, rows), 0)
    cc = lax.broadcasted_iota(I32, (rows, rows), 1)
    tri = jnp.where(jnp.logical_and(r // c == cc // c, r >= cc), 1.0, 0.0).astype(BF16)
    bd_k = (lax.broadcasted_iota(I32, (hk, hk), 0) // GLA_K_DIM
            == lax.broadcasted_iota(I32, (hk, hk), 1) // GLA_K_DIM)
    bd_v = (lax.broadcasted_iota(I32, (hk, hv), 0) // GLA_K_DIM
            == lax.broadcasted_iota(I32, (hk, hv), 1) // GLA_V_DIM)
    pair_w = 2 * GLA_K_DIM
    low_sq = lax.broadcasted_iota(I32, (GLA_V_DIM, pair_w), 1) < GLA_K_DIM
    low_q = lax.broadcasted_iota(I32, (c, pair_w), 1) < GLA_K_DIM
    lower = (lax.broadcasted_iota(I32, (c, hk), 0)
             >= lax.broadcasted_iota(I32, (c, hk), 1) % c)

    def tiled(t, mask):
        t4 = jnp.concatenate([t] * GLA_HEADS, axis=0)
        return jnp.where(mask, t4, jnp.zeros_like(t4))

    def chunk_row(t, row):
        return jnp.concatenate([jnp.broadcast_to(t[i * c + row:i * c + row + 1, :], (c, hk)) for i in range(group)],
                               axis=0)

    def cum_decay(gi):
        la_hi, la_lo = _split_bf16(la_ref[0, pl.ds(pl.multiple_of(gi * rows, rows), rows), :])
        return _dot(tri, la_hi) + _dot(tri, la_lo)

    n_groups = ts // rows

    def body(gi, big_l):
        sl = pl.ds(pl.multiple_of(gi * rows, rows), rows)
        l_end = chunk_row(big_l, c - 1)
        lc = big_l - chunk_row(big_l, c // 2 - 1)
        e_pos = jnp.exp2(lc)
        e_neg = jnp.exp2(-lc)
        q = q_ref[0, sl, :]
        k = k_ref[0, sl, :]
        v = v_ref[0, sl, :]
        q_pos = (q * e_pos).astype(BF16)
        q_neg = (q * e_neg).astype(BF16)
        k_pos = (k * e_pos).astype(BF16)
        k_neg = (k * e_neg).astype(BF16)
        q_in = (q * jnp.exp2(big_l)).astype(BF16)
        k_out = (k * jnp.exp2(l_end - big_l)).astype(BF16)
        decay = jnp.exp2(l_end)

        chunks = [slice(i * c, (i + 1) * c) for i in range(group)]
        a_past = [_dot_nt(q_pos[cs], tiled(k_neg[cs], bd_k)) for cs in chunks]
        a_fut = [_dot_nt(q_neg[cs], tiled(k_pos[cs], bd_k)) for cs in chunks]
        next_l = cum_decay(jnp.minimum(gi + 1, n_groups - 1))
        inc = [[_dot_tn(v[cs, h * GLA_V_DIM:(h + 1) * GLA_V_DIM], k_out[cs, h // 2 * pair_w:(h // 2 + 1) * pair_w])
                for h in range(GLA_HEADS)] for cs in chunks]
        u_t = [[jnp.where(low_sq, r[2 * j], r[2 * j + 1]) for j in range(GLA_HEADS // 2)] for r in inc]
        a = [jnp.where(lower, p, f).astype(BF16) for p, f in zip(a_past, a_fut)]
        o_intra = [_dot(a[i], tiled(v[cs], bd_v)) for i, cs in enumerate(chunks)]

        st = [st_ref[j] for j in range(GLA_HEADS // 2)]
        o_inter = []
        for i in range(group):
            cs = slice(i * c, (i + 1) * c)
            st_bf = [t.astype(BF16) for t in st]
            heads = []
            for h in range(GLA_HEADS):
                j = h // 2
                qp = q_in[cs, j * pair_w:(j + 1) * pair_w]
                qh = jnp.where(low_q if h % 2 == 0 else jnp.logical_not(low_q), qp, jnp.zeros_like(qp))
                heads.append(_dot_nt(qh, st_bf[j]))
            o_inter.append(jnp.concatenate(heads, axis=-1))
            st = [st[j] * decay[i * c:i * c + 1, j * pair_w:(j + 1) * pair_w] + u_t[i][j]
                  for j in range(GLA_HEADS // 2)]
        for j in range(GLA_HEADS // 2):
            st_ref[j] = st[j]

        o = jnp.concatenate(o_intra, axis=0) + jnp.concatenate(o_inter, axis=0)
        g = g_ref[0, sl, :]
        silu = g / (1.0 + jnp.exp(-g))
        for h in range(GLA_HEADS):
            hs = slice(h * GLA_V_DIM, (h + 1) * GLA_V_DIM)
            out_ref[0, sl, hs] = (_rms(o[:, hs], gain_ref[...]) * silu[:, hs]).astype(BF16)
        return next_l

    big_l = cum_decay(0)
    for gi in range(n_groups):
        big_l = body(gi, big_l)


def _gla(gq, gk, la, gv, gg, gla_on):
    b, s, _ = gq.shape
    ts = TS_GLA
    tile = lambda width: pl.BlockSpec((1, ts, width), lambda i, j: (i, j, 0))
    return pl.pallas_call(
        functools.partial(_gla_kernel, ts=ts, group=GLA_GROUP),
        grid=(b, s // ts),
        in_specs=[tile(_GQK), tile(_GQK), tile(_GQK), tile(_GW), tile(_GW),
                  pl.BlockSpec((1, GLA_V_DIM), lambda i, j: (0, 0))],
        out_specs=tile(_GW),
        out_shape=jax.ShapeDtypeStruct((b, s, _GW), BF16),
        scratch_shapes=[pltpu.VMEM((GLA_HEADS // 2, GLA_V_DIM, 2 * GLA_K_DIM), F32)],
        compiler_params=_params("parallel", "arbitrary"),
        name="gla",
    )(gq, gk, la, gv, gg, gla_on.reshape(1, -1))


_META_E0, _META_E1, _META_G0, _META_G1, _META_P0, _META_P1 = range(6)
_EXP_LANE0 = N_GROUPS


def _post_kernel(*refs, d, tm, sub, rider):
    if rider:
        (size_ref, pend_ref, x_ref, da_ref, gla_ref, wo_ref, gc_ref, wq_ref, qn_ref, km_ref, vm_ref, wco_ref,
         gf_ref, wr_ref, br_ref, rdest_ref, rsrc_ref,
         h_ref, xn_ref, meta_ref, ids_ref, cnt_ref, rpad_ref, zero_ref, rsem, zsem) = refs
    else:
        (x_ref, da_ref, gla_ref, wo_ref, gc_ref, wq_ref, qn_ref, km_ref, vm_ref, wco_ref,
         gf_ref, wr_ref, br_ref, h_ref, xn_ref, meta_ref, ids_ref, cnt_ref) = refs
    first = jnp.logical_and(pl.program_id(0) == 0, pl.program_id(1) == 0)

    @pl.when(first)
    def _():
        cnt_ref[...] = jnp.zeros(cnt_ref.shape, F32)
        if rider:
            _zero_fill(size_ref, pend_ref, rpad_ref, zero_ref, zsem)

    if rider:
        _scatter_rows(rdest_ref, rsrc_ref, rpad_ref, rsem)

    half = d // 2
    hd = d // CROSS_HEADS
    lane = lax.broadcasted_iota(I32, (sub, LANES), 1)
    big = jnp.int32(LANES)
    strict_lower = jnp.where(lax.broadcasted_iota(I32, (sub, sub), 0) > lax.broadcasted_iota(I32, (sub, sub), 1),
                             1.0, 0.0).astype(BF16)

    def lane_argmax(vals):
        m = jnp.max(vals, axis=-1, keepdims=True)
        idx = jnp.min(jnp.where(vals == m, lane, big), axis=-1, keepdims=True)
        return m, idx

    groups = [slice(r0, r0 + sub) for r0 in range(0, tm, sub)]
    heads = [slice(h * hd, (h + 1) * hd) for h in range(CROSS_HEADS)]
    da = [jnp.concatenate([da_ref[0, h, rs, :] for h in range(DA_HEADS)], axis=-1) for rs in groups]
    h1 = [x_ref[0, rs, :] + _dot(da[g], wo_ref[:half, :]) + _dot(gla_ref[0, rs, :], wo_ref[half:, :])
          for g, rs in enumerate(groups)]

    u = [_rms(t, gc_ref[...]).astype(BF16) for t in h1]
    q = [_dot(t, wq_ref[...]) for t in u]
    qh = [[_rms(t[:, hs], qn_ref[...]).astype(BF16) for hs in heads] for t in q]
    sc = [[_dot_nt(t[h], km_ref[0, :, hs]) for h, hs in enumerate(heads)] for t in qh]
    pr = []
    for t in sc:
        e = [jnp.exp(v - jnp.max(v, axis=-1, keepdims=True)) for v in t]
        pr.append([(v / jnp.sum(v, axis=-1, keepdims=True)).astype(BF16) for v in e])
    o = [jnp.concatenate([_dot(t[h], vm_ref[0, :, hs]) for h, hs in enumerate(heads)], axis=-1).astype(BF16)
         for t in pr]
    h2 = [h1[g] + _dot(o[g], wco_ref[...]) for g in range(len(groups))]
    for g, rs in enumerate(groups):
        h_ref[0, rs, :] = h2[g]

    xn = [_rms(t, gf_ref[...]).astype(BF16) for t in h2]
    logits = [_dot(t, wr_ref[...]) + br_ref[...] for t in xn]
    base = cnt_ref[0:1, :]
    for g, rs in enumerate(groups):
        bits = lax.bitcast_convert_type(xn[g].astype(F32), U32)
        xn_ref[0, rs, :] = (bits[:, :half] >> 16) | (bits[:, half:] & HI16)

        lg = jnp.where(lane < N_GROUPS, logits[g], NEG_INF)
        g_max, g_sel = lane_argmax(lg)
        p_g = 1.0 / jnp.sum(jnp.exp(lg - g_max), axis=-1, keepdims=True)
        e_lo = _EXP_LANE0 + g_sel * EXPERTS_PER_GROUP
        in_group = jnp.logical_and(lane >= e_lo, lane < e_lo + EXPERTS_PER_GROUP)
        le = jnp.where(in_group, logits[g], NEG_INF)
        m1, i1 = lane_argmax(le)
        m2, i2 = lane_argmax(jnp.where(lane == i1, NEG_INF, le))
        e2 = jnp.exp(m2 - m1)
        gate0 = p_g / (1.0 + e2)
        gate1 = p_g * e2 / (1.0 + e2)
        e0 = i1 - _EXP_LANE0
        e1 = i2 - _EXP_LANE0

        hot0 = lane == e0
        hot1 = lane == e1
        onehot = jnp.where(jnp.logical_or(hot0, hot1), 1.0, 0.0)
        before = _dot(strict_lower, onehot.astype(BF16)) + base
        pos0 = jnp.sum(jnp.where(hot0, before, 0.0), axis=-1, keepdims=True)
        pos1 = jnp.sum(jnp.where(hot1, before, 0.0), axis=-1, keepdims=True)

        meta = jnp.zeros(logits[g].shape, F32)
        for idx, val in ((_META_E0, e0.astype(F32)), (_META_E1, e1.astype(F32)), (_META_G0, gate0),
                         (_META_G1, gate1), (_META_P0, pos0), (_META_P1, pos1)):
            meta = jnp.where(lane == idx, val, meta)
        meta_ref[0, rs, :] = meta
        ids_ref[0, 0, :, rs] = meta.T[:ids_ref.shape[2], :]
        base = base + jnp.sum(onehot, axis=0, keepdims=True)
    cnt_ref[...] = jnp.broadcast_to(base, cnt_ref.shape)
    if rider:
        _scatter_wait(rsrc_ref, rpad_ref, rsem)


def _post(x, da, gla, w_o, norm_cross, w_cq, cross_qn, k_mem, v_mem, w_co, norm_ffn, w_group, b_group,
          w_expert, b_expert, *, half, rider=None):
    b, s, d = x.shape
    bh = b // 2
    b0 = half * bh
    tm = TM_POST
    ns = s // tm
    m = k_mem.shape[1]
    w_r = jnp.pad(jnp.concatenate([w_group, w_expert], axis=1), ((0, 0), (0, LANES - N_GROUPS - N_EXPERTS)))
    b_r = jnp.pad(jnp.concatenate([b_group, b_expert]), (0, LANES - N_GROUPS - N_EXPERTS)).reshape(1, LANES)
    const = lambda shape: pl.BlockSpec(shape, lambda i, j, *_: (0,) * len(shape))
    tile_in = lambda width: pl.BlockSpec((1, tm, width), lambda i, j, *_: (i + b0, j, 0))
    tile_out = lambda width: pl.BlockSpec((1, tm, width), lambda i, j, *_: (i, j, 0))
    per_b = lambda: pl.BlockSpec((1, m, d), lambda i, j, *_: (i + b0, 0, 0))
    da_spec = pl.BlockSpec((1, DA_HEADS, tm, DA_V_DIM), lambda i, j, *_: (i + b0, 0, j, 0))
    in_specs = [tile_in(d), da_spec, tile_in(d // 2), const((d, d)), const((1, d)), const((d, d)),
                const((1, d // CROSS_HEADS)), per_b(), per_b(), const((d, d)), const((1, d)),
                const((d, LANES)), const((1, LANES))]
    out_specs = [tile_out(d), tile_out(d // 2), tile_out(LANES),
                 pl.BlockSpec((1, 1, 8, tm), lambda i, j, *_: (i, j, 0, 0)), const((8, LANES))]
    out_shape = [
        jax.ShapeDtypeStruct((bh, s, d), F32),
        jax.ShapeDtypeStruct((bh, s, d // 2), U32),
        jax.ShapeDtypeStruct((bh, s, LANES), F32),
        jax.ShapeDtypeStruct((bh, ns, 8, tm), F32),
        jax.ShapeDtypeStruct((8, LANES), F32),
    ]
    args = (x, da, gla, w_o.astype(BF16), norm_cross.reshape(1, d), w_cq.astype(BF16), cross_qn.reshape(1, -1),
            k_mem, v_mem, w_co.astype(BF16), norm_ffn.reshape(1, d), w_r.astype(BF16), b_r)
    scalars, scratch = (), []
    if rider is not None:
        sizes, pend, dest, xn_src, n_rows = rider
        scalars = (sizes, pend)
        in_specs += [pl.BlockSpec((1, 2, tm), lambda i, j, *_: (i * ns + j, 0, 0), memory_space=pltpu.SMEM),
                     pl.BlockSpec((tm, d // 2), lambda i, j, *_: (i * ns + j, 0))]
        out_specs.append(pl.BlockSpec(memory_space=pl.ANY))
        out_shape.append(jax.ShapeDtypeStruct((n_rows, d // 2), U32))
        scratch = [pltpu.VMEM((EXPERT_ROWS, d // 2), U32), pltpu.SemaphoreType.DMA, pltpu.SemaphoreType.DMA]
        args += (dest, xn_src)
    return pl.pallas_call(
        functools.partial(_post_kernel, d=d, tm=tm, sub=SUB_POST, rider=rider is not None),
        grid_spec=pltpu.PrefetchScalarGridSpec(
            num_scalar_prefetch=len(scalars), grid=(bh, ns), in_specs=in_specs, out_specs=out_specs,
            scratch_shapes=scratch),
        out_shape=out_shape,
        compiler_params=_params("arbitrary", "arbitrary"),
        name="post_scatter" if rider is not None else "post",
    )(*scalars, *args)


def _row_copy(src_ref, src_row, dst_ref, dst_row, sem):
    return pltpu.make_async_copy(src_ref.at[pl.ds(src_row, 1)], dst_ref.at[pl.ds(dst_row, 1)], sem)


def _zero_fill(size_ref, pend_ref, xpad_ref, zero_ref, zsem):
    zero_ref[...] = jnp.zeros(zero_ref.shape, zero_ref.dtype)
    rows = zero_ref.shape[0]
    n_blocks = xpad_ref.shape[0] // rows
    n_used = pend_ref[N_EXPERTS - 1] // rows

    def zero_block(blk):
        return pltpu.make_async_copy(zero_ref, xpad_ref.at[pl.ds(pl.multiple_of(blk * rows, rows), rows)], zsem)

    def last_block(e, fn):
        @pl.when(size_ref[e] > 0)
        def _():
            fn(zero_block(pend_ref[e] // rows - 1))

    for fn in (lambda cp: cp.start(), lambda cp: cp.wait()):
        lax.fori_loop(0, N_EXPERTS, lambda e, c: (last_block(e, fn), c)[1], 0)
        lax.fori_loop(n_used, n_blocks, lambda blk, c: (fn(zero_block(blk)), c)[1], 0)


def _scatter_rows(dest_ref, src_ref, xpad_ref, sem):
    for t in range(src_ref.shape[0]):
        for k in range(2):
            _row_copy(src_ref, t, xpad_ref, dest_ref[0, k, t], sem).start(priority=k)


def _scatter_wait(src_ref, xpad_ref, sem):
    for _ in range(2):
        pltpu.make_async_copy(src_ref, xpad_ref.at[pl.ds(0, src_ref.shape[0])], sem).wait()


def _cast_expert_weights(fresh, wg_ref, wu_ref, wd_ref, wg_s, wu_s, wd_s):
    @pl.when(fresh)
    def _():
        wg_s[...] = wg_ref[0].astype(BF16)
        wu_s[...] = wu_ref[0].astype(BF16)
        wd_s[...] = wd_ref[0].astype(BF16)


def _expert_mlp(x_ref, wg_s, wu_s, wd_s, out_ref):
    words = x_ref[...]
    half = words.shape[1]
    lo = lax.bitcast_convert_type(words << 16, F32).astype(BF16)
    hi = lax.bitcast_convert_type(words & HI16, F32).astype(BF16)
    gate = _dot(lo, wg_s[:half, :]) + _dot(hi, wg_s[half:, :])
    up = _dot(lo, wu_s[:half, :]) + _dot(hi, wu_s[half:, :])
    hid = gate / (1.0 + jnp.exp(-gate)) * up
    out_ref[...] = _dot(hid.astype(BF16), wd_s[...])


def _experts_scatter_kernel(be_ref, nused_ref, size_ref, pend_ref, x_ref, wg_ref, wu_ref, wd_ref, rdest_ref, rsrc_ref,
                            out_ref, rpad_ref, wg_s, wu_s, wd_s, zero_ref, rsem, zsem, *, rider_steps):
    i = pl.program_id(0)
    used = i < nused_ref[0]
    new_expert = jnp.logical_or(i == 0, be_ref[i] != be_ref[jnp.maximum(i - 1, 0)])
    mlp = functools.partial(_expert_mlp, x_ref, wg_s, wu_s, wd_s, out_ref)

    @pl.when(i == 0)
    def _():
        _zero_fill(size_ref, pend_ref, rpad_ref, zero_ref, zsem)

    _cast_expert_weights(jnp.logical_and(used, new_expert), wg_ref, wu_ref, wd_ref, wg_s, wu_s, wd_s)
    riding = i < rider_steps

    @pl.when(riding)
    def _():
        _scatter_rows(rdest_ref, rsrc_ref, rpad_ref, rsem)
        mlp()
        _scatter_wait(rsrc_ref, rpad_ref, rsem)

    pl.when(jnp.logical_and(used, jnp.logical_not(riding)))(mlp)

    @pl.when(jnp.logical_not(used))
    def _():
        out_ref[...] = jnp.zeros(out_ref.shape, F32)


def _gather_rows(dest_ref, src_ref, buf_ref, sem):
    for t in range(buf_ref.shape[1]):
        for k in range(2):
            _row_copy(src_ref, dest_ref[0, k, t], buf_ref.at[k], t, sem).start(priority=k)


def _combine_rows(h_ref, meta_ref, src_ref, buf_ref, sem):
    for k in range(2):
        pltpu.make_async_copy(src_ref.at[pl.ds(0, buf_ref.shape[1])], buf_ref.at[k], sem).wait()
    meta = meta_ref[...]
    g0 = meta[:, _META_G0:_META_G0 + 1]
    g1 = meta[:, _META_G1:_META_G1 + 1]
    return h_ref[...] + g0 * buf_ref[0] + g1 * buf_ref[1]


def _experts_gather_kernel(be_ref, nused_ref, x_ref, wg_ref, wu_ref, wd_ref, gdest_ref, h_ref, meta_ref, opad_ref,
                           out_ref, y_ref, wg_s, wu_s, wd_s, gbuf_ref, gsem, *, rider_steps):
    i = pl.program_id(0)
    used = i < nused_ref[0]
    new_expert = jnp.logical_or(i == 0, be_ref[i] != be_ref[jnp.maximum(i - 1, 0)])
    mlp = functools.partial(_expert_mlp, x_ref, wg_s, wu_s, wd_s, out_ref)
    _cast_expert_weights(jnp.logical_and(used, new_expert), wg_ref, wu_ref, wd_ref, wg_s, wu_s, wd_s)

    def gather():
        _gather_rows(gdest_ref, opad_ref, gbuf_ref.at[i % 2], gsem.at[i % 2])

    def combine():
        slot = (i - 1) % 2
        y_ref[...] = _combine_rows(h_ref, meta_ref, opad_ref, gbuf_ref.at[slot], gsem.at[slot])

    @pl.when(i == 0)
    def _():
        gather()
        mlp()

    @pl.when(jnp.logical_and(i > 0, i < rider_steps))
    def _():
        gather()
        mlp()
        combine()

    @pl.when(i == rider_steps)
    def _():
        combine()
        pl.when(used)(mlp)

    pl.when(jnp.logical_and(i > rider_steps, used))(mlp)

    @pl.when(jnp.logical_not(used))
    def _():
        out_ref[...] = jnp.zeros(out_ref.shape, F32)


def _experts_kernel(be_ref, nused_ref, x_ref, wg_ref, wu_ref, wd_ref, out_ref, wg_s, wu_s, wd_s):
    i = pl.program_id(0)
    used = i < nused_ref[0]
    new_expert = jnp.logical_or(i == 0, be_ref[i] != be_ref[jnp.maximum(i - 1, 0)])
    _cast_expert_weights(jnp.logical_and(used, new_expert), wg_ref, wu_ref, wd_ref, wg_s, wu_s, wd_s)
    pl.when(used)(functools.partial(_expert_mlp, x_ref, wg_s, wu_s, wd_s, out_ref))

    @pl.when(jnp.logical_not(used))
    def _():
        out_ref[...] = jnp.zeros(out_ref.shape, F32)


def _retile(dest, tr):
    tiles, _, tm = dest.shape
    return dest.reshape(tiles, 2, tm // tr, tr).transpose(0, 2, 1, 3).reshape(tiles * (tm // tr), 2, tr)


def _experts(x_pad, block_expert, n_used, w_gate, w_up, w_down, scatter=None, gather=None):
    n_rows = x_pad.shape[0]
    _, d, f = w_gate.shape
    rows, tr = EXPERT_ROWS, RIDER_ROWS
    row_blk = lambda i, be, nu, *_: (jnp.minimum(i, nu[0] - 1), 0)
    weights = lambda shape: pl.BlockSpec(shape, lambda i, be, *_: (be[i], 0, 0))
    in_specs = [pl.BlockSpec((rows, d // 2), row_blk), weights((1, d, f)), weights((1, d, f)), weights((1, f, d))]
    out_specs = [pl.BlockSpec((rows, d), lambda i, *_: (i, 0))]
    out_shape = [jax.ShapeDtypeStruct((n_rows, d), F32)]
    scratch = [pltpu.VMEM((d, f), BF16), pltpu.VMEM((d, f), BF16), pltpu.VMEM((f, d), BF16)]
    scalars, args = (block_expert, n_used), (x_pad, w_gate, w_up, w_down)
    body, name = _experts_kernel, "experts"
    if scatter is not None or gather is not None:
        n_tokens = (scatter[3] if scatter is not None else gather[1]).shape[0]
        steps = n_tokens // tr
        assert 2 * n_tokens >= steps * rows, "riding steps must all be used expert blocks"
        tile = lambda width, shift: pl.BlockSpec((tr, width), lambda i, *_: (jnp.clip(i - shift, 0, steps - 1), 0))
        ids = lambda: pl.BlockSpec((1, 2, tr), lambda i, *_: (jnp.minimum(i, steps - 1), 0, 0),
                                   memory_space=pltpu.SMEM)
    if scatter is not None:
        sizes, pend, dest, xn_src, other_rows = scatter
        scalars += (sizes, pend)
        in_specs += [ids(), tile(d // 2, 0)]
        out_specs.append(pl.BlockSpec(memory_space=pl.ANY))
        out_shape.append(jax.ShapeDtypeStruct((other_rows, d // 2), U32))
        scratch += [pltpu.VMEM((rows, d // 2), U32), pltpu.SemaphoreType.DMA, pltpu.SemaphoreType.DMA]
        args += (_retile(dest, tr), xn_src)
        body, name = functools.partial(_experts_scatter_kernel, rider_steps=steps), "experts_scatter"
    elif gather is not None:
        dest, h2, meta, opad = gather
        in_specs += [ids(), tile(d, 1), tile(LANES, 1), pl.BlockSpec(memory_space=pl.ANY)]
        out_specs.append(tile(d, 1))
        out_shape.append(jax.ShapeDtypeStruct((n_tokens, d), F32))
        scratch += [pltpu.VMEM((2, 2, tr, d), F32), pltpu.SemaphoreType.DMA((2,))]
        args += (_retile(dest, tr), h2, meta, opad)
        body, name = functools.partial(_experts_gather_kernel, rider_steps=steps), "experts_gather"
    out = pl.pallas_call(
        body,
        grid_spec=pltpu.PrefetchScalarGridSpec(
            num_scalar_prefetch=len(scalars), grid=(n_rows // rows,), in_specs=in_specs, out_specs=out_specs,
            scratch_shapes=scratch),
        out_shape=out_shape,
        compiler_params=_params("arbitrary"),
        name=name,
    )(*scalars, *args)
    return out if len(out) > 1 else out[0]


def _combine_tail_kernel(dest_ref, y0_ref, h_ref, meta_ref, opad_ref, y_ref, buf_ref, sem, *, nt):
    s = pl.program_id(0)

    @pl.when(s < nt)
    def _():
        _gather_rows(dest_ref, opad_ref, buf_ref.at[s % 2], sem.at[s % 2])

    @pl.when(s > 0)
    def _():
        slot = (s - 1) % 2
        y_ref[0] = y0_ref[...]
        y_ref[1] = _combine_rows(h_ref, meta_ref, opad_ref, buf_ref.at[slot], sem.at[slot])


def _combine_tail(y0, h2, meta, dest, out_pad):
    th, d = h2.shape
    tm = TM_TAIL
    nt = th // tm
    prev = lambda s: (jnp.maximum(s - 1, 0), 0)
    return pl.pallas_call(
        functools.partial(_combine_tail_kernel, nt=nt),
        grid=(nt + 1,),
        in_specs=[
            pl.BlockSpec((1, 2, tm), lambda s: (jnp.minimum(s, nt - 1), 0, 0), memory_space=pltpu.SMEM),
            pl.BlockSpec((tm, d), prev), pl.BlockSpec((tm, d), prev), pl.BlockSpec((tm, LANES), prev),
            pl.BlockSpec(memory_space=pl.ANY),
        ],
        out_specs=pl.BlockSpec((2, tm, d), lambda s: (0, jnp.maximum(s - 1, 0), 0)),
        scratch_shapes=[pltpu.VMEM((2, 2, tm, d), F32), pltpu.SemaphoreType.DMA((2,))],
        out_shape=jax.ShapeDtypeStruct((2, th, d), F32),
        compiler_params=_params("arbitrary"),
        name="combine_tail",
    )(_retile(dest, tm), y0, h2, meta, out_pad)


def _route_tables(counts, ids, n_tokens):
    rows = EXPERT_ROWS
    sizes = counts[0, :N_EXPERTS].astype(I32)
    padded = (sizes + rows - 1) // rows * rows
    pend = jnp.cumsum(padded)
    pstart = pend - padded
    n_rows = 2 * n_tokens + N_EXPERTS * rows
    block_start = jnp.arange(n_rows // rows, dtype=I32) * rows
    block_expert = jnp.minimum(jnp.sum(pend[None, :] <= block_start[:, None], axis=1), N_EXPERTS - 1).astype(I32)
    n_used = (pend[-1:] // rows).astype(I32)
    expert = ids[:, _META_E0:_META_E1 + 1].astype(I32)
    rank = ids[:, _META_P0:_META_P1 + 1].astype(I32)
    experts = jnp.arange(N_EXPERTS, dtype=I32).reshape(-1, 1, 1, 1)
    dest = jnp.sum(jnp.where(expert[None] == experts, pstart.reshape(-1, 1, 1, 1), 0), axis=0) + rank
    return sizes, pend, block_expert, n_used, dest, n_rows


def kernel(x, mem, norm_mix, w_in, da_q_norm, da_k_norm, lambda_q1, lambda_k1, lambda_q2, lambda_k2,
           da_out_norm, gla_gate_w, gla_gate_b, gla_out_norm, w_o, norm_cross, norm_mem, w_cq, w_ckv,
           cross_q_norm, cross_k_norm, w_co, norm_ffn, w_group, b_group, w_expert, b_expert,
           w_e_gate, w_e_up, w_e_down):
    b, s, d = x.shape
    th = b // 2 * s
    h = x
    for l in range(norm_mix.shape[0]):
        assert l == 0, "lam_init is fixed for a single layer"
        qt, kda, vt, gq, gk, gv, gg, la = _in_proj(h, norm_mix[l], w_in[l], da_q_norm[l], da_k_norm[l],
                                                   gla_gate_w[l], gla_gate_b[l])
        da = _diff_attn(qt, kda, vt, lambda_q1[l], lambda_k1[l], lambda_q2[l], lambda_k2[l], da_out_norm[l],
                        da_q_norm[l], da_k_norm[l])
        gla = _gla(gq, gk, la, gv, gg, gla_out_norm[l])
        k_mem, v_mem = _mem_kv(mem, norm_mem[l], w_ckv[l], cross_k_norm[l])
        post = functools.partial(_post, h, da, gla, w_o[l], norm_cross[l], w_cq[l], cross_q_norm[l], k_mem, v_mem,
                                 w_co[l], norm_ffn[l], w_group[l], b_group[l], w_expert[l], b_expert[l])
        experts = functools.partial(_experts, w_gate=w_e_gate[l], w_up=w_e_up[l], w_down=w_e_down[l])

        h2_0, xn_0, meta_0, ids_0, counts_0 = post(half=0)
        sizes_0, pend_0, be_0, used_0, dest_0, n_rows = _route_tables(counts_0, ids_0.reshape(-1, 8, TM_POST), th)
        h2_1, xn_1, meta_1, ids_1, counts_1, xpad_0 = post(
            half=1, rider=(sizes_0, pend_0, dest_0, xn_0.reshape(th, d // 2), n_rows))
        sizes_1, pend_1, be_1, used_1, dest_1, _ = _route_tables(counts_1, ids_1.reshape(-1, 8, TM_POST), th)
        opad_0, xpad_1 = experts(xpad_0, be_0, used_0,
                                 scatter=(sizes_1, pend_1, dest_1, xn_1.reshape(th, d // 2), n_rows))
        opad_1, y_0 = experts(xpad_1, be_1, used_1,
                              gather=(dest_0, h2_0.reshape(th, d), meta_0.reshape(th, LANES), opad_0))
        h = _combine_tail(y_0, h2_1.reshape(th, d), meta_1.reshape(th, LANES), dest_1, opad_1).reshape(b, s, d)
    return h
```

```python
import functools
import math

import jax
import jax.numpy as jnp
import numpy as np
from jax import lax
from jax.experimental import pallas as pl
from jax.experimental.pallas import tpu as pltpu

F32 = jnp.float32
BF16 = jnp.bfloat16
I32 = jnp.int32
U32 = jnp.uint32
HI16 = np.uint32(0xFFFF0000)

EPS = 1e-6
CHUNK = 64

DA_HEADS = 4
DA_QK_DIM = 64
DA_V_DIM = 128
GLA_HEADS = 4
GLA_K_DIM = 64
GLA_V_DIM = 128
GLA_GATE_RANK = 16
GLA_TAU = 16.0
CROSS_HEADS = 4
N_GROUPS = 4
EXPERTS_PER_GROUP = 8
N_EXPERTS = N_GROUPS * EXPERTS_PER_GROUP
LAM_INIT = 0.8 - 0.6 * math.exp(-0.3 * 0)

LANES = 128
VMEM_LIMIT = 56 * 1024 * 1024

TM_PROJ = 1024
ATT_BLK = 512
TS_GLA = 2048
GLA_GROUP = 4
TM_POST = 1024
SUB_POST = 256
TM_TAIL = 512
RIDER_ROWS = 256
EXPERT_ROWS = 512

NEG_INF = float("-inf")


def _params(*sem):
    return pltpu.CompilerParams(dimension_semantics=sem, vmem_limit_bytes=VMEM_LIMIT)


def _rms(t, g):
    ms = jnp.mean(t * t, axis=-1, keepdims=True)
    return t * lax.rsqrt(ms + EPS) * g


def _dot(a, b):
    return jnp.dot(a, b, preferred_element_type=F32)


def _dot_nt(a, b):
    return lax.dot_general(a, b, (((1,), (1,)), ((), ())), preferred_element_type=F32)


def _dot_tn(a, b):
    return lax.dot_general(a, b, (((0,), (0,)), ((), ())), preferred_element_type=F32)


def _split_bf16(t):
    hi = t.astype(BF16)
    lo = (t - hi.astype(F32)).astype(BF16)
    return hi, lo


def _mem_kv_kernel(mem_ref, g_ref, w_ref, kn_ref, k_ref, v_ref, *, d, heads):
    mn = _rms(mem_ref[0], g_ref[...]).astype(BF16)
    kv = _dot(mn, w_ref[...])
    hd = d // heads
    scale = hd ** -0.5
    for h in range(heads):
        kh = _rms(kv[:, h * hd:(h + 1) * hd], kn_ref[...]) * scale
        k_ref[0, :, h * hd:(h + 1) * hd] = kh.astype(BF16)
    v_ref[0] = kv[:, d:].astype(BF16)


def _mem_kv(mem, norm_m, w_ckv, kn):
    b, m, d = mem.shape
    return pl.pallas_call(
        functools.partial(_mem_kv_kernel, d=d, heads=CROSS_HEADS),
        grid=(b,),
        in_specs=[
            pl.BlockSpec((1, m, d), lambda i: (i, 0, 0)),
            pl.BlockSpec((1, d), lambda i: (0, 0)),
            pl.BlockSpec((d, 2 * d), lambda i: (0, 0)),
            pl.BlockSpec((1, d // CROSS_HEADS), lambda i: (0, 0)),
        ],
        out_specs=[
            pl.BlockSpec((1, m, d), lambda i: (i, 0, 0)),
            pl.BlockSpec((1, m, d), lambda i: (i, 0, 0)),
        ],
        out_shape=[jax.ShapeDtypeStruct((b, m, d), BF16)] * 2,
        compiler_params=_params("parallel"),
        name="mem_kv",
    )(mem, norm_m.reshape(1, d), w_ckv.astype(BF16), kn.reshape(1, -1))


_QK = DA_HEADS * 2 * DA_QK_DIM
_DAW = DA_HEADS * DA_V_DIM
_GQK = GLA_HEADS * GLA_K_DIM
_GW = GLA_HEADS * GLA_V_DIM
_OFF_DQ = 0
_OFF_DK = _OFF_DQ + _QK
_OFF_DV = _OFF_DK + _QK
_OFF_GQ = _OFF_DV + _DAW
_OFF_GK = _OFF_GQ + _GQK
_OFF_GV = _OFF_GK + _GQK
_OFF_GG = _OFF_GV + _GW
_OFF_GR = _OFF_GG + _GW
_IN_PAD = _OFF_GR + LANES


def _in_proj_kernel(x_ref, g_ref, w_ref, qg_ref, kg_ref, grp_ref, gw_ref, gb_ref,
                    qt_ref, k_ref, vt_ref, gq_ref, gk_ref, gv_ref, gg_ref, la_ref):
    u = _rms(x_ref[0], g_ref[...]).astype(BF16)

    def proj(off, width):
        return _dot(u, w_ref[:, off:off + width])

    def mean_square(p):
        return _dot((p * p).astype(BF16), grp_ref[...])

    p_q = proj(_OFF_DQ, _QK)
    p_k = proj(_OFF_DK, _QK)
    ms_q = mean_square(p_q)
    qn = p_q * lax.rsqrt(ms_q + EPS) * qg_ref[...] * (DA_QK_DIM ** -0.5 * math.log2(math.e))
    blk = qt_ref.shape[3]
    for t in range(qt_ref.shape[1]):
        qt_ref[0, t] = qn[t * blk:(t + 1) * blk].T.astype(BF16)
    dv = proj(_OFF_DV, _DAW)
    ms_k = mean_square(p_k)
    kn = (p_k * lax.rsqrt(ms_k + EPS) * kg_ref[...]).astype(BF16)
    for h in range(DA_HEADS):
        k_ref[0, h] = kn[:, h * 2 * DA_QK_DIM:(h + 1) * 2 * DA_QK_DIM]
    for t in range(vt_ref.shape[1]):
        vt_ref[0, t] = dv[t * blk:(t + 1) * blk].T.astype(BF16)
    g_r = proj(_OFF_GR, LANES)
    gq_ref[0] = proj(_OFF_GQ, _GQK) * (GLA_K_DIM ** -0.5)
    gk_ref[0] = proj(_OFF_GK, _GQK)
    z = _dot(g_r.astype(BF16), gw_ref[...]) + gb_ref[...]
    gv_ref[0] = proj(_OFF_GV, _GW).astype(BF16)
    gg_ref[0] = proj(_OFF_GG, _GW)
    log_sig = jnp.minimum(z, 0.0) - jnp.log(1.0 + jnp.exp(-jnp.abs(z)))
    la_ref[0] = log_sig * (math.log2(math.e) / GLA_TAU)


def _in_proj(x, norm_g, w_in, da_qn, da_kn, gate_w, gate_b):
    b, s, d = x.shape
    tm, blk = TM_PROJ, ATT_BLK
    ns, nb = s // tm, tm // blk
    w = jnp.pad(w_in, ((0, 0), (0, _IN_PAD - w_in.shape[1]))).astype(BF16)
    gw = jnp.pad(gate_w, ((0, LANES - GLA_GATE_RANK), (0, 0))).astype(BF16)
    lane = jnp.arange(_QK)
    grp = jnp.where((lane[:, None] // DA_QK_DIM) == (lane[None, :] // DA_QK_DIM),
                    1.0 / DA_QK_DIM, 0.0).astype(BF16)
    const = lambda shape: pl.BlockSpec(shape, lambda i, j: (0,) * len(shape))
    tile = lambda width: pl.BlockSpec((1, tm, width), lambda i, j: (i, j, 0))
    tile_t = lambda width: pl.BlockSpec((1, nb, width, blk), lambda i, j: (i, j, 0, 0))
    return pl.pallas_call(
        _in_proj_kernel,
        grid=(b, ns),
        in_specs=[tile(d), const((1, d)), const((d, _IN_PAD)), const((1, _QK)), const((1, _QK)),
                  const((_QK, _QK)), const((LANES, _GQK)), const((1, _GQK))],
        out_specs=[tile_t(_QK), pl.BlockSpec((1, DA_HEADS, tm, 2 * DA_QK_DIM), lambda i, j: (i, 0, j, 0)),
                   tile_t(_DAW), tile(_GQK), tile(_GQK), tile(_GW), tile(_GW), tile(_GQK)],
        out_shape=[
            jax.ShapeDtypeStruct((b, s // blk, _QK, blk), BF16),
            jax.ShapeDtypeStruct((b, DA_HEADS, s, 2 * DA_QK_DIM), BF16),
            jax.ShapeDtypeStruct((b, s // blk, _DAW, blk), BF16),
            jax.ShapeDtypeStruct((b, s, _GQK), F32),
            jax.ShapeDtypeStruct((b, s, _GQK), F32),
            jax.ShapeDtypeStruct((b, s, _GW), BF16),
            jax.ShapeDtypeStruct((b, s, _GW), F32),
            jax.ShapeDtypeStruct((b, s, _GQK), F32),
        ],
        compiler_params=_params("parallel", "parallel"),
        name="in_proj",
    )(x, norm_g.reshape(1, d), w, jnp.tile(da_qn, 2 * DA_HEADS).reshape(1, _QK),
      jnp.tile(da_kn, 2 * DA_HEADS).reshape(1, _QK), grp, gw, gate_b.reshape(1, _GQK))


def _split_q(qt):
    row = lax.broadcasted_iota(I32, qt.shape, 0)
    zero = jnp.zeros_like(qt)
    return jnp.where(row < DA_QK_DIM, qt, zero), jnp.where(row >= DA_QK_DIM, qt, zero)


def _chunk_causal_mask(blk):
    key_chunk = lax.broadcasted_iota(I32, (blk, blk), 0) // CHUNK
    qry_chunk = lax.broadcasted_iota(I32, (blk, blk), 1) // CHUNK
    return key_chunk <= qry_chunk


def _diff_attn_finish(lq1_ref, lk1_ref, lq2_ref, lk2_ref, gain_ref, a1, l1, a2, l2):
    lam = (jnp.exp(jnp.sum(lq1_ref[...] * lk1_ref[...], axis=-1, keepdims=True))
           - jnp.exp(jnp.sum(lq2_ref[...] * lk2_ref[...], axis=-1, keepdims=True)) + LAM_INIT)
    o = a1 / l1 - lam * (a2 / l2)
    ms = jnp.mean(o * o, axis=0, keepdims=True)
    o = o * lax.rsqrt(ms + EPS) * gain_ref[...] * (1.0 - LAM_INIT)
    return o.T.astype(BF16)


def _diff_attn_bounded_kernel(lq1_ref, lk1_ref, lq2_ref, lk2_ref, gain_ref, qt_ref, k_ref, vt_ref, out_ref,
                              s_ref, l1_ref, a1_ref, l2_ref, a2_ref, *, blk, nb):
    stats = ((l1_ref, a1_ref), (l2_ref, a2_ref))
    mask = _chunk_causal_mask(blk)

    def reset():
        for l_ref, a_ref in stats:
            l_ref[...] = jnp.zeros(l_ref.shape, F32)
            a_ref[...] = jnp.zeros(a_ref.shape, F32)

    def scores(q, j, slot):
        kb = k_ref[0, 0, pl.ds(pl.multiple_of(j * blk, blk), blk), :]
        s_ref[slot, 0] = _dot(kb, q[0])
        s_ref[slot, 1] = _dot(kb, q[1])

    def consume(j, slot, masked):
        vb = vt_ref[0, j]
        for m, (l_ref, a_ref) in enumerate(stats):
            s = s_ref[slot, m]
            if masked:
                s = jnp.where(mask, s, NEG_INF)
            p = jnp.exp2(s)
            l_ref[...] += jnp.sum(p, axis=0, keepdims=True)
            a_ref[...] += _dot(vb, p.astype(BF16))

    def step(q, j, slot):
        scores(q, j + 1, 1 - slot)
        consume(j, slot, False)

    reset()
    q = _split_q(qt_ref[0, 0])
    slot = 0
    scores(q, 0, slot)
    for qi in range(nb):
        for j in range(qi):
            step(q, j, slot)
            slot = 1 - slot
        if qi + 1 < nb:
            q = _split_q(qt_ref[0, qi + 1])
            scores(q, 0, 1 - slot)
        consume(qi, slot, True)
        out_ref[0, 0, qi * blk:(qi + 1) * blk, :] = _diff_attn_finish(
            lq1_ref, lk1_ref, lq2_ref, lk2_ref, gain_ref, a1_ref[...], l1_ref[...], a2_ref[...], l2_ref[...])
        if qi + 1 < nb:
            reset()
        slot = 1 - slot


def _diff_attn_online_kernel(lq1_ref, lk1_ref, lq2_ref, lk2_ref, gain_ref, qt_ref, k_ref, vt_ref, out_ref,
                             m1_ref, l1_ref, a1_ref, m2_ref, l2_ref, a2_ref, *, blk):
    qi = pl.program_id(2)
    q1, q2 = _split_q(qt_ref[0, 0])

    for m_ref, l_ref, a_ref in ((m1_ref, l1_ref, a1_ref), (m2_ref, l2_ref, a2_ref)):
        m_ref[...] = jnp.full(m_ref.shape, NEG_INF, F32)
        l_ref[...] = jnp.zeros(l_ref.shape, F32)
        a_ref[...] = jnp.zeros(a_ref.shape, F32)

    def update(s, vb, m_ref, l_ref, a_ref):
        m_old = m_ref[...]
        m_new = jnp.maximum(m_old, jnp.max(s, axis=0, keepdims=True))
        alpha = jnp.exp2(m_old - m_new)
        p = jnp.exp2(s - m_new)
        l_ref[...] = alpha * l_ref[...] + jnp.sum(p, axis=0, keepdims=True)
        a_ref[...] = alpha * a_ref[...] + _dot(vb, p.astype(BF16))
        m_ref[...] = m_new

    def block(j, mask):
        kb = k_ref[0, 0, pl.ds(pl.multiple_of(j * blk, blk), blk), :]
        vb = vt_ref[0, j]
        s1 = _dot(kb, q1)
        s2 = _dot(kb, q2)
        if mask is not None:
            s1 = jnp.where(mask, s1, NEG_INF)
            s2 = jnp.where(mask, s2, NEG_INF)
        update(s1, vb, m1_ref, l1_ref, a1_ref)
        update(s2, vb, m2_ref, l2_ref, a2_ref)

    def body(j, carry):
        block(j, None)
        return carry

    lax.fori_loop(0, qi, body, 0)
    block(qi, _chunk_causal_mask(blk))
    out_ref[0, 0] = _diff_attn_finish(lq1_ref, lk1_ref, lq2_ref, lk2_ref, gain_ref,
                                      a1_ref[...], l1_ref[...], a2_ref[...], l2_ref[...])


SCORE_BOUND = 60.0


def _diff_attn(qt, k, vt, lq1, lk1, lq2, lk2, da_on, da_qn, da_kn):
    b, nb, _, blk = qt.shape
    s = nb * blk
    stat = lambda: pltpu.VMEM((1, blk), F32)
    acc = lambda: pltpu.VMEM((DA_V_DIM, blk), F32)

    args = (lq1.reshape(1, -1), lk1.reshape(1, -1), lq2.reshape(1, -1), lk2.reshape(1, -1),
            da_on.reshape(-1, 1), qt, k, vt)
    out_shape = jax.ShapeDtypeStruct((b, DA_HEADS, s, DA_V_DIM), BF16)
    head = lambda *trailing: (lambda i, h: (i, 0, h) + trailing)
    vec2 = lambda: pl.BlockSpec((1, DA_QK_DIM), lambda i, h: (0, 0))
    bounded = pl.pallas_call(
        functools.partial(_diff_attn_bounded_kernel, blk=blk, nb=nb),
        grid=(b, DA_HEADS),
        in_specs=[
            vec2(), vec2(), vec2(), vec2(),
            pl.BlockSpec((DA_V_DIM, 1), lambda i, h: (0, 0)),
            pl.BlockSpec((1, nb, 2 * DA_QK_DIM, blk), head(0)),
            pl.BlockSpec((1, 1, s, 2 * DA_QK_DIM), lambda i, h: (i, h, 0, 0)),
            pl.BlockSpec((1, nb, DA_V_DIM, blk), head(0)),
        ],
        out_specs=pl.BlockSpec((1, 1, s, DA_V_DIM), lambda i, h: (i, h, 0, 0)),
        out_shape=out_shape,
        scratch_shapes=[pltpu.VMEM((2, 2, blk, blk), F32), stat(), acc(), stat(), acc()],
        compiler_params=_params("parallel", "parallel"),
        name="diff_attn",
    )
    vec3 = lambda: pl.BlockSpec((1, DA_QK_DIM), lambda i, h, q: (0, 0))
    online = pl.pallas_call(
        functools.partial(_diff_attn_online_kernel, blk=blk),
        grid=(b, DA_HEADS, nb),
        in_specs=[
            vec3(), vec3(), vec3(), vec3(),
            pl.BlockSpec((DA_V_DIM, 1), lambda i, h, q: (0, 0)),
            pl.BlockSpec((1, 1, 2 * DA_QK_DIM, blk), lambda i, h, q: (i, q, h, 0)),
            pl.BlockSpec((1, 1, s, 2 * DA_QK_DIM), lambda i, h, q: (i, h, 0, 0)),
            pl.BlockSpec((1, nb, DA_V_DIM, blk), lambda i, h, q: (i, 0, h, 0)),
        ],
        out_specs=pl.BlockSpec((1, 1, blk, DA_V_DIM), lambda i, h, q: (i, h, q, 0)),
        out_shape=out_shape,
        scratch_shapes=[stat(), stat(), acc(), stat(), stat(), acc()],
        compiler_params=_params("parallel", "parallel", "parallel"),
        name="diff_attn_online",
    )
    bound = (1.01 * DA_QK_DIM ** 0.5 * math.log2(math.e)) * jnp.max(jnp.abs(da_qn)) * jnp.max(jnp.abs(da_kn))
    return lax.cond(bound <= SCORE_BOUND, bounded, online, *args)


def _gla_kernel(q_ref, k_ref, la_ref, v_ref, g_ref, gain_ref, out_ref, st_ref, *, ts, group):
    @pl.when(pl.program_id(1) == 0)
    def _():
        st_ref[...] = jnp.zeros(st_ref.shape, F32)

    c = CHUNK
    rows = group * c
    hk, hv = _GQK, _GW
    r = lax.broadcasted_iota(I32, (rows, rows), 0)
    cc = lax.broadcasted_iota(I32, (rows, rows), 1)
    tri = jnp.where(jnp.logical_and(r // c == cc // c, r >= cc), 1.0, 0.0).astype(BF16)
    bd_k = (lax.broadcasted_iota(I32, (hk, hk), 0) // GLA_K_DIM
            == lax.broadcasted_iota(I32, (hk, hk), 1) // GLA_K_DIM)
    bd_v = (lax.broadcasted_iota(I32, (hk, hv), 0) // GLA_K_DIM
            == lax.broadcasted_iota(I32, (hk, hv), 1) // GLA_V_DIM)
    pair_w = 2 * GLA_K_DIM
    low_sq = lax.broadcasted_iota(I32, (GLA_V_DIM, pair_w), 1) < GLA_K_DIM
    low_q = lax.broadcasted_iota(I32, (c, pair_w), 1) < GLA_K_DIM
    lower = (lax.broadcasted_iota(I32, (c, hk), 0)
             >= lax.broadcasted_iota(I32, (c, hk), 1) % c)

    def tiled(t, mask):
        t4 = jnp.concatenate([t] * GLA_HEADS, axis=0)
        return jnp.where(mask, t4, jnp.zeros_like(t4))

    def chunk_row(t, row):
        return jnp.concatenate([jnp.broadcast_to(t[i * c + row:i * c + row + 1, :], (c, hk)) for i in range(group)],
                               axis=0)

    def cum_decay(gi):
        la_hi, la_lo = _split_bf16(la_ref[0, pl.ds(pl.multiple_of(gi * rows, rows), rows), :])
        return _dot(tri, la_hi) + _dot(tri, la_lo)

    n_groups = ts // rows

    def body(gi, big_l):
        sl = pl.ds(pl.multiple_of(gi * rows, rows), rows)
        l_end = chunk_row(big_l, c - 1)
        lc = big_l - chunk_row(big_l, c // 2 - 1)
        e_pos = jnp.exp2(lc)
        e_neg = jnp.exp2(-lc)
        q = q_ref[0, sl, :]
        k = k_ref[0, sl, :]
        v = v_ref[0, sl, :]
        q_pos = (q * e_pos).astype(BF16)
        q_neg = (q * e_neg).astype(BF16)
        k_pos = (k * e_pos).astype(BF16)
        k_neg = (k * e_neg).astype(BF16)
        q_in = (q * jnp.exp2(big_l)).astype(BF16)
        k_out = (k * jnp.exp2(l_end - big_l)).astype(BF16)
        decay = jnp.exp2(l_end)

        chunks = [slice(i * c, (i + 1) * c) for i in range(group)]
        a_past = [_dot_nt(q_pos[cs], tiled(k_neg[cs], bd_k)) for cs in chunks]
        a_fut = [_dot_nt(q_neg[cs], tiled(k_pos[cs], bd_k)) for cs in chunks]
        next_l = cum_decay(jnp.minimum(gi + 1, n_groups - 1))
        inc = [[_dot_tn(v[cs, h * GLA_V_DIM:(h + 1) * GLA_V_DIM], k_out[cs, h // 2 * pair_w:(h // 2 + 1) * pair_w])
                for h in range(GLA_HEADS)] for cs in chunks]
        u_t = [[jnp.where(low_sq, r[2 * j], r[2 * j + 1]) for j in range(GLA_HEADS // 2)] for r in inc]
        a = [jnp.where(lower, p, f).astype(BF16) for p, f in zip(a_past, a_fut)]
        o_intra = [_dot(a[i], tiled(v[cs], bd_v)) for i, cs in enumerate(chunks)]

        st = [st_ref[j] for j in range(GLA_HEADS // 2)]
        o_inter = []
        for i in range(group):
            cs = slice(i * c, (i + 1) * c)
            st_bf = [t.astype(BF16) for t in st]
            heads = []
            for h in range(GLA_HEADS):
                j = h // 2
                qp = q_in[cs, j * pair_w:(j + 1) * pair_w]
                qh = jnp.where(low_q if h % 2 == 0 else jnp.logical_not(low_q), qp, jnp.zeros_like(qp))
                heads.append(_dot_nt(qh, st_bf[j]))
            o_inter.append(jnp.concatenate(heads, axis=-1))
            st = [st[j] * decay[i * c:i * c + 1, j * pair_w:(j + 1) * pair_w] + u_t[i][j]
                  for j in range(GLA_HEADS // 2)]
        for j in range(GLA_HEADS // 2):
            st_ref[j] = st[j]

        o = jnp.concatenate(o_intra, axis=0) + jnp.concatenate(o_inter, axis=0)
        g = g_ref[0, sl, :]
        silu = g / (1.0 + jnp.exp(-g))
        for h in range(GLA_HEADS):
            hs = slice(h * GLA_V_DIM, (h + 1) * GLA_V_DIM)
            out_ref[0, sl, hs] = (_rms(o[:, hs], gain_ref[...]) * silu[:, hs]).astype(BF16)
        return next_l

    big_l = cum_decay(0)
    for gi in range(n_groups):
        big_l = body(gi, big_l)


def _gla(gq, gk, la, gv, gg, gla_on):
    b, s, _ = gq.shape
    ts = TS_GLA
    tile = lambda width: pl.BlockSpec((1, ts, width), lambda i, j: (i, j, 0))
    return pl.pallas_call(
        functools.partial(_gla_kernel, ts=ts, group=GLA_GROUP),
        grid=(b, s // ts),
        in_specs=[tile(_GQK), tile(_GQK), tile(_GQK), tile(_GW), tile(_GW),
                  pl.BlockSpec((1, GLA_V_DIM), lambda i, j: (0, 0))],
        out_specs=tile(_GW),
        out_shape=jax.ShapeDtypeStruct((b, s, _GW), BF16),
        scratch_shapes=[pltpu.VMEM((GLA_HEADS // 2, GLA_V_DIM, 2 * GLA_K_DIM), F32)],
        compiler_params=_params("parallel", "arbitrary"),
        name="gla",
    )(gq, gk, la, gv, gg, gla_on.reshape(1, -1))


_META_E0, _META_E1, _META_G0, _META_G1, _META_P0, _META_P1 = range(6)
_EXP_LANE0 = N_GROUPS


def _post_kernel(*refs, d, tm, sub, rider):
    if rider:
        (size_ref, pend_ref, x_ref, da_ref, gla_ref, wo_ref, gc_ref, wq_ref, qn_ref, km_ref, vm_ref, wco_ref,
         gf_ref, wr_ref, br_ref, rdest_ref, rsrc_ref,
         h_ref, xn_ref, meta_ref, ids_ref, cnt_ref, rpad_ref, zero_ref, rsem, zsem) = refs
    else:
        (x_ref, da_ref, gla_ref, wo_ref, gc_ref, wq_ref, qn_ref, km_ref, vm_ref, wco_ref,
         gf_ref, wr_ref, br_ref, h_ref, xn_ref, meta_ref, ids_ref, cnt_ref) = refs
    first = jnp.logical_and(pl.program_id(0) == 0, pl.program_id(1) == 0)

    @pl.when(first)
    def _():
        cnt_ref[...] = jnp.zeros(cnt_ref.shape, F32)
        if rider:
            _zero_fill(size_ref, pend_ref, rpad_ref, zero_ref, zsem)

    if rider:
        _scatter_rows(rdest_ref, rsrc_ref, rpad_ref, rsem)

    half = d // 2
    hd = d // CROSS_HEADS
    lane = lax.broadcasted_iota(I32, (sub, LANES), 1)
    big = jnp.int32(LANES)
    strict_lower = jnp.where(lax.broadcasted_iota(I32, (sub, sub), 0) > lax.broadcasted_iota(I32, (sub, sub), 1),
                             1.0, 0.0).astype(BF16)

    def lane_argmax(vals):
        m = jnp.max(vals, axis=-1, keepdims=True)
        idx = jnp.min(jnp.where(vals == m, lane, big), axis=-1, keepdims=True)
        return m, idx

    groups = [slice(r0, r0 + sub) for r0 in range(0, tm, sub)]
    heads = [slice(h * hd, (h + 1) * hd) for h in range(CROSS_HEADS)]
    da = [jnp.concatenate([da_ref[0, h, rs, :] for h in range(DA_HEADS)], axis=-1) for rs in groups]
    h1 = [x_ref[0, rs, :] + _dot(da[g], wo_ref[:half, :]) + _dot(gla_ref[0, rs, :], wo_ref[half:, :])
          for g, rs in enumerate(groups)]

    u = [_rms(t, gc_ref[...]).astype(BF16) for t in h1]
    q = [_dot(t, wq_ref[...]) for t in u]
    qh = [[_rms(t[:, hs], qn_ref[...]).astype(BF16) for hs in heads] for t in q]
    sc = [[_dot_nt(t[h], km_ref[0, :, hs]) for h, hs in enumerate(heads)] for t in qh]
    pr = []
    for t in sc:
        e = [jnp.exp(v - jnp.max(v, axis=-1, keepdims=True)) for v in t]
        pr.append([(v / jnp.sum(v, axis=-1, keepdims=True)).astype(BF16) for v in e])
    o = [jnp.concatenate([_dot(t[h], vm_ref[0, :, hs]) for h, hs in enumerate(heads)], axis=-1).astype(BF16)
         for t in pr]
    h2 = [h1[g] + _dot(o[g], wco_ref[...]) for g in range(len(groups))]
    for g, rs in enumerate(groups):
        h_ref[0, rs, :] = h2[g]

    xn = [_rms(t, gf_ref[...]).astype(BF16) for t in h2]
    logits = [_dot(t, wr_ref[...]) + br_ref[...] for t in xn]
    base = cnt_ref[0:1, :]
    for g, rs in enumerate(groups):
        bits = lax.bitcast_convert_type(xn[g].astype(F32), U32)
        xn_ref[0, rs, :] = (bits[:, :half] >> 16) | (bits[:, half:] & HI16)

        lg = jnp.where(lane < N_GROUPS, logits[g], NEG_INF)
        g_max, g_sel = lane_argmax(lg)
        p_g = 1.0 / jnp.sum(jnp.exp(lg - g_max), axis=-1, keepdims=True)
        e_lo = _EXP_LANE0 + g_sel * EXPERTS_PER_GROUP
        in_group = jnp.logical_and(lane >= e_lo, lane < e_lo + EXPERTS_PER_GROUP)
        le = jnp.where(in_group, logits[g], NEG_INF)
        m1, i1 = lane_argmax(le)
        m2, i2 = lane_argmax(jnp.where(lane == i1, NEG_INF, le))
        e2 = jnp.exp(m2 - m1)
        gate0 = p_g / (1.0 + e2)
        gate1 = p_g * e2 / (1.0 + e2)
        e0 = i1 - _EXP_LANE0
        e1 = i2 - _EXP_LANE0

        hot0 = lane == e0
        hot1 = lane == e1
        onehot = jnp.where(jnp.logical_or(hot0, hot1), 1.0, 0.0)
        before = _dot(strict_lower, onehot.astype(BF16)) + base
        pos0 = jnp.sum(jnp.where(hot0, before, 0.0), axis=-1, keepdims=True)
        pos1 = jnp.sum(jnp.where(hot1, before, 0.0), axis=-1, keepdims=True)

        meta = jnp.zeros(logits[g].shape, F32)
        for idx, val in ((_META_E0, e0.astype(F32)), (_META_E1, e1.astype(F32)), (_META_G0, gate0),
                         (_META_G1, gate1), (_META_P0, pos0), (_META_P1, pos1)):
            meta = jnp.where(lane == idx, val, meta)
        meta_ref[0, rs, :] = meta
        ids_ref[0, 0, :, rs] = meta.T[:ids_ref.shape[2], :]
        base = base + jnp.sum(onehot, axis=0, keepdims=True)
    cnt_ref[...] = jnp.broadcast_to(base, cnt_ref.shape)
    if rider:
        _scatter_wait(rsrc_ref, rpad_ref, rsem)


def _post(x, da, gla, w_o, norm_cross, w_cq, cross_qn, k_mem, v_mem, w_co, norm_ffn, w_group, b_group,
          w_expert, b_expert, *, half, rider=None):
    b, s, d = x.shape
    bh = b // 2
    b0 = half * bh
    tm = TM_POST
    ns = s // tm
    m = k_mem.shape[1]
    w_r = jnp.pad(jnp.concatenate([w_group, w_expert], axis=1), ((0, 0), (0, LANES - N_GROUPS - N_EXPERTS)))
    b_r = jnp.pad(jnp.concatenate([b_group, b_expert]), (0, LANES - N_GROUPS - N_EXPERTS)).reshape(1, LANES)
    const = lambda shape: pl.BlockSpec(shape, lambda i, j, *_: (0,) * len(shape))
    tile_in = lambda width: pl.BlockSpec((1, tm, width), lambda i, j, *_: (i + b0, j, 0))
    tile_out = lambda width: pl.BlockSpec((1, tm, width), lambda i, j, *_: (i, j, 0))
    per_b = lambda: pl.BlockSpec((1, m, d), lambda i, j, *_: (i + b0, 0, 0))
    da_spec = pl.BlockSpec((1, DA_HEADS, tm, DA_V_DIM), lambda i, j, *_: (i + b0, 0, j, 0))
    in_specs = [tile_in(d), da_spec, tile_in(d // 2), const((d, d)), const((1, d)), const((d, d)),
                const((1, d // CROSS_HEADS)), per_b(), per_b(), const((d, d)), const((1, d)),
                const((d, LANES)), const((1, LANES))]
    out_specs = [tile_out(d), tile_out(d // 2), tile_out(LANES),
                 pl.BlockSpec((1, 1, 8, tm), lambda i, j, *_: (i, j, 0, 0)), const((8, LANES))]
    out_shape = [
        jax.ShapeDtypeStruct((bh, s, d), F32),
        jax.ShapeDtypeStruct((bh, s, d // 2), U32),
        jax.ShapeDtypeStruct((bh, s, LANES), F32),
        jax.ShapeDtypeStruct((bh, ns, 8, tm), F32),
        jax.ShapeDtypeStruct((8, LANES), F32),
    ]
    args = (x, da, gla, w_o.astype(BF16), norm_cross.reshape(1, d), w_cq.astype(BF16), cross_qn.reshape(1, -1),
            k_mem, v_mem, w_co.astype(BF16), norm_ffn.reshape(1, d), w_r.astype(BF16), b_r)
    scalars, scratch = (), []
    if rider is not None:
        sizes, pend, dest, xn_src, n_rows = rider
        scalars = (sizes, pend)
        in_specs += [pl.BlockSpec((1, 2, tm), lambda i, j, *_: (i * ns + j, 0, 0), memory_space=pltpu.SMEM),
                     pl.BlockSpec((tm, d // 2), lambda i, j, *_: (i * ns + j, 0))]
        out_specs.append(pl.BlockSpec(memory_space=pl.ANY))
        out_shape.append(jax.ShapeDtypeStruct((n_rows, d // 2), U32))
        scratch = [pltpu.VMEM((EXPERT_ROWS, d // 2), U32), pltpu.SemaphoreType.DMA, pltpu.SemaphoreType.DMA]
        args += (dest, xn_src)
    return pl.pallas_call(
        functools.partial(_post_kernel, d=d, tm=tm, sub=SUB_POST, rider=rider is not None),
        grid_spec=pltpu.PrefetchScalarGridSpec(
            num_scalar_prefetch=len(scalars), grid=(bh, ns), in_specs=in_specs, out_specs=out_specs,
            scratch_shapes=scratch),
        out_shape=out_shape,
        compiler_params=_params("arbitrary", "arbitrary"),
        name="post_scatter" if rider is not None else "post",
    )(*scalars, *args)


def _row_copy(src_ref, src_row, dst_ref, dst_row, sem):
    return pltpu.make_async_copy(src_ref.at[pl.ds(src_row, 1)], dst_ref.at[pl.ds(dst_row, 1)], sem)


def _zero_fill(size_ref, pend_ref, xpad_ref, zero_ref, zsem):
    zero_ref[...] = jnp.zeros(zero_ref.shape, zero_ref.dtype)
    rows = zero_ref.shape[0]
    n_blocks = xpad_ref.shape[0] // rows
    n_used = pend_ref[N_EXPERTS - 1] // rows

    def zero_block(blk):
        return pltpu.make_async_copy(zero_ref, xpad_ref.at[pl.ds(pl.multiple_of(blk * rows, rows), rows)], zsem)

    def last_block(e, fn):
        @pl.when(size_ref[e] > 0)
        def _():
            fn(zero_block(pend_ref[e] // rows - 1))

    for fn in (lambda cp: cp.start(), lambda cp: cp.wait()):
        lax.fori_loop(0, N_EXPERTS, lambda e, c: (last_block(e, fn), c)[1], 0)
        lax.fori_loop(n_used, n_blocks, lambda blk, c: (fn(zero_block(blk)), c)[1], 0)


def _scatter_rows(dest_ref, src_ref, xpad_ref, sem):
    for t in range(src_ref.shape[0]):
        for k in range(2):
            _row_copy(src_ref, t, xpad_ref, dest_ref[0, k, t], sem).start(priority=k)


def _scatter_wait(src_ref, xpad_ref, sem):
    for _ in range(2):
        pltpu.make_async_copy(src_ref, xpad_ref.at[pl.ds(0, src_ref.shape[0])], sem).wait()


def _cast_expert_weights(fresh, wg_ref, wu_ref, wd_ref, wg_s, wu_s, wd_s):
    @pl.when(fresh)
    def _():
        wg_s[...] = wg_ref[0].astype(BF16)
        wu_s[...] = wu_ref[0].astype(BF16)
        wd_s[...] = wd_ref[0].astype(BF16)


def _expert_mlp(x_ref, wg_s, wu_s, wd_s, out_ref):
    words = x_ref[...]
    half = words.shape[1]
    lo = lax.bitcast_convert_type(words << 16, F32).astype(BF16)
    hi = lax.bitcast_convert_type(words & HI16, F32).astype(BF16)
    gate = _dot(lo, wg_s[:half, :]) + _dot(hi, wg_s[half:, :])
    up = _dot(lo, wu_s[:half, :]) + _dot(hi, wu_s[half:, :])
    hid = gate / (1.0 + jnp.exp(-gate)) * up
    out_ref[...] = _dot(hid.astype(BF16), wd_s[...])


def _experts_scatter_kernel(be_ref, nused_ref, size_ref, pend_ref, x_ref, wg_ref, wu_ref, wd_ref, rdest_ref, rsrc_ref,
                            out_ref, rpad_ref, wg_s, wu_s, wd_s, zero_ref, rsem, zsem, *, rider_steps):
    i = pl.program_id(0)
    used = i < nused_ref[0]
    new_expert = jnp.logical_or(i == 0, be_ref[i] != be_ref[jnp.maximum(i - 1, 0)])
    mlp = functools.partial(_expert_mlp, x_ref, wg_s, wu_s, wd_s, out_ref)

    @pl.when(i == 0)
    def _():
        _zero_fill(size_ref, pend_ref, rpad_ref, zero_ref, zsem)

    _cast_expert_weights(jnp.logical_and(used, new_expert), wg_ref, wu_ref, wd_ref, wg_s, wu_s, wd_s)
    riding = i < rider_steps

    @pl.when(riding)
    def _():
        _scatter_rows(rdest_ref, rsrc_ref, rpad_ref, rsem)
        mlp()
        _scatter_wait(rsrc_ref, rpad_ref, rsem)

    pl.when(jnp.logical_and(used, jnp.logical_not(riding)))(mlp)

    @pl.when(jnp.logical_not(used))
    def _():
        out_ref[...] = jnp.zeros(out_ref.shape, F32)


def _gather_rows(dest_ref, src_ref, buf_ref, sem):
    for t in range(buf_ref.shape[1]):
        for k in range(2):
            _row_copy(src_ref, dest_ref[0, k, t], buf_ref.at[k], t, sem).start(priority=k)


def _combine_rows(h_ref, meta_ref, src_ref, buf_ref, sem):
    for k in range(2):
        pltpu.make_async_copy(src_ref.at[pl.ds(0, buf_ref.shape[1])], buf_ref.at[k], sem).wait()
    meta = meta_ref[...]
    g0 = meta[:, _META_G0:_META_G0 + 1]
    g1 = meta[:, _META_G1:_META_G1 + 1]
    return h_ref[...] + g0 * buf_ref[0] + g1 * buf_ref[1]


def _experts_gather_kernel(be_ref, nused_ref, x_ref, wg_ref, wu_ref, wd_ref, gdest_ref, h_ref, meta_ref, opad_ref,
                           out_ref, y_ref, wg_s, wu_s, wd_s, gbuf_ref, gsem, *, rider_steps):
    i = pl.program_id(0)
    used = i < nused_ref[0]
    new_expert = jnp.logical_or(i == 0, be_ref[i] != be_ref[jnp.maximum(i - 1, 0)])
    mlp = functools.partial(_expert_mlp, x_ref, wg_s, wu_s, wd_s, out_ref)
    _cast_expert_weights(jnp.logical_and(used, new_expert), wg_ref, wu_ref, wd_ref, wg_s, wu_s, wd_s)

    def gather():
        _gather_rows(gdest_ref, opad_ref, gbuf_ref.at[i % 2], gsem.at[i % 2])

    def combine():
        slot = (i - 1) % 2
        y_ref[...] = _combine_rows(h_ref, meta_ref, opad_ref, gbuf_ref.at[slot], gsem.at[slot])

    @pl.when(i == 0)
    def _():
        gather()
        mlp()

    @pl.when(jnp.logical_and(i > 0, i < rider_steps))
    def _():
        gather()
        mlp()
        combine()

    @pl.when(i == rider_steps)
    def _():
        combine()
        pl.when(used)(mlp)

    pl.when(jnp.logical_and(i > rider_steps, used))(mlp)

    @pl.when(jnp.logical_not(used))
    def _():
        out_ref[...] = jnp.zeros(out_ref.shape, F32)


def _experts_kernel(be_ref, nused_ref, x_ref, wg_ref, wu_ref, wd_ref, out_ref, wg_s, wu_s, wd_s):
    i = pl.program_id(0)
    used = i < nused_ref[0]
    new_expert = jnp.logical_or(i == 0, be_ref[i] != be_ref[jnp.maximum(i - 1, 0)])
    _cast_expert_weights(jnp.logical_and(used, new_expert), wg_ref, wu_ref, wd_ref, wg_s, wu_s, wd_s)
    pl.when(used)(functools.partial(_expert_mlp, x_ref, wg_s, wu_s, wd_s, out_ref))

    @pl.when(jnp.logical_not(used))
    def _():
        out_ref[...] = jnp.zeros(out_ref.shape, F32)


def _retile(dest, tr):
    tiles, _, tm = dest.shape
    return dest.reshape(tiles, 2, tm // tr, tr).transpose(0, 2, 1, 3).reshape(tiles * (tm // tr), 2, tr)


def _experts(x_pad, block_expert, n_used, w_gate, w_up, w_down, scatter=None, gather=None):
    n_rows = x_pad.shape[0]
    _, d, f = w_gate.shape
    rows, tr = EXPERT_ROWS, RIDER_ROWS
    row_blk = lambda i, be, nu, *_: (jnp.minimum(i, nu[0] - 1), 0)
    weights = lambda shape: pl.BlockSpec(shape, lambda i, be, *_: (be[i], 0, 0))
    in_specs = [pl.BlockSpec((rows, d // 2), row_blk), weights((1, d, f)), weights((1, d, f)), weights((1, f, d))]
    out_specs = [pl.BlockSpec((rows, d), lambda i, *_: (i, 0))]
    out_shape = [jax.ShapeDtypeStruct((n_rows, d), F32)]
    scratch = [pltpu.VMEM((d, f), BF16), pltpu.VMEM((d, f), BF16), pltpu.VMEM((f, d), BF16)]
    scalars, args = (block_expert, n_used), (x_pad, w_gate, w_up, w_down)
    body, name = _experts_kernel, "experts"
    if scatter is not None or gather is not None:
        n_tokens = (scatter[3] if scatter is not None else gather[1]).shape[0]
        steps = n_tokens // tr
        assert 2 * n_tokens >= steps * rows, "riding steps must all be used expert blocks"
        tile = lambda width, shift: pl.BlockSpec((tr, width), lambda i, *_: (jnp.clip(i - shift, 0, steps - 1), 0))
        ids = lambda: pl.BlockSpec((1, 2, tr), lambda i, *_: (jnp.minimum(i, steps - 1), 0, 0),
                                   memory_space=pltpu.SMEM)
    if scatter is not None:
        sizes, pend, dest, xn_src, other_rows = scatter
        scalars += (sizes, pend)
        in_specs += [ids(), tile(d // 2, 0)]
        out_specs.append(pl.BlockSpec(memory_space=pl.ANY))
        out_shape.append(jax.ShapeDtypeStruct((other_rows, d // 2), U32))
        scratch += [pltpu.VMEM((rows, d // 2), U32), pltpu.SemaphoreType.DMA, pltpu.SemaphoreType.DMA]
        args += (_retile(dest, tr), xn_src)
        body, name = functools.partial(_experts_scatter_kernel, rider_steps=steps), "experts_scatter"
    elif gather is not None:
        dest, h2, meta, opad = gather
        in_specs += [ids(), tile(d, 1), tile(LANES, 1), pl.BlockSpec(memory_space=pl.ANY)]
        out_specs.append(tile(d, 1))
        out_shape.append(jax.ShapeDtypeStruct((n_tokens, d), F32))
        scratch += [pltpu.VMEM((2, 2, tr, d), F32), pltpu.SemaphoreType.DMA((2,))]
        args += (_retile(dest, tr), h2, meta, opad)
        body, name = functools.partial(_experts_gather_kernel, rider_steps=steps), "experts_gather"
    out = pl.pallas_call(
        body,
        grid_spec=pltpu.PrefetchScalarGridSpec(
            num_scalar_prefetch=len(scalars), grid=(n_rows // rows,), in_specs=in_specs, out_specs=out_specs,
            scratch_shapes=scratch),
        out_shape=out_shape,
        compiler_params=_params("arbitrary"),
        name=name,
    )(*scalars, *args)
    return out if len(out) > 1 else out[0]


def _combine_tail_kernel(dest_ref, y0_ref, h_ref, meta_ref, opad_ref, y_ref, buf_ref, sem, *, nt):
    s = pl.program_id(0)

    @pl.when(s < nt)
    def _():
        _gather_rows(dest_ref, opad_ref, buf_ref.at[s % 2], sem.at[s % 2])

    @pl.when(s > 0)
    def _():
        slot = (s - 1) % 2
        y_ref[0] = y0_ref[...]
        y_ref[1] = _combine_rows(h_ref, meta_ref, opad_ref, buf_ref.at[slot], sem.at[slot])


def _combine_tail(y0, h2, meta, dest, out_pad):
    th, d = h2.shape
    tm = TM_TAIL
    nt = th // tm
    prev = lambda s: (jnp.maximum(s - 1, 0), 0)
    return pl.pallas_call(
        functools.partial(_combine_tail_kernel, nt=nt),
        grid=(nt + 1,),
        in_specs=[
            pl.BlockSpec((1, 2, tm), lambda s: (jnp.minimum(s, nt - 1), 0, 0), memory_space=pltpu.SMEM),
            pl.BlockSpec((tm, d), prev), pl.BlockSpec((tm, d), prev), pl.BlockSpec((tm, LANES), prev),
            pl.BlockSpec(memory_space=pl.ANY),
        ],
        out_specs=pl.BlockSpec((2, tm, d), lambda s: (0, jnp.maximum(s - 1, 0), 0)),
        scratch_shapes=[pltpu.VMEM((2, 2, tm, d), F32), pltpu.SemaphoreType.DMA((2,))],
        out_shape=jax.ShapeDtypeStruct((2, th, d), F32),
        compiler_params=_params("arbitrary"),
        name="combine_tail",
    )(_retile(dest, tm), y0, h2, meta, out_pad)


def _route_tables(counts, ids, n_tokens):
    rows = EXPERT_ROWS
    sizes = counts[0, :N_EXPERTS].astype(I32)
    padded = (sizes + rows - 1) // rows * rows
    pend = jnp.cumsum(padded)
    pstart = pend - padded
    n_rows = 2 * n_tokens + N_EXPERTS * rows
    block_start = jnp.arange(n_rows // rows, dtype=I32) * rows
    block_expert = jnp.minimum(jnp.sum(pend[None, :] <= block_start[:, None], axis=1), N_EXPERTS - 1).astype(I32)
    n_used = (pend[-1:] // rows).astype(I32)
    expert = ids[:, _META_E0:_META_E1 + 1].astype(I32)
    rank = ids[:, _META_P0:_META_P1 + 1].astype(I32)
    experts = jnp.arange(N_EXPERTS, dtype=I32).reshape(-1, 1, 1, 1)
    dest = jnp.sum(jnp.where(expert[None] == experts, pstart.reshape(-1, 1, 1, 1), 0), axis=0) + rank
    return sizes, pend, block_expert, n_used, dest, n_rows


def kernel(x, mem, norm_mix, w_in, da_q_norm, da_k_norm, lambda_q1, lambda_k1, lambda_q2, lambda_k2,
           da_out_norm, gla_gate_w, gla_gate_b, gla_out_norm, w_o, norm_cross, norm_mem, w_cq, w_ckv,
           cross_q_norm, cross_k_norm, w_co, norm_ffn, w_group, b_group, w_expert, b_expert,
           w_e_gate, w_e_up, w_e_down):
    b, s, d = x.shape
    th = b // 2 * s
    assert b % 2 == 0 and d == 2 * _QK, "the MoE pipeline splits the batch in two; widths follow the layer constants"
    assert all(s % tile == 0 for tile in (TM_PROJ, TS_GLA, TM_POST, ATT_BLK)), "sequence tiles must divide SEQ"
    assert th % TM_TAIL == 0 and th % RIDER_ROWS == 0 and (2 * th) % EXPERT_ROWS == 0, "token tiles must divide"
    h = x
    for l in range(norm_mix.shape[0]):
        assert l == 0, "lam_init is fixed for a single layer"
        qt, kda, vt, gq, gk, gv, gg, la = _in_proj(h, norm_mix[l], w_in[l], da_q_norm[l], da_k_norm[l],
                                                   gla_gate_w[l], gla_gate_b[l])
        da = _diff_attn(qt, kda, vt, lambda_q1[l], lambda_k1[l], lambda_q2[l], lambda_k2[l], da_out_norm[l],
                        da_q_norm[l], da_k_norm[l])
        gla = _gla(gq, gk, la, gv, gg, gla_out_norm[l])
        k_mem, v_mem = _mem_kv(mem, norm_mem[l], w_ckv[l], cross_k_norm[l])
        post = functools.partial(_post, h, da, gla, w_o[l], norm_cross[l], w_cq[l], cross_q_norm[l], k_mem, v_mem,
                                 w_co[l], norm_ffn[l], w_group[l], b_group[l], w_expert[l], b_expert[l])
        experts = functools.partial(_experts, w_gate=w_e_gate[l], w_up=w_e_up[l], w_down=w_e_down[l])

        h2_0, xn_0, meta_0, ids_0, counts_0 = post(half=0)
        sizes_0, pend_0, be_0, used_0, dest_0, n_rows = _route_tables(counts_0, ids_0.reshape(-1, 8, TM_POST), th)
        h2_1, xn_1, meta_1, ids_1, counts_1, xpad_0 = post(
            half=1, rider=(sizes_0, pend_0, dest_0, xn_0.reshape(th, d // 2), n_rows))
        sizes_1, pend_1, be_1, used_1, dest_1, _ = _route_tables(counts_1, ids_1.reshape(-1, 8, TM_POST), th)
        opad_0, xpad_1 = experts(xpad_0, be_0, used_0,
                                 scatter=(sizes_1, pend_1, dest_1, xn_1.reshape(th, d // 2), n_rows))
        opad_1, y_0 = experts(xpad_1, be_1, used_1,
                              gather=(dest_0, h2_0.reshape(th, d), meta_0.reshape(th, LANES), opad_0))
        h = _combine_tail(y_0, h2_1.reshape(th, d), meta_1.reshape(th, LANES), dest_1, opad_1).reshape(b, s, d)
    return h
```
